```python
import math
import jax, jax.numpy as jnp
from jax import lax
import numpy as np

D_MODEL = 2048
BATCH = 4
SEQ = 4096
DEPTH = 2

GRID_W = 64
CTX_LEN = 256
CONV_DIM = 1024
CONV_K = 3
ATT_HEADS = 8
ATT_KV_HEADS = 2
HEAD_DIM = 128
MLA_HEADS = 16
Q_LORA = 512
KV_LORA = 512
QK_NOPE = 128
QK_ROPE = 64
V_DIM = 128
N_EXPERTS = 64
TOP_K = 6
N_GROUPS = 8
TOPK_GROUPS = 4
EXPERT_FF = 512
SHARED_FF = 512
ROUTED_SCALE = 2.5
MOE_BLOCK = 256
Q_BLOCK = 128
ROPE_THETA = 10000.0
LN_EPS = 1e-5
RMS_EPS = 1e-6
DN_ALPHA = (2 * DEPTH) ** 0.25
DN_BETA = (8 * DEPTH) ** -0.25

kernel_name = 'hybrid_conv_gqa_mla_moe_diffusion_trunk'


def rms_norm(x, g):
    xf = x.astype(jnp.float32)
    y = xf * lax.rsqrt(jnp.mean(xf * xf, axis=-1, keepdims=True) + RMS_EPS)
    return (y * g.astype(jnp.float32)).astype(x.dtype)


def layer_norm(x, g, b):
    xf = x.astype(jnp.float32)
    mu = jnp.mean(xf, axis=-1, keepdims=True)
    xc = xf - mu
    var = jnp.mean(xc * xc, axis=-1, keepdims=True)
    y = xc * lax.rsqrt(var + LN_EPS) * g.astype(jnp.float32) + b.astype(jnp.float32)
    return y.astype(x.dtype)


def axial_rope(n_tok, rot_dim):
    rows = n_tok // GRID_W
    n_freq = rot_dim // 4
    inv = ROPE_THETA ** (-jnp.arange(n_freq, dtype=jnp.float32) / n_freq)
    row = jnp.repeat(jnp.arange(rows, dtype=jnp.float32), GRID_W)
    col = jnp.tile(jnp.arange(GRID_W, dtype=jnp.float32), rows)
    ang = jnp.concatenate([row[:, None] * inv, col[:, None] * inv], axis=-1)
    return jnp.cos(ang), jnp.sin(ang)


def apply_rope(x, cos, sin):
    xf = x.astype(jnp.float32)
    x1, x2 = jnp.split(xf, 2, axis=-1)
    c = cos[:, None, :]
    s = sin[:, None, :]
    return jnp.concatenate([x1 * c - x2 * s, x2 * c + x1 * s], axis=-1).astype(x.dtype)


def attend(q, k, v, scale):
    s = jnp.einsum('bqhgd,bkhd->bhgqk', q, k, preferred_element_type=jnp.float32) * scale
    p = jax.nn.softmax(s, axis=-1).astype(v.dtype)
    o = jnp.einsum('bhgqk,bkhd->bqhgd', p, v)
    return o.reshape(o.shape[0], o.shape[1], -1, o.shape[-1])


def attend_blocked(q, k, v, scale):
    b, s = q.shape[0], q.shape[1]
    nb = s // Q_BLOCK
    qb = jnp.moveaxis(q.reshape(b, nb, Q_BLOCK, *q.shape[2:]), 1, 0)
    ob = lax.map(lambda qq: attend(qq, k, v, scale), qb)
    return jnp.moveaxis(ob, 0, 1).reshape(b, s, ob.shape[3], ob.shape[4])


def short_conv(x, w):
    return lax.conv_general_dilated(
        x, w[:, None, :].astype(x.dtype), window_strides=(1,),
        padding=((CONV_K // 2, CONV_K // 2),),
        dimension_numbers=('NWC', 'WIO', 'NWC'),
        feature_group_count=x.shape[-1])


def conv_attn_mixer(u_lat, u_ctx, w_in, conv_w, q_gain, k_gain, w_out, ctx_out):
    d_q = ATT_HEADS * HEAD_DIM
    d_kv = ATT_KV_HEADS * HEAD_DIM
    kv_lo = 3 * CONV_DIM + d_q
    cuts = [CONV_DIM, 2 * CONV_DIM, 3 * CONV_DIM, kv_lo, kv_lo + d_kv]
    grp = ATT_HEADS // ATT_KV_HEADS
    scale = 1.0 / math.sqrt(HEAD_DIM)

    def heads(t, n, gain):
        return rms_norm(t.reshape(t.shape[0], t.shape[1], n, HEAD_DIM), gain)

    gb, gc, hv, q, k, v = jnp.split(u_lat @ w_in, cuts, axis=-1)
    cos, sin = axial_rope(u_lat.shape[1], HEAD_DIM)
    q = apply_rope(heads(q, ATT_HEADS, q_gain), cos, sin)
    k = apply_rope(heads(k, ATT_KV_HEADS, k_gain), cos, sin)
    v = v.reshape(v.shape[0], v.shape[1], ATT_KV_HEADS, HEAD_DIM)

    if ctx_out:
        gb_c, gc_c, hv_c, q_c, k_c, v_c = jnp.split(u_ctx @ w_in, cuts, axis=-1)
    else:
        k_c, v_c = jnp.split(u_ctx @ w_in[:, kv_lo:], [d_kv], axis=-1)
    k_c = heads(k_c, ATT_KV_HEADS, k_gain)
    v_c = v_c.reshape(v_c.shape[0], v_c.shape[1], ATT_KV_HEADS, HEAD_DIM)

    k_all = jnp.concatenate([k_c, k], axis=1)
    v_all = jnp.concatenate([v_c, v], axis=1)
    q5 = q.reshape(q.shape[0], q.shape[1], ATT_KV_HEADS, grp, HEAD_DIM)
    att = attend_blocked(q5, k_all, v_all, scale)
    att = att.reshape(att.shape[0], att.shape[1], d_q)
    conv = gb * short_conv(gc * hv, conv_w)
    y_lat = jnp.concatenate([conv, att], axis=-1) @ w_out

    y_ctx = None
    if ctx_out:
        q_c = heads(q_c, ATT_HEADS, q_gain).reshape(q_c.shape[0], q_c.shape[1], ATT_KV_HEADS, grp, HEAD_DIM)
        att_c = attend(q_c, k_c, v_c, scale).reshape(q_c.shape[0], q_c.shape[1], d_q)
        conv_c = gb_c * short_conv(gc_c * hv_c, conv_w)
        y_ctx = jnp.concatenate([conv_c, att_c], axis=-1) @ w_out
    return y_lat, y_ctx


def mla_mixer(u_lat, u_ctx, w_down, q_gain, kv_gain, w_uq, w_ukv, w_out, ctx_out):
    scale = 1.0 / math.sqrt(QK_NOPE + QK_ROPE)

    def queries(cq, rope):
        qq = rms_norm(cq, q_gain) @ w_uq
        qq = qq.reshape(cq.shape[0], cq.shape[1], MLA_HEADS, QK_NOPE + QK_ROPE)
        q_nope, q_rope = jnp.split(qq, [QK_NOPE], axis=-1)
        if rope is not None:
            q_rope = apply_rope(q_rope, *rope)
        return jnp.concatenate([q_nope, q_rope], axis=-1)[:, :, :, None, :]

    def keys_values(ckv, kr, rope):
        kv = rms_norm(ckv, kv_gain) @ w_ukv
        kv = kv.reshape(ckv.shape[0], ckv.shape[1], MLA_HEADS, QK_NOPE + V_DIM)
        k_nope, vv = jnp.split(kv, [QK_NOPE], axis=-1)
        kr = kr[:, :, None, :]
        if rope is not None:
            kr = apply_rope(kr, *rope)
        kk = jnp.concatenate([k_nope, jnp.broadcast_to(kr, k_nope.shape[:-1] + (QK_ROPE,))], axis=-1)
        return kk, vv

    rope = axial_rope(u_lat.shape[1], QK_ROPE)
    cq, ckv, kr = jnp.split(u_lat @ w_down, [Q_LORA, Q_LORA + KV_LORA], axis=-1)
    q = queries(cq, rope)
    k, v = keys_values(ckv, kr, rope)

    if ctx_out:
        cq_c, ckv_c, kr_c = jnp.split(u_ctx @ w_down, [Q_LORA, Q_LORA + KV_LORA], axis=-1)
    else:
        ckv_c, kr_c = jnp.split(u_ctx @ w_down[:, Q_LORA:], [KV_LORA], axis=-1)
    k_c, v_c = keys_values(ckv_c, kr_c, None)

    k_all = jnp.concatenate([k_c, k], axis=1)
    v_all = jnp.concatenate([v_c, v], axis=1)
    att = attend_blocked(q, k_all, v_all, scale)
    y_lat = att.reshape(att.shape[0], att.shape[1], MLA_HEADS * V_DIM) @ w_out

    y_ctx = None
    if ctx_out:
        att_c = attend(queries(cq_c, None), k_c, v_c, scale)
        y_ctx = att_c.reshape(att_c.shape[0], att_c.shape[1], MLA_HEADS * V_DIM) @ w_out
    return y_lat, y_ctx


def moe_ffn(t, router_w, router_b, w_gate, w_up, w_down, s_gate, s_up, s_down):
    n_tok, d = t.shape
    logits = jnp.einsum('td,de->te', t, router_w, preferred_element_type=jnp.float32)
    scores = jax.nn.sigmoid(logits)
    sel = scores + router_b.astype(jnp.float32)
    grp_score = lax.top_k(sel.reshape(n_tok, N_GROUPS, N_EXPERTS // N_GROUPS), 2)[0].sum(-1)
    _, gidx = lax.top_k(grp_score, TOPK_GROUPS)
    gmask = jnp.any(gidx[:, :, None] == jnp.arange(N_GROUPS)[None, None, :], axis=1)
    sel = jnp.where(jnp.repeat(gmask, N_EXPERTS // N_GROUPS, axis=1), sel, -jnp.inf)
    _, eidx = lax.top_k(sel, TOP_K)
    gw = jnp.take_along_axis(scores, eidx, axis=1)
    gw = gw / jnp.sum(gw, axis=-1, keepdims=True) * ROUTED_SCALE

    n_asg = n_tok * TOP_K
    flat_e = eidx.reshape(-1)
    flat_t = jnp.repeat(jnp.arange(n_tok, dtype=jnp.int32), TOP_K)
    flat_w = gw.reshape(-1)
    order = jnp.argsort(flat_e)
    se = flat_e[order]
    counts = jnp.bincount(flat_e, length=N_EXPERTS)
    padded = (counts + MOE_BLOCK - 1) // MOE_BLOCK * MOE_BLOCK
    pend = jnp.cumsum(padded)
    pstart = pend - padded
    ustart = jnp.cumsum(counts) - counts
    dest = pstart[se] + jnp.arange(n_asg, dtype=jnp.int32) - ustart[se]
    n_blocks = (n_asg + N_EXPERTS * (MOE_BLOCK - 1)) // MOE_BLOCK + 1
    cap = n_blocks * MOE_BLOCK
    slot_tok = jnp.full((cap,), n_tok, jnp.int32).at[dest].set(flat_t[order])
    slot_w = jnp.zeros((cap,), jnp.float32).at[dest].set(flat_w[order])
    block_start = jnp.arange(n_blocks, dtype=pend.dtype) * MOE_BLOCK
    block_e = jnp.minimum(jnp.searchsorted(pend, block_start, side='right'), N_EXPERTS - 1)

    t_pad = jnp.concatenate([t, jnp.zeros((1, d), t.dtype)], axis=0)

    def expert_block(args):
        tok, e = args
        xb = t_pad[tok]
        hb = jax.nn.silu(xb @ w_gate[e]) * (xb @ w_up[e])
        return hb @ w_down[e]

    yb = lax.map(expert_block, (slot_tok.reshape(n_blocks, MOE_BLOCK), block_e))
    yb = yb.reshape(cap, d) * slot_w[:, None].astype(yb.dtype)
    routed = jax.ops.segment_sum(yb, slot_tok, num_segments=n_tok + 1)[:n_tok]
    shared = (jax.nn.silu(t @ s_gate) * (t @ s_up)) @ s_down
    return routed + shared


def setup_inputs(seed: int = 0) -> dict:
    key = jax.random.key(seed)
    ks = iter(jax.random.split(key, 32))
    f32 = jnp.float32

    def nrm(shape, scale):
        return jax.random.normal(next(ks), shape, f32) * scale

    d = D_MODEL
    n_even = (DEPTH + 1) // 2
    n_odd = DEPTH // 2
    a_in = 3 * CONV_DIM + (ATT_HEADS + 2 * ATT_KV_HEADS) * HEAD_DIM
    a_mix = CONV_DIM + ATT_HEADS * HEAD_DIM
    m_mix = MLA_HEADS * V_DIM
    return {
        'x': nrm((BATCH, SEQ, d), 1.0),
        'c': nrm((BATCH, d), 1.0),
        'ctx': nrm((BATCH, CTX_LEN, d), 1.0),
        'c_ctx': nrm((d,), 1.0),
        'w_ada': nrm((DEPTH, d, 6 * d), 0.5 * d ** -0.5),
        'b_ada': nrm((DEPTH, 6 * d), 0.02),
        'ln_g': 1.0 + nrm((DEPTH, 2, d), 0.02),
        'ln_b': nrm((DEPTH, 2, d), 0.02),
        'a_w_in': nrm((n_even, d, a_in), d ** -0.5),
        'a_conv_w': nrm((n_even, CONV_K, CONV_DIM), CONV_K ** -0.5),
        'a_q_gain': 1.0 + nrm((n_even, HEAD_DIM), 0.02),
        'a_k_gain': 1.0 + nrm((n_even, HEAD_DIM), 0.02),
        'a_w_out': nrm((n_even, a_mix, d), DN_BETA * a_mix ** -0.5),
        'm_w_down': nrm((n_odd, d, Q_LORA + KV_LORA + QK_ROPE), d ** -0.5),
        'm_q_gain': 1.0 + nrm((n_odd, Q_LORA), 0.02),
        'm_kv_gain': 1.0 + nrm((n_odd, KV_LORA), 0.02),
        'm_w_uq': nrm((n_odd, Q_LORA, MLA_HEADS * (QK_NOPE + QK_ROPE)), Q_LORA ** -0.5),
        'm_w_ukv': nrm((n_odd, KV_LORA, MLA_HEADS * (QK_NOPE + V_DIM)), KV_LORA ** -0.5),
        'm_w_out': nrm((n_odd, m_mix, d), DN_BETA * m_mix ** -0.5),
        'router_w': nrm((DEPTH, d, N_EXPERTS), d ** -0.5),
        'router_b': nrm((DEPTH, N_EXPERTS), 0.01),
        'e_w_gate': nrm((DEPTH, N_EXPERTS, d, EXPERT_FF), d ** -0.5),
        'e_w_up': nrm((DEPTH, N_EXPERTS, d, EXPERT_FF), d ** -0.5),
        'e_w_down': nrm((DEPTH, N_EXPERTS, EXPERT_FF, d), DN_BETA * EXPERT_FF ** -0.5),
        's_w_gate': nrm((DEPTH, d, SHARED_FF), d ** -0.5),
        's_w_up': nrm((DEPTH, d, SHARED_FF), d ** -0.5),
        's_w_down': nrm((DEPTH, SHARED_FF, d), DN_BETA * SHARED_FF ** -0.5),
    }


def reference(x, c, ctx, c_ctx, w_ada, b_ada, ln_g, ln_b, a_w_in, a_conv_w, a_q_gain, a_k_gain, a_w_out,
              m_w_down, m_q_gain, m_kv_gain, m_w_uq, m_w_ukv, m_w_out, router_w, router_b,
              e_w_gate, e_w_up, e_w_down, s_w_gate, s_w_up, s_w_down):
    h = ctx
    n_lat = x.shape[0] * x.shape[1]
    silu_c = jax.nn.silu(c)
    silu_cc = jax.nn.silu(c_ctx)
    for i in range(DEPTH):
        last = i == DEPTH - 1
        j = i // 2
        mod = (silu_c @ w_ada[i] + b_ada[i])[:, None, :]
        mod_c = silu_cc @ w_ada[i] + b_ada[i]
        sh_m, sc_m, g_m, sh_f, sc_f, g_f = jnp.split(mod, 6, axis=-1)
        shc_m, scc_m, gc_m, shc_f, scc_f, gc_f = jnp.split(mod_c, 6, axis=-1)

        u = x * (1 + sc_m) + sh_m
        uc = h * (1 + scc_m) + shc_m
        if i % 2 == 0:
            y, yc = conv_attn_mixer(u, uc, a_w_in[j], a_conv_w[j], a_q_gain[j], a_k_gain[j], a_w_out[j], not last)
        else:
            y, yc = mla_mixer(u, uc, m_w_down[j], m_q_gain[j], m_kv_gain[j], m_w_uq[j], m_w_ukv[j], m_w_out[j], not last)
        x = layer_norm(DN_ALPHA * x + g_m * y, ln_g[i, 0], ln_b[i, 0])

        v = x * (1 + sc_f) + sh_f
        tok = v.reshape(-1, v.shape[-1])
        if not last:
            h = layer_norm(DN_ALPHA * h + gc_m * yc, ln_g[i, 0], ln_b[i, 0])
            vc = h * (1 + scc_f) + shc_f
            tok = jnp.concatenate([tok, vc.reshape(-1, vc.shape[-1])], axis=0)
        f = moe_ffn(tok, router_w[i], router_b[i], e_w_gate[i], e_w_up[i], e_w_down[i],
                    s_w_gate[i], s_w_up[i], s_w_down[i])
        x = layer_norm(DN_ALPHA * x + g_f * f[:n_lat].reshape(x.shape), ln_g[i, 1], ln_b[i, 1])
        if not last:
            h = layer_norm(DN_ALPHA * h + gc_f * f[n_lat:].reshape(h.shape), ln_g[i, 1], ln_b[i, 1])
    return x
```

```python
import functools
import math

import jax
import jax.numpy as jnp
from jax import lax
from jax.experimental import pallas as pl
from jax.experimental.pallas import tpu as pltpu

F32 = jnp.float32
BF16 = jnp.bfloat16

GRID_W = 64
CONV_DIM = 1024
ATT_HEADS = 8
ATT_KV_HEADS = 2
HEAD_DIM = 128
MLA_HEADS = 16
Q_LORA = 512
KV_LORA = 512
QK_NOPE = 128
QK_ROPE = 64
V_DIM = 128
N_EXPERTS = 64
TOP_K = 6
N_GROUPS = 8
TOPK_GROUPS = 4
ROUTED_SCALE = 2.5
ROPE_THETA = 10000.0
LN_EPS = 1e-5
RMS_EPS = 1e-6

V7X_VMEM_LIMIT_BYTES = 56 * 1024 * 1024
LANES = 128
SUBLANES = 8
MOE_ROWS = 256
MLA_DK_PAD = 256


def _params(sem):
    return pltpu.CompilerParams(dimension_semantics=sem, vmem_limit_bytes=V7X_VMEM_LIMIT_BYTES)


def _const_spec(shape):
    nd = len(shape)
    return pl.BlockSpec(shape, lambda *_: (0,) * nd)


def _ada_kernel(s_ref, w_ref, b_ref, o_ref):
    s = s_ref[...]
    s = s * (1.0 / (1.0 + jnp.exp(-s)))
    o_ref[...] = jnp.dot(s.astype(BF16), w_ref[...].astype(BF16), preferred_element_type=F32) + b_ref[...]


def _ada_table(cond, w_ada, b_ada):
    depth, d, n = w_ada.shape
    r = cond.shape[0]
    tn = 1024
    return pl.pallas_call(
        _ada_kernel,
        grid=(depth, n // tn),
        in_specs=[
            pl.BlockSpec((r, d), lambda l, j: (0, 0)),
            pl.BlockSpec((None, d, tn), lambda l, j: (l, 0, j)),
            pl.BlockSpec((None, 1, tn), lambda l, j: (l, 0, j)),
        ],
        out_specs=pl.BlockSpec((None, r, tn), lambda l, j: (l, 0, j)),
        out_shape=jax.ShapeDtypeStruct((depth, r, n), F32),
        compiler_params=_params(("parallel", "parallel")),
        name="ada_table",
    )(cond, w_ada, b_ada.reshape(depth, 1, n))


def _mod_spec(d, chunk, mod_row, tm):
    return pl.BlockSpec((None, 1, d), lambda i: (mod_row(i * tm), 0, chunk))


def _modulate_kernel(x_ref, sc_ref, sh_ref, o_ref):
    o_ref[...] = (x_ref[...] * (1.0 + sc_ref[...]) + sh_ref[...]).astype(o_ref.dtype)


def _modulate(x, mod, mod_row, tm):
    t, d = x.shape
    return pl.pallas_call(
        _modulate_kernel,
        grid=(t // tm,),
        in_specs=[
            pl.BlockSpec((tm, d), lambda i: (i, 0)),
            _mod_spec(d, 1, mod_row, tm),
            _mod_spec(d, 0, mod_row, tm),
        ],
        out_specs=pl.BlockSpec((tm, d), lambda i: (i, 0)),
        out_shape=jax.ShapeDtypeStruct((t, d), BF16),
        compiler_params=_params(("parallel",)),
        name="modulate",
    )(x, mod, mod)


def _mm_kernel(a_ref, w_ref, o_ref):
    o_ref[...] = jnp.dot(a_ref[...], w_ref[...], preferred_element_type=F32).astype(o_ref.dtype)


def _matmul(a, w, out_dtype, tm, tn):
    m, k = a.shape
    n = w.shape[1]
    return pl.pallas_call(
        _mm_kernel,
        grid=(m // tm, n // tn),
        in_specs=[
            pl.BlockSpec((tm, k), lambda i, j: (i, 0)),
            pl.BlockSpec((k, tn), lambda i, j: (0, j)),
        ],
        out_specs=pl.BlockSpec((tm, tn), lambda i, j: (i, j)),
        out_shape=jax.ShapeDtypeStruct((m, n), out_dtype),
        compiler_params=_params(("parallel", "parallel")),
        name="matmul",
    )(a, w)


def _rms(t, gain):
    return t * lax.rsqrt(jnp.mean(t * t, axis=-1, keepdims=True) + RMS_EPS) * gain


def _qkprep_kernel(p_ref, cos_ref, sin_ref, qg_ref, kg_ref, q_ref, k_ref, v_ref, *, scale):
    cos = cos_ref[...]
    sin = sin_ref[...]

    def norm_rope(t, gain):
        y = _rms(t.astype(F32), gain)
        return y * cos + pltpu.roll(y, HEAD_DIM // 2, 1) * sin

    for h in range(ATT_HEADS):
        sl = slice(h * HEAD_DIM, (h + 1) * HEAD_DIM)
        q_ref[:, sl] = (norm_rope(p_ref[:, sl], qg_ref[...]) * scale).astype(q_ref.dtype)
    k0 = ATT_HEADS * HEAD_DIM
    for h in range(ATT_KV_HEADS):
        sl = slice(h * HEAD_DIM, (h + 1) * HEAD_DIM)
        k_ref[:, sl] = norm_rope(p_ref[:, k0 + h * HEAD_DIM:k0 + (h + 1) * HEAD_DIM], kg_ref[...]).astype(k_ref.dtype)
    v0 = k0 + ATT_KV_HEADS * HEAD_DIM
    v_ref[...] = p_ref[:, v0:v0 + ATT_KV_HEADS * HEAD_DIM].astype(v_ref.dtype)


def _attn_kernel(q_ref, k_ref, v_ref, o_ref, *, group, dk, dv):
    k = k_ref[...]
    v = v_ref[...]
    for h in range(group):
        q = q_ref[:, h * dk:(h + 1) * dk]
        s = lax.dot_general(q, k, (((1,), (1,)), ((), ())), preferred_element_type=F32)
        m = jnp.max(s, axis=-1, keepdims=True)
        p = jnp.exp(s - m)
        l = jnp.sum(p, axis=-1, keepdims=True)
        o = jnp.dot(p.astype(v.dtype), v, preferred_element_type=F32)
        o_ref[:, h * dv:(h + 1) * dv] = (o / l).astype(o_ref.dtype)


def _attention(q, k, v, *, batch, sq, lk, n_kv, group, dk, dv, tq, q_row_off):
    nq = sq // tq
    off = q_row_off // tq
    return pl.pallas_call(
        functools.partial(_attn_kernel, group=group, dk=dk, dv=dv),
        grid=(batch, n_kv, nq),
        in_specs=[
            pl.BlockSpec((tq, group * dk), lambda b, g, i: (off + b * nq + i, g)),
            pl.BlockSpec((None, lk, dk), lambda b, g, i: (b, 0, g)),
            pl.BlockSpec((None, lk, dv), lambda b, g, i: (b, 0, g)),
        ],
        out_specs=pl.BlockSpec((tq, group * dv), lambda b, g, i: (b * nq + i, g)),
        out_shape=jax.ShapeDtypeStruct((batch * sq, n_kv * group * dv), BF16),
        compiler_params=_params(("parallel", "parallel", "parallel")),
        name="attention",
    )(q, k, v)


def _conv_kernel(gb_ref, gc_ref, hv_ref, gcp_ref, hvp_ref, gcn_ref, hvn_ref, w_ref, o_ref, *,
                 tm, lat_tiles, lat_seq_tiles, ctx_seq_tiles):
    i = pl.program_id(0)
    is_lat = i < lat_tiles
    pos = jnp.where(is_lat, i % lat_seq_tiles, (i - lat_tiles) % ctx_seq_tiles)
    seq_tiles = jnp.where(is_lat, lat_seq_tiles, ctx_seq_tiles)
    not_first = (pos != 0).astype(F32)
    not_last = (pos != seq_tiles - 1).astype(F32)
    p = gc_ref[...].astype(F32) * hv_ref[...].astype(F32)
    halo_prev = gcp_ref[SUBLANES - 1:SUBLANES, :].astype(F32) * hvp_ref[SUBLANES - 1:SUBLANES, :].astype(F32) * not_first
    halo_next = gcn_ref[0:1, :].astype(F32) * hvn_ref[0:1, :].astype(F32) * not_last
    row = lax.broadcasted_iota(jnp.int32, p.shape, 0)
    prev = jnp.where(row == 0, halo_prev, pltpu.roll(p, 1, 0))
    nxt = jnp.where(row == tm - 1, halo_next, pltpu.roll(p, tm - 1, 0))
    w = w_ref[...]
    conv = w[0:1, :] * prev + w[1:2, :] * p + w[2:3, :] * nxt
    o_ref[...] = (gb_ref[...].astype(F32) * conv).astype(o_ref.dtype)


def _conv_gate(p, conv_w, *, t, tm, tc, lat_tiles, lat_seq_tiles, ctx_seq_tiles):
    nct = CONV_DIM // tc
    hb = tm // SUBLANES
    n_halo = t // SUBLANES

    def cur(part):
        return pl.BlockSpec((tm, tc), lambda i, j: (i, part * nct + j))

    def prev(part):
        return pl.BlockSpec((SUBLANES, tc), lambda i, j: (jnp.maximum(i * hb - 1, 0), part * nct + j))

    def nxt(part):
        return pl.BlockSpec((SUBLANES, tc), lambda i, j: (jnp.minimum((i + 1) * hb, n_halo - 1), part * nct + j))

    return pl.pallas_call(
        functools.partial(_conv_kernel, tm=tm, lat_tiles=lat_tiles, lat_seq_tiles=lat_seq_tiles,
                          ctx_seq_tiles=ctx_seq_tiles),
        grid=(t // tm, nct),
        in_specs=[cur(0), cur(1), cur(2), prev(1), prev(2), nxt(1), nxt(2),
                  pl.BlockSpec((3, tc), lambda i, j: (0, j))],
        out_specs=pl.BlockSpec((tm, tc), lambda i, j: (i, j)),
        out_shape=jax.ShapeDtypeStruct((t, CONV_DIM), BF16),
        compiler_params=_params(("parallel", "parallel")),
        name="conv_gate",
    )(p, p, p, p, p, p, p, conv_w)


def _layer_norm(z, g, b):
    mu = jnp.mean(z, axis=-1, keepdims=True)
    zc = z - mu
    var = jnp.mean(zc * zc, axis=-1, keepdims=True)
    return zc * lax.rsqrt(var + LN_EPS) * g + b


def _outproj_ln_kernel(*refs, n_a, alpha):
    a_refs = refs[:n_a]
    w_refs = refs[n_a:2 * n_a]
    x_ref, gate_ref, lng_ref, lnb_ref, sc_ref, sh_ref, xo_ref, tok_ref = refs[2 * n_a:]
    y = jnp.dot(a_refs[0][...], w_refs[0][...], preferred_element_type=F32)
    for a_ref, w_ref in zip(a_refs[1:], w_refs[1:]):
        y = y + jnp.dot(a_ref[...], w_ref[...], preferred_element_type=F32)
    xn = _layer_norm(alpha * x_ref[...] + gate_ref[...] * y, lng_ref[...], lnb_ref[...])
    xo_ref[...] = xn
    tok_ref[...] = xn * (1.0 + sc_ref[...]) + sh_ref[...]


def _outproj_ln(a_list, w_list, x, mod, ln_g, ln_b, *, t, tm, alpha, mod_row):
    d = x.shape[1]
    n_a = len(a_list)
    in_specs = [pl.BlockSpec((tm, a.shape[1]), lambda i: (i, 0)) for a in a_list]
    in_specs += [_const_spec(w.shape) for w in w_list]
    in_specs += [
        pl.BlockSpec((tm, d), lambda i: (i, 0)),
        _mod_spec(d, 2, mod_row, tm),
        _const_spec((1, d)), _const_spec((1, d)),
        _mod_spec(d, 4, mod_row, tm),
        _mod_spec(d, 3, mod_row, tm),
    ]
    return pl.pallas_call(
        functools.partial(_outproj_ln_kernel, n_a=n_a, alpha=alpha),
        grid=(t // tm,),
        in_specs=in_specs,
        out_specs=[pl.BlockSpec((tm, d), lambda i: (i, 0))] * 2,
        out_shape=[jax.ShapeDtypeStruct((t, d), F32)] * 2,
        compiler_params=_params(("parallel",)),
        name="outproj_ln",
    )(*a_list, *w_list, x, mod, ln_g, ln_b, mod, mod)


def _router_kernel(t_ref, rw_ref, rb_ref, idx_ref, gw_ref):
    tok = t_ref[...].astype(BF16)
    logits = lax.dot_general(rw_ref[...], tok, (((1,), (1,)), ((), ())), preferred_element_type=F32)
    scores = 1.0 / (1.0 + jnp.exp(-logits))
    sel = scores + rb_ref[...]
    gsz = N_EXPERTS // N_GROUPS
    neg = -jnp.inf
    sub = lax.broadcasted_iota(jnp.int32, (gsz, sel.shape[1]), 0)
    slabs = [sel[g * gsz:(g + 1) * gsz, :] for g in range(N_GROUPS)]
    gscore = []
    for s in slabs:
        m1 = jnp.max(s, axis=0, keepdims=True)
        a1 = jnp.min(jnp.where(s == m1, sub, gsz), axis=0, keepdims=True)
        m2 = jnp.max(jnp.where(sub == a1, neg, s), axis=0, keepdims=True)
        gscore.append(m1 + m2)
    masked = []
    for g in range(N_GROUPS):
        ahead = jnp.zeros(gscore[g].shape, jnp.int32)
        for h in range(N_GROUPS):
            if h == g:
                continue
            beats = gscore[h] >= gscore[g] if h < g else gscore[h] > gscore[g]
            ahead = ahead + beats.astype(jnp.int32)
        masked.append(jnp.where(ahead < TOPK_GROUPS, slabs[g], neg))
    cur = jnp.concatenate(masked, axis=0)
    eio = lax.broadcasted_iota(jnp.int32, cur.shape, 0)
    picks, weights = [], []
    for _ in range(TOP_K):
        m = jnp.max(cur, axis=0, keepdims=True)
        a = jnp.min(jnp.where(cur == m, eio, N_EXPERTS), axis=0, keepdims=True)
        hit = eio == a
        picks.append(a)
        weights.append(jnp.sum(jnp.where(hit, scores, 0.0), axis=0, keepdims=True))
        cur = jnp.where(hit, neg, cur)
    total = weights[0]
    for w in weights[1:]:
        total = total + w
    for k in range(TOP_K):
        idx_ref[k:k + 1, :] = picks[k]
        gw_ref[k:k + 1, :] = weights[k] / total * ROUTED_SCALE
    for k in range(TOP_K, SUBLANES):
        idx_ref[k:k + 1, :] = jnp.zeros_like(picks[0])
        gw_ref[k:k + 1, :] = jnp.zeros_like(weights[0])


def _router(tok, rw_t, rb, *, t, tt):
    d = tok.shape[1]
    return pl.pallas_call(
        _router_kernel,
        grid=(t // tt,),
        in_specs=[
            pl.BlockSpec((tt, d), lambda i: (i, 0)),
            _const_spec((N_EXPERTS, d)),
            _const_spec((N_EXPERTS, 1)),
        ],
        out_specs=[pl.BlockSpec((SUBLANES, tt), lambda i: (0, i))] * 2,
        out_shape=[jax.ShapeDtypeStruct((SUBLANES, t), jnp.int32), jax.ShapeDtypeStruct((SUBLANES, t), F32)],
        compiler_params=_params(("parallel",)),
        name="router",
    )(tok, rw_t, rb)


def _experts_kernel(be_ref, nbu_ref, nv_ref, tokc_ref, tokn_ref, dst_ref, sw_ref, wg_ref, wu_ref, wd_ref, x_hbm,
                    y_hbm, xbuf, ybuf, wgb, wub, wdb, gsem, ssem, *, n_blocks):
    b = pl.program_id(0)
    nbu = nbu_ref[0]
    slot = b % 2

    def gather_copy(tok_row, j, s):
        return pltpu.make_async_copy(x_hbm.at[pl.ds(tok_row, 1)], xbuf.at[s, pl.ds(j, 1)], gsem.at[s])

    def scatter_copy(dst_row, j, s):
        return pltpu.make_async_copy(ybuf.at[s, pl.ds(j, 1)], y_hbm.at[pl.ds(dst_row, 1)], ssem.at[s])

    def issue_gather(idx_ref, s):
        def body(j, c):
            gather_copy(idx_ref[0, j], j, s).start()
            return c
        lax.fori_loop(0, MOE_ROWS, body, 0, unroll=8)

    def wait_gather(s):
        def body(j, c):
            gather_copy(0, j, s).wait()
            return c
        lax.fori_loop(0, MOE_ROWS, body, 0, unroll=8)

    def issue_scatter(s, n_valid):
        def body(j, c):
            scatter_copy(dst_ref[0, j], j, s).start()
            return c
        lax.fori_loop(0, n_valid, body, 0)

    def wait_scatter(s, n_valid):
        def body(j, c):
            scatter_copy(0, j, s).wait()
            return c
        lax.fori_loop(0, n_valid, body, 0)

    @pl.when(b == 0)
    def _():
        issue_gather(tokc_ref, 0)

    @pl.when(b + 1 < nbu)
    def _():
        issue_gather(tokn_ref, 1 - slot)

    @pl.when(b < nbu)
    def _():
        changed = jnp.logical_or(b == 0, be_ref[b] != be_ref[jnp.maximum(b - 1, 0)])

        @pl.when(changed)
        def _():
            wgb[...] = wg_ref[...].astype(BF16)
            wub[...] = wu_ref[...].astype(BF16)
            wdb[...] = wd_ref[...].astype(BF16)

        wait_gather(slot)
        x = xbuf[slot].astype(BF16)
        hg = jnp.dot(x, wgb[...], preferred_element_type=F32)
        hu = jnp.dot(x, wub[...], preferred_element_type=F32)
        h = hg * (1.0 / (1.0 + jnp.exp(-hg))) * hu
        y = jnp.dot(h.astype(BF16), wdb[...], preferred_element_type=F32) * sw_ref[...]

        @pl.when(b >= 2)
        def _():
            wait_scatter(slot, nv_ref[jnp.maximum(b - 2, 0)])

        ybuf[slot] = y
        issue_scatter(slot, nv_ref[b])

    @pl.when(b == n_blocks - 1)
    def _():
        for back in (1, 2):
            @pl.when(nbu >= back)
            def _():
                wait_scatter((nbu - back) % 2, nv_ref[jnp.maximum(nbu - back, 0)])


def _experts(tok, block_e, nb_used, n_valid, slot_tok, slot_dst, slot_w, wg, wu, wd, *, n_blocks, y_rows):
    d = tok.shape[1]
    ff = wg.shape[2]
    idx_spec_cur = pl.BlockSpec((None, 1, MOE_ROWS), lambda b, be, nbu, nv: (b, 0, 0), memory_space=pltpu.SMEM)
    idx_spec_next = pl.BlockSpec((None, 1, MOE_ROWS), lambda b, be, nbu, nv: (jnp.minimum(b + 1, n_blocks - 1), 0, 0),
                                 memory_space=pltpu.SMEM)
    grid_spec = pltpu.PrefetchScalarGridSpec(
        num_scalar_prefetch=3,
        grid=(n_blocks,),
        in_specs=[
            idx_spec_cur, idx_spec_next, idx_spec_cur,
            pl.BlockSpec((None, MOE_ROWS, 1), lambda b, be, nbu, nv: (b, 0, 0)),
            pl.BlockSpec((None, d, ff), lambda b, be, nbu, nv: (be[b], 0, 0)),
            pl.BlockSpec((None, d, ff), lambda b, be, nbu, nv: (be[b], 0, 0)),
            pl.BlockSpec((None, ff, d), lambda b, be, nbu, nv: (be[b], 0, 0)),
            pl.BlockSpec(memory_space=pl.ANY),
        ],
        out_specs=pl.BlockSpec(memory_space=pl.ANY),
        scratch_shapes=[
            pltpu.VMEM((2, MOE_ROWS, d), F32),
            pltpu.VMEM((2, MOE_ROWS, d), F32),
            pltpu.VMEM((d, ff), BF16),
            pltpu.VMEM((d, ff), BF16),
            pltpu.VMEM((ff, d), BF16),
            pltpu.SemaphoreType.DMA((2,)),
            pltpu.SemaphoreType.DMA((2,)),
        ],
    )
    return pl.pallas_call(
        functools.partial(_experts_kernel, n_blocks=n_blocks),
        grid_spec=grid_spec,
        out_shape=jax.ShapeDtypeStruct((y_rows, d), F32),
        compiler_params=_params(("arbitrary",)),
        name="experts",
    )(block_e, nb_used, n_valid, slot_tok.reshape(n_blocks, 1, MOE_ROWS), slot_tok.reshape(n_blocks, 1, MOE_ROWS),
      slot_dst.reshape(n_blocks, 1, MOE_ROWS), slot_w.reshape(n_blocks, MOE_ROWS, 1), wg, wu, wd, tok)


def _combine_ln_kernel(*refs, alpha, emit_next):
    y_ref, tok_ref, x_ref, sg_ref, su_ref, sd_ref, gate_ref, lng_ref, lnb_ref = refs[:9]
    if emit_next:
        sc_ref, sh_ref, xo_ref, u_ref = refs[9:]
    else:
        (xo_ref,) = refs[9:]
    t = tok_ref[...].astype(BF16)
    hg = jnp.dot(t, sg_ref[...], preferred_element_type=F32)
    hu = jnp.dot(t, su_ref[...], preferred_element_type=F32)
    h = hg * (1.0 / (1.0 + jnp.exp(-hg))) * hu
    f = y_ref[0]
    for k in range(1, TOP_K):
        f = f + y_ref[k]
    f = f + jnp.dot(h.astype(BF16), sd_ref[...], preferred_element_type=F32)
    xn = _layer_norm(alpha * x_ref[...] + gate_ref[...] * f, lng_ref[...], lnb_ref[...])
    xo_ref[...] = xn
    if emit_next:
        u_ref[...] = (xn * (1.0 + sc_ref[...]) + sh_ref[...]).astype(u_ref.dtype)


def _combine_ln(y3, tok, x, sg, su, sd, mod, ln_g, ln_b, mod_next, *, t, tm, alpha, mod_row):
    d = x.shape[1]
    emit_next = mod_next is not None
    in_specs = [
        pl.BlockSpec((TOP_K, tm, d), lambda i: (0, i, 0)),
        pl.BlockSpec((tm, d), lambda i: (i, 0)),
        pl.BlockSpec((tm, d), lambda i: (i, 0)),
        _const_spec(sg.shape), _const_spec(su.shape), _const_spec(sd.shape),
        _mod_spec(d, 5, mod_row, tm),
        _const_spec((1, d)), _const_spec((1, d)),
    ]
    args = [y3, tok, x, sg, su, sd, mod, ln_g, ln_b]
    out_specs = [pl.BlockSpec((tm, d), lambda i: (i, 0))]
    out_shape = [jax.ShapeDtypeStruct((t, d), F32)]
    if emit_next:
        in_specs += [_mod_spec(d, 1, mod_row, tm), _mod_spec(d, 0, mod_row, tm)]
        args += [mod_next, mod_next]
        out_specs.append(pl.BlockSpec((tm, d), lambda i: (i, 0)))
        out_shape.append(jax.ShapeDtypeStruct((t, d), BF16))
    return pl.pallas_call(
        functools.partial(_combine_ln_kernel, alpha=alpha, emit_next=emit_next),
        grid=(t // tm,),
        in_specs=in_specs,
        out_specs=out_specs,
        out_shape=out_shape,
        compiler_params=_params(("parallel",)),
        name="combine_ln",
    )(*args)


def _moe(tok, x, t, router_w, router_b, wg, wu, wd, sg, su, sd, mod, ln_g, ln_b, mod_next, *, alpha, mod_row,
         tm):
    idx, gw = _router(tok, router_w.T.astype(BF16), router_b.reshape(N_EXPERTS, 1), t=t, tt=512)
    n_asg = t * TOP_K
    flat_e = idx[:TOP_K].reshape(-1)
    flat_w = gw[:TOP_K].reshape(-1)
    order = jnp.argsort(flat_e).astype(jnp.int32)
    se = flat_e[order]
    experts = jnp.arange(N_EXPERTS, dtype=jnp.int32)
    ustart = jnp.searchsorted(se, experts, side="left").astype(jnp.int32)
    uend = jnp.searchsorted(se, experts, side="right").astype(jnp.int32)
    counts = uend - ustart
    padded = (counts + MOE_ROWS - 1) // MOE_ROWS * MOE_ROWS
    pend = jnp.cumsum(padded)
    pstart = pend - padded
    n_blocks = (n_asg + N_EXPERTS * (MOE_ROWS - 1)) // MOE_ROWS + 1
    cap = n_blocks * MOE_ROWS
    block_start = jnp.arange(n_blocks, dtype=jnp.int32) * MOE_ROWS
    block_e = jnp.minimum(jnp.searchsorted(pend, block_start, side="right"), N_EXPERTS - 1).astype(jnp.int32)
    nb_used = (pend[-1] // MOE_ROWS).astype(jnp.int32).reshape(1)
    slot = jnp.arange(cap, dtype=jnp.int32)
    slot_e = jnp.repeat(block_e, MOE_ROWS)
    rank = slot - pstart[slot_e]
    valid = rank < counts[slot_e]
    slot_asg = order[jnp.clip(ustart[slot_e] + rank, 0, n_asg - 1)]
    slot_tok = jnp.where(valid, slot_asg % t, 0)
    slot_dst = jnp.where(valid, slot_asg, 0)
    n_valid = jnp.clip(counts[block_e] - (block_start - pstart[block_e]), 0, MOE_ROWS).astype(jnp.int32)
    slot_w = jnp.where(valid, flat_w[slot_asg], 0.0)
    y = _experts(tok, block_e, nb_used, n_valid, slot_tok, slot_dst, slot_w, wg, wu, wd, n_blocks=n_blocks,
                 y_rows=n_asg)
    y3 = y.reshape(TOP_K, t, tok.shape[1])
    return _combine_ln(y3, tok, x, sg, su, sd, mod, ln_g, ln_b, mod_next, t=t, tm=tm, alpha=alpha,
                       mod_row=mod_row)


def _rope64(r, c_ref, sa_ref, sb_ref):
    return r * c_ref[...] + pltpu.roll(r, LANES - QK_ROPE // 2, 1) * sa_ref[...] + pltpu.roll(r, QK_ROPE // 2, 1) * sb_ref[...]


def _mla_q_kernel(d_ref, gain_ref, w_ref, c_ref, sa_ref, sb_ref, q_ref, *, scale):
    n = _rms(d_ref[...], gain_ref[...]).astype(BF16)
    q = jnp.dot(n, w_ref[...], preferred_element_type=F32)
    for h in range(MLA_HEADS):
        lo = h * MLA_DK_PAD
        q_ref[:, lo:lo + QK_NOPE] = (q[:, lo:lo + QK_NOPE] * scale).astype(q_ref.dtype)
        r = _rope64(q[:, lo + QK_NOPE:lo + MLA_DK_PAD], c_ref, sa_ref, sb_ref)
        q_ref[:, lo + QK_NOPE:lo + MLA_DK_PAD] = (r * scale).astype(q_ref.dtype)


def _mla_kv_kernel(ckv_ref, kr_ref, gain_ref, wk_ref, wv_ref, c_ref, sa_ref, sb_ref, k_ref, v_ref):
    n = _rms(ckv_ref[...], gain_ref[...]).astype(BF16)
    kn = jnp.dot(n, wk_ref[...], preferred_element_type=F32)
    v_ref[...] = jnp.dot(n, wv_ref[...], preferred_element_type=F32).astype(v_ref.dtype)
    kr = _rope64(kr_ref[...], c_ref, sa_ref, sb_ref).astype(k_ref.dtype)
    for h in range(MLA_HEADS):
        lo = h * MLA_DK_PAD
        k_ref[:, lo:lo + QK_NOPE] = kn[:, h * QK_NOPE:(h + 1) * QK_NOPE].astype(k_ref.dtype)
        k_ref[:, lo + QK_NOPE:lo + MLA_DK_PAD] = kr


def _axial_angles(n_tok, rot_dim):
    rows = n_tok // GRID_W
    n_freq = rot_dim // 4
    inv = ROPE_THETA ** (-jnp.arange(n_freq, dtype=F32) / n_freq)
    row = jnp.repeat(jnp.arange(rows, dtype=F32), GRID_W)
    col = jnp.tile(jnp.arange(GRID_W, dtype=F32), rows)
    return jnp.concatenate([row[:, None] * inv, col[:, None] * inv], axis=-1)


def _rope_tables_128(n_tok, ident_rows):
    ang = _axial_angles(n_tok, HEAD_DIM)
    cos, sin = jnp.cos(ang), jnp.sin(ang)
    c = jnp.concatenate([cos, cos], axis=-1)
    s = jnp.concatenate([-sin, sin], axis=-1)
    c = jnp.concatenate([c, jnp.ones((ident_rows, HEAD_DIM), F32)], axis=0)
    s = jnp.concatenate([s, jnp.zeros((ident_rows, HEAD_DIM), F32)], axis=0)
    return c, s


def _rope_tables_64(n_tok, ident_rows):
    ang = _axial_angles(n_tok, QK_ROPE)
    cos, sin = jnp.cos(ang), jnp.sin(ang)
    half = QK_ROPE // 2
    z = jnp.zeros((n_tok, LANES - QK_ROPE), F32)
    zh = jnp.zeros((n_tok, half), F32)
    c = jnp.concatenate([cos, cos, z], axis=-1)
    sa = jnp.concatenate([-sin, zh, z], axis=-1)
    sb = jnp.concatenate([zh, sin, z], axis=-1)
    ci = jnp.concatenate([jnp.ones((ident_rows, QK_ROPE), F32), jnp.zeros((ident_rows, LANES - QK_ROPE), F32)], axis=-1)
    zi = jnp.zeros((ident_rows, LANES), F32)
    return jnp.concatenate([c, ci], 0), jnp.concatenate([sa, zi], 0), jnp.concatenate([sb, zi], 0)


def kernel(x, c, ctx, c_ctx, w_ada, b_ada, ln_g, ln_b, a_w_in, a_conv_w, a_q_gain, a_k_gain, a_w_out, m_w_down, m_q_gain, m_kv_gain, m_w_uq, m_w_ukv, m_w_out, router_w, router_b, e_w_gate, e_w_up, e_w_down, s_w_gate, s_w_up, s_w_down):
    batch, seq, d = x.shape
    ctx_len = ctx.shape[1]
    depth = w_ada.shape[0]
    assert depth == 2, "one conv+GQA layer followed by one MLA layer"
    alpha = (2 * depth) ** 0.25
    t_lat = batch * seq
    t_ctx = batch * ctx_len
    t_all = t_lat + t_ctx
    tr = 256
    assert seq % tr == 0 and ctx_len % tr == 0 and seq % GRID_W == 0
    lat_tiles = t_lat // tr
    lat_seq_tiles = seq // tr
    ctx_seq_tiles = ctx_len // tr
    lk = ctx_len + seq

    def mod_row(r):
        return jnp.minimum(r // seq, batch)

    def kv_block(i):
        is_lat = i < lat_tiles
        cidx = i - lat_tiles
        b = jnp.where(is_lat, i // lat_seq_tiles, cidx // ctx_seq_tiles)
        rb = jnp.where(is_lat, ctx_seq_tiles + i % lat_seq_tiles, cidx % ctx_seq_tiles)
        return b, rb

    def pos_block(i):
        return jnp.where(i < lat_tiles, i % lat_seq_tiles, lat_seq_tiles)

    rows = -(-(batch + 1) // SUBLANES) * SUBLANES
    cond = jnp.concatenate([c, c_ctx[None, :], jnp.zeros((rows - batch - 1, d), F32)], axis=0)
    mod = _ada_table(cond, w_ada, b_ada).reshape(depth, rows, 1, 6 * d)

    x_all = jnp.concatenate([x.reshape(t_lat, d), ctx.reshape(t_ctx, d)], axis=0)

    u0 = _modulate(x_all, mod[0], mod_row, tr)
    proj = _matmul(u0, a_w_in[0].astype(BF16), BF16, 512, 512)

    cos128, sin128 = _rope_tables_128(seq, tr)
    d_q = ATT_HEADS * HEAD_DIM
    d_kv = ATT_KV_HEADS * HEAD_DIM
    qkv_w = d_q + 2 * d_kv
    qkv_blk = 3 * CONV_DIM // qkv_w
    assert qkv_blk * qkv_w == 3 * CONV_DIM
    q0, k0, v0 = pl.pallas_call(
        functools.partial(_qkprep_kernel, scale=1.0 / math.sqrt(HEAD_DIM)),
        grid=(t_all // tr,),
        in_specs=[
            pl.BlockSpec((tr, qkv_w), lambda i: (i, qkv_blk)),
            pl.BlockSpec((tr, HEAD_DIM), lambda i: (pos_block(i), 0)),
            pl.BlockSpec((tr, HEAD_DIM), lambda i: (pos_block(i), 0)),
            _const_spec((1, HEAD_DIM)), _const_spec((1, HEAD_DIM)),
        ],
        out_specs=[
            pl.BlockSpec((tr, d_q), lambda i: (i, 0)),
            pl.BlockSpec((None, tr, d_kv), lambda i: (*kv_block(i), 0)),
            pl.BlockSpec((None, tr, d_kv), lambda i: (*kv_block(i), 0)),
        ],
        out_shape=[
            jax.ShapeDtypeStruct((t_all, d_q), BF16),
            jax.ShapeDtypeStruct((batch, lk, d_kv), BF16),
            jax.ShapeDtypeStruct((batch, lk, d_kv), BF16),
        ],
        compiler_params=_params(("parallel",)),
        name="qk_prep",
    )(proj, cos128, sin128, a_q_gain[0].reshape(1, HEAD_DIM), a_k_gain[0].reshape(1, HEAD_DIM))

    grp = ATT_HEADS // ATT_KV_HEADS
    att_lat = _attention(q0, k0, v0, batch=batch, sq=seq, lk=lk, n_kv=ATT_KV_HEADS, group=grp, dk=HEAD_DIM,
                         dv=HEAD_DIM, tq=256, q_row_off=0)
    att_ctx = _attention(q0, k0, v0, batch=batch, sq=ctx_len, lk=ctx_len, n_kv=ATT_KV_HEADS, group=grp,
                         dk=HEAD_DIM, dv=HEAD_DIM, tq=256, q_row_off=t_lat)
    att0 = jnp.concatenate([att_lat, att_ctx], axis=0)

    conv0 = _conv_gate(proj, a_conv_w[0], t=t_all, tm=tr, tc=512, lat_tiles=lat_tiles,
                       lat_seq_tiles=lat_seq_tiles, ctx_seq_tiles=ctx_seq_tiles)

    w_out0 = a_w_out[0].astype(BF16)
    x1, tok0 = _outproj_ln([conv0, att0], [w_out0[:CONV_DIM], w_out0[CONV_DIM:]], x_all, mod[0],
                           ln_g[0, 0].reshape(1, d), ln_b[0, 0].reshape(1, d), t=t_all, tm=tr, alpha=alpha,
                           mod_row=mod_row)

    x2, u1 = _moe(tok0, x1, t_all, router_w[0], router_b[0], e_w_gate[0], e_w_up[0], e_w_down[0],
                  s_w_gate[0].astype(BF16), s_w_up[0].astype(BF16), s_w_down[0].astype(BF16), mod[0],
                  ln_g[0, 1].reshape(1, d), ln_b[0, 1].reshape(1, d), mod[1], alpha=alpha,
                  mod_row=mod_row, tm=128)

    n_down = Q_LORA + KV_LORA + QK_ROPE
    n_down_pad = -(-n_down // LANES) * LANES
    w_down = jnp.pad(m_w_down[0], ((0, 0), (0, n_down_pad - n_down))).astype(BF16)
    down = _matmul(u1, w_down, F32, 512, n_down_pad)

    dqk = QK_NOPE + QK_ROPE
    w_uq = m_w_uq[0].reshape(Q_LORA, MLA_HEADS, dqk)
    w_uq = jnp.pad(w_uq, ((0, 0), (0, 0), (0, MLA_DK_PAD - dqk))).reshape(Q_LORA, MLA_HEADS * MLA_DK_PAD).astype(BF16)
    w_ukv = m_w_ukv[0].reshape(KV_LORA, MLA_HEADS, QK_NOPE + V_DIM)
    w_uk = w_ukv[:, :, :QK_NOPE].reshape(KV_LORA, MLA_HEADS * QK_NOPE).astype(BF16)
    w_uv = w_ukv[:, :, QK_NOPE:].reshape(KV_LORA, MLA_HEADS * V_DIM).astype(BF16)

    c64, sa64, sb64 = _rope_tables_64(seq, tr)
    rope_specs = [pl.BlockSpec((tr, LANES), lambda i: (pos_block(i), 0))] * 3
    q1 = pl.pallas_call(
        functools.partial(_mla_q_kernel, scale=1.0 / math.sqrt(dqk)),
        grid=(lat_tiles,),
        in_specs=[
            pl.BlockSpec((tr, Q_LORA), lambda i: (i, 0)),
            _const_spec((1, Q_LORA)),
            _const_spec(w_uq.shape),
        ] + rope_specs,
        out_specs=pl.BlockSpec((tr, MLA_HEADS * MLA_DK_PAD), lambda i: (i, 0)),
        out_shape=jax.ShapeDtypeStruct((t_lat, MLA_HEADS * MLA_DK_PAD), BF16),
        compiler_params=_params(("parallel",)),
        name="mla_q",
    )(down, m_q_gain[0].reshape(1, Q_LORA), w_uq, c64, sa64, sb64)

    assert KV_LORA == Q_LORA and (Q_LORA + KV_LORA) % LANES == 0
    k1, v1 = pl.pallas_call(
        _mla_kv_kernel,
        grid=(t_all // tr,),
        in_specs=[
            pl.BlockSpec((tr, KV_LORA), lambda i: (i, 1)),
            pl.BlockSpec((tr, LANES), lambda i: (i, (Q_LORA + KV_LORA) // LANES)),
            _const_spec((1, KV_LORA)),
            _const_spec(w_uk.shape), _const_spec(w_uv.shape),
        ] + rope_specs,
        out_specs=[
            pl.BlockSpec((None, tr, MLA_HEADS * MLA_DK_PAD), lambda i: (*kv_block(i), 0)),
            pl.BlockSpec((None, tr, MLA_HEADS * V_DIM), lambda i: (*kv_block(i), 0)),
        ],
        out_shape=[
            jax.ShapeDtypeStruct((batch, lk, MLA_HEADS * MLA_DK_PAD), BF16),
            jax.ShapeDtypeStruct((batch, lk, MLA_HEADS * V_DIM), BF16),
        ],
        compiler_params=_params(("parallel",)),
        name="mla_kv",
    )(down, down, m_kv_gain[0].reshape(1, KV_LORA), w_uk, w_uv, c64, sa64, sb64)

    att1 = _attention(q1, k1, v1, batch=batch, sq=seq, lk=lk, n_kv=MLA_HEADS, group=1, dk=MLA_DK_PAD, dv=V_DIM,
                      tq=512, q_row_off=0)

    x3, tok1 = _outproj_ln([att1], [m_w_out[0].astype(BF16)], x2, mod[1], ln_g[1, 0].reshape(1, d),
                           ln_b[1, 0].reshape(1, d), t=t_lat, tm=tr, alpha=alpha, mod_row=mod_row)

    (x4,) = _moe(tok1, x3, t_lat, router_w[1], router_b[1], e_w_gate[1], e_w_up[1], e_w_down[1],
                 s_w_gate[1].astype(BF16), s_w_up[1].astype(BF16), s_w_down[1].astype(BF16), mod[1],
                 ln_g[1, 1].reshape(1, d), ln_b[1, 1].reshape(1, d), None, alpha=alpha,
                 mod_row=mod_row, tm=128)
    return x4.reshape(batch, seq, d)
```

```python
import functools
import math

import jax
import jax.numpy as jnp
from jax import lax
from jax.experimental import pallas as pl
from jax.experimental.pallas import tpu as pltpu
from jax.experimental.pallas import tpu_sc as plsc

F32 = jnp.float32
BF16 = jnp.bfloat16

GRID_W = 64
CONV_DIM = 1024
ATT_HEADS = 8
ATT_KV_HEADS = 2
HEAD_DIM = 128
MLA_HEADS = 16
Q_LORA = 512
KV_LORA = 512
QK_NOPE = 128
QK_ROPE = 64
V_DIM = 128
N_EXPERTS = 64
TOP_K = 6
N_GROUPS = 8
TOPK_GROUPS = 4
ROUTED_SCALE = 2.5
ROPE_THETA = 10000.0
LN_EPS = 1e-5
RMS_EPS = 1e-6

V7X_VMEM_LIMIT_BYTES = 56 * 1024 * 1024
LANES = 128
SUBLANES = 8
MOE_ROWS = 256
V7X_SC_CORES = 2
V7X_SC_SUBCORES = 16
SC_GATHER_ROWS = 16
MLA_DK_PAD = 256


def _params(sem):
    return pltpu.CompilerParams(dimension_semantics=sem, vmem_limit_bytes=V7X_VMEM_LIMIT_BYTES)


def _const_spec(shape):
    nd = len(shape)
    return pl.BlockSpec(shape, lambda *_: (0,) * nd)


def _ada_kernel(s_ref, w_ref, b_ref, o_ref):
    s = s_ref[...]
    s = s * (1.0 / (1.0 + jnp.exp(-s)))
    o_ref[...] = jnp.dot(s.astype(BF16), w_ref[...].astype(BF16), preferred_element_type=F32) + b_ref[...]


def _ada_table(cond, w_ada, b_ada):
    depth, d, n = w_ada.shape
    r = cond.shape[0]
    tn = 1024
    return pl.pallas_call(
        _ada_kernel,
        grid=(depth, n // tn),
        in_specs=[
            pl.BlockSpec((r, d), lambda l, j: (0, 0)),
            pl.BlockSpec((None, d, tn), lambda l, j: (l, 0, j)),
            pl.BlockSpec((None, 1, tn), lambda l, j: (l, 0, j)),
        ],
        out_specs=pl.BlockSpec((None, r, tn), lambda l, j: (l, 0, j)),
        out_shape=jax.ShapeDtypeStruct((depth, r, n), F32),
        compiler_params=_params(("parallel", "parallel")),
        name="ada_table",
    )(cond, w_ada, b_ada.reshape(depth, 1, n))


def _mod_spec(d, chunk, mod_row, tm):
    return pl.BlockSpec((None, 1, d), lambda i: (mod_row(i * tm), 0, chunk))


def _modulate_kernel(x_ref, sc_ref, sh_ref, o_ref):
    o_ref[...] = (x_ref[...] * (1.0 + sc_ref[...]) + sh_ref[...]).astype(o_ref.dtype)


def _modulate(x, mod, mod_row, tm):
    t, d = x.shape
    return pl.pallas_call(
        _modulate_kernel,
        grid=(t // tm,),
        in_specs=[
            pl.BlockSpec((tm, d), lambda i: (i, 0)),
            _mod_spec(d, 1, mod_row, tm),
            _mod_spec(d, 0, mod_row, tm),
        ],
        out_specs=pl.BlockSpec((tm, d), lambda i: (i, 0)),
        out_shape=jax.ShapeDtypeStruct((t, d), BF16),
        compiler_params=_params(("parallel",)),
        name="modulate",
    )(x, mod, mod)


def _mm_kernel(a_ref, w_ref, o_ref):
    o_ref[...] = jnp.dot(a_ref[...], w_ref[...], preferred_element_type=F32).astype(o_ref.dtype)


def _matmul(a, w, out_dtype, tm, tn):
    m, k = a.shape
    n = w.shape[1]
    return pl.pallas_call(
        _mm_kernel,
        grid=(m // tm, n // tn),
        in_specs=[
            pl.BlockSpec((tm, k), lambda i, j: (i, 0)),
            pl.BlockSpec((k, tn), lambda i, j: (0, j)),
        ],
        out_specs=pl.BlockSpec((tm, tn), lambda i, j: (i, j)),
        out_shape=jax.ShapeDtypeStruct((m, n), out_dtype),
        compiler_params=_params(("parallel", "parallel")),
        name="matmul",
    )(a, w)


def _rms(t, gain):
    return t * lax.rsqrt(jnp.mean(t * t, axis=-1, keepdims=True) + RMS_EPS) * gain


def _qkprep_kernel(p_ref, cos_ref, sin_ref, qg_ref, kg_ref, q_ref, k_ref, v_ref, *, scale):
    cos = cos_ref[...]
    sin = sin_ref[...]

    def norm_rope(t, gain):
        y = _rms(t.astype(F32), gain)
        return y * cos + pltpu.roll(y, HEAD_DIM // 2, 1) * sin

    for h in range(ATT_HEADS):
        sl = slice(h * HEAD_DIM, (h + 1) * HEAD_DIM)
        q_ref[:, sl] = (norm_rope(p_ref[:, sl], qg_ref[...]) * scale).astype(q_ref.dtype)
    k0 = ATT_HEADS * HEAD_DIM
    for h in range(ATT_KV_HEADS):
        sl = slice(h * HEAD_DIM, (h + 1) * HEAD_DIM)
        k_ref[:, sl] = norm_rope(p_ref[:, k0 + h * HEAD_DIM:k0 + (h + 1) * HEAD_DIM], kg_ref[...]).astype(k_ref.dtype)
    v0 = k0 + ATT_KV_HEADS * HEAD_DIM
    v_ref[...] = p_ref[:, v0:v0 + ATT_KV_HEADS * HEAD_DIM].astype(v_ref.dtype)


def _attn_kernel(q_ref, k_ref, v_ref, o_ref, *, group, dk, dv):
    k = k_ref[...]
    v = v_ref[...]
    for h in range(group):
        q = q_ref[:, h * dk:(h + 1) * dk]
        s = lax.dot_general(q, k, (((1,), (1,)), ((), ())), preferred_element_type=F32)
        m = jnp.max(s, axis=-1, keepdims=True)
        p = jnp.exp(s - m)
        l = jnp.sum(p, axis=-1, keepdims=True)
        o = jnp.dot(p.astype(v.dtype), v, preferred_element_type=F32)
        o_ref[:, h * dv:(h + 1) * dv] = (o / l).astype(o_ref.dtype)


def _attention(q, k, v, *, batch, sq, lk, n_kv, group, dk, dv, tq, q_row_off):
    nq = sq // tq
    off = q_row_off // tq
    return pl.pallas_call(
        functools.partial(_attn_kernel, group=group, dk=dk, dv=dv),
        grid=(batch, n_kv, nq),
        in_specs=[
            pl.BlockSpec((tq, group * dk), lambda b, g, i: (off + b * nq + i, g)),
            pl.BlockSpec((None, lk, dk), lambda b, g, i: (b, 0, g)),
            pl.BlockSpec((None, lk, dv), lambda b, g, i: (b, 0, g)),
        ],
        out_specs=pl.BlockSpec((tq, group * dv), lambda b, g, i: (b * nq + i, g)),
        out_shape=jax.ShapeDtypeStruct((batch * sq, n_kv * group * dv), BF16),
        compiler_params=_params(("parallel", "parallel", "parallel")),
        name="attention",
    )(q, k, v)


def _conv_kernel(gb_ref, gc_ref, hv_ref, gcp_ref, hvp_ref, gcn_ref, hvn_ref, w_ref, o_ref, *,
                 tm, lat_tiles, lat_seq_tiles, ctx_seq_tiles):
    i = pl.program_id(0)
    is_lat = i < lat_tiles
    pos = jnp.where(is_lat, i % lat_seq_tiles, (i - lat_tiles) % ctx_seq_tiles)
    seq_tiles = jnp.where(is_lat, lat_seq_tiles, ctx_seq_tiles)
    not_first = (pos != 0).astype(F32)
    not_last = (pos != seq_tiles - 1).astype(F32)
    p = gc_ref[...].astype(F32) * hv_ref[...].astype(F32)
    halo_prev = gcp_ref[SUBLANES - 1:SUBLANES, :].astype(F32) * hvp_ref[SUBLANES - 1:SUBLANES, :].astype(F32) * not_first
    halo_next = gcn_ref[0:1, :].astype(F32) * hvn_ref[0:1, :].astype(F32) * not_last
    row = lax.broadcasted_iota(jnp.int32, p.shape, 0)
    prev = jnp.where(row == 0, halo_prev, pltpu.roll(p, 1, 0))
    nxt = jnp.where(row == tm - 1, halo_next, pltpu.roll(p, tm - 1, 0))
    w = w_ref[...]
    conv = w[0:1, :] * prev + w[1:2, :] * p + w[2:3, :] * nxt
    o_ref[...] = (gb_ref[...].astype(F32) * conv).astype(o_ref.dtype)


def _conv_gate(p, conv_w, *, t, tm, tc, lat_tiles, lat_seq_tiles, ctx_seq_tiles):
    nct = CONV_DIM // tc
    hb = tm // SUBLANES
    n_halo = t // SUBLANES

    def cur(part):
        return pl.BlockSpec((tm, tc), lambda i, j: (i, part * nct + j))

    def prev(part):
        return pl.BlockSpec((SUBLANES, tc), lambda i, j: (jnp.maximum(i * hb - 1, 0), part * nct + j))

    def nxt(part):
        return pl.BlockSpec((SUBLANES, tc), lambda i, j: (jnp.minimum((i + 1) * hb, n_halo - 1), part * nct + j))

    return pl.pallas_call(
        functools.partial(_conv_kernel, tm=tm, lat_tiles=lat_tiles, lat_seq_tiles=lat_seq_tiles,
                          ctx_seq_tiles=ctx_seq_tiles),
        grid=(t // tm, nct),
        in_specs=[cur(0), cur(1), cur(2), prev(1), prev(2), nxt(1), nxt(2),
                  pl.BlockSpec((3, tc), lambda i, j: (0, j))],
        out_specs=pl.BlockSpec((tm, tc), lambda i, j: (i, j)),
        out_shape=jax.ShapeDtypeStruct((t, CONV_DIM), BF16),
        compiler_params=_params(("parallel", "parallel")),
        name="conv_gate",
    )(p, p, p, p, p, p, p, conv_w)


def _layer_norm(z, g, b):
    mu = jnp.mean(z, axis=-1, keepdims=True)
    zc = z - mu
    var = jnp.mean(zc * zc, axis=-1, keepdims=True)
    return zc * lax.rsqrt(var + LN_EPS) * g + b


def _outproj_ln_kernel(*refs, n_a, alpha):
    a_refs = refs[:n_a]
    w_refs = refs[n_a:2 * n_a]
    x_ref, gate_ref, lng_ref, lnb_ref, sc_ref, sh_ref, xo_ref, tok_ref = refs[2 * n_a:]
    y = jnp.dot(a_refs[0][...], w_refs[0][...], preferred_element_type=F32)
    for a_ref, w_ref in zip(a_refs[1:], w_refs[1:]):
        y = y + jnp.dot(a_ref[...], w_ref[...], preferred_element_type=F32)
    xn = _layer_norm(alpha * x_ref[...] + gate_ref[...] * y, lng_ref[...], lnb_ref[...])
    xo_ref[...] = xn
    tok_ref[...] = xn * (1.0 + sc_ref[...]) + sh_ref[...]


def _outproj_ln(a_list, w_list, x, mod, ln_g, ln_b, *, t, tm, alpha, mod_row):
    d = x.shape[1]
    n_a = len(a_list)
    in_specs = [pl.BlockSpec((tm, a.shape[1]), lambda i: (i, 0)) for a in a_list]
    in_specs += [_const_spec(w.shape) for w in w_list]
    in_specs += [
        pl.BlockSpec((tm, d), lambda i: (i, 0)),
        _mod_spec(d, 2, mod_row, tm),
        _const_spec((1, d)), _const_spec((1, d)),
        _mod_spec(d, 4, mod_row, tm),
        _mod_spec(d, 3, mod_row, tm),
    ]
    return pl.pallas_call(
        functools.partial(_outproj_ln_kernel, n_a=n_a, alpha=alpha),
        grid=(t // tm,),
        in_specs=in_specs,
        out_specs=[pl.BlockSpec((tm, d), lambda i: (i, 0))] * 2,
        out_shape=[jax.ShapeDtypeStruct((t, d), F32)] * 2,
        compiler_params=_params(("parallel",)),
        name="outproj_ln",
    )(*a_list, *w_list, x, mod, ln_g, ln_b, mod, mod)


def _router_kernel(t_ref, rw_ref, rb_ref, idx_ref, gw_ref):
    tok = t_ref[...].astype(BF16)
    logits = lax.dot_general(rw_ref[...], tok, (((1,), (1,)), ((), ())), preferred_element_type=F32)
    scores = 1.0 / (1.0 + jnp.exp(-logits))
    sel = scores + rb_ref[...]
    gsz = N_EXPERTS // N_GROUPS
    neg = -jnp.inf
    sub = lax.broadcasted_iota(jnp.int32, (gsz, sel.shape[1]), 0)
    slabs = [sel[g * gsz:(g + 1) * gsz, :] for g in range(N_GROUPS)]
    gscore = []
    for s in slabs:
        m1 = jnp.max(s, axis=0, keepdims=True)
        a1 = jnp.min(jnp.where(s == m1, sub, gsz), axis=0, keepdims=True)
        m2 = jnp.max(jnp.where(sub == a1, neg, s), axis=0, keepdims=True)
        gscore.append(m1 + m2)
    masked = []
    for g in range(N_GROUPS):
        ahead = jnp.zeros(gscore[g].shape, jnp.int32)
        for h in range(N_GROUPS):
            if h == g:
                continue
            beats = gscore[h] >= gscore[g] if h < g else gscore[h] > gscore[g]
            ahead = ahead + beats.astype(jnp.int32)
        masked.append(jnp.where(ahead < TOPK_GROUPS, slabs[g], neg))
    cur = jnp.concatenate(masked, axis=0)
    eio = lax.broadcasted_iota(jnp.int32, cur.shape, 0)
    picks, weights = [], []
    for _ in range(TOP_K):
        m = jnp.max(cur, axis=0, keepdims=True)
        a = jnp.min(jnp.where(cur == m, eio, N_EXPERTS), axis=0, keepdims=True)
        hit = eio == a
        picks.append(a)
        weights.append(jnp.sum(jnp.where(hit, scores, 0.0), axis=0, keepdims=True))
        cur = jnp.where(hit, neg, cur)
    total = weights[0]
    for w in weights[1:]:
        total = total + w
    for k in range(TOP_K):
        idx_ref[k:k + 1, :] = picks[k]
        gw_ref[k:k + 1, :] = weights[k] / total * ROUTED_SCALE
    for k in range(TOP_K, SUBLANES):
        idx_ref[k:k + 1, :] = jnp.zeros_like(picks[0])
        gw_ref[k:k + 1, :] = jnp.zeros_like(weights[0])


def _router(tok, rw_t, rb, *, t, tt):
    d = tok.shape[1]
    return pl.pallas_call(
        _router_kernel,
        grid=(t // tt,),
        in_specs=[
            pl.BlockSpec((tt, d), lambda i: (i, 0)),
            _const_spec((N_EXPERTS, d)),
            _const_spec((N_EXPERTS, 1)),
        ],
        out_specs=[pl.BlockSpec((SUBLANES, tt), lambda i: (0, i))] * 2,
        out_shape=[jax.ShapeDtypeStruct((SUBLANES, t), jnp.int32), jax.ShapeDtypeStruct((SUBLANES, t), F32)],
        compiler_params=_params(("parallel",)),
        name="router",
    )(tok, rw_t, rb)


def _sc_gather_rows(table, idx):
    n = idx.shape[0]
    d = table.shape[1]
    n_workers = V7X_SC_CORES * V7X_SC_SUBCORES
    per_w = n // n_workers
    n_chunks = per_w // SC_GATHER_ROWS
    assert per_w * n_workers == n and n_chunks * SC_GATHER_ROWS == per_w and n_chunks % 2 == 0
    mesh = plsc.VectorSubcoreMesh(core_axis_name="c", subcore_axis_name="s", num_cores=V7X_SC_CORES,
                                  num_subcores=V7X_SC_SUBCORES)

    @functools.partial(
        pl.kernel,
        out_type=jax.ShapeDtypeStruct((n, d), table.dtype),
        mesh=mesh,
        scratch_types=[
            pltpu.VMEM((per_w,), jnp.int32),
            pltpu.VMEM((2, SC_GATHER_ROWS, d), table.dtype),
            pltpu.SemaphoreType.DMA((2,)),
            pltpu.SemaphoreType.DMA((2,)),
        ],
        name="sc_gather_rows",
    )
    def gather(table_hbm, idx_hbm, out_hbm, idx_v, rows_v, gsem, wsem):
        wid = lax.axis_index("s") * V7X_SC_CORES + lax.axis_index("c")
        base = wid * per_w
        pltpu.sync_copy(idx_hbm.at[pl.ds(base, per_w)], idx_v)

        def gather_copy(c, b):
            return pltpu.make_async_copy(table_hbm.at[idx_v.at[pl.ds(c * SC_GATHER_ROWS, SC_GATHER_ROWS)]],
                                         rows_v.at[b], gsem.at[b])

        def write_copy(c, b):
            return pltpu.make_async_copy(rows_v.at[b], out_hbm.at[pl.ds(base + c * SC_GATHER_ROWS, SC_GATHER_ROWS)],
                                         wsem.at[b])

        gather_copy(0, 0).start()

        @pl.loop(0, n_chunks, step=2)
        def _(g):
            for b in range(2):
                c = g + b
                nb = 1 - b

                @pl.when(c >= 1)
                def _():
                    write_copy(c - 1, nb).wait()

                @pl.when(c + 1 < n_chunks)
                def _():
                    gather_copy(c + 1, nb).start()

                gather_copy(c, b).wait()
                write_copy(c, b).start()

        write_copy(n_chunks - 1, (n_chunks - 1) % 2).wait()

    return gather(table, idx)


def _experts_kernel(be_ref, nbu_ref, x_ref, sw_ref, wg_ref, wu_ref, wd_ref, y_ref, wgb, wub, wdb):
    b = pl.program_id(0)
    nbu = nbu_ref[0]

    @pl.when(b < nbu)
    def _():
        changed = jnp.logical_or(b == 0, be_ref[b] != be_ref[jnp.maximum(b - 1, 0)])

        @pl.when(changed)
        def _():
            wgb[...] = wg_ref[...].astype(BF16)
            wub[...] = wu_ref[...].astype(BF16)
            wdb[...] = wd_ref[...].astype(BF16)

        x = x_ref[...].astype(BF16)
        hg = jnp.dot(x, wgb[...], preferred_element_type=F32)
        hu = jnp.dot(x, wub[...], preferred_element_type=F32)
        h = hg * (1.0 / (1.0 + jnp.exp(-hg))) * hu
        y_ref[...] = jnp.dot(h.astype(BF16), wdb[...], preferred_element_type=F32) * sw_ref[...]

    @pl.when(b >= nbu)
    def _():
        y_ref[...] = jnp.zeros_like(y_ref)


def _experts(xs, block_e, nb_used, slot_w, wg, wu, wd, layer, *, n_blocks):
    d = xs.shape[1]
    ff = wg.shape[3]

    def used(b, nbu):
        return jnp.minimum(b, jnp.maximum(nbu[0] - 1, 0))

    grid_spec = pltpu.PrefetchScalarGridSpec(
        num_scalar_prefetch=2,
        grid=(n_blocks,),
        in_specs=[
            pl.BlockSpec((MOE_ROWS, d), lambda b, be, nbu: (used(b, nbu), 0)),
            pl.BlockSpec((None, MOE_ROWS, 1), lambda b, be, nbu: (used(b, nbu), 0, 0)),
            pl.BlockSpec((None, None, d, ff), lambda b, be, nbu: (layer, be[b], 0, 0)),
            pl.BlockSpec((None, None, d, ff), lambda b, be, nbu: (layer, be[b], 0, 0)),
            pl.BlockSpec((None, None, ff, d), lambda b, be, nbu: (layer, be[b], 0, 0)),
        ],
        out_specs=pl.BlockSpec((MOE_ROWS, d), lambda b, be, nbu: (b, 0)),
        scratch_shapes=[
            pltpu.VMEM((d, ff), BF16),
            pltpu.VMEM((d, ff), BF16),
            pltpu.VMEM((ff, d), BF16),
        ],
    )
    return pl.pallas_call(
        _experts_kernel,
        grid_spec=grid_spec,
        out_shape=jax.ShapeDtypeStruct((n_blocks * MOE_ROWS, d), F32),
        compiler_params=_params(("arbitrary",)),
        name="experts",
    )(block_e, nb_used, xs, slot_w.reshape(n_blocks, MOE_ROWS, 1), wg, wu, wd)


def _combine_ln_kernel(*refs, alpha, emit_next):
    y_ref, tok_ref, x_ref, sg_ref, su_ref, sd_ref, gate_ref, lng_ref, lnb_ref = refs[:9]
    if emit_next:
        sc_ref, sh_ref, xo_ref, u_ref = refs[9:]
    else:
        (xo_ref,) = refs[9:]
    t = tok_ref[...].astype(BF16)
    hg = jnp.dot(t, sg_ref[...], preferred_element_type=F32)
    hu = jnp.dot(t, su_ref[...], preferred_element_type=F32)
    h = hg * (1.0 / (1.0 + jnp.exp(-hg))) * hu
    f = y_ref[0]
    for k in range(1, TOP_K):
        f = f + y_ref[k]
    f = f + jnp.dot(h.astype(BF16), sd_ref[...], preferred_element_type=F32)
    xn = _layer_norm(alpha * x_ref[...] + gate_ref[...] * f, lng_ref[...], lnb_ref[...])
    xo_ref[...] = xn
    if emit_next:
        u_ref[...] = (xn * (1.0 + sc_ref[...]) + sh_ref[...]).astype(u_ref.dtype)


def _combine_ln(y3, tok, x, sg, su, sd, mod, ln_g, ln_b, mod_next, *, t, tm, alpha, mod_row):
    d = x.shape[1]
    emit_next = mod_next is not None
    in_specs = [
        pl.BlockSpec((TOP_K, tm, d), lambda i: (0, i, 0)),
        pl.BlockSpec((tm, d), lambda i: (i, 0)),
        pl.BlockSpec((tm, d), lambda i: (i, 0)),
        _const_spec(sg.shape), _const_spec(su.shape), _const_spec(sd.shape),
        _mod_spec(d, 5, mod_row, tm),
        _const_spec((1, d)), _const_spec((1, d)),
    ]
    args = [y3, tok, x, sg, su, sd, mod, ln_g, ln_b]
    out_specs = [pl.BlockSpec((tm, d), lambda i: (i, 0))]
    out_shape = [jax.ShapeDtypeStruct((t, d), F32)]
    if emit_next:
        in_specs += [_mod_spec(d, 1, mod_row, tm), _mod_spec(d, 0, mod_row, tm)]
        args += [mod_next, mod_next]
        out_specs.append(pl.BlockSpec((tm, d), lambda i: (i, 0)))
        out_shape.append(jax.ShapeDtypeStruct((t, d), BF16))
    return pl.pallas_call(
        functools.partial(_combine_ln_kernel, alpha=alpha, emit_next=emit_next),
        grid=(t // tm,),
        in_specs=in_specs,
        out_specs=out_specs,
        out_shape=out_shape,
        compiler_params=_params(("parallel",)),
        name="combine_ln",
    )(*args)


def _moe(tok, x, t, layer, router_w, router_b, wg, wu, wd, sg, su, sd, mod, ln_g, ln_b, mod_next, *, alpha, mod_row,
         tm):
    d = tok.shape[1]
    idx, gw = _router(tok, router_w.T.astype(BF16), router_b.reshape(N_EXPERTS, 1), t=t, tt=512)
    n_asg = t * TOP_K
    flat_e = idx[:TOP_K].reshape(-1)
    flat_w = gw[:TOP_K].reshape(-1)
    experts = jnp.arange(N_EXPERTS, dtype=jnp.int32)
    counts = jnp.sum((flat_e[:, None] == experts[None, :]).astype(jnp.int32), axis=0)
    ustart = jnp.cumsum(counts) - counts
    padded = (counts + MOE_ROWS - 1) // MOE_ROWS * MOE_ROWS
    pend = jnp.cumsum(padded)
    pstart = pend - padded
    sc_rows = V7X_SC_CORES * V7X_SC_SUBCORES * SC_GATHER_ROWS * 2
    blocks_granule = sc_rows // MOE_ROWS
    n_blocks = -(-((n_asg + N_EXPERTS * (MOE_ROWS - 1)) // MOE_ROWS + 1) // blocks_granule) * blocks_granule
    assert n_asg % sc_rows == 0
    cap = n_blocks * MOE_ROWS
    block_start = jnp.arange(n_blocks, dtype=jnp.int32) * MOE_ROWS
    block_e = jnp.minimum(jnp.sum((pend[None, :] <= block_start[:, None]).astype(jnp.int32), axis=1), N_EXPERTS - 1)
    nb_used = (pend[-1] // MOE_ROWS).astype(jnp.int32).reshape(1)
    order = jnp.argsort(flat_e).astype(jnp.int32)
    se = flat_e[order]
    dest_sorted = pstart[se] + jnp.arange(n_asg, dtype=jnp.int32) - ustart[se]
    pos = jnp.zeros((n_asg,), jnp.int32).at[order].set(dest_sorted, unique_indices=True)
    slot = jnp.arange(cap, dtype=jnp.int32)
    slot_e = jnp.repeat(block_e, MOE_ROWS)
    rank = slot - pstart[slot_e]
    valid = rank < counts[slot_e]
    slot_asg = order[jnp.clip(ustart[slot_e] + rank, 0, n_asg - 1)]
    slot_tok = jnp.where(valid, slot_asg % t, 0)
    slot_w = jnp.where(valid, flat_w[slot_asg], 0.0)
    xs = _sc_gather_rows(tok, slot_tok)
    y = _experts(xs, block_e, nb_used, slot_w, wg, wu, wd, layer, n_blocks=n_blocks)
    y3 = _sc_gather_rows(y, pos).reshape(TOP_K, t, d)
    return _combine_ln(y3, tok, x, sg, su, sd, mod, ln_g, ln_b, mod_next, t=t, tm=tm, alpha=alpha,
                       mod_row=mod_row)


def _rope64(r, c_ref, sa_ref, sb_ref):
    return r * c_ref[...] + pltpu.roll(r, LANES - QK_ROPE // 2, 1) * sa_ref[...] + pltpu.roll(r, QK_ROPE // 2, 1) * sb_ref[...]


def _mla_q_kernel(d_ref, gain_ref, w_ref, c_ref, sa_ref, sb_ref, q_ref, *, scale):
    n = _rms(d_ref[...], gain_ref[...]).astype(BF16)
    q = jnp.dot(n, w_ref[...], preferred_element_type=F32)
    for h in range(MLA_HEADS):
        lo = h * MLA_DK_PAD
        q_ref[:, lo:lo + QK_NOPE] = (q[:, lo:lo + QK_NOPE] * scale).astype(q_ref.dtype)
        r = _rope64(q[:, lo + QK_NOPE:lo + MLA_DK_PAD], c_ref, sa_ref, sb_ref)
        q_ref[:, lo + QK_NOPE:lo + MLA_DK_PAD] = (r * scale).astype(q_ref.dtype)


def _mla_kv_kernel(ckv_ref, kr_ref, gain_ref, wk_ref, wv_ref, c_ref, sa_ref, sb_ref, k_ref, v_ref):
    n = _rms(ckv_ref[...], gain_ref[...]).astype(BF16)
    kn = jnp.dot(n, wk_ref[...], preferred_element_type=F32)
    v_ref[...] = jnp.dot(n, wv_ref[...], preferred_element_type=F32).astype(v_ref.dtype)
    kr = _rope64(kr_ref[...], c_ref, sa_ref, sb_ref).astype(k_ref.dtype)
    for h in range(MLA_HEADS):
        lo = h * MLA_DK_PAD
        k_ref[:, lo:lo + QK_NOPE] = kn[:, h * QK_NOPE:(h + 1) * QK_NOPE].astype(k_ref.dtype)
        k_ref[:, lo + QK_NOPE:lo + MLA_DK_PAD] = kr


def _axial_angles(n_tok, rot_dim):
    rows = n_tok // GRID_W
    n_freq = rot_dim // 4
    inv = ROPE_THETA ** (-jnp.arange(n_freq, dtype=F32) / n_freq)
    row = jnp.repeat(jnp.arange(rows, dtype=F32), GRID_W)
    col = jnp.tile(jnp.arange(GRID_W, dtype=F32), rows)
    return jnp.concatenate([row[:, None] * inv, col[:, None] * inv], axis=-1)


def _rope_tables_128(n_tok, ident_rows):
    ang = _axial_angles(n_tok, HEAD_DIM)
    cos, sin = jnp.cos(ang), jnp.sin(ang)
    c = jnp.concatenate([cos, cos], axis=-1)
    s = jnp.concatenate([-sin, sin], axis=-1)
    c = jnp.concatenate([c, jnp.ones((ident_rows, HEAD_DIM), F32)], axis=0)
    s = jnp.concatenate([s, jnp.zeros((ident_rows, HEAD_DIM), F32)], axis=0)
    return c, s


def _rope_tables_64(n_tok, ident_rows):
    ang = _axial_angles(n_tok, QK_ROPE)
    cos, sin = jnp.cos(ang), jnp.sin(ang)
    half = QK_ROPE // 2
    z = jnp.zeros((n_tok, LANES - QK_ROPE), F32)
    zh = jnp.zeros((n_tok, half), F32)
    c = jnp.concatenate([cos, cos, z], axis=-1)
    sa = jnp.concatenate([-sin, zh, z], axis=-1)
    sb = jnp.concatenate([zh, sin, z], axis=-1)
    ci = jnp.concatenate([jnp.ones((ident_rows, QK_ROPE), F32), jnp.zeros((ident_rows, LANES - QK_ROPE), F32)], axis=-1)
    zi = jnp.zeros((ident_rows, LANES), F32)
    return jnp.concatenate([c, ci], 0), jnp.concatenate([sa, zi], 0), jnp.concatenate([sb, zi], 0)


def kernel(x, c, ctx, c_ctx, w_ada, b_ada, ln_g, ln_b, a_w_in, a_conv_w, a_q_gain, a_k_gain, a_w_out, m_w_down, m_q_gain, m_kv_gain, m_w_uq, m_w_ukv, m_w_out, router_w, router_b, e_w_gate, e_w_up, e_w_down, s_w_gate, s_w_up, s_w_down):
    batch, seq, d = x.shape
    ctx_len = ctx.shape[1]
    depth = w_ada.shape[0]
    assert depth == 2, "one conv+GQA layer followed by one MLA layer"
    alpha = (2 * depth) ** 0.25
    t_lat = batch * seq
    t_ctx = batch * ctx_len
    t_all = t_lat + t_ctx
    tr = 256
    assert seq % tr == 0 and ctx_len % tr == 0 and seq % GRID_W == 0
    lat_tiles = t_lat // tr
    lat_seq_tiles = seq // tr
    ctx_seq_tiles = ctx_len // tr
    lk = ctx_len + seq

    def mod_row(r):
        return jnp.minimum(r // seq, batch)

    def kv_block(i):
        is_lat = i < lat_tiles
        cidx = i - lat_tiles
        b = jnp.where(is_lat, i // lat_seq_tiles, cidx // ctx_seq_tiles)
        rb = jnp.where(is_lat, ctx_seq_tiles + i % lat_seq_tiles, cidx % ctx_seq_tiles)
        return b, rb

    def pos_block(i):
        return jnp.where(i < lat_tiles, i % lat_seq_tiles, lat_seq_tiles)

    rows = -(-(batch + 1) // SUBLANES) * SUBLANES
    cond = jnp.concatenate([c, c_ctx[None, :], jnp.zeros((rows - batch - 1, d), F32)], axis=0)
    mod = _ada_table(cond, w_ada, b_ada).reshape(depth, rows, 1, 6 * d)

    x_all = jnp.concatenate([x.reshape(t_lat, d), ctx.reshape(t_ctx, d)], axis=0)

    u0 = _modulate(x_all, mod[0], mod_row, tr)
    proj = _matmul(u0, a_w_in[0].astype(BF16), BF16, 512, 512)

    cos128, sin128 = _rope_tables_128(seq, tr)
    d_q = ATT_HEADS * HEAD_DIM
    d_kv = ATT_KV_HEADS * HEAD_DIM
    qkv_w = d_q + 2 * d_kv
    qkv_blk = 3 * CONV_DIM // qkv_w
    assert qkv_blk * qkv_w == 3 * CONV_DIM
    q0, k0, v0 = pl.pallas_call(
        functools.partial(_qkprep_kernel, scale=1.0 / math.sqrt(HEAD_DIM)),
        grid=(t_all // tr,),
        in_specs=[
            pl.BlockSpec((tr, qkv_w), lambda i: (i, qkv_blk)),
            pl.BlockSpec((tr, HEAD_DIM), lambda i: (pos_block(i), 0)),
            pl.BlockSpec((tr, HEAD_DIM), lambda i: (pos_block(i), 0)),
            _const_spec((1, HEAD_DIM)), _const_spec((1, HEAD_DIM)),
        ],
        out_specs=[
            pl.BlockSpec((tr, d_q), lambda i: (i, 0)),
            pl.BlockSpec((None, tr, d_kv), lambda i: (*kv_block(i), 0)),
            pl.BlockSpec((None, tr, d_kv), lambda i: (*kv_block(i), 0)),
        ],
        out_shape=[
            jax.ShapeDtypeStruct((t_all, d_q), BF16),
            jax.ShapeDtypeStruct((batch, lk, d_kv), BF16),
            jax.ShapeDtypeStruct((batch, lk, d_kv), BF16),
        ],
        compiler_params=_params(("parallel",)),
        name="qk_prep",
    )(proj, cos128, sin128, a_q_gain[0].reshape(1, HEAD_DIM), a_k_gain[0].reshape(1, HEAD_DIM))

    grp = ATT_HEADS // ATT_KV_HEADS
    att_lat = _attention(q0, k0, v0, batch=batch, sq=seq, lk=lk, n_kv=ATT_KV_HEADS, group=grp, dk=HEAD_DIM,
                         dv=HEAD_DIM, tq=256, q_row_off=0)
    att_ctx = _attention(q0, k0, v0, batch=batch, sq=ctx_len, lk=ctx_len, n_kv=ATT_KV_HEADS, group=grp,
                         dk=HEAD_DIM, dv=HEAD_DIM, tq=256, q_row_off=t_lat)
    att0 = jnp.concatenate([att_lat, att_ctx], axis=0)

    conv0 = _conv_gate(proj, a_conv_w[0], t=t_all, tm=tr, tc=512, lat_tiles=lat_tiles,
                       lat_seq_tiles=lat_seq_tiles, ctx_seq_tiles=ctx_seq_tiles)

    w_out0 = a_w_out[0].astype(BF16)
    x1, tok0 = _outproj_ln([conv0, att0], [w_out0[:CONV_DIM], w_out0[CONV_DIM:]], x_all, mod[0],
                           ln_g[0, 0].reshape(1, d), ln_b[0, 0].reshape(1, d), t=t_all, tm=tr, alpha=alpha,
                           mod_row=mod_row)

    x2, u1 = _moe(tok0, x1, t_all, 0, router_w[0], router_b[0], e_w_gate, e_w_up, e_w_down,
                  s_w_gate[0].astype(BF16), s_w_up[0].astype(BF16), s_w_down[0].astype(BF16), mod[0],
                  ln_g[0, 1].reshape(1, d), ln_b[0, 1].reshape(1, d), mod[1], alpha=alpha,
                  mod_row=mod_row, tm=128)

    n_down = Q_LORA + KV_LORA + QK_ROPE
    n_down_pad = -(-n_down // LANES) * LANES
    w_down = jnp.pad(m_w_down[0], ((0, 0), (0, n_down_pad - n_down))).astype(BF16)
    down = _matmul(u1, w_down, F32, 512, n_down_pad)

    dqk = QK_NOPE + QK_ROPE
    w_uq = m_w_uq[0].reshape(Q_LORA, MLA_HEADS, dqk)
    w_uq = jnp.pad(w_uq, ((0, 0), (0, 0), (0, MLA_DK_PAD - dqk))).reshape(Q_LORA, MLA_HEADS * MLA_DK_PAD).astype(BF16)
    w_ukv = m_w_ukv[0].reshape(KV_LORA, MLA_HEADS, QK_NOPE + V_DIM)
    w_uk = w_ukv[:, :, :QK_NOPE].reshape(KV_LORA, MLA_HEADS * QK_NOPE).astype(BF16)
    w_uv = w_ukv[:, :, QK_NOPE:].reshape(KV_LORA, MLA_HEADS * V_DIM).astype(BF16)

    c64, sa64, sb64 = _rope_tables_64(seq, tr)
    rope_specs = [pl.BlockSpec((tr, LANES), lambda i: (pos_block(i), 0))] * 3
    q1 = pl.pallas_call(
        functools.partial(_mla_q_kernel, scale=1.0 / math.sqrt(dqk)),
        grid=(lat_tiles,),
        in_specs=[
            pl.BlockSpec((tr, Q_LORA), lambda i: (i, 0)),
            _const_spec((1, Q_LORA)),
            _const_spec(w_uq.shape),
        ] + rope_specs,
        out_specs=pl.BlockSpec((tr, MLA_HEADS * MLA_DK_PAD), lambda i: (i, 0)),
        out_shape=jax.ShapeDtypeStruct((t_lat, MLA_HEADS * MLA_DK_PAD), BF16),
        compiler_params=_params(("parallel",)),
        name="mla_q",
    )(down, m_q_gain[0].reshape(1, Q_LORA), w_uq, c64, sa64, sb64)

    assert KV_LORA == Q_LORA and (Q_LORA + KV_LORA) % LANES == 0
    k1, v1 = pl.pallas_call(
        _mla_kv_kernel,
        grid=(t_all // tr,),
        in_specs=[
            pl.BlockSpec((tr, KV_LORA), lambda i: (i, 1)),
            pl.BlockSpec((tr, LANES), lambda i: (i, (Q_LORA + KV_LORA) // LANES)),
            _const_spec((1, KV_LORA)),
            _const_spec(w_uk.shape), _const_spec(w_uv.shape),
        ] + rope_specs,
        out_specs=[
            pl.BlockSpec((None, tr, MLA_HEADS * MLA_DK_PAD), lambda i: (*kv_block(i), 0)),
            pl.BlockSpec((None, tr, MLA_HEADS * V_DIM), lambda i: (*kv_block(i), 0)),
        ],
        out_shape=[
            jax.ShapeDtypeStruct((batch, lk, MLA_HEADS * MLA_DK_PAD), BF16),
            jax.ShapeDtypeStruct((batch, lk, MLA_HEADS * V_DIM), BF16),
        ],
        compiler_params=_params(("parallel",)),
        name="mla_kv",
    )(down, down, m_kv_gain[0].reshape(1, KV_LORA), w_uk, w_uv, c64, sa64, sb64)

    att1 = _attention(q1, k1, v1, batch=batch, sq=seq, lk=lk, n_kv=MLA_HEADS, group=1, dk=MLA_DK_PAD, dv=V_DIM,
                      tq=512, q_row_off=0)

    x3, tok1 = _outproj_ln([att1], [m_w_out[0].astype(BF16)], x2, mod[1], ln_g[1, 0].reshape(1, d),
                           ln_b[1, 0].reshape(1, d), t=t_lat, tm=tr, alpha=alpha, mod_row=mod_row)

    (x4,) = _moe(tok1, x3, t_lat, 1, router_w[1], router_b[1], e_w_gate, e_w_up, e_w_down,
                 s_w_gate[1].astype(BF16), s_w_up[1].astype(BF16), s_w_down[1].astype(BF16), mod[1],
                 ln_g[1, 1].reshape(1, d), ln_b[1, 1].reshape(1, d), None, alpha=alpha,
                 mod_row=mod_row, tm=128)
    return x4.reshape(batch, seq, d)
```

```python
import functools
import math

import jax
import jax.numpy as jnp
from jax import lax
from jax.experimental import pallas as pl
from jax.experimental.pallas import tpu as pltpu
from jax.experimental.pallas import tpu_sc as plsc

F32 = jnp.float32
BF16 = jnp.bfloat16

GRID_W = 64
CONV_DIM = 1024
ATT_HEADS = 8
ATT_KV_HEADS = 2
HEAD_DIM = 128
MLA_HEADS = 16
Q_LORA = 512
KV_LORA = 512
QK_NOPE = 128
QK_ROPE = 64
V_DIM = 128
N_EXPERTS = 64
TOP_K = 6
N_GROUPS = 8
TOPK_GROUPS = 4
ROUTED_SCALE = 2.5
ROPE_THETA = 10000.0
LN_EPS = 1e-5
RMS_EPS = 1e-6

V7X_VMEM_LIMIT_BYTES = 56 * 1024 * 1024
LANES = 128
SUBLANES = 8
MOE_ROWS = 256
V7X_SC_CORES = 2
V7X_SC_SUBCORES = 16
SC_GATHER_ROWS = 16
MLA_DK_PAD = 256


def _params(sem):
    return pltpu.CompilerParams(dimension_semantics=sem, vmem_limit_bytes=V7X_VMEM_LIMIT_BYTES)


def _const_spec(shape):
    nd = len(shape)
    return pl.BlockSpec(shape, lambda *_: (0,) * nd)


def _ada_kernel(s_ref, w_ref, b_ref, o_ref):
    s = s_ref[...]
    s = s * (1.0 / (1.0 + jnp.exp(-s)))
    o_ref[...] = jnp.dot(s.astype(BF16), w_ref[...].astype(BF16), preferred_element_type=F32) + b_ref[...]


def _ada_table(cond, w_ada, b_ada):
    depth, d, n = w_ada.shape
    r = cond.shape[0]
    tn = 1024
    return pl.pallas_call(
        _ada_kernel,
        grid=(depth, n // tn),
        in_specs=[
            pl.BlockSpec((r, d), lambda l, j: (0, 0)),
            pl.BlockSpec((None, d, tn), lambda l, j: (l, 0, j)),
            pl.BlockSpec((None, 1, tn), lambda l, j: (l, 0, j)),
        ],
        out_specs=pl.BlockSpec((None, r, tn), lambda l, j: (l, 0, j)),
        out_shape=jax.ShapeDtypeStruct((depth, r, n), F32),
        compiler_params=_params(("parallel", "parallel")),
        name="ada_table",
    )(cond, w_ada, b_ada.reshape(depth, 1, n))


def _mod_spec(d, chunk, mod_row, tm):
    return pl.BlockSpec((None, 1, d), lambda i: (mod_row(i * tm), 0, chunk))


def _modulate_kernel(x_ref, sc_ref, sh_ref, o_ref):
    o_ref[...] = (x_ref[...] * (1.0 + sc_ref[...]) + sh_ref[...]).astype(o_ref.dtype)


def _modulate(x, mod, mod_row, tm):
    t, d = x.shape
    return pl.pallas_call(
        _modulate_kernel,
        grid=(t // tm,),
        in_specs=[
            pl.BlockSpec((tm, d), lambda i: (i, 0)),
            _mod_spec(d, 1, mod_row, tm),
            _mod_spec(d, 0, mod_row, tm),
        ],
        out_specs=pl.BlockSpec((tm, d), lambda i: (i, 0)),
        out_shape=jax.ShapeDtypeStruct((t, d), BF16),
        compiler_params=_params(("parallel",)),
        name="modulate",
    )(x, mod, mod)


def _mm_kernel(a_ref, w_ref, o_ref):
    o_ref[...] = jnp.dot(a_ref[...], w_ref[...], preferred_element_type=F32).astype(o_ref.dtype)


def _matmul(a, w, out_dtype, tm, tn):
    m, k = a.shape
    n = w.shape[1]
    return pl.pallas_call(
        _mm_kernel,
        grid=(m // tm, n // tn),
        in_specs=[
            pl.BlockSpec((tm, k), lambda i, j: (i, 0)),
            pl.BlockSpec((k, tn), lambda i, j: (0, j)),
        ],
        out_specs=pl.BlockSpec((tm, tn), lambda i, j: (i, j)),
        out_shape=jax.ShapeDtypeStruct((m, n), out_dtype),
        compiler_params=_params(("parallel", "parallel")),
        name="matmul",
    )(a, w)


def _rms(t, gain):
    return t * lax.rsqrt(jnp.mean(t * t, axis=-1, keepdims=True) + RMS_EPS) * gain


def _qkprep_kernel(p_ref, cos_ref, sin_ref, qg_ref, kg_ref, q_ref, k_ref, v_ref, *, scale):
    cos = cos_ref[...]
    sin = sin_ref[...]

    def norm_rope(t, gain):
        y = _rms(t.astype(F32), gain)
        return y * cos + pltpu.roll(y, HEAD_DIM // 2, 1) * sin

    for h in range(ATT_HEADS):
        sl = slice(h * HEAD_DIM, (h + 1) * HEAD_DIM)
        q_ref[:, sl] = (norm_rope(p_ref[:, sl], qg_ref[...]) * scale).astype(q_ref.dtype)
    k0 = ATT_HEADS * HEAD_DIM
    for h in range(ATT_KV_HEADS):
        sl = slice(h * HEAD_DIM, (h + 1) * HEAD_DIM)
        k_ref[:, sl] = norm_rope(p_ref[:, k0 + h * HEAD_DIM:k0 + (h + 1) * HEAD_DIM], kg_ref[...]).astype(k_ref.dtype)
    v0 = k0 + ATT_KV_HEADS * HEAD_DIM
    v_ref[...] = p_ref[:, v0:v0 + ATT_KV_HEADS * HEAD_DIM].astype(v_ref.dtype)


def _attn_kernel(q_ref, k_ref, v_ref, o_ref, *, group, tq, rows, dk, dv):
    k = k_ref[...]
    v = v_ref[...]
    for h in range(group):
        for r in range(0, tq, rows):
            q = q_ref[r:r + rows, h * dk:(h + 1) * dk]
            s = lax.dot_general(q, k, (((1,), (1,)), ((), ())), preferred_element_type=F32)
            m = jnp.max(s, axis=-1, keepdims=True)
            p = jnp.exp(s - m)
            l = jnp.sum(p, axis=-1, keepdims=True)
            o = jnp.dot(p.astype(v.dtype), v, preferred_element_type=F32)
            o_ref[r:r + rows, h * dv:(h + 1) * dv] = (o / l).astype(o_ref.dtype)


def _attention(q, k, v, *, batch, sq, lk, n_kv, group, dk, dv, tq, rows, q_row_off):
    nq = sq // tq
    off = q_row_off // tq
    assert tq % rows == 0 and q_row_off % tq == 0 and sq % tq == 0
    return pl.pallas_call(
        functools.partial(_attn_kernel, group=group, tq=tq, rows=rows, dk=dk, dv=dv),
        grid=(batch, n_kv, nq),
        in_specs=[
            pl.BlockSpec((tq, group * dk), lambda b, g, i: (off + b * nq + i, g)),
            pl.BlockSpec((None, lk, dk), lambda b, g, i: (b, 0, g)),
            pl.BlockSpec((None, lk, dv), lambda b, g, i: (b, 0, g)),
        ],
        out_specs=pl.BlockSpec((tq, group * dv), lambda b, g, i: (b * nq + i, g)),
        out_shape=jax.ShapeDtypeStruct((batch * sq, n_kv * group * dv), BF16),
        compiler_params=_params(("parallel", "parallel", "parallel")),
        name="attention",
    )(q, k, v)


def _conv_kernel(gb_ref, gc_ref, hv_ref, gcp_ref, hvp_ref, gcn_ref, hvn_ref, w_ref, o_ref, *,
                 tm, lat_tiles, lat_seq_tiles, ctx_seq_tiles):
    i = pl.program_id(0)
    is_lat = i < lat_tiles
    pos = jnp.where(is_lat, i % lat_seq_tiles, (i - lat_tiles) % ctx_seq_tiles)
    seq_tiles = jnp.where(is_lat, lat_seq_tiles, ctx_seq_tiles)
    not_first = (pos != 0).astype(F32)
    not_last = (pos != seq_tiles - 1).astype(F32)
    p = gc_ref[...].astype(F32) * hv_ref[...].astype(F32)
    halo_prev = gcp_ref[SUBLANES - 1:SUBLANES, :].astype(F32) * hvp_ref[SUBLANES - 1:SUBLANES, :].astype(F32) * not_first
    halo_next = gcn_ref[0:1, :].astype(F32) * hvn_ref[0:1, :].astype(F32) * not_last
    row = lax.broadcasted_iota(jnp.int32, p.shape, 0)
    prev = jnp.where(row == 0, halo_prev, pltpu.roll(p, 1, 0))
    nxt = jnp.where(row == tm - 1, halo_next, pltpu.roll(p, tm - 1, 0))
    w = w_ref[...]
    conv = w[0:1, :] * prev + w[1:2, :] * p + w[2:3, :] * nxt
    o_ref[...] = (gb_ref[...].astype(F32) * conv).astype(o_ref.dtype)


def _conv_gate(p, conv_w, *, t, tm, tc, lat_tiles, lat_seq_tiles, ctx_seq_tiles):
    nct = CONV_DIM // tc
    hb = tm // SUBLANES
    n_halo = t // SUBLANES

    def cur(part):
        return pl.BlockSpec((tm, tc), lambda i, j: (i, part * nct + j))

    def prev(part):
        return pl.BlockSpec((SUBLANES, tc), lambda i, j: (jnp.maximum(i * hb - 1, 0), part * nct + j))

    def nxt(part):
        return pl.BlockSpec((SUBLANES, tc), lambda i, j: (jnp.minimum((i + 1) * hb, n_halo - 1), part * nct + j))

    return pl.pallas_call(
        functools.partial(_conv_kernel, tm=tm, lat_tiles=lat_tiles, lat_seq_tiles=lat_seq_tiles,
                          ctx_seq_tiles=ctx_seq_tiles),
        grid=(t // tm, nct),
        in_specs=[cur(0), cur(1), cur(2), prev(1), prev(2), nxt(1), nxt(2),
                  pl.BlockSpec((3, tc), lambda i, j: (0, j))],
        out_specs=pl.BlockSpec((tm, tc), lambda i, j: (i, j)),
        out_shape=jax.ShapeDtypeStruct((t, CONV_DIM), BF16),
        compiler_params=_params(("parallel", "parallel")),
        name="conv_gate",
    )(p, p, p, p, p, p, p, conv_w)


def _layer_norm(z, g, b):
    mu = jnp.mean(z, axis=-1, keepdims=True)
    zc = z - mu
    var = jnp.mean(zc * zc, axis=-1, keepdims=True)
    return zc * lax.rsqrt(var + LN_EPS) * g + b


def _pack_bf16_pairs(x):
    half = x.shape[1] // 2
    lo = lax.bitcast_convert_type(x[:, :half].astype(BF16).astype(F32), jnp.uint32) >> 16
    hi = lax.bitcast_convert_type(x[:, half:].astype(BF16).astype(F32), jnp.uint32) & jnp.uint32(0xFFFF0000)
    return lax.bitcast_convert_type(lo | hi, jnp.int32)


def _unpack_bf16_pairs(w):
    u = lax.bitcast_convert_type(w, jnp.uint32)
    lo = lax.bitcast_convert_type(u << 16, F32).astype(BF16)
    hi = lax.bitcast_convert_type(u & jnp.uint32(0xFFFF0000), F32).astype(BF16)
    return lo, hi


def _dot_halves(lo, hi, w_ref):
    half = lo.shape[1]
    return (jnp.dot(lo, w_ref[:half, :], preferred_element_type=F32)
            + jnp.dot(hi, w_ref[half:, :], preferred_element_type=F32))


def _outproj_ln_kernel(*refs, n_a, alpha):
    a_refs = refs[:n_a]
    w_refs = refs[n_a:2 * n_a]
    x_ref, gate_ref, lng_ref, lnb_ref, sc_ref, sh_ref, xo_ref, tok_ref = refs[2 * n_a:]
    y = jnp.dot(a_refs[0][...], w_refs[0][...], preferred_element_type=F32)
    for a_ref, w_ref in zip(a_refs[1:], w_refs[1:]):
        y = y + jnp.dot(a_ref[...], w_ref[...], preferred_element_type=F32)
    xn = _layer_norm(alpha * x_ref[...] + gate_ref[...] * y, lng_ref[...], lnb_ref[...])
    xo_ref[...] = xn
    tok_ref[...] = _pack_bf16_pairs(xn * (1.0 + sc_ref[...]) + sh_ref[...])


def _outproj_ln(a_list, w_list, x, mod, ln_g, ln_b, *, t, tm, alpha, mod_row):
    d = x.shape[1]
    n_a = len(a_list)
    in_specs = [pl.BlockSpec((tm, a.shape[1]), lambda i: (i, 0)) for a in a_list]
    in_specs += [_const_spec(w.shape) for w in w_list]
    in_specs += [
        pl.BlockSpec((tm, d), lambda i: (i, 0)),
        _mod_spec(d, 2, mod_row, tm),
        _const_spec((1, d)), _const_spec((1, d)),
        _mod_spec(d, 4, mod_row, tm),
        _mod_spec(d, 3, mod_row, tm),
    ]
    return pl.pallas_call(
        functools.partial(_outproj_ln_kernel, n_a=n_a, alpha=alpha),
        grid=(t // tm,),
        in_specs=in_specs,
        out_specs=[pl.BlockSpec((tm, d), lambda i: (i, 0)), pl.BlockSpec((tm, d // 2), lambda i: (i, 0))],
        out_shape=[jax.ShapeDtypeStruct((t, d), F32), jax.ShapeDtypeStruct((t, d // 2), jnp.int32)],
        compiler_params=_params(("parallel",)),
        name="outproj_ln",
    )(*a_list, *w_list, x, mod, ln_g, ln_b, mod, mod)


def _router_kernel(t_ref, rw_ref, rb_ref, idx_ref, gw_ref):
    lo, hi = _unpack_bf16_pairs(t_ref[...])
    half = lo.shape[1]
    nt = (((1,), (1,)), ((), ()))
    logits = (lax.dot_general(rw_ref[:, :half], lo, nt, preferred_element_type=F32)
              + lax.dot_general(rw_ref[:, half:], hi, nt, preferred_element_type=F32))
    scores = 1.0 / (1.0 + jnp.exp(-logits))
    sel = scores + rb_ref[...]
    gsz = N_EXPERTS // N_GROUPS
    neg = -jnp.inf
    sub = lax.broadcasted_iota(jnp.int32, (gsz, sel.shape[1]), 0)
    slabs = [sel[g * gsz:(g + 1) * gsz, :] for g in range(N_GROUPS)]
    gscore = []
    for s in slabs:
        m1 = jnp.max(s, axis=0, keepdims=True)
        a1 = jnp.min(jnp.where(s == m1, sub, gsz), axis=0, keepdims=True)
        m2 = jnp.max(jnp.where(sub == a1, neg, s), axis=0, keepdims=True)
        gscore.append(m1 + m2)
    masked = []
    for g in range(N_GROUPS):
        ahead = jnp.zeros(gscore[g].shape, jnp.int32)
        for h in range(N_GROUPS):
            if h == g:
                continue
            beats = gscore[h] >= gscore[g] if h < g else gscore[h] > gscore[g]
            ahead = ahead + beats.astype(jnp.int32)
        masked.append(jnp.where(ahead < TOPK_GROUPS, slabs[g], neg))
    cur = jnp.concatenate(masked, axis=0)
    eio = lax.broadcasted_iota(jnp.int32, cur.shape, 0)
    picks, weights = [], []
    for _ in range(TOP_K):
        m = jnp.max(cur, axis=0, keepdims=True)
        a = jnp.min(jnp.where(cur == m, eio, N_EXPERTS), axis=0, keepdims=True)
        hit = eio == a
        picks.append(a)
        weights.append(jnp.sum(jnp.where(hit, scores, 0.0), axis=0, keepdims=True))
        cur = jnp.where(hit, neg, cur)
    total = weights[0]
    for w in weights[1:]:
        total = total + w
    for k in range(TOP_K):
        idx_ref[k:k + 1, :] = picks[k]
        gw_ref[k:k + 1, :] = weights[k] / total * ROUTED_SCALE
    for k in range(TOP_K, SUBLANES):
        idx_ref[k:k + 1, :] = jnp.zeros_like(picks[0])
        gw_ref[k:k + 1, :] = jnp.zeros_like(weights[0])


def _router(tok, rw_t, rb, *, t, tt):
    half = tok.shape[1]
    return pl.pallas_call(
        _router_kernel,
        grid=(t // tt,),
        in_specs=[
            pl.BlockSpec((tt, half), lambda i: (i, 0)),
            _const_spec((N_EXPERTS, 2 * half)),
            _const_spec((N_EXPERTS, 1)),
        ],
        out_specs=[pl.BlockSpec((SUBLANES, tt), lambda i: (0, i))] * 2,
        out_shape=[jax.ShapeDtypeStruct((SUBLANES, t), jnp.int32), jax.ShapeDtypeStruct((SUBLANES, t), F32)],
        compiler_params=_params(("parallel",)),
        name="router",
    )(tok, rw_t, rb)


def _sc_gather_rows(table, idx):
    n = idx.shape[0]
    d = table.shape[1]
    n_workers = V7X_SC_CORES * V7X_SC_SUBCORES
    per_w = n // n_workers
    n_chunks = per_w // SC_GATHER_ROWS
    assert per_w * n_workers == n and n_chunks * SC_GATHER_ROWS == per_w and n_chunks % 2 == 0
    mesh = plsc.VectorSubcoreMesh(core_axis_name="c", subcore_axis_name="s", num_cores=V7X_SC_CORES,
                                  num_subcores=V7X_SC_SUBCORES)

    @functools.partial(
        pl.kernel,
        out_type=jax.ShapeDtypeStruct((n, d), table.dtype),
        mesh=mesh,
        scratch_types=[
            pltpu.VMEM((per_w,), jnp.int32),
            pltpu.VMEM((2, SC_GATHER_ROWS, d), table.dtype),
            pltpu.SemaphoreType.DMA((2,)),
            pltpu.SemaphoreType.DMA((2,)),
        ],
        name="sc_gather_rows",
    )
    def gather(table_hbm, idx_hbm, out_hbm, idx_v, rows_v, gsem, wsem):
        wid = lax.axis_index("s") * V7X_SC_CORES + lax.axis_index("c")
        base = wid * per_w
        pltpu.sync_copy(idx_hbm.at[pl.ds(base, per_w)], idx_v)

        def gather_copy(c, b):
            return pltpu.make_async_copy(table_hbm.at[idx_v.at[pl.ds(c * SC_GATHER_ROWS, SC_GATHER_ROWS)]],
                                         rows_v.at[b], gsem.at[b])

        def write_copy(c, b):
            return pltpu.make_async_copy(rows_v.at[b], out_hbm.at[pl.ds(base + c * SC_GATHER_ROWS, SC_GATHER_ROWS)],
                                         wsem.at[b])

        gather_copy(0, 0).start()

        @pl.loop(0, n_chunks, step=2)
        def _(g):
            for b in range(2):
                c = g + b
                nb = 1 - b

                @pl.when(c >= 1)
                def _():
                    write_copy(c - 1, nb).wait()

                @pl.when(c + 1 < n_chunks)
                def _():
                    gather_copy(c + 1, nb).start()

                gather_copy(c, b).wait()
                write_copy(c, b).start()

        write_copy(n_chunks - 1, (n_chunks - 1) % 2).wait()

    return gather(table, idx)


def _experts_kernel(be_ref, nbu_ref, x_ref, sw_ref, wg_ref, wu_ref, wd_ref, y_ref, wgb, wub, wdb):
    b = pl.program_id(0)
    nbu = nbu_ref[0]

    @pl.when(b < nbu)
    def _():
        changed = jnp.logical_or(b == 0, be_ref[b] != be_ref[jnp.maximum(b - 1, 0)])

        @pl.when(changed)
        def _():
            wgb[...] = wg_ref[...].astype(BF16)
            wub[...] = wu_ref[...].astype(BF16)
            wdb[...] = wd_ref[...].astype(BF16)

        lo, hi = _unpack_bf16_pairs(x_ref[...])
        hg = _dot_halves(lo, hi, wgb)
        hu = _dot_halves(lo, hi, wub)
        h = hg * (1.0 / (1.0 + jnp.exp(-hg))) * hu
        y_ref[...] = _pack_bf16_pairs(jnp.dot(h.astype(BF16), wdb[...], preferred_element_type=F32) * sw_ref[...])

    @pl.when(b >= nbu)
    def _():
        y_ref[...] = jnp.zeros_like(y_ref)


def _experts(xs, block_e, nb_used, slot_w, wg, wu, wd, layer, *, n_blocks):
    half = xs.shape[1]
    d = 2 * half
    ff = wg.shape[3]

    def used(b, nbu):
        return jnp.minimum(b, jnp.maximum(nbu[0] - 1, 0))

    grid_spec = pltpu.PrefetchScalarGridSpec(
        num_scalar_prefetch=2,
        grid=(n_blocks,),
        in_specs=[
            pl.BlockSpec((MOE_ROWS, half), lambda b, be, nbu: (used(b, nbu), 0)),
            pl.BlockSpec((None, MOE_ROWS, 1), lambda b, be, nbu: (used(b, nbu), 0, 0)),
            pl.BlockSpec((None, None, d, ff), lambda b, be, nbu: (layer, be[b], 0, 0)),
            pl.BlockSpec((None, None, d, ff), lambda b, be, nbu: (layer, be[b], 0, 0)),
            pl.BlockSpec((None, None, ff, d), lambda b, be, nbu: (layer, be[b], 0, 0)),
        ],
        out_specs=pl.BlockSpec((MOE_ROWS, half), lambda b, be, nbu: (b, 0)),
        scratch_shapes=[
            pltpu.VMEM((d, ff), BF16),
            pltpu.VMEM((d, ff), BF16),
            pltpu.VMEM((ff, d), BF16),
        ],
    )
    return pl.pallas_call(
        _experts_kernel,
        grid_spec=grid_spec,
        out_shape=jax.ShapeDtypeStruct((n_blocks * MOE_ROWS, half), jnp.int32),
        compiler_params=_params(("arbitrary",)),
        name="experts",
    )(block_e, nb_used, xs, slot_w.reshape(n_blocks, MOE_ROWS, 1), wg, wu, wd)


def _combine_ln_kernel(*refs, alpha, emit_next):
    y_ref, tok_ref, x_ref, sg_ref, su_ref, sd_ref, gate_ref, lng_ref, lnb_ref = refs[:9]
    if emit_next:
        sc_ref, sh_ref, xo_ref, u_ref = refs[9:]
    else:
        (xo_ref,) = refs[9:]
    lo, hi = _unpack_bf16_pairs(tok_ref[...])
    hg = _dot_halves(lo, hi, sg_ref)
    hu = _dot_halves(lo, hi, su_ref)
    h = hg * (1.0 / (1.0 + jnp.exp(-hg))) * hu
    f_lo, f_hi = (part.astype(F32) for part in _unpack_bf16_pairs(y_ref[0]))
    for k in range(1, TOP_K):
        y_lo, y_hi = _unpack_bf16_pairs(y_ref[k])
        f_lo = f_lo + y_lo.astype(F32)
        f_hi = f_hi + y_hi.astype(F32)
    f = jnp.concatenate([f_lo, f_hi], axis=-1) + jnp.dot(h.astype(BF16), sd_ref[...], preferred_element_type=F32)
    xn = _layer_norm(alpha * x_ref[...] + gate_ref[...] * f, lng_ref[...], lnb_ref[...])
    xo_ref[...] = xn
    if emit_next:
        u_ref[...] = (xn * (1.0 + sc_ref[...]) + sh_ref[...]).astype(u_ref.dtype)


def _combine_ln(y3, tok, x, sg, su, sd, mod, ln_g, ln_b, mod_next, *, t, tm, alpha, mod_row):
    d = x.shape[1]
    emit_next = mod_next is not None
    in_specs = [
        pl.BlockSpec((TOP_K, tm, d // 2), lambda i: (0, i, 0)),
        pl.BlockSpec((tm, d // 2), lambda i: (i, 0)),
        pl.BlockSpec((tm, d), lambda i: (i, 0)),
        _const_spec(sg.shape), _const_spec(su.shape), _const_spec(sd.shape),
        _mod_spec(d, 5, mod_row, tm),
        _const_spec((1, d)), _const_spec((1, d)),
    ]
    args = [y3, tok, x, sg, su, sd, mod, ln_g, ln_b]
    out_specs = [pl.BlockSpec((tm, d), lambda i: (i, 0))]
    out_shape = [jax.ShapeDtypeStruct((t, d), F32)]
    if emit_next:
        in_specs += [_mod_spec(d, 1, mod_row, tm), _mod_spec(d, 0, mod_row, tm)]
        args += [mod_next, mod_next]
        out_specs.append(pl.BlockSpec((tm, d), lambda i: (i, 0)))
        out_shape.append(jax.ShapeDtypeStruct((t, d), BF16))
    return pl.pallas_call(
        functools.partial(_combine_ln_kernel, alpha=alpha, emit_next=emit_next),
        grid=(t // tm,),
        in_specs=in_specs,
        out_specs=out_specs,
        out_shape=out_shape,
        compiler_params=_params(("parallel",)),
        name="combine_ln",
    )(*args)


def _moe(tok, x, t, layer, router_w, router_b, wg, wu, wd, sg, su, sd, mod, ln_g, ln_b, mod_next, *, alpha, mod_row,
         tm):
    half = tok.shape[1]
    idx, gw = _router(tok, router_w.T.astype(BF16), router_b.reshape(N_EXPERTS, 1), t=t, tt=512)
    n_asg = t * TOP_K
    flat_e = idx[:TOP_K].reshape(-1)
    flat_w = gw[:TOP_K].reshape(-1)
    experts = jnp.arange(N_EXPERTS, dtype=jnp.int32)
    counts = jnp.sum((flat_e[:, None] == experts[None, :]).astype(jnp.int32), axis=0)
    ustart = jnp.cumsum(counts) - counts
    padded = (counts + MOE_ROWS - 1) // MOE_ROWS * MOE_ROWS
    pend = jnp.cumsum(padded)
    pstart = pend - padded
    sc_rows = V7X_SC_CORES * V7X_SC_SUBCORES * SC_GATHER_ROWS * 2
    blocks_granule = sc_rows // MOE_ROWS
    n_blocks = -(-((n_asg + N_EXPERTS * (MOE_ROWS - 1)) // MOE_ROWS + 1) // blocks_granule) * blocks_granule
    assert n_asg % sc_rows == 0
    cap = n_blocks * MOE_ROWS
    block_start = jnp.arange(n_blocks, dtype=jnp.int32) * MOE_ROWS
    block_e = jnp.minimum(jnp.sum((pend[None, :] <= block_start[:, None]).astype(jnp.int32), axis=1), N_EXPERTS - 1)
    nb_used = (pend[-1] // MOE_ROWS).astype(jnp.int32).reshape(1)
    order = jnp.argsort(flat_e).astype(jnp.int32)
    se = flat_e[order]
    dest_sorted = pstart[se] + jnp.arange(n_asg, dtype=jnp.int32) - ustart[se]
    pos = dest_sorted[jnp.argsort(order)]
    slot = jnp.arange(cap, dtype=jnp.int32)
    slot_e = jnp.repeat(block_e, MOE_ROWS)
    rank = slot - pstart[slot_e]
    valid = rank < counts[slot_e]
    slot_asg = order[jnp.clip(ustart[slot_e] + rank, 0, n_asg - 1)]
    slot_tok = jnp.where(valid, slot_asg % t, 0)
    slot_w = jnp.where(valid, flat_w[slot_asg], 0.0)
    xs = _sc_gather_rows(tok, slot_tok)
    y = _experts(xs, block_e, nb_used, slot_w, wg, wu, wd, layer, n_blocks=n_blocks)
    y3 = _sc_gather_rows(y, pos).reshape(TOP_K, t, half)
    return _combine_ln(y3, tok, x, sg, su, sd, mod, ln_g, ln_b, mod_next, t=t, tm=tm, alpha=alpha,
                       mod_row=mod_row)


def _rope64(r, c_ref, sa_ref, sb_ref):
    return r * c_ref[...] + pltpu.roll(r, LANES - QK_ROPE // 2, 1) * sa_ref[...] + pltpu.roll(r, QK_ROPE // 2, 1) * sb_ref[...]


def _mla_q_kernel(d_ref, gain_ref, w_ref, c_ref, sa_ref, sb_ref, q_ref, *, scale):
    n = _rms(d_ref[...], gain_ref[...]).astype(BF16)
    q = jnp.dot(n, w_ref[...], preferred_element_type=F32)
    for h in range(MLA_HEADS):
        lo = h * MLA_DK_PAD
        q_ref[:, lo:lo + QK_NOPE] = (q[:, lo:lo + QK_NOPE] * scale).astype(q_ref.dtype)
        r = _rope64(q[:, lo + QK_NOPE:lo + MLA_DK_PAD], c_ref, sa_ref, sb_ref)
        q_ref[:, lo + QK_NOPE:lo + MLA_DK_PAD] = (r * scale).astype(q_ref.dtype)


def _mla_kv_kernel(ckv_ref, kr_ref, gain_ref, wk_ref, wv_ref, c_ref, sa_ref, sb_ref, k_ref, v_ref):
    n = _rms(ckv_ref[...], gain_ref[...]).astype(BF16)
    kn = jnp.dot(n, wk_ref[...], preferred_element_type=F32)
    v_ref[...] = jnp.dot(n, wv_ref[...], preferred_element_type=F32).astype(v_ref.dtype)
    kr = _rope64(kr_ref[...], c_ref, sa_ref, sb_ref).astype(k_ref.dtype)
    for h in range(MLA_HEADS):
        lo = h * MLA_DK_PAD
        k_ref[:, lo:lo + QK_NOPE] = kn[:, h * QK_NOPE:(h + 1) * QK_NOPE].astype(k_ref.dtype)
        k_ref[:, lo + QK_NOPE:lo + MLA_DK_PAD] = kr


def _axial_angles(n_tok, rot_dim):
    rows = n_tok // GRID_W
    n_freq = rot_dim // 4
    inv = ROPE_THETA ** (-jnp.arange(n_freq, dtype=F32) / n_freq)
    row = jnp.repeat(jnp.arange(rows, dtype=F32), GRID_W)
    col = jnp.tile(jnp.arange(GRID_W, dtype=F32), rows)
    return jnp.concatenate([row[:, None] * inv, col[:, None] * inv], axis=-1)


def _rope_tables_128(n_tok, ident_rows):
    ang = _axial_angles(n_tok, HEAD_DIM)
    cos, sin = jnp.cos(ang), jnp.sin(ang)
    c = jnp.concatenate([cos, cos], axis=-1)
    s = jnp.concatenate([-sin, sin], axis=-1)
    c = jnp.concatenate([c, jnp.ones((ident_rows, HEAD_DIM), F32)], axis=0)
    s = jnp.concatenate([s, jnp.zeros((ident_rows, HEAD_DIM), F32)], axis=0)
    return c, s


def _rope_tables_64(n_tok, ident_rows):
    ang = _axial_angles(n_tok, QK_ROPE)
    cos, sin = jnp.cos(ang), jnp.sin(ang)
    half = QK_ROPE // 2
    z = jnp.zeros((n_tok, LANES - QK_ROPE), F32)
    zh = jnp.zeros((n_tok, half), F32)
    c = jnp.concatenate([cos, cos, z], axis=-1)
    sa = jnp.concatenate([-sin, zh, z], axis=-1)
    sb = jnp.concatenate([zh, sin, z], axis=-1)
    ci = jnp.concatenate([jnp.ones((ident_rows, QK_ROPE), F32), jnp.zeros((ident_rows, LANES - QK_ROPE), F32)], axis=-1)
    zi = jnp.zeros((ident_rows, LANES), F32)
    return jnp.concatenate([c, ci], 0), jnp.concatenate([sa, zi], 0), jnp.concatenate([sb, zi], 0)


def kernel(x, c, ctx, c_ctx, w_ada, b_ada, ln_g, ln_b, a_w_in, a_conv_w, a_q_gain, a_k_gain, a_w_out, m_w_down, m_q_gain, m_kv_gain, m_w_uq, m_w_ukv, m_w_out, router_w, router_b, e_w_gate, e_w_up, e_w_down, s_w_gate, s_w_up, s_w_down):
    batch, seq, d = x.shape
    ctx_len = ctx.shape[1]
    depth = w_ada.shape[0]
    assert depth == 2, "one conv+GQA layer followed by one MLA layer"
    alpha = (2 * depth) ** 0.25
    t_lat = batch * seq
    t_ctx = batch * ctx_len
    t_all = t_lat + t_ctx
    tr = 256
    assert seq % tr == 0 and ctx_len % tr == 0 and seq % GRID_W == 0
    lat_tiles = t_lat // tr
    lat_seq_tiles = seq // tr
    ctx_seq_tiles = ctx_len // tr
    lk = ctx_len + seq

    def mod_row(r):
        return jnp.minimum(r // seq, batch)

    def kv_block(i):
        is_lat = i < lat_tiles
        cidx = i - lat_tiles
        b = jnp.where(is_lat, i // lat_seq_tiles, cidx // ctx_seq_tiles)
        rb = jnp.where(is_lat, ctx_seq_tiles + i % lat_seq_tiles, cidx % ctx_seq_tiles)
        return b, rb

    def pos_block(i):
        return jnp.where(i < lat_tiles, i % lat_seq_tiles, lat_seq_tiles)

    rows = -(-(batch + 1) // SUBLANES) * SUBLANES
    cond = jnp.concatenate([c, c_ctx[None, :], jnp.zeros((rows - batch - 1, d), F32)], axis=0)
    mod = _ada_table(cond, w_ada, b_ada).reshape(depth, rows, 1, 6 * d)

    x_all = jnp.concatenate([x.reshape(t_lat, d), ctx.reshape(t_ctx, d)], axis=0)

    u0 = _modulate(x_all, mod[0], mod_row, tr)
    proj = _matmul(u0, a_w_in[0].astype(BF16), BF16, 512, 512)

    cos128, sin128 = _rope_tables_128(seq, tr)
    d_q = ATT_HEADS * HEAD_DIM
    d_kv = ATT_KV_HEADS * HEAD_DIM
    qkv_w = d_q + 2 * d_kv
    qkv_blk = 3 * CONV_DIM // qkv_w
    assert qkv_blk * qkv_w == 3 * CONV_DIM
    q0, k0, v0 = pl.pallas_call(
        functools.partial(_qkprep_kernel, scale=1.0 / math.sqrt(HEAD_DIM)),
        grid=(t_all // tr,),
        in_specs=[
            pl.BlockSpec((tr, qkv_w), lambda i: (i, qkv_blk)),
            pl.BlockSpec((tr, HEAD_DIM), lambda i: (pos_block(i), 0)),
            pl.BlockSpec((tr, HEAD_DIM), lambda i: (pos_block(i), 0)),
            _const_spec((1, HEAD_DIM)), _const_spec((1, HEAD_DIM)),
        ],
        out_specs=[
            pl.BlockSpec((tr, d_q), lambda i: (i, 0)),
            pl.BlockSpec((None, tr, d_kv), lambda i: (*kv_block(i), 0)),
            pl.BlockSpec((None, tr, d_kv), lambda i: (*kv_block(i), 0)),
        ],
        out_shape=[
            jax.ShapeDtypeStruct((t_all, d_q), BF16),
            jax.ShapeDtypeStruct((batch, lk, d_kv), BF16),
            jax.ShapeDtypeStruct((batch, lk, d_kv), BF16),
        ],
        compiler_params=_params(("parallel",)),
        name="qk_prep",
    )(proj, cos128, sin128, a_q_gain[0].reshape(1, HEAD_DIM), a_k_gain[0].reshape(1, HEAD_DIM))

    grp = ATT_HEADS // ATT_KV_HEADS
    att_lat = _attention(q0, k0, v0, batch=batch, sq=seq, lk=lk, n_kv=ATT_KV_HEADS, group=grp, dk=HEAD_DIM,
                         dv=HEAD_DIM, tq=256, rows=256, q_row_off=0)
    att_ctx = _attention(q0, k0, v0, batch=batch, sq=ctx_len, lk=ctx_len, n_kv=ATT_KV_HEADS, group=grp,
                         dk=HEAD_DIM, dv=HEAD_DIM, tq=256, rows=256, q_row_off=t_lat)
    att0 = jnp.concatenate([att_lat, att_ctx], axis=0)

    conv0 = _conv_gate(proj, a_conv_w[0], t=t_all, tm=tr, tc=512, lat_tiles=lat_tiles,
                       lat_seq_tiles=lat_seq_tiles, ctx_seq_tiles=ctx_seq_tiles)

    w_out0 = a_w_out[0].astype(BF16)
    x1, tok0 = _outproj_ln([conv0, att0], [w_out0[:CONV_DIM], w_out0[CONV_DIM:]], x_all, mod[0],
                           ln_g[0, 0].reshape(1, d), ln_b[0, 0].reshape(1, d), t=t_all, tm=tr, alpha=alpha,
                           mod_row=mod_row)

    x2, u1 = _moe(tok0, x1, t_all, 0, router_w[0], router_b[0], e_w_gate, e_w_up, e_w_down,
                  s_w_gate[0].astype(BF16), s_w_up[0].astype(BF16), s_w_down[0].astype(BF16), mod[0],
                  ln_g[0, 1].reshape(1, d), ln_b[0, 1].reshape(1, d), mod[1], alpha=alpha,
                  mod_row=mod_row, tm=256)

    n_down = Q_LORA + KV_LORA + QK_ROPE
    n_down_pad = -(-n_down // LANES) * LANES
    w_down = jnp.pad(m_w_down[0], ((0, 0), (0, n_down_pad - n_down))).astype(BF16)
    down = _matmul(u1, w_down, F32, 512, n_down_pad)

    dqk = QK_NOPE + QK_ROPE
    w_uq = m_w_uq[0].reshape(Q_LORA, MLA_HEADS, dqk)
    w_uq = jnp.pad(w_uq, ((0, 0), (0, 0), (0, MLA_DK_PAD - dqk))).reshape(Q_LORA, MLA_HEADS * MLA_DK_PAD).astype(BF16)
    w_ukv = m_w_ukv[0].reshape(KV_LORA, MLA_HEADS, QK_NOPE + V_DIM)
    w_uk = w_ukv[:, :, :QK_NOPE].reshape(KV_LORA, MLA_HEADS * QK_NOPE).astype(BF16)
    w_uv = w_ukv[:, :, QK_NOPE:].reshape(KV_LORA, MLA_HEADS * V_DIM).astype(BF16)

    c64, sa64, sb64 = _rope_tables_64(seq, tr)
    rope_specs = [pl.BlockSpec((tr, LANES), lambda i: (pos_block(i), 0))] * 3
    q1 = pl.pallas_call(
        functools.partial(_mla_q_kernel, scale=1.0 / math.sqrt(dqk)),
        grid=(lat_tiles,),
        in_specs=[
            pl.BlockSpec((tr, Q_LORA), lambda i: (i, 0)),
            _const_spec((1, Q_LORA)),
            _const_spec(w_uq.shape),
        ] + rope_specs,
        out_specs=pl.BlockSpec((tr, MLA_HEADS * MLA_DK_PAD), lambda i: (i, 0)),
        out_shape=jax.ShapeDtypeStruct((t_lat, MLA_HEADS * MLA_DK_PAD), BF16),
        compiler_params=_params(("parallel",)),
        name="mla_q",
    )(down, m_q_gain[0].reshape(1, Q_LORA), w_uq, c64, sa64, sb64)

    assert KV_LORA == Q_LORA and (Q_LORA + KV_LORA) % LANES == 0
    k1, v1 = pl.pallas_call(
        _mla_kv_kernel,
        grid=(t_all // tr,),
        in_specs=[
            pl.BlockSpec((tr, KV_LORA), lambda i: (i, 1)),
            pl.BlockSpec((tr, LANES), lambda i: (i, (Q_LORA + KV_LORA) // LANES)),
            _const_spec((1, KV_LORA)),
            _const_spec(w_uk.shape), _const_spec(w_uv.shape),
        ] + rope_specs,
        out_specs=[
            pl.BlockSpec((None, tr, MLA_HEADS * MLA_DK_PAD), lambda i: (*kv_block(i), 0)),
            pl.BlockSpec((None, tr, MLA_HEADS * V_DIM), lambda i: (*kv_block(i), 0)),
        ],
        out_shape=[
            jax.ShapeDtypeStruct((batch, lk, MLA_HEADS * MLA_DK_PAD), BF16),
            jax.ShapeDtypeStruct((batch, lk, MLA_HEADS * V_DIM), BF16),
        ],
        compiler_params=_params(("parallel",)),
        name="mla_kv",
    )(down, down, m_kv_gain[0].reshape(1, KV_LORA), w_uk, w_uv, c64, sa64, sb64)

    att1 = _attention(q1, k1, v1, batch=batch, sq=seq, lk=lk, n_kv=MLA_HEADS, group=1, dk=MLA_DK_PAD, dv=V_DIM,
                      tq=min(1024, seq), rows=256, q_row_off=0)

    x3, tok1 = _outproj_ln([att1], [m_w_out[0].astype(BF16)], x2, mod[1], ln_g[1, 0].reshape(1, d),
                           ln_b[1, 0].reshape(1, d), t=t_lat, tm=tr, alpha=alpha, mod_row=mod_row)

    (x4,) = _moe(tok1, x3, t_lat, 1, router_w[1], router_b[1], e_w_gate, e_w_up, e_w_down,
                 s_w_gate[1].astype(BF16), s_w_up[1].astype(BF16), s_w_down[1].astype(BF16), mod[1],
                 ln_g[1, 1].reshape(1, d), ln_b[1, 1].reshape(1, d), None, alpha=alpha,
                 mod_row=mod_row, tm=256)
    return x4.reshape(batch, seq, d)
```

```python
import functools
import math

import jax
import jax.numpy as jnp
from jax import lax
from jax.experimental import pallas as pl
from jax.experimental.pallas import tpu as pltpu
from jax.experimental.pallas import tpu_sc as plsc

F32 = jnp.float32
BF16 = jnp.bfloat16

GRID_W = 64
CONV_DIM = 1024
ATT_HEADS = 8
ATT_KV_HEADS = 2
HEAD_DIM = 128
MLA_HEADS = 16
Q_LORA = 512
KV_LORA = 512
QK_NOPE = 128
QK_ROPE = 64
V_DIM = 128
N_EXPERTS = 64
TOP_K = 6
N_GROUPS = 8
TOPK_GROUPS = 4
ROUTED_SCALE = 2.5
ROPE_THETA = 10000.0
LN_EPS = 1e-5
RMS_EPS = 1e-6

V7X_VMEM_LIMIT_BYTES = 56 * 1024 * 1024
LANES = 128
SUBLANES = 8
MOE_ROWS = 256
V7X_SC_CORES = 2
V7X_SC_SUBCORES = 16
SC_GATHER_ROWS = 16
MLA_DK_PAD = 256


def _params(sem):
    return pltpu.CompilerParams(dimension_semantics=sem, vmem_limit_bytes=V7X_VMEM_LIMIT_BYTES)


def _const_spec(shape):
    nd = len(shape)
    return pl.BlockSpec(shape, lambda *_: (0,) * nd)


def _ada_kernel(s_ref, w_ref, b_ref, o_ref):
    s = s_ref[...]
    s = s * (1.0 / (1.0 + jnp.exp(-s)))
    o_ref[...] = jnp.dot(s.astype(BF16), w_ref[...].astype(BF16), preferred_element_type=F32) + b_ref[...]


def _ada_table(cond, w_ada, b_ada):
    depth, d, n = w_ada.shape
    r = cond.shape[0]
    tn = 1024
    return pl.pallas_call(
        _ada_kernel,
        grid=(depth, n // tn),
        in_specs=[
            pl.BlockSpec((r, d), lambda l, j: (0, 0)),
            pl.BlockSpec((None, d, tn), lambda l, j: (l, 0, j)),
            pl.BlockSpec((None, 1, tn), lambda l, j: (l, 0, j)),
        ],
        out_specs=pl.BlockSpec((None, r, tn), lambda l, j: (l, 0, j)),
        out_shape=jax.ShapeDtypeStruct((depth, r, n), F32),
        compiler_params=_params(("parallel", "parallel")),
        name="ada_table",
    )(cond, w_ada, b_ada.reshape(depth, 1, n))


def _mod_spec(d, chunk, mod_row, tm):
    return pl.BlockSpec((None, 1, d), lambda i: (mod_row(i * tm), 0, chunk))


def _modulate_kernel(x_ref, sc_ref, sh_ref, o_ref):
    o_ref[...] = (x_ref[...] * (1.0 + sc_ref[...]) + sh_ref[...]).astype(o_ref.dtype)


def _modulate(x, mod, mod_row, tm):
    t, d = x.shape
    return pl.pallas_call(
        _modulate_kernel,
        grid=(t // tm,),
        in_specs=[
            pl.BlockSpec((tm, d), lambda i: (i, 0)),
            _mod_spec(d, 1, mod_row, tm),
            _mod_spec(d, 0, mod_row, tm),
        ],
        out_specs=pl.BlockSpec((tm, d), lambda i: (i, 0)),
        out_shape=jax.ShapeDtypeStruct((t, d), BF16),
        compiler_params=_params(("parallel",)),
        name="modulate",
    )(x, mod, mod)


def _mm_kernel(a_ref, w_ref, o_ref):
    o_ref[...] = jnp.dot(a_ref[...], w_ref[...], preferred_element_type=F32).astype(o_ref.dtype)


def _matmul(a, w, out_dtype, tm, tn):
    m, k = a.shape
    n = w.shape[1]
    return pl.pallas_call(
        _mm_kernel,
        grid=(m // tm, n // tn),
        in_specs=[
            pl.BlockSpec((tm, k), lambda i, j: (i, 0)),
            pl.BlockSpec((k, tn), lambda i, j: (0, j)),
        ],
        out_specs=pl.BlockSpec((tm, tn), lambda i, j: (i, j)),
        out_shape=jax.ShapeDtypeStruct((m, n), out_dtype),
        compiler_params=_params(("parallel", "parallel")),
        name="matmul",
    )(a, w)


def _rms(t, gain):
    return t * lax.rsqrt(jnp.mean(t * t, axis=-1, keepdims=True) + RMS_EPS) * gain


def _qkprep_kernel(p_ref, cos_ref, sin_ref, qg_ref, kg_ref, q_ref, k_ref, v_ref, *, scale):
    cos = cos_ref[...]
    sin = sin_ref[...]

    def norm_rope(t, gain):
        y = _rms(t.astype(F32), gain)
        return y * cos + pltpu.roll(y, HEAD_DIM // 2, 1) * sin

    for h in range(ATT_HEADS):
        sl = slice(h * HEAD_DIM, (h + 1) * HEAD_DIM)
        q_ref[:, sl] = (norm_rope(p_ref[:, sl], qg_ref[...]) * scale).astype(q_ref.dtype)
    k0 = ATT_HEADS * HEAD_DIM
    for h in range(ATT_KV_HEADS):
        sl = slice(h * HEAD_DIM, (h + 1) * HEAD_DIM)
        k_ref[:, sl] = norm_rope(p_ref[:, k0 + h * HEAD_DIM:k0 + (h + 1) * HEAD_DIM], kg_ref[...]).astype(k_ref.dtype)
    v0 = k0 + ATT_KV_HEADS * HEAD_DIM
    v_ref[...] = p_ref[:, v0:v0 + ATT_KV_HEADS * HEAD_DIM].astype(v_ref.dtype)


def _attn_kernel(q_ref, k_ref, v_ref, o_ref, *, group, tq, rows, dk, dv):
    k = k_ref[...]
    v = v_ref[...]
    for h in range(group):
        for r in range(0, tq, rows):
            q = q_ref[r:r + rows, h * dk:(h + 1) * dk]
            s = lax.dot_general(q, k, (((1,), (1,)), ((), ())), preferred_element_type=F32)
            m = jnp.max(s, axis=-1, keepdims=True)
            p = jnp.exp(s - m)
            l = jnp.sum(p, axis=-1, keepdims=True)
            o = jnp.dot(p.astype(v.dtype), v, preferred_element_type=F32)
            o_ref[r:r + rows, h * dv:(h + 1) * dv] = (o / l).astype(o_ref.dtype)


def _attention(q, k, v, *, batch, sq, lk, n_kv, group, dk, dv, tq, rows, q_row_off):
    nq = sq // tq
    off = q_row_off // tq
    assert tq % rows == 0 and q_row_off % tq == 0 and sq % tq == 0
    return pl.pallas_call(
        functools.partial(_attn_kernel, group=group, tq=tq, rows=rows, dk=dk, dv=dv),
        grid=(batch, n_kv, nq),
        in_specs=[
            pl.BlockSpec((tq, group * dk), lambda b, g, i: (off + b * nq + i, g)),
            pl.BlockSpec((None, lk, dk), lambda b, g, i: (b, 0, g)),
            pl.BlockSpec((None, lk, dv), lambda b, g, i: (b, 0, g)),
        ],
        out_specs=pl.BlockSpec((tq, group * dv), lambda b, g, i: (b * nq + i, g)),
        out_shape=jax.ShapeDtypeStruct((batch * sq, n_kv * group * dv), BF16),
        compiler_params=_params(("parallel", "parallel", "parallel")),
        name="attention",
    )(q, k, v)


def _conv_kernel(gb_ref, gc_ref, hv_ref, gcp_ref, hvp_ref, gcn_ref, hvn_ref, w_ref, o_ref, *,
                 tm, lat_tiles, lat_seq_tiles, ctx_seq_tiles):
    i = pl.program_id(0)
    is_lat = i < lat_tiles
    pos = jnp.where(is_lat, i % lat_seq_tiles, (i - lat_tiles) % ctx_seq_tiles)
    seq_tiles = jnp.where(is_lat, lat_seq_tiles, ctx_seq_tiles)
    not_first = (pos != 0).astype(F32)
    not_last = (pos != seq_tiles - 1).astype(F32)
    p = gc_ref[...].astype(F32) * hv_ref[...].astype(F32)
    halo_prev = gcp_ref[SUBLANES - 1:SUBLANES, :].astype(F32) * hvp_ref[SUBLANES - 1:SUBLANES, :].astype(F32) * not_first
    halo_next = gcn_ref[0:1, :].astype(F32) * hvn_ref[0:1, :].astype(F32) * not_last
    row = lax.broadcasted_iota(jnp.int32, p.shape, 0)
    prev = jnp.where(row == 0, halo_prev, pltpu.roll(p, 1, 0))
    nxt = jnp.where(row == tm - 1, halo_next, pltpu.roll(p, tm - 1, 0))
    w = w_ref[...]
    conv = w[0:1, :] * prev + w[1:2, :] * p + w[2:3, :] * nxt
    o_ref[...] = (gb_ref[...].astype(F32) * conv).astype(o_ref.dtype)


def _conv_gate(p, conv_w, *, t, tm, tc, lat_tiles, lat_seq_tiles, ctx_seq_tiles):
    nct = CONV_DIM // tc
    hb = tm // SUBLANES
    n_halo = t // SUBLANES

    def cur(part):
        return pl.BlockSpec((tm, tc), lambda i, j: (i, part * nct + j))

    def prev(part):
        return pl.BlockSpec((SUBLANES, tc), lambda i, j: (jnp.maximum(i * hb - 1, 0), part * nct + j))

    def nxt(part):
        return pl.BlockSpec((SUBLANES, tc), lambda i, j: (jnp.minimum((i + 1) * hb, n_halo - 1), part * nct + j))

    return pl.pallas_call(
        functools.partial(_conv_kernel, tm=tm, lat_tiles=lat_tiles, lat_seq_tiles=lat_seq_tiles,
                          ctx_seq_tiles=ctx_seq_tiles),
        grid=(t // tm, nct),
        in_specs=[cur(0), cur(1), cur(2), prev(1), prev(2), nxt(1), nxt(2),
                  pl.BlockSpec((3, tc), lambda i, j: (0, j))],
        out_specs=pl.BlockSpec((tm, tc), lambda i, j: (i, j)),
        out_shape=jax.ShapeDtypeStruct((t, CONV_DIM), BF16),
        compiler_params=_params(("parallel", "parallel")),
        name="conv_gate",
    )(p, p, p, p, p, p, p, conv_w)


def _layer_norm(z, g, b):
    mu = jnp.mean(z, axis=-1, keepdims=True)
    zc = z - mu
    var = jnp.mean(zc * zc, axis=-1, keepdims=True)
    return zc * lax.rsqrt(var + LN_EPS) * g + b


def _pack_bf16_pairs(x):
    half = x.shape[1] // 2
    lo = lax.bitcast_convert_type(x[:, :half].astype(BF16).astype(F32), jnp.uint32) >> 16
    hi = lax.bitcast_convert_type(x[:, half:].astype(BF16).astype(F32), jnp.uint32) & jnp.uint32(0xFFFF0000)
    return lax.bitcast_convert_type(lo | hi, jnp.int32)


def _unpack_bf16_pairs(w):
    u = lax.bitcast_convert_type(w, jnp.uint32)
    lo = lax.bitcast_convert_type(u << 16, F32).astype(BF16)
    hi = lax.bitcast_convert_type(u & jnp.uint32(0xFFFF0000), F32).astype(BF16)
    return lo, hi


def _dot_halves(lo, hi, w_ref):
    half = lo.shape[1]
    return (jnp.dot(lo, w_ref[:half, :], preferred_element_type=F32)
            + jnp.dot(hi, w_ref[half:, :], preferred_element_type=F32))


def _outproj_ln_kernel(*refs, n_a, alpha):
    a_refs = refs[:n_a]
    w_refs = refs[n_a:2 * n_a]
    x_ref, gate_ref, lng_ref, lnb_ref, sc_ref, sh_ref, xo_ref, tok_ref = refs[2 * n_a:]
    y = jnp.dot(a_refs[0][...], w_refs[0][...], preferred_element_type=F32)
    for a_ref, w_ref in zip(a_refs[1:], w_refs[1:]):
        y = y + jnp.dot(a_ref[...], w_ref[...], preferred_element_type=F32)
    xn = _layer_norm(alpha * x_ref[...] + gate_ref[...] * y, lng_ref[...], lnb_ref[...])
    xo_ref[...] = xn
    tok_ref[...] = _pack_bf16_pairs(xn * (1.0 + sc_ref[...]) + sh_ref[...])


def _outproj_ln(a_list, w_list, x, mod, ln_g, ln_b, *, t, tm, alpha, mod_row):
    d = x.shape[1]
    n_a = len(a_list)
    in_specs = [pl.BlockSpec((tm, a.shape[1]), lambda i: (i, 0)) for a in a_list]
    in_specs += [_const_spec(w.shape) for w in w_list]
    in_specs += [
        pl.BlockSpec((tm, d), lambda i: (i, 0)),
        _mod_spec(d, 2, mod_row, tm),
        _const_spec((1, d)), _const_spec((1, d)),
        _mod_spec(d, 4, mod_row, tm),
        _mod_spec(d, 3, mod_row, tm),
    ]
    return pl.pallas_call(
        functools.partial(_outproj_ln_kernel, n_a=n_a, alpha=alpha),
        grid=(t // tm,),
        in_specs=in_specs,
        out_specs=[pl.BlockSpec((tm, d), lambda i: (i, 0)), pl.BlockSpec((tm, d // 2), lambda i: (i, 0))],
        out_shape=[jax.ShapeDtypeStruct((t, d), F32), jax.ShapeDtypeStruct((t, d // 2), jnp.int32)],
        compiler_params=_params(("parallel",)),
        name="outproj_ln",
    )(*a_list, *w_list, x, mod, ln_g, ln_b, mod, mod)


def _router_kernel(t_ref, rw_ref, rb_ref, tri_ref, idx_ref, gw_ref, rank_ref, cnt_ref):
    @pl.when(pl.program_id(0) == 0)
    def _():
        cnt_ref[...] = jnp.zeros_like(cnt_ref)

    lo, hi = _unpack_bf16_pairs(t_ref[...])
    half = lo.shape[1]
    nt = (((1,), (1,)), ((), ()))
    logits = (lax.dot_general(rw_ref[:, :half], lo, nt, preferred_element_type=F32)
              + lax.dot_general(rw_ref[:, half:], hi, nt, preferred_element_type=F32))
    scores = 1.0 / (1.0 + jnp.exp(-logits))
    sel = scores + rb_ref[...]
    gsz = N_EXPERTS // N_GROUPS
    neg = -jnp.inf
    sub = lax.broadcasted_iota(jnp.int32, (gsz, sel.shape[1]), 0)
    slabs = [sel[g * gsz:(g + 1) * gsz, :] for g in range(N_GROUPS)]
    gscore = []
    for s in slabs:
        m1 = jnp.max(s, axis=0, keepdims=True)
        a1 = jnp.min(jnp.where(s == m1, sub, gsz), axis=0, keepdims=True)
        m2 = jnp.max(jnp.where(sub == a1, neg, s), axis=0, keepdims=True)
        gscore.append(m1 + m2)
    masked = []
    for g in range(N_GROUPS):
        ahead = jnp.zeros(gscore[g].shape, jnp.int32)
        for h in range(N_GROUPS):
            if h == g:
                continue
            beats = gscore[h] >= gscore[g] if h < g else gscore[h] > gscore[g]
            ahead = ahead + beats.astype(jnp.int32)
        masked.append(jnp.where(ahead < TOPK_GROUPS, slabs[g], neg))
    cur = jnp.concatenate(masked, axis=0)
    eio = lax.broadcasted_iota(jnp.int32, cur.shape, 0)
    picks, weights = [], []
    for _ in range(TOP_K):
        m = jnp.max(cur, axis=0, keepdims=True)
        a = jnp.min(jnp.where(cur == m, eio, N_EXPERTS), axis=0, keepdims=True)
        hit = eio == a
        picks.append(a)
        weights.append(jnp.sum(jnp.where(hit, scores, 0.0), axis=0, keepdims=True))
        cur = jnp.where(hit, neg, cur)
    total = weights[0]
    for w in weights[1:]:
        total = total + w
    for k in range(TOP_K):
        idx_ref[k:k + 1, :] = picks[k]
        gw_ref[k:k + 1, :] = weights[k] / total * ROUTED_SCALE
    for k in range(TOP_K, SUBLANES):
        idx_ref[k:k + 1, :] = jnp.zeros_like(picks[0])
        gw_ref[k:k + 1, :] = jnp.zeros_like(weights[0])
        rank_ref[k:k + 1, :] = jnp.zeros_like(picks[0])
    base = cnt_ref[:, 0:1]
    for k in range(TOP_K):
        onehot = jnp.where(eio == picks[k], 1.0, 0.0)
        before = jnp.dot(onehot.astype(BF16), tri_ref[...], preferred_element_type=F32)
        rank_ref[k:k + 1, :] = jnp.sum(onehot * (before + base), axis=0, keepdims=True).astype(jnp.int32)
        base = base + jnp.sum(onehot, axis=1, keepdims=True)
    cnt_ref[...] = jnp.broadcast_to(base, cnt_ref.shape)


def _router(tok, rw_t, rb, *, t, tt):
    half = tok.shape[1]
    tri = (jnp.arange(tt)[:, None] < jnp.arange(tt)[None, :]).astype(BF16)
    blk = pl.BlockSpec((SUBLANES, tt), lambda i: (0, i))
    return pl.pallas_call(
        _router_kernel,
        grid=(t // tt,),
        in_specs=[
            pl.BlockSpec((tt, half), lambda i: (i, 0)),
            _const_spec((N_EXPERTS, 2 * half)),
            _const_spec((N_EXPERTS, 1)),
            _const_spec((tt, tt)),
        ],
        out_specs=[blk, blk, blk, _const_spec((N_EXPERTS, LANES))],
        out_shape=[jax.ShapeDtypeStruct((SUBLANES, t), jnp.int32), jax.ShapeDtypeStruct((SUBLANES, t), F32),
                   jax.ShapeDtypeStruct((SUBLANES, t), jnp.int32), jax.ShapeDtypeStruct((N_EXPERTS, LANES), F32)],
        compiler_params=_params(("arbitrary",)),
        name="router",
    )(tok, rw_t, rb, tri)


def _slots_kernel(idx_ref, rank_ref, start_ref, pos_ref):
    start = start_ref[...]
    eio = lax.broadcasted_iota(jnp.int32, (N_EXPERTS, idx_ref.shape[1]), 0)
    for k in range(TOP_K):
        seg = jnp.sum(jnp.where(eio == idx_ref[k:k + 1, :], start, 0.0), axis=0, keepdims=True)
        pos_ref[k:k + 1, :] = rank_ref[k:k + 1, :] + seg.astype(jnp.int32)
    for k in range(TOP_K, SUBLANES):
        pos_ref[k:k + 1, :] = jnp.zeros((1, idx_ref.shape[1]), jnp.int32)


def _assign_slots(idx, rank, seg_start, *, t, tt):
    blk = pl.BlockSpec((SUBLANES, tt), lambda i: (0, i))
    return pl.pallas_call(
        _slots_kernel,
        grid=(t // tt,),
        in_specs=[blk, blk, _const_spec((N_EXPERTS, 1))],
        out_specs=blk,
        out_shape=jax.ShapeDtypeStruct((SUBLANES, t), jnp.int32),
        compiler_params=_params(("parallel",)),
        name="assign_slots",
    )(idx, rank, seg_start)


def _sc_gather_rows(table, idx):
    n = idx.shape[0]
    d = table.shape[1]
    n_workers = V7X_SC_CORES * V7X_SC_SUBCORES
    per_w = n // n_workers
    n_chunks = per_w // SC_GATHER_ROWS
    assert per_w * n_workers == n and n_chunks * SC_GATHER_ROWS == per_w and n_chunks % 2 == 0
    mesh = plsc.VectorSubcoreMesh(core_axis_name="c", subcore_axis_name="s", num_cores=V7X_SC_CORES,
                                  num_subcores=V7X_SC_SUBCORES)

    @functools.partial(
        pl.kernel,
        out_type=jax.ShapeDtypeStruct((n, d), table.dtype),
        mesh=mesh,
        scratch_types=[
            pltpu.VMEM((per_w,), jnp.int32),
            pltpu.VMEM((2, SC_GATHER_ROWS, d), table.dtype),
            pltpu.SemaphoreType.DMA((2,)),
            pltpu.SemaphoreType.DMA((2,)),
        ],
        name="sc_gather_rows",
    )
    def gather(table_hbm, idx_hbm, out_hbm, idx_v, rows_v, gsem, wsem):
        wid = lax.axis_index("s") * V7X_SC_CORES + lax.axis_index("c")
        base = wid * per_w
        pltpu.sync_copy(idx_hbm.at[pl.ds(base, per_w)], idx_v)

        def gather_copy(c, b):
            return pltpu.make_async_copy(table_hbm.at[idx_v.at[pl.ds(c * SC_GATHER_ROWS, SC_GATHER_ROWS)]],
                                         rows_v.at[b], gsem.at[b])

        def write_copy(c, b):
            return pltpu.make_async_copy(rows_v.at[b], out_hbm.at[pl.ds(base + c * SC_GATHER_ROWS, SC_GATHER_ROWS)],
                                         wsem.at[b])

        gather_copy(0, 0).start()

        @pl.loop(0, n_chunks, step=2)
        def _(g):
            for b in range(2):
                c = g + b
                nb = 1 - b

                @pl.when(c >= 1)
                def _():
                    write_copy(c - 1, nb).wait()

                @pl.when(c + 1 < n_chunks)
                def _():
                    gather_copy(c + 1, nb).start()

                gather_copy(c, b).wait()
                write_copy(c, b).start()

        write_copy(n_chunks - 1, (n_chunks - 1) % 2).wait()

    return gather(table, idx)


def _experts_kernel(be_ref, nbu_ref, x_ref, wg_ref, wu_ref, wd_ref, y_ref, wgb, wub, wdb):
    b = pl.program_id(0)
    nbu = nbu_ref[0]

    @pl.when(b < nbu)
    def _():
        changed = jnp.logical_or(b == 0, be_ref[b] != be_ref[jnp.maximum(b - 1, 0)])

        @pl.when(changed)
        def _():
            wgb[...] = wg_ref[...].astype(BF16)
            wub[...] = wu_ref[...].astype(BF16)
            wdb[...] = wd_ref[...].astype(BF16)

        lo, hi = _unpack_bf16_pairs(x_ref[...])
        hg = _dot_halves(lo, hi, wgb)
        hu = _dot_halves(lo, hi, wub)
        h = hg * (1.0 / (1.0 + jnp.exp(-hg))) * hu
        y_ref[...] = _pack_bf16_pairs(jnp.dot(h.astype(BF16), wdb[...], preferred_element_type=F32))

    @pl.when(b >= nbu)
    def _():
        y_ref[...] = jnp.zeros_like(y_ref)


def _experts(xs, block_e, nb_used, wg, wu, wd, layer, *, n_blocks):
    half = xs.shape[1]
    d = 2 * half
    ff = wg.shape[3]

    def used(b, nbu):
        return jnp.minimum(b, jnp.maximum(nbu[0] - 1, 0))

    grid_spec = pltpu.PrefetchScalarGridSpec(
        num_scalar_prefetch=2,
        grid=(n_blocks,),
        in_specs=[
            pl.BlockSpec((MOE_ROWS, half), lambda b, be, nbu: (used(b, nbu), 0)),
            pl.BlockSpec((None, None, d, ff), lambda b, be, nbu: (layer, be[b], 0, 0)),
            pl.BlockSpec((None, None, d, ff), lambda b, be, nbu: (layer, be[b], 0, 0)),
            pl.BlockSpec((None, None, ff, d), lambda b, be, nbu: (layer, be[b], 0, 0)),
        ],
        out_specs=pl.BlockSpec((MOE_ROWS, half), lambda b, be, nbu: (b, 0)),
        scratch_shapes=[
            pltpu.VMEM((d, ff), BF16),
            pltpu.VMEM((d, ff), BF16),
            pltpu.VMEM((ff, d), BF16),
        ],
    )
    return pl.pallas_call(
        _experts_kernel,
        grid_spec=grid_spec,
        out_shape=jax.ShapeDtypeStruct((n_blocks * MOE_ROWS, half), jnp.int32),
        compiler_params=_params(("arbitrary",)),
        name="experts",
    )(block_e, nb_used, xs, wg, wu, wd)


def _combine_ln_kernel(*refs, alpha, emit_next):
    y_ref, gw_ref, tok_ref, x_ref, sg_ref, su_ref, sd_ref, gate_ref, lng_ref, lnb_ref = refs[:10]
    if emit_next:
        sc_ref, sh_ref, xo_ref, u_ref = refs[10:]
    else:
        (xo_ref,) = refs[10:]
    lo, hi = _unpack_bf16_pairs(tok_ref[...])
    hg = _dot_halves(lo, hi, sg_ref)
    hu = _dot_halves(lo, hi, su_ref)
    h = hg * (1.0 / (1.0 + jnp.exp(-hg))) * hu
    gw = gw_ref[...]
    f_lo = f_hi = None
    for k in range(TOP_K):
        y_lo, y_hi = _unpack_bf16_pairs(y_ref[k])
        w = gw[:, k:k + 1]
        f_lo = y_lo.astype(F32) * w if f_lo is None else f_lo + y_lo.astype(F32) * w
        f_hi = y_hi.astype(F32) * w if f_hi is None else f_hi + y_hi.astype(F32) * w
    f = jnp.concatenate([f_lo, f_hi], axis=-1) + jnp.dot(h.astype(BF16), sd_ref[...], preferred_element_type=F32)
    xn = _layer_norm(alpha * x_ref[...] + gate_ref[...] * f, lng_ref[...], lnb_ref[...])
    xo_ref[...] = xn
    if emit_next:
        u_ref[...] = (xn * (1.0 + sc_ref[...]) + sh_ref[...]).astype(u_ref.dtype)


def _combine_ln(y3, gw_t, tok, x, sg, su, sd, mod, ln_g, ln_b, mod_next, *, t, tm, alpha, mod_row):
    d = x.shape[1]
    emit_next = mod_next is not None
    in_specs = [
        pl.BlockSpec((TOP_K, tm, d // 2), lambda i: (0, i, 0)),
        pl.BlockSpec((tm, SUBLANES), lambda i: (i, 0)),
        pl.BlockSpec((tm, d // 2), lambda i: (i, 0)),
        pl.BlockSpec((tm, d), lambda i: (i, 0)),
        _const_spec(sg.shape), _const_spec(su.shape), _const_spec(sd.shape),
        _mod_spec(d, 5, mod_row, tm),
        _const_spec((1, d)), _const_spec((1, d)),
    ]
    args = [y3, gw_t, tok, x, sg, su, sd, mod, ln_g, ln_b]
    out_specs = [pl.BlockSpec((tm, d), lambda i: (i, 0))]
    out_shape = [jax.ShapeDtypeStruct((t, d), F32)]
    if emit_next:
        in_specs += [_mod_spec(d, 1, mod_row, tm), _mod_spec(d, 0, mod_row, tm)]
        args += [mod_next, mod_next]
        out_specs.append(pl.BlockSpec((tm, d), lambda i: (i, 0)))
        out_shape.append(jax.ShapeDtypeStruct((t, d), BF16))
    return pl.pallas_call(
        functools.partial(_combine_ln_kernel, alpha=alpha, emit_next=emit_next),
        grid=(t // tm,),
        in_specs=in_specs,
        out_specs=out_specs,
        out_shape=out_shape,
        compiler_params=_params(("parallel",)),
        name="combine_ln",
    )(*args)


def _moe(tok, x, t, layer, router_w, router_b, wg, wu, wd, sg, su, sd, mod, ln_g, ln_b, mod_next, *, alpha, mod_row,
         tm):
    half = tok.shape[1]
    tt = 512
    idx, gw, rank, cnt = _router(tok, router_w.T.astype(BF16), router_b.reshape(N_EXPERTS, 1), t=t, tt=tt)
    n_asg = t * TOP_K
    counts = cnt[:, 0].astype(jnp.int32)
    padded = (counts + MOE_ROWS - 1) // MOE_ROWS * MOE_ROWS
    pend = jnp.cumsum(padded)
    pstart = pend - padded
    ustart = jnp.cumsum(counts) - counts
    sc_rows = V7X_SC_CORES * V7X_SC_SUBCORES * SC_GATHER_ROWS * 2
    blocks_granule = sc_rows // MOE_ROWS
    n_blocks = -(-((n_asg + N_EXPERTS * (MOE_ROWS - 1)) // MOE_ROWS + 1) // blocks_granule) * blocks_granule
    assert n_asg % sc_rows == 0
    block_start = jnp.arange(n_blocks, dtype=jnp.int32) * MOE_ROWS
    block_e = jnp.minimum(jnp.sum((pend[None, :] <= block_start[:, None]).astype(jnp.int32), axis=1), N_EXPERTS - 1)
    nb_used = (pend[-1] // MOE_ROWS).astype(jnp.int32).reshape(1)
    pos = _assign_slots(idx, rank, pstart.astype(F32).reshape(N_EXPERTS, 1), t=t, tt=tt)[:TOP_K].reshape(-1)
    tok_of_asg = jnp.tile(jnp.arange(t, dtype=jnp.int32), TOP_K)
    _, tok_sorted = lax.sort((pos, tok_of_asg), num_keys=1)
    block_valid = jnp.clip(counts[block_e] - (block_start - pstart[block_e]), 0, MOE_ROWS)
    block_shift = (pstart - ustart)[block_e]
    row = jnp.arange(MOE_ROWS, dtype=jnp.int32)[None, :]
    compact = jnp.clip(block_start[:, None] + row - block_shift[:, None], 0, n_asg - 1)
    slot_tok = jnp.where(row < block_valid[:, None], tok_sorted[compact], 0).reshape(-1)
    xs = _sc_gather_rows(tok, slot_tok)
    y = _experts(xs, block_e, nb_used, wg, wu, wd, layer, n_blocks=n_blocks)
    y3 = _sc_gather_rows(y, pos).reshape(TOP_K, t, half)
    return _combine_ln(y3, gw.T, tok, x, sg, su, sd, mod, ln_g, ln_b, mod_next, t=t, tm=tm, alpha=alpha,
                       mod_row=mod_row)


def _rope64(r, c_ref, sa_ref, sb_ref):
    return r * c_ref[...] + pltpu.roll(r, LANES - QK_ROPE // 2, 1) * sa_ref[...] + pltpu.roll(r, QK_ROPE // 2, 1) * sb_ref[...]


def _mla_q_kernel(d_ref, gain_ref, w_ref, c_ref, sa_ref, sb_ref, q_ref, *, scale):
    n = _rms(d_ref[...], gain_ref[...]).astype(BF16)
    q = jnp.dot(n, w_ref[...], preferred_element_type=F32)
    for h in range(MLA_HEADS):
        lo = h * MLA_DK_PAD
        q_ref[:, lo:lo + QK_NOPE] = (q[:, lo:lo + QK_NOPE] * scale).astype(q_ref.dtype)
        r = _rope64(q[:, lo + QK_NOPE:lo + MLA_DK_PAD], c_ref, sa_ref, sb_ref)
        q_ref[:, lo + QK_NOPE:lo + MLA_DK_PAD] = (r * scale).astype(q_ref.dtype)


def _mla_kv_kernel(ckv_ref, kr_ref, gain_ref, wk_ref, wv_ref, c_ref, sa_ref, sb_ref, k_ref, v_ref):
    n = _rms(ckv_ref[...], gain_ref[...]).astype(BF16)
    kn = jnp.dot(n, wk_ref[...], preferred_element_type=F32)
    v_ref[...] = jnp.dot(n, wv_ref[...], preferred_element_type=F32).astype(v_ref.dtype)
    kr = _rope64(kr_ref[...], c_ref, sa_ref, sb_ref).astype(k_ref.dtype)
    for h in range(MLA_HEADS):
        lo = h * MLA_DK_PAD
        k_ref[:, lo:lo + QK_NOPE] = kn[:, h * QK_NOPE:(h + 1) * QK_NOPE].astype(k_ref.dtype)
        k_ref[:, lo + QK_NOPE:lo + MLA_DK_PAD] = kr


def _axial_angles(n_tok, rot_dim):
    rows = n_tok // GRID_W
    n_freq = rot_dim // 4
    inv = ROPE_THETA ** (-jnp.arange(n_freq, dtype=F32) / n_freq)
    row = jnp.repeat(jnp.arange(rows, dtype=F32), GRID_W)
    col = jnp.tile(jnp.arange(GRID_W, dtype=F32), rows)
    return jnp.concatenate([row[:, None] * inv, col[:, None] * inv], axis=-1)


def _rope_tables_128(n_tok, ident_rows):
    ang = _axial_angles(n_tok, HEAD_DIM)
    cos, sin = jnp.cos(ang), jnp.sin(ang)
    c = jnp.concatenate([cos, cos], axis=-1)
    s = jnp.concatenate([-sin, sin], axis=-1)
    c = jnp.concatenate([c, jnp.ones((ident_rows, HEAD_DIM), F32)], axis=0)
    s = jnp.concatenate([s, jnp.zeros((ident_rows, HEAD_DIM), F32)], axis=0)
    return c, s


def _rope_tables_64(n_tok, ident_rows):
    ang = _axial_angles(n_tok, QK_ROPE)
    cos, sin = jnp.cos(ang), jnp.sin(ang)
    half = QK_ROPE // 2
    z = jnp.zeros((n_tok, LANES - QK_ROPE), F32)
    zh = jnp.zeros((n_tok, half), F32)
    c = jnp.concatenate([cos, cos, z], axis=-1)
    sa = jnp.concatenate([-sin, zh, z], axis=-1)
    sb = jnp.concatenate([zh, sin, z], axis=-1)
    ci = jnp.concatenate([jnp.ones((ident_rows, QK_ROPE), F32), jnp.zeros((ident_rows, LANES - QK_ROPE), F32)], axis=-1)
    zi = jnp.zeros((ident_rows, LANES), F32)
    return jnp.concatenate([c, ci], 0), jnp.concatenate([sa, zi], 0), jnp.concatenate([sb, zi], 0)


def kernel(x, c, ctx, c_ctx, w_ada, b_ada, ln_g, ln_b, a_w_in, a_conv_w, a_q_gain, a_k_gain, a_w_out, m_w_down, m_q_gain, m_kv_gain, m_w_uq, m_w_ukv, m_w_out, router_w, router_b, e_w_gate, e_w_up, e_w_down, s_w_gate, s_w_up, s_w_down):
    batch, seq, d = x.shape
    ctx_len = ctx.shape[1]
    depth = w_ada.shape[0]
    assert depth == 2, "one conv+GQA layer followed by one MLA layer"
    alpha = (2 * depth) ** 0.25
    t_lat = batch * seq
    t_ctx = batch * ctx_len
    t_all = t_lat + t_ctx
    tr = 256
    assert seq % tr == 0 and ctx_len % tr == 0 and seq % GRID_W == 0
    lat_tiles = t_lat // tr
    lat_seq_tiles = seq // tr
    ctx_seq_tiles = ctx_len // tr
    lk = ctx_len + seq

    def mod_row(r):
        return jnp.minimum(r // seq, batch)

    def kv_block(i):
        is_lat = i < lat_tiles
        cidx = i - lat_tiles
        b = jnp.where(is_lat, i // lat_seq_tiles, cidx // ctx_seq_tiles)
        rb = jnp.where(is_lat, ctx_seq_tiles + i % lat_seq_tiles, cidx % ctx_seq_tiles)
        return b, rb

    def pos_block(i):
        return jnp.where(i < lat_tiles, i % lat_seq_tiles, lat_seq_tiles)

    rows = -(-(batch + 1) // SUBLANES) * SUBLANES
    cond = jnp.concatenate([c, c_ctx[None, :], jnp.zeros((rows - batch - 1, d), F32)], axis=0)
    mod = _ada_table(cond, w_ada, b_ada).reshape(depth, rows, 1, 6 * d)

    x_all = jnp.concatenate([x.reshape(t_lat, d), ctx.reshape(t_ctx, d)], axis=0)

    u0 = _modulate(x_all, mod[0], mod_row, tr)
    proj = _matmul(u0, a_w_in[0].astype(BF16), BF16, 1024 if t_all % 1024 == 0 else 512,
                   768)

    cos128, sin128 = _rope_tables_128(seq, tr)
    d_q = ATT_HEADS * HEAD_DIM
    d_kv = ATT_KV_HEADS * HEAD_DIM
    qkv_w = d_q + 2 * d_kv
    qkv_blk = 3 * CONV_DIM // qkv_w
    assert qkv_blk * qkv_w == 3 * CONV_DIM
    q0, k0, v0 = pl.pallas_call(
        functools.partial(_qkprep_kernel, scale=1.0 / math.sqrt(HEAD_DIM)),
        grid=(t_all // tr,),
        in_specs=[
            pl.BlockSpec((tr, qkv_w), lambda i: (i, qkv_blk)),
            pl.BlockSpec((tr, HEAD_DIM), lambda i: (pos_block(i), 0)),
            pl.BlockSpec((tr, HEAD_DIM), lambda i: (pos_block(i), 0)),
            _const_spec((1, HEAD_DIM)), _const_spec((1, HEAD_DIM)),
        ],
        out_specs=[
            pl.BlockSpec((tr, d_q), lambda i: (i, 0)),
            pl.BlockSpec((None, tr, d_kv), lambda i: (*kv_block(i), 0)),
            pl.BlockSpec((None, tr, d_kv), lambda i: (*kv_block(i), 0)),
        ],
        out_shape=[
            jax.ShapeDtypeStruct((t_all, d_q), BF16),
            jax.ShapeDtypeStruct((batch, lk, d_kv), BF16),
            jax.ShapeDtypeStruct((batch, lk, d_kv), BF16),
        ],
        compiler_params=_params(("parallel",)),
        name="qk_prep",
    )(proj, cos128, sin128, a_q_gain[0].reshape(1, HEAD_DIM), a_k_gain[0].reshape(1, HEAD_DIM))

    grp = ATT_HEADS // ATT_KV_HEADS
    att_lat = _attention(q0, k0, v0, batch=batch, sq=seq, lk=lk, n_kv=ATT_KV_HEADS, group=grp, dk=HEAD_DIM,
                         dv=HEAD_DIM, tq=256, rows=256, q_row_off=0)
    att_ctx = _attention(q0, k0, v0, batch=batch, sq=ctx_len, lk=ctx_len, n_kv=ATT_KV_HEADS, group=grp,
                         dk=HEAD_DIM, dv=HEAD_DIM, tq=256, rows=256, q_row_off=t_lat)
    att0 = jnp.concatenate([att_lat, att_ctx], axis=0)

    conv0 = _conv_gate(proj, a_conv_w[0], t=t_all, tm=tr, tc=512, lat_tiles=lat_tiles,
                       lat_seq_tiles=lat_seq_tiles, ctx_seq_tiles=ctx_seq_tiles)

    w_out0 = a_w_out[0].astype(BF16)
    x1, tok0 = _outproj_ln([conv0, att0], [w_out0[:CONV_DIM], w_out0[CONV_DIM:]], x_all, mod[0],
                           ln_g[0, 0].reshape(1, d), ln_b[0, 0].reshape(1, d), t=t_all, tm=tr, alpha=alpha,
                           mod_row=mod_row)

    x2, u1 = _moe(tok0, x1, t_all, 0, router_w[0], router_b[0], e_w_gate, e_w_up, e_w_down,
                  s_w_gate[0].astype(BF16), s_w_up[0].astype(BF16), s_w_down[0].astype(BF16), mod[0],
                  ln_g[0, 1].reshape(1, d), ln_b[0, 1].reshape(1, d), mod[1], alpha=alpha,
                  mod_row=mod_row, tm=256)

    n_down = Q_LORA + KV_LORA + QK_ROPE
    n_down_pad = -(-n_down // LANES) * LANES
    w_down = jnp.pad(m_w_down[0], ((0, 0), (0, n_down_pad - n_down))).astype(BF16)
    down = _matmul(u1, w_down, F32, 512, n_down_pad)

    dqk = QK_NOPE + QK_ROPE
    w_uq = m_w_uq[0].reshape(Q_LORA, MLA_HEADS, dqk)
    w_uq = jnp.pad(w_uq, ((0, 0), (0, 0), (0, MLA_DK_PAD - dqk))).reshape(Q_LORA, MLA_HEADS * MLA_DK_PAD).astype(BF16)
    w_ukv = m_w_ukv[0].reshape(KV_LORA, MLA_HEADS, QK_NOPE + V_DIM)
    w_uk = w_ukv[:, :, :QK_NOPE].reshape(KV_LORA, MLA_HEADS * QK_NOPE).astype(BF16)
    w_uv = w_ukv[:, :, QK_NOPE:].reshape(KV_LORA, MLA_HEADS * V_DIM).astype(BF16)

    c64, sa64, sb64 = _rope_tables_64(seq, tr)
    rope_specs = [pl.BlockSpec((tr, LANES), lambda i: (pos_block(i), 0))] * 3
    q1 = pl.pallas_call(
        functools.partial(_mla_q_kernel, scale=1.0 / math.sqrt(dqk)),
        grid=(lat_tiles,),
        in_specs=[
            pl.BlockSpec((tr, Q_LORA), lambda i: (i, 0)),
            _const_spec((1, Q_LORA)),
            _const_spec(w_uq.shape),
        ] + rope_specs,
        out_specs=pl.BlockSpec((tr, MLA_HEADS * MLA_DK_PAD), lambda i: (i, 0)),
        out_shape=jax.ShapeDtypeStruct((t_lat, MLA_HEADS * MLA_DK_PAD), BF16),
        compiler_params=_params(("parallel",)),
        name="mla_q",
    )(down, m_q_gain[0].reshape(1, Q_LORA), w_uq, c64, sa64, sb64)

    assert KV_LORA == Q_LORA and (Q_LORA + KV_LORA) % LANES == 0
    k1, v1 = pl.pallas_call(
        _mla_kv_kernel,
        grid=(t_all // tr,),
        in_specs=[
            pl.BlockSpec((tr, KV_LORA), lambda i: (i, 1)),
            pl.BlockSpec((tr, LANES), lambda i: (i, (Q_LORA + KV_LORA) // LANES)),
            _const_spec((1, KV_LORA)),
            _const_spec(w_uk.shape), _const_spec(w_uv.shape),
        ] + rope_specs,
        out_specs=[
            pl.BlockSpec((None, tr, MLA_HEADS * MLA_DK_PAD), lambda i: (*kv_block(i), 0)),
            pl.BlockSpec((None, tr, MLA_HEADS * V_DIM), lambda i: (*kv_block(i), 0)),
        ],
        out_shape=[
            jax.ShapeDtypeStruct((batch, lk, MLA_HEADS * MLA_DK_PAD), BF16),
            jax.ShapeDtypeStruct((batch, lk, MLA_HEADS * V_DIM), BF16),
        ],
        compiler_params=_params(("parallel",)),
        name="mla_kv",
    )(down, down, m_kv_gain[0].reshape(1, KV_LORA), w_uk, w_uv, c64, sa64, sb64)

    att1 = _attention(q1, k1, v1, batch=batch, sq=seq, lk=lk, n_kv=MLA_HEADS, group=1, dk=MLA_DK_PAD, dv=V_DIM,
                      tq=min(1024, seq), rows=256, q_row_off=0)

    x3, tok1 = _outproj_ln([att1], [m_w_out[0].astype(BF16)], x2, mod[1], ln_g[1, 0].reshape(1, d),
                           ln_b[1, 0].reshape(1, d), t=t_lat, tm=tr, alpha=alpha, mod_row=mod_row)

    (x4,) = _moe(tok1, x3, t_lat, 1, router_w[1], router_b[1], e_w_gate, e_w_up, e_w_down,
                 s_w_gate[1].astype(BF16), s_w_up[1].astype(BF16), s_w_down[1].astype(BF16), mod[1],
                 ln_g[1, 1].reshape(1, d), ln_b[1, 1].reshape(1, d), None, alpha=alpha,
                 mod_row=mod_row, tm=256)
    return x4.reshape(batch, seq, d)
```

```python
import functools
import math

import jax
import jax.numpy as jnp
from jax import lax
from jax.experimental import pallas as pl
from jax.experimental.pallas import tpu as pltpu
from jax.experimental.pallas import tpu_sc as plsc

F32 = jnp.float32
BF16 = jnp.bfloat16

GRID_W = 64
CONV_DIM = 1024
ATT_HEADS = 8
ATT_KV_HEADS = 2
HEAD_DIM = 128
MLA_HEADS = 16
Q_LORA = 512
KV_LORA = 512
QK_NOPE = 128
QK_ROPE = 64
V_DIM = 128
N_EXPERTS = 64
TOP_K = 6
N_GROUPS = 8
TOPK_GROUPS = 4
ROUTED_SCALE = 2.5
ROPE_THETA = 10000.0
LN_EPS = 1e-5
RMS_EPS = 1e-6

V7X_VMEM_LIMIT_BYTES = 56 * 1024 * 1024
LANES = 128
SUBLANES = 8
MOE_ROWS = 256
V7X_SC_CORES = 2
V7X_SC_SUBCORES = 16
SC_GATHER_ROWS = 16
SC_GATHER_BUFFERS = 4
MLA_DK_PAD = 256


def _params(sem):
    return pltpu.CompilerParams(dimension_semantics=sem, vmem_limit_bytes=V7X_VMEM_LIMIT_BYTES)


def _const_spec(shape):
    nd = len(shape)
    return pl.BlockSpec(shape, lambda *_: (0,) * nd)


def _ada_kernel(s_ref, w_ref, b_ref, o_ref):
    s = s_ref[...]
    s = s * (1.0 / (1.0 + jnp.exp(-s)))
    o_ref[...] = jnp.dot(s.astype(BF16), w_ref[...].astype(BF16), preferred_element_type=F32) + b_ref[...]


def _ada_table(cond, w_ada, b_ada):
    depth, d, n = w_ada.shape
    r = cond.shape[0]
    tn = 1024
    return pl.pallas_call(
        _ada_kernel,
        grid=(depth, n // tn),
        in_specs=[
            pl.BlockSpec((r, d), lambda l, j: (0, 0)),
            pl.BlockSpec((None, d, tn), lambda l, j: (l, 0, j)),
            pl.BlockSpec((None, 1, tn), lambda l, j: (l, 0, j)),
        ],
        out_specs=pl.BlockSpec((None, r, tn), lambda l, j: (l, 0, j)),
        out_shape=jax.ShapeDtypeStruct((depth, r, n), F32),
        compiler_params=_params(("parallel", "parallel")),
        name="ada_table",
    )(cond, w_ada, b_ada.reshape(depth, 1, n))


def _mod_spec(d, chunk, mod_row, tm):
    return pl.BlockSpec((None, 1, d), lambda i: (mod_row(i * tm), 0, chunk))


def _modulate_kernel(x_ref, sc_ref, sh_ref, o_ref):
    o_ref[...] = (x_ref[...] * (1.0 + sc_ref[...]) + sh_ref[...]).astype(o_ref.dtype)


def _modulate(x, mod, mod_row, tm):
    t, d = x.shape
    return pl.pallas_call(
        _modulate_kernel,
        grid=(t // tm,),
        in_specs=[
            pl.BlockSpec((tm, d), lambda i: (i, 0)),
            _mod_spec(d, 1, mod_row, tm),
            _mod_spec(d, 0, mod_row, tm),
        ],
        out_specs=pl.BlockSpec((tm, d), lambda i: (i, 0)),
        out_shape=jax.ShapeDtypeStruct((t, d), BF16),
        compiler_params=_params(("parallel",)),
        name="modulate",
    )(x, mod, mod)


def _mm_kernel(a_ref, w_ref, o_ref):
    o_ref[...] = jnp.dot(a_ref[...], w_ref[...], preferred_element_type=F32).astype(o_ref.dtype)


def _matmul(a, w, out_dtype, tm, tn):
    m, k = a.shape
    n = w.shape[1]
    return pl.pallas_call(
        _mm_kernel,
        grid=(m // tm, n // tn),
        in_specs=[
            pl.BlockSpec((tm, k), lambda i, j: (i, 0)),
            pl.BlockSpec((k, tn), lambda i, j: (0, j)),
        ],
        out_specs=pl.BlockSpec((tm, tn), lambda i, j: (i, j)),
        out_shape=jax.ShapeDtypeStruct((m, n), out_dtype),
        compiler_params=_params(("parallel", "parallel")),
        name="matmul",
    )(a, w)


def _rms(t, gain):
    return t * lax.rsqrt(jnp.mean(t * t, axis=-1, keepdims=True) + RMS_EPS) * gain


def _qkprep_kernel(p_ref, cos_ref, sin_ref, qg_ref, kg_ref, q_ref, k_ref, v_ref, *, scale):
    cos = cos_ref[...]
    sin = sin_ref[...]

    def norm_rope(t, gain):
        y = _rms(t.astype(F32), gain)
        return y * cos + pltpu.roll(y, HEAD_DIM // 2, 1) * sin

    for h in range(ATT_HEADS):
        sl = slice(h * HEAD_DIM, (h + 1) * HEAD_DIM)
        q_ref[:, sl] = (norm_rope(p_ref[:, sl], qg_ref[...]) * scale).astype(q_ref.dtype)
    k0 = ATT_HEADS * HEAD_DIM
    for h in range(ATT_KV_HEADS):
        sl = slice(h * HEAD_DIM, (h + 1) * HEAD_DIM)
        k_ref[:, sl] = norm_rope(p_ref[:, k0 + h * HEAD_DIM:k0 + (h + 1) * HEAD_DIM], kg_ref[...]).astype(k_ref.dtype)
    v0 = k0 + ATT_KV_HEADS * HEAD_DIM
    v_ref[...] = p_ref[:, v0:v0 + ATT_KV_HEADS * HEAD_DIM].astype(v_ref.dtype)


def _attn_kernel(q_ref, k_ref, v_ref, o_ref, *, group, tq, rows, dk, dv):
    k = k_ref[...]
    v = v_ref[...]
    for h in range(group):
        for r in range(0, tq, rows):
            q = q_ref[r:r + rows, h * dk:(h + 1) * dk]
            s = lax.dot_general(q, k, (((1,), (1,)), ((), ())), preferred_element_type=F32)
            m = jnp.max(s, axis=-1, keepdims=True)
            p = jnp.exp(s - m)
            l = jnp.sum(p, axis=-1, keepdims=True)
            o = jnp.dot(p.astype(v.dtype), v, preferred_element_type=F32)
            o_ref[r:r + rows, h * dv:(h + 1) * dv] = (o / l).astype(o_ref.dtype)


def _attention(q, k, v, *, batch, sq, lk, n_kv, group, dk, dv, tq, rows, q_row_off):
    nq = sq // tq
    off = q_row_off // tq
    assert tq % rows == 0 and q_row_off % tq == 0 and sq % tq == 0
    return pl.pallas_call(
        functools.partial(_attn_kernel, group=group, tq=tq, rows=rows, dk=dk, dv=dv),
        grid=(batch, n_kv, nq),
        in_specs=[
            pl.BlockSpec((tq, group * dk), lambda b, g, i: (off + b * nq + i, g)),
            pl.BlockSpec((None, lk, dk), lambda b, g, i: (b, 0, g)),
            pl.BlockSpec((None, lk, dv), lambda b, g, i: (b, 0, g)),
        ],
        out_specs=pl.BlockSpec((tq, group * dv), lambda b, g, i: (b * nq + i, g)),
        out_shape=jax.ShapeDtypeStruct((batch * sq, n_kv * group * dv), BF16),
        compiler_params=_params(("parallel", "parallel", "parallel")),
        name="attention",
    )(q, k, v)


def _conv_kernel(gb_ref, gc_ref, hv_ref, gcp_ref, hvp_ref, gcn_ref, hvn_ref, w_ref, o_ref, *,
                 tm, lat_tiles, lat_seq_tiles, ctx_seq_tiles):
    i = pl.program_id(0)
    is_lat = i < lat_tiles
    pos = jnp.where(is_lat, i % lat_seq_tiles, (i - lat_tiles) % ctx_seq_tiles)
    seq_tiles = jnp.where(is_lat, lat_seq_tiles, ctx_seq_tiles)
    not_first = (pos != 0).astype(F32)
    not_last = (pos != seq_tiles - 1).astype(F32)
    p = gc_ref[...].astype(F32) * hv_ref[...].astype(F32)
    halo_prev = gcp_ref[SUBLANES - 1:SUBLANES, :].astype(F32) * hvp_ref[SUBLANES - 1:SUBLANES, :].astype(F32) * not_first
    halo_next = gcn_ref[0:1, :].astype(F32) * hvn_ref[0:1, :].astype(F32) * not_last
    row = lax.broadcasted_iota(jnp.int32, p.shape, 0)
    prev = jnp.where(row == 0, halo_prev, pltpu.roll(p, 1, 0))
    nxt = jnp.where(row == tm - 1, halo_next, pltpu.roll(p, tm - 1, 0))
    w = w_ref[...]
    conv = w[0:1, :] * prev + w[1:2, :] * p + w[2:3, :] * nxt
    o_ref[...] = (gb_ref[...].astype(F32) * conv).astype(o_ref.dtype)


def _conv_gate(p, conv_w, *, t, tm, tc, lat_tiles, lat_seq_tiles, ctx_seq_tiles):
    nct = CONV_DIM // tc
    hb = tm // SUBLANES
    n_halo = t // SUBLANES

    def cur(part):
        return pl.BlockSpec((tm, tc), lambda i, j: (i, part * nct + j))

    def prev(part):
        return pl.BlockSpec((SUBLANES, tc), lambda i, j: (jnp.maximum(i * hb - 1, 0), part * nct + j))

    def nxt(part):
        return pl.BlockSpec((SUBLANES, tc), lambda i, j: (jnp.minimum((i + 1) * hb, n_halo - 1), part * nct + j))

    return pl.pallas_call(
        functools.partial(_conv_kernel, tm=tm, lat_tiles=lat_tiles, lat_seq_tiles=lat_seq_tiles,
                          ctx_seq_tiles=ctx_seq_tiles),
        grid=(t // tm, nct),
        in_specs=[cur(0), cur(1), cur(2), prev(1), prev(2), nxt(1), nxt(2),
                  pl.BlockSpec((3, tc), lambda i, j: (0, j))],
        out_specs=pl.BlockSpec((tm, tc), lambda i, j: (i, j)),
        out_shape=jax.ShapeDtypeStruct((t, CONV_DIM), BF16),
        compiler_params=_params(("parallel", "parallel")),
        name="conv_gate",
    )(p, p, p, p, p, p, p, conv_w)


def _layer_norm(z, g, b):
    mu = jnp.mean(z, axis=-1, keepdims=True)
    zc = z - mu
    var = jnp.mean(zc * zc, axis=-1, keepdims=True)
    return zc * lax.rsqrt(var + LN_EPS) * g + b


def _pack_bf16_pairs(x):
    half = x.shape[1] // 2
    lo = lax.bitcast_convert_type(x[:, :half].astype(BF16).astype(F32), jnp.uint32) >> 16
    hi = lax.bitcast_convert_type(x[:, half:].astype(BF16).astype(F32), jnp.uint32) & jnp.uint32(0xFFFF0000)
    return lax.bitcast_convert_type(lo | hi, jnp.int32)


def _unpack_bf16_pairs(w):
    u = lax.bitcast_convert_type(w, jnp.uint32)
    lo = lax.bitcast_convert_type(u << 16, F32).astype(BF16)
    hi = lax.bitcast_convert_type(u & jnp.uint32(0xFFFF0000), F32).astype(BF16)
    return lo, hi


def _dot_halves(lo, hi, w_ref):
    half = lo.shape[1]
    return (jnp.dot(lo, w_ref[:half, :], preferred_element_type=F32)
            + jnp.dot(hi, w_ref[half:, :], preferred_element_type=F32))


def _outproj_ln_kernel(*refs, n_a, alpha):
    a_refs = refs[:n_a]
    w_refs = refs[n_a:2 * n_a]
    x_ref, gate_ref, lng_ref, lnb_ref, sc_ref, sh_ref, xo_ref, tok_ref = refs[2 * n_a:]
    y = jnp.dot(a_refs[0][...], w_refs[0][...], preferred_element_type=F32)
    for a_ref, w_ref in zip(a_refs[1:], w_refs[1:]):
        y = y + jnp.dot(a_ref[...], w_ref[...], preferred_element_type=F32)
    xn = _layer_norm(alpha * x_ref[...] + gate_ref[...] * y, lng_ref[...], lnb_ref[...])
    xo_ref[...] = xn
    tok_ref[...] = _pack_bf16_pairs(xn * (1.0 + sc_ref[...]) + sh_ref[...])


def _outproj_ln(a_list, w_list, x, mod, ln_g, ln_b, *, t, tm, alpha, mod_row):
    d = x.shape[1]
    n_a = len(a_list)
    in_specs = [pl.BlockSpec((tm, a.shape[1]), lambda i: (i, 0)) for a in a_list]
    in_specs += [_const_spec(w.shape) for w in w_list]
    in_specs += [
        pl.BlockSpec((tm, d), lambda i: (i, 0)),
        _mod_spec(d, 2, mod_row, tm),
        _const_spec((1, d)), _const_spec((1, d)),
        _mod_spec(d, 4, mod_row, tm),
        _mod_spec(d, 3, mod_row, tm),
    ]
    return pl.pallas_call(
        functools.partial(_outproj_ln_kernel, n_a=n_a, alpha=alpha),
        grid=(t // tm,),
        in_specs=in_specs,
        out_specs=[pl.BlockSpec((tm, d), lambda i: (i, 0)), pl.BlockSpec((tm, d // 2), lambda i: (i, 0))],
        out_shape=[jax.ShapeDtypeStruct((t, d), F32), jax.ShapeDtypeStruct((t, d // 2), jnp.int32)],
        compiler_params=_params(("parallel",)),
        name="outproj_ln",
    )(*a_list, *w_list, x, mod, ln_g, ln_b, mod, mod)


def _router_kernel(t_ref, rw_ref, rb_ref, tri_ref, idx_ref, gw_ref, rank_ref, cnt_ref):
    @pl.when(pl.program_id(0) == 0)
    def _():
        cnt_ref[...] = jnp.zeros_like(cnt_ref)

    lo, hi = _unpack_bf16_pairs(t_ref[...])
    half = lo.shape[1]
    nt = (((1,), (1,)), ((), ()))
    logits = (lax.dot_general(rw_ref[:, :half], lo, nt, preferred_element_type=F32)
              + lax.dot_general(rw_ref[:, half:], hi, nt, preferred_element_type=F32))
    scores = 1.0 / (1.0 + jnp.exp(-logits))
    sel = scores + rb_ref[...]
    gsz = N_EXPERTS // N_GROUPS
    neg = -jnp.inf
    sub = lax.broadcasted_iota(jnp.int32, (gsz, sel.shape[1]), 0)
    slabs = [sel[g * gsz:(g + 1) * gsz, :] for g in range(N_GROUPS)]
    gscore = []
    for s in slabs:
        m1 = jnp.max(s, axis=0, keepdims=True)
        a1 = jnp.min(jnp.where(s == m1, sub, gsz), axis=0, keepdims=True)
        m2 = jnp.max(jnp.where(sub == a1, neg, s), axis=0, keepdims=True)
        gscore.append(m1 + m2)
    masked = []
    for g in range(N_GROUPS):
        ahead = jnp.zeros(gscore[g].shape, jnp.int32)
        for h in range(N_GROUPS):
            if h == g:
                continue
            beats = gscore[h] >= gscore[g] if h < g else gscore[h] > gscore[g]
            ahead = ahead + beats.astype(jnp.int32)
        masked.append(jnp.where(ahead < TOPK_GROUPS, slabs[g], neg))
    cur = jnp.concatenate(masked, axis=0)
    eio = lax.broadcasted_iota(jnp.int32, cur.shape, 0)
    picks, weights = [], []
    for _ in range(TOP_K):
        m = jnp.max(cur, axis=0, keepdims=True)
        a = jnp.min(jnp.where(cur == m, eio, N_EXPERTS), axis=0, keepdims=True)
        hit = eio == a
        picks.append(a)
        weights.append(jnp.sum(jnp.where(hit, scores, 0.0), axis=0, keepdims=True))
        cur = jnp.where(hit, neg, cur)
    total = weights[0]
    for w in weights[1:]:
        total = total + w
    for k in range(TOP_K):
        idx_ref[k:k + 1, :] = picks[k]
        gw_ref[k:k + 1, :] = weights[k] / total * ROUTED_SCALE
    for k in range(TOP_K, SUBLANES):
        idx_ref[k:k + 1, :] = jnp.zeros_like(picks[0])
        gw_ref[k:k + 1, :] = jnp.zeros_like(weights[0])
        rank_ref[k:k + 1, :] = jnp.zeros_like(picks[0])
    base = cnt_ref[:, 0:1]
    for k in range(TOP_K):
        onehot = jnp.where(eio == picks[k], 1.0, 0.0)
        before = jnp.dot(onehot.astype(BF16), tri_ref[...], preferred_element_type=F32)
        rank_ref[k:k + 1, :] = jnp.sum(onehot * (before + base), axis=0, keepdims=True).astype(jnp.int32)
        base = base + jnp.sum(onehot, axis=1, keepdims=True)
    cnt_ref[...] = jnp.broadcast_to(base, cnt_ref.shape)


def _router(tok, rw_t, rb, *, t, tt):
    half = tok.shape[1]
    tri = (jnp.arange(tt)[:, None] < jnp.arange(tt)[None, :]).astype(BF16)
    blk = pl.BlockSpec((SUBLANES, tt), lambda i: (0, i))
    return pl.pallas_call(
        _router_kernel,
        grid=(t // tt,),
        in_specs=[
            pl.BlockSpec((tt, half), lambda i: (i, 0)),
            _const_spec((N_EXPERTS, 2 * half)),
            _const_spec((N_EXPERTS, 1)),
            _const_spec((tt, tt)),
        ],
        out_specs=[blk, blk, blk, _const_spec((N_EXPERTS, LANES))],
        out_shape=[jax.ShapeDtypeStruct((SUBLANES, t), jnp.int32), jax.ShapeDtypeStruct((SUBLANES, t), F32),
                   jax.ShapeDtypeStruct((SUBLANES, t), jnp.int32), jax.ShapeDtypeStruct((N_EXPERTS, LANES), F32)],
        compiler_params=_params(("arbitrary",)),
        name="router",
    )(tok, rw_t, rb, tri)


def _slots_kernel(idx_ref, rank_ref, start_ref, pos_ref):
    start = start_ref[...]
    eio = lax.broadcasted_iota(jnp.int32, (N_EXPERTS, idx_ref.shape[1]), 0)
    for k in range(TOP_K):
        seg = jnp.sum(jnp.where(eio == idx_ref[k:k + 1, :], start, 0.0), axis=0, keepdims=True)
        pos_ref[k:k + 1, :] = rank_ref[k:k + 1, :] + seg.astype(jnp.int32)
    for k in range(TOP_K, SUBLANES):
        pos_ref[k:k + 1, :] = jnp.zeros((1, idx_ref.shape[1]), jnp.int32)


def _assign_slots(idx, rank, seg_start, *, t, tt):
    blk = pl.BlockSpec((SUBLANES, tt), lambda i: (0, i))
    return pl.pallas_call(
        _slots_kernel,
        grid=(t // tt,),
        in_specs=[blk, blk, _const_spec((N_EXPERTS, 1))],
        out_specs=blk,
        out_shape=jax.ShapeDtypeStruct((SUBLANES, t), jnp.int32),
        compiler_params=_params(("parallel",)),
        name="assign_slots",
    )(idx, rank, seg_start)


def _sc_gather_rows(table, idx):
    n = idx.shape[0]
    d = table.shape[1]
    n_workers = V7X_SC_CORES * V7X_SC_SUBCORES
    per_w = n // n_workers
    n_chunks = per_w // SC_GATHER_ROWS
    assert per_w * n_workers == n and n_chunks * SC_GATHER_ROWS == per_w
    n_buf = next(b for b in range(SC_GATHER_BUFFERS, 1, -1) if n_chunks % b == 0)
    mesh = plsc.VectorSubcoreMesh(core_axis_name="c", subcore_axis_name="s", num_cores=V7X_SC_CORES,
                                  num_subcores=V7X_SC_SUBCORES)

    @functools.partial(
        pl.kernel,
        out_type=jax.ShapeDtypeStruct((n, d), table.dtype),
        mesh=mesh,
        scratch_types=[
            pltpu.VMEM((per_w,), jnp.int32),
            pltpu.VMEM((n_buf, SC_GATHER_ROWS, d), table.dtype),
            pltpu.SemaphoreType.DMA((n_buf,)),
            pltpu.SemaphoreType.DMA((n_buf,)),
        ],
        name="sc_gather_rows",
    )
    def gather(table_hbm, idx_hbm, out_hbm, idx_v, rows_v, gsem, wsem):
        wid = lax.axis_index("s") * V7X_SC_CORES + lax.axis_index("c")
        base = wid * per_w
        pltpu.sync_copy(idx_hbm.at[pl.ds(base, per_w)], idx_v)

        def gather_copy(c, b):
            return pltpu.make_async_copy(table_hbm.at[idx_v.at[pl.ds(c * SC_GATHER_ROWS, SC_GATHER_ROWS)]],
                                         rows_v.at[b], gsem.at[b])

        def write_copy(c, b):
            return pltpu.make_async_copy(rows_v.at[b], out_hbm.at[pl.ds(base + c * SC_GATHER_ROWS, SC_GATHER_ROWS)],
                                         wsem.at[b])

        for b in range(n_buf - 1):
            gather_copy(b, b).start()

        @pl.loop(0, n_chunks, step=n_buf)
        def _(g):
            for b in range(n_buf):
                c = g + b
                prev = (b + n_buf - 1) % n_buf
                gather_copy(c, b).wait()
                write_copy(c, b).start()

                @pl.when(c >= 1)
                def _():
                    write_copy(c - 1, prev).wait()

                @pl.when(c + n_buf - 1 < n_chunks)
                def _():
                    gather_copy(c + n_buf - 1, prev).start()

        write_copy(n_chunks - 1, (n_chunks - 1) % n_buf).wait()

    return gather(table, idx)


def _experts_kernel(be_ref, nbu_ref, x_ref, wg_ref, wu_ref, wd_ref, y_ref, wgb, wub, wdb):
    b = pl.program_id(0)
    nbu = nbu_ref[0]

    @pl.when(b < nbu)
    def _():
        changed = jnp.logical_or(b == 0, be_ref[b] != be_ref[jnp.maximum(b - 1, 0)])

        @pl.when(changed)
        def _():
            wgb[...] = wg_ref[...].astype(BF16)
            wub[...] = wu_ref[...].astype(BF16)
            wdb[...] = wd_ref[...].astype(BF16)

        lo, hi = _unpack_bf16_pairs(x_ref[...])
        hg = _dot_halves(lo, hi, wgb)
        hu = _dot_halves(lo, hi, wub)
        h = hg * (1.0 / (1.0 + jnp.exp(-hg))) * hu
        y_ref[...] = _pack_bf16_pairs(jnp.dot(h.astype(BF16), wdb[...], preferred_element_type=F32))

    @pl.when(b >= nbu)
    def _():
        y_ref[...] = jnp.zeros_like(y_ref)


def _experts(xs, block_e, nb_used, wg, wu, wd, layer, *, n_blocks):
    half = xs.shape[1]
    d = 2 * half
    ff = wg.shape[3]

    def used(b, nbu):
        return jnp.minimum(b, jnp.maximum(nbu[0] - 1, 0))

    grid_spec = pltpu.PrefetchScalarGridSpec(
        num_scalar_prefetch=2,
        grid=(n_blocks,),
        in_specs=[
            pl.BlockSpec((MOE_ROWS, half), lambda b, be, nbu: (used(b, nbu), 0)),
            pl.BlockSpec((None, None, d, ff), lambda b, be, nbu: (layer, be[b], 0, 0)),
            pl.BlockSpec((None, None, d, ff), lambda b, be, nbu: (layer, be[b], 0, 0)),
            pl.BlockSpec((None, None, ff, d), lambda b, be, nbu: (layer, be[b], 0, 0)),
        ],
        out_specs=pl.BlockSpec((MOE_ROWS, half), lambda b, be, nbu: (b, 0)),
        scratch_shapes=[
            pltpu.VMEM((d, ff), BF16),
            pltpu.VMEM((d, ff), BF16),
            pltpu.VMEM((ff, d), BF16),
        ],
    )
    return pl.pallas_call(
        _experts_kernel,
        grid_spec=grid_spec,
        out_shape=jax.ShapeDtypeStruct((n_blocks * MOE_ROWS, half), jnp.int32),
        compiler_params=_params(("arbitrary",)),
        name="experts",
    )(block_e, nb_used, xs, wg, wu, wd)


def _combine_ln_kernel(*refs, alpha, emit_next):
    y_ref, gw_ref, tok_ref, x_ref, sg_ref, su_ref, sd_ref, gate_ref, lng_ref, lnb_ref = refs[:10]
    if emit_next:
        sc_ref, sh_ref, xo_ref, u_ref = refs[10:]
    else:
        (xo_ref,) = refs[10:]
    lo, hi = _unpack_bf16_pairs(tok_ref[...])
    hg = _dot_halves(lo, hi, sg_ref)
    hu = _dot_halves(lo, hi, su_ref)
    h = hg * (1.0 / (1.0 + jnp.exp(-hg))) * hu
    gw = gw_ref[...]
    f_lo = f_hi = None
    for k in range(TOP_K):
        y_lo, y_hi = _unpack_bf16_pairs(y_ref[k])
        w = gw[:, k:k + 1]
        f_lo = y_lo.astype(F32) * w if f_lo is None else f_lo + y_lo.astype(F32) * w
        f_hi = y_hi.astype(F32) * w if f_hi is None else f_hi + y_hi.astype(F32) * w
    f = jnp.concatenate([f_lo, f_hi], axis=-1) + jnp.dot(h.astype(BF16), sd_ref[...], preferred_element_type=F32)
    xn = _layer_norm(alpha * x_ref[...] + gate_ref[...] * f, lng_ref[...], lnb_ref[...])
    xo_ref[...] = xn
    if emit_next:
        u_ref[...] = (xn * (1.0 + sc_ref[...]) + sh_ref[...]).astype(u_ref.dtype)


def _combine_ln(y3, gw_t, tok, x, sg, su, sd, mod, ln_g, ln_b, mod_next, *, t, tm, alpha, mod_row):
    d = x.shape[1]
    emit_next = mod_next is not None
    in_specs = [
        pl.BlockSpec((TOP_K, tm, d // 2), lambda i: (0, i, 0)),
        pl.BlockSpec((tm, SUBLANES), lambda i: (i, 0)),
        pl.BlockSpec((tm, d // 2), lambda i: (i, 0)),
        pl.BlockSpec((tm, d), lambda i: (i, 0)),
        _const_spec(sg.shape), _const_spec(su.shape), _const_spec(sd.shape),
        _mod_spec(d, 5, mod_row, tm),
        _const_spec((1, d)), _const_spec((1, d)),
    ]
    args = [y3, gw_t, tok, x, sg, su, sd, mod, ln_g, ln_b]
    out_specs = [pl.BlockSpec((tm, d), lambda i: (i, 0))]
    out_shape = [jax.ShapeDtypeStruct((t, d), F32)]
    if emit_next:
        in_specs += [_mod_spec(d, 1, mod_row, tm), _mod_spec(d, 0, mod_row, tm)]
        args += [mod_next, mod_next]
        out_specs.append(pl.BlockSpec((tm, d), lambda i: (i, 0)))
        out_shape.append(jax.ShapeDtypeStruct((t, d), BF16))
    return pl.pallas_call(
        functools.partial(_combine_ln_kernel, alpha=alpha, emit_next=emit_next),
        grid=(t // tm,),
        in_specs=in_specs,
        out_specs=out_specs,
        out_shape=out_shape,
        compiler_params=_params(("parallel",)),
        name="combine_ln",
    )(*args)


def _moe(tok, x, t, layer, router_w, router_b, wg, wu, wd, sg, su, sd, mod, ln_g, ln_b, mod_next, *, alpha, mod_row,
         tm):
    half = tok.shape[1]
    tt = 512
    idx, gw, rank, cnt = _router(tok, router_w.T.astype(BF16), router_b.reshape(N_EXPERTS, 1), t=t, tt=tt)
    n_asg = t * TOP_K
    counts = cnt[:, 0].astype(jnp.int32)
    padded = (counts + MOE_ROWS - 1) // MOE_ROWS * MOE_ROWS
    pend = jnp.cumsum(padded)
    pstart = pend - padded
    ustart = jnp.cumsum(counts) - counts
    sc_rows = V7X_SC_CORES * V7X_SC_SUBCORES * SC_GATHER_ROWS * 2
    blocks_granule = sc_rows // MOE_ROWS
    n_blocks = -(-((n_asg + N_EXPERTS * (MOE_ROWS - 1)) // MOE_ROWS + 1) // blocks_granule) * blocks_granule
    assert n_asg % sc_rows == 0
    block_start = jnp.arange(n_blocks, dtype=jnp.int32) * MOE_ROWS
    block_e = jnp.minimum(jnp.sum((pend[None, :] <= block_start[:, None]).astype(jnp.int32), axis=1), N_EXPERTS - 1)
    nb_used = (pend[-1] // MOE_ROWS).astype(jnp.int32).reshape(1)
    pos = _assign_slots(idx, rank, pstart.astype(F32).reshape(N_EXPERTS, 1), t=t, tt=tt)[:TOP_K].reshape(-1)
    tok_of_asg = jnp.tile(jnp.arange(t, dtype=jnp.int32), TOP_K)
    _, tok_sorted = lax.sort((pos, tok_of_asg), num_keys=1)
    block_valid = jnp.clip(counts[block_e] - (block_start - pstart[block_e]), 0, MOE_ROWS)
    block_shift = (pstart - ustart)[block_e]
    row = jnp.arange(MOE_ROWS, dtype=jnp.int32)[None, :]
    compact = jnp.clip(block_start[:, None] + row - block_shift[:, None], 0, n_asg - 1)
    spread = (block_start[:, None] + row) % t
    slot_tok = jnp.where(row < block_valid[:, None], tok_sorted[compact], spread).reshape(-1)
    xs = _sc_gather_rows(tok, slot_tok)
    y = _experts(xs, block_e, nb_used, wg, wu, wd, layer, n_blocks=n_blocks)
    y3 = _sc_gather_rows(y, pos).reshape(TOP_K, t, half)
    return _combine_ln(y3, gw.T, tok, x, sg, su, sd, mod, ln_g, ln_b, mod_next, t=t, tm=tm, alpha=alpha,
                       mod_row=mod_row)


def _rope64(r, c_ref, sa_ref, sb_ref):
    return r * c_ref[...] + pltpu.roll(r, LANES - QK_ROPE // 2, 1) * sa_ref[...] + pltpu.roll(r, QK_ROPE // 2, 1) * sb_ref[...]


def _mla_q_kernel(d_ref, gain_ref, w_ref, c_ref, sa_ref, sb_ref, q_ref, *, scale):
    n = _rms(d_ref[...], gain_ref[...]).astype(BF16)
    q = jnp.dot(n, w_ref[...], preferred_element_type=F32)
    for h in range(MLA_HEADS):
        lo = h * MLA_DK_PAD
        q_ref[:, lo:lo + QK_NOPE] = (q[:, lo:lo + QK_NOPE] * scale).astype(q_ref.dtype)
        r = _rope64(q[:, lo + QK_NOPE:lo + MLA_DK_PAD], c_ref, sa_ref, sb_ref)
        q_ref[:, lo + QK_NOPE:lo + MLA_DK_PAD] = (r * scale).astype(q_ref.dtype)


def _mla_kv_kernel(ckv_ref, kr_ref, gain_ref, wk_ref, wv_ref, c_ref, sa_ref, sb_ref, k_ref, v_ref):
    n = _rms(ckv_ref[...], gain_ref[...]).astype(BF16)
    kn = jnp.dot(n, wk_ref[...], preferred_element_type=F32)
    v_ref[...] = jnp.dot(n, wv_ref[...], preferred_element_type=F32).astype(v_ref.dtype)
    kr = _rope64(kr_ref[...], c_ref, sa_ref, sb_ref).astype(k_ref.dtype)
    for h in range(MLA_HEADS):
        lo = h * MLA_DK_PAD
        k_ref[:, lo:lo + QK_NOPE] = kn[:, h * QK_NOPE:(h + 1) * QK_NOPE].astype(k_ref.dtype)
        k_ref[:, lo + QK_NOPE:lo + MLA_DK_PAD] = kr


def _axial_angles(n_tok, rot_dim):
    rows = n_tok // GRID_W
    n_freq = rot_dim // 4
    inv = ROPE_THETA ** (-jnp.arange(n_freq, dtype=F32) / n_freq)
    row = jnp.repeat(jnp.arange(rows, dtype=F32), GRID_W)
    col = jnp.tile(jnp.arange(GRID_W, dtype=F32), rows)
    return jnp.concatenate([row[:, None] * inv, col[:, None] * inv], axis=-1)


def _rope_tables_128(n_tok, ident_rows):
    ang = _axial_angles(n_tok, HEAD_DIM)
    cos, sin = jnp.cos(ang), jnp.sin(ang)
    c = jnp.concatenate([cos, cos], axis=-1)
    s = jnp.concatenate([-sin, sin], axis=-1)
    c = jnp.concatenate([c, jnp.ones((ident_rows, HEAD_DIM), F32)], axis=0)
    s = jnp.concatenate([s, jnp.zeros((ident_rows, HEAD_DIM), F32)], axis=0)
    return c, s


def _rope_tables_64(n_tok, ident_rows):
    ang = _axial_angles(n_tok, QK_ROPE)
    cos, sin = jnp.cos(ang), jnp.sin(ang)
    half = QK_ROPE // 2
    z = jnp.zeros((n_tok, LANES - QK_ROPE), F32)
    zh = jnp.zeros((n_tok, half), F32)
    c = jnp.concatenate([cos, cos, z], axis=-1)
    sa = jnp.concatenate([-sin, zh, z], axis=-1)
    sb = jnp.concatenate([zh, sin, z], axis=-1)
    ci = jnp.concatenate([jnp.ones((ident_rows, QK_ROPE), F32), jnp.zeros((ident_rows, LANES - QK_ROPE), F32)], axis=-1)
    zi = jnp.zeros((ident_rows, LANES), F32)
    return jnp.concatenate([c, ci], 0), jnp.concatenate([sa, zi], 0), jnp.concatenate([sb, zi], 0)


def kernel(x, c, ctx, c_ctx, w_ada, b_ada, ln_g, ln_b, a_w_in, a_conv_w, a_q_gain, a_k_gain, a_w_out, m_w_down, m_q_gain, m_kv_gain, m_w_uq, m_w_ukv, m_w_out, router_w, router_b, e_w_gate, e_w_up, e_w_down, s_w_gate, s_w_up, s_w_down):
    batch, seq, d = x.shape
    ctx_len = ctx.shape[1]
    depth = w_ada.shape[0]
    assert depth == 2, "one conv+GQA layer followed by one MLA layer"
    alpha = (2 * depth) ** 0.25
    t_lat = batch * seq
    t_ctx = batch * ctx_len
    t_all = t_lat + t_ctx
    tr = 256
    assert seq % tr == 0 and ctx_len % tr == 0 and seq % GRID_W == 0
    lat_tiles = t_lat // tr
    lat_seq_tiles = seq // tr
    ctx_seq_tiles = ctx_len // tr
    lk = ctx_len + seq

    def mod_row(r):
        return jnp.minimum(r // seq, batch)

    def kv_block(i):
        is_lat = i < lat_tiles
        cidx = i - lat_tiles
        b = jnp.where(is_lat, i // lat_seq_tiles, cidx // ctx_seq_tiles)
        rb = jnp.where(is_lat, ctx_seq_tiles + i % lat_seq_tiles, cidx % ctx_seq_tiles)
        return b, rb

    def pos_block(i):
        return jnp.where(i < lat_tiles, i % lat_seq_tiles, lat_seq_tiles)

    rows = -(-(batch + 1) // SUBLANES) * SUBLANES
    cond = jnp.concatenate([c, c_ctx[None, :], jnp.zeros((rows - batch - 1, d), F32)], axis=0)
    mod = _ada_table(cond, w_ada, b_ada).reshape(depth, rows, 1, 6 * d)

    x_all = jnp.concatenate([x.reshape(t_lat, d), ctx.reshape(t_ctx, d)], axis=0)

    u0 = _modulate(x_all, mod[0], mod_row, tr)
    proj = _matmul(u0, a_w_in[0].astype(BF16), BF16, 1024 if t_all % 1024 == 0 else 512,
                   768)

    cos128, sin128 = _rope_tables_128(seq, tr)
    d_q = ATT_HEADS * HEAD_DIM
    d_kv = ATT_KV_HEADS * HEAD_DIM
    qkv_w = d_q + 2 * d_kv
    qkv_blk = 3 * CONV_DIM // qkv_w
    assert qkv_blk * qkv_w == 3 * CONV_DIM
    q0, k0, v0 = pl.pallas_call(
        functools.partial(_qkprep_kernel, scale=1.0 / math.sqrt(HEAD_DIM)),
        grid=(t_all // tr,),
        in_specs=[
            pl.BlockSpec((tr, qkv_w), lambda i: (i, qkv_blk)),
            pl.BlockSpec((tr, HEAD_DIM), lambda i: (pos_block(i), 0)),
            pl.BlockSpec((tr, HEAD_DIM), lambda i: (pos_block(i), 0)),
            _const_spec((1, HEAD_DIM)), _const_spec((1, HEAD_DIM)),
        ],
        out_specs=[
            pl.BlockSpec((tr, d_q), lambda i: (i, 0)),
            pl.BlockSpec((None, tr, d_kv), lambda i: (*kv_block(i), 0)),
            pl.BlockSpec((None, tr, d_kv), lambda i: (*kv_block(i), 0)),
        ],
        out_shape=[
            jax.ShapeDtypeStruct((t_all, d_q), BF16),
            jax.ShapeDtypeStruct((batch, lk, d_kv), BF16),
            jax.ShapeDtypeStruct((batch, lk, d_kv), BF16),
        ],
        compiler_params=_params(("parallel",)),
        name="qk_prep",
    )(proj, cos128, sin128, a_q_gain[0].reshape(1, HEAD_DIM), a_k_gain[0].reshape(1, HEAD_DIM))

    grp = ATT_HEADS // ATT_KV_HEADS
    att_lat = _attention(q0, k0, v0, batch=batch, sq=seq, lk=lk, n_kv=ATT_KV_HEADS, group=grp, dk=HEAD_DIM,
                         dv=HEAD_DIM, tq=256, rows=256, q_row_off=0)
    att_ctx = _attention(q0, k0, v0, batch=batch, sq=ctx_len, lk=ctx_len, n_kv=ATT_KV_HEADS, group=grp,
                         dk=HEAD_DIM, dv=HEAD_DIM, tq=256, rows=256, q_row_off=t_lat)
    att0 = jnp.concatenate([att_lat, att_ctx], axis=0)

    conv0 = _conv_gate(proj, a_conv_w[0], t=t_all, tm=tr, tc=512, lat_tiles=lat_tiles,
                       lat_seq_tiles=lat_seq_tiles, ctx_seq_tiles=ctx_seq_tiles)

    w_out0 = a_w_out[0].astype(BF16)
    x1, tok0 = _outproj_ln([conv0, att0], [w_out0[:CONV_DIM], w_out0[CONV_DIM:]], x_all, mod[0],
                           ln_g[0, 0].reshape(1, d), ln_b[0, 0].reshape(1, d), t=t_all, tm=tr, alpha=alpha,
                           mod_row=mod_row)

    x2, u1 = _moe(tok0, x1, t_all, 0, router_w[0], router_b[0], e_w_gate, e_w_up, e_w_down,
                  s_w_gate[0].astype(BF16), s_w_up[0].astype(BF16), s_w_down[0].astype(BF16), mod[0],
                  ln_g[0, 1].reshape(1, d), ln_b[0, 1].reshape(1, d), mod[1], alpha=alpha,
                  mod_row=mod_row, tm=256)

    n_down = Q_LORA + KV_LORA + QK_ROPE
    n_down_pad = -(-n_down // LANES) * LANES
    w_down = jnp.pad(m_w_down[0], ((0, 0), (0, n_down_pad - n_down))).astype(BF16)
    down = _matmul(u1, w_down, F32, 512, n_down_pad)

    dqk = QK_NOPE + QK_ROPE
    w_uq = m_w_uq[0].reshape(Q_LORA, MLA_HEADS, dqk)
    w_uq = jnp.pad(w_uq, ((0, 0), (0, 0), (0, MLA_DK_PAD - dqk))).reshape(Q_LORA, MLA_HEADS * MLA_DK_PAD).astype(BF16)
    w_ukv = m_w_ukv[0].reshape(KV_LORA, MLA_HEADS, QK_NOPE + V_DIM)
    w_uk = w_ukv[:, :, :QK_NOPE].reshape(KV_LORA, MLA_HEADS * QK_NOPE).astype(BF16)
    w_uv = w_ukv[:, :, QK_NOPE:].reshape(KV_LORA, MLA_HEADS * V_DIM).astype(BF16)

    c64, sa64, sb64 = _rope_tables_64(seq, tr)
    rope_specs = [pl.BlockSpec((tr, LANES), lambda i: (pos_block(i), 0))] * 3
    q1 = pl.pallas_call(
        functools.partial(_mla_q_kernel, scale=1.0 / math.sqrt(dqk)),
        grid=(lat_tiles,),
        in_specs=[
            pl.BlockSpec((tr, Q_LORA), lambda i: (i, 0)),
            _const_spec((1, Q_LORA)),
            _const_spec(w_uq.shape),
        ] + rope_specs,
        out_specs=pl.BlockSpec((tr, MLA_HEADS * MLA_DK_PAD), lambda i: (i, 0)),
        out_shape=jax.ShapeDtypeStruct((t_lat, MLA_HEADS * MLA_DK_PAD), BF16),
        compiler_params=_params(("parallel",)),
        name="mla_q",
    )(down, m_q_gain[0].reshape(1, Q_LORA), w_uq, c64, sa64, sb64)

    assert KV_LORA == Q_LORA and (Q_LORA + KV_LORA) % LANES == 0
    k1, v1 = pl.pallas_call(
        _mla_kv_kernel,
        grid=(t_all // tr,),
        in_specs=[
            pl.BlockSpec((tr, KV_LORA), lambda i: (i, 1)),
            pl.BlockSpec((tr, LANES), lambda i: (i, (Q_LORA + KV_LORA) // LANES)),
            _const_spec((1, KV_LORA)),
            _const_spec(w_uk.shape), _const_spec(w_uv.shape),
        ] + rope_specs,
        out_specs=[
            pl.BlockSpec((None, tr, MLA_HEADS * MLA_DK_PAD), lambda i: (*kv_block(i), 0)),
            pl.BlockSpec((None, tr, MLA_HEADS * V_DIM), lambda i: (*kv_block(i), 0)),
        ],
        out_shape=[
            jax.ShapeDtypeStruct((batch, lk, MLA_HEADS * MLA_DK_PAD), BF16),
            jax.ShapeDtypeStruct((batch, lk, MLA_HEADS * V_DIM), BF16),
        ],
        compiler_params=_params(("parallel",)),
        name="mla_kv",
    )(down, down, m_kv_gain[0].reshape(1, KV_LORA), w_uk, w_uv, c64, sa64, sb64)

    att1 = _attention(q1, k1, v1, batch=batch, sq=seq, lk=lk, n_kv=MLA_HEADS, group=1, dk=MLA_DK_PAD, dv=V_DIM,
                      tq=min(1024, seq), rows=256, q_row_off=0)

    x3, tok1 = _outproj_ln([att1], [m_w_out[0].astype(BF16)], x2, mod[1], ln_g[1, 0].reshape(1, d),
                           ln_b[1, 0].reshape(1, d), t=t_lat, tm=tr, alpha=alpha, mod_row=mod_row)

    (x4,) = _moe(tok1, x3, t_lat, 1, router_w[1], router_b[1], e_w_gate, e_w_up, e_w_down,
                 s_w_gate[1].astype(BF16), s_w_up[1].astype(BF16), s_w_down[1].astype(BF16), mod[1],
                 ln_g[1, 1].reshape(1, d), ln_b[1, 1].reshape(1, d), None, alpha=alpha,
                 mod_row=mod_row, tm=256)
    return x4.reshape(batch, seq, d)
```

```python
import functools
import math

import jax
import jax.numpy as jnp
from jax import lax
from jax.experimental import pallas as pl
from jax.experimental.pallas import tpu as pltpu
from jax.experimental.pallas import tpu_sc as plsc

F32 = jnp.float32
BF16 = jnp.bfloat16

GRID_W = 64
CONV_DIM = 1024
ATT_HEADS = 8
ATT_KV_HEADS = 2
HEAD_DIM = 128
MLA_HEADS = 16
Q_LORA = 512
KV_LORA = 512
QK_NOPE = 128
QK_ROPE = 64
V_DIM = 128
N_EXPERTS = 64
TOP_K = 6
N_GROUPS = 8
TOPK_GROUPS = 4
ROUTED_SCALE = 2.5
ROPE_THETA = 10000.0
LN_EPS = 1e-5
RMS_EPS = 1e-6

V7X_VMEM_LIMIT_BYTES = 56 * 1024 * 1024
LANES = 128
SUBLANES = 8
MOE_ROWS = 512
V7X_SC_CORES = 2
V7X_SC_SUBCORES = 16
SC_GATHER_ROWS = 16
SC_GATHER_BUFFERS = 4
MLA_DK_PAD = 256


def _params(sem):
    return pltpu.CompilerParams(dimension_semantics=sem, vmem_limit_bytes=V7X_VMEM_LIMIT_BYTES)


def _const_spec(shape):
    nd = len(shape)
    return pl.BlockSpec(shape, lambda *_: (0,) * nd)


def _ada_kernel(s_ref, w_ref, b_ref, o_ref):
    s = s_ref[...]
    s = s * (1.0 / (1.0 + jnp.exp(-s)))
    o_ref[...] = jnp.dot(s.astype(BF16), w_ref[...].astype(BF16), preferred_element_type=F32) + b_ref[...]


def _ada_table(cond, w_ada, b_ada):
    depth, d, n = w_ada.shape
    r = cond.shape[0]
    tn = 1024
    return pl.pallas_call(
        _ada_kernel,
        grid=(depth, n // tn),
        in_specs=[
            pl.BlockSpec((r, d), lambda l, j: (0, 0)),
            pl.BlockSpec((None, d, tn), lambda l, j: (l, 0, j)),
            pl.BlockSpec((None, 1, tn), lambda l, j: (l, 0, j)),
        ],
        out_specs=pl.BlockSpec((None, r, tn), lambda l, j: (l, 0, j)),
        out_shape=jax.ShapeDtypeStruct((depth, r, n), F32),
        compiler_params=_params(("parallel", "parallel")),
        name="ada_table",
    )(cond, w_ada, b_ada.reshape(depth, 1, n))


def _mod_spec(d, chunk, mod_row, tm):
    return pl.BlockSpec((None, 1, d), lambda i: (mod_row(i * tm), 0, chunk))


def _modulate_kernel(x_ref, sc_ref, sh_ref, o_ref):
    o_ref[...] = (x_ref[...] * (1.0 + sc_ref[...]) + sh_ref[...]).astype(o_ref.dtype)


def _modulate(x, mod, mod_row, tm):
    t, d = x.shape
    return pl.pallas_call(
        _modulate_kernel,
        grid=(t // tm,),
        in_specs=[
            pl.BlockSpec((tm, d), lambda i: (i, 0)),
            _mod_spec(d, 1, mod_row, tm),
            _mod_spec(d, 0, mod_row, tm),
        ],
        out_specs=pl.BlockSpec((tm, d), lambda i: (i, 0)),
        out_shape=jax.ShapeDtypeStruct((t, d), BF16),
        compiler_params=_params(("parallel",)),
        name="modulate",
    )(x, mod, mod)


def _mm_kernel(a_ref, w_ref, o_ref):
    o_ref[...] = jnp.dot(a_ref[...], w_ref[...], preferred_element_type=F32).astype(o_ref.dtype)


def _matmul(a, w, out_dtype, tm, tn):
    m, k = a.shape
    n = w.shape[1]
    return pl.pallas_call(
        _mm_kernel,
        grid=(m // tm, n // tn),
        in_specs=[
            pl.BlockSpec((tm, k), lambda i, j: (i, 0)),
            pl.BlockSpec((k, tn), lambda i, j: (0, j)),
        ],
        out_specs=pl.BlockSpec((tm, tn), lambda i, j: (i, j)),
        out_shape=jax.ShapeDtypeStruct((m, n), out_dtype),
        compiler_params=_params(("parallel", "parallel")),
        name="matmul",
    )(a, w)


def _rms(t, gain):
    return t * lax.rsqrt(jnp.mean(t * t, axis=-1, keepdims=True) + RMS_EPS) * gain


def _qkprep_kernel(p_ref, cos_ref, sin_ref, qg_ref, kg_ref, q_ref, k_ref, v_ref, *, scale):
    cos = cos_ref[...]
    sin = sin_ref[...]

    def norm_rope(t, gain):
        y = _rms(t.astype(F32), gain)
        return y * cos + pltpu.roll(y, HEAD_DIM // 2, 1) * sin

    for h in range(ATT_HEADS):
        sl = slice(h * HEAD_DIM, (h + 1) * HEAD_DIM)
        q_ref[:, sl] = (norm_rope(p_ref[:, sl], qg_ref[...]) * scale).astype(q_ref.dtype)
    k0 = ATT_HEADS * HEAD_DIM
    for h in range(ATT_KV_HEADS):
        sl = slice(h * HEAD_DIM, (h + 1) * HEAD_DIM)
        k_ref[:, sl] = norm_rope(p_ref[:, k0 + h * HEAD_DIM:k0 + (h + 1) * HEAD_DIM], kg_ref[...]).astype(k_ref.dtype)
    v0 = k0 + ATT_KV_HEADS * HEAD_DIM
    v_ref[...] = p_ref[:, v0:v0 + ATT_KV_HEADS * HEAD_DIM].astype(v_ref.dtype)


def _attn_kernel(q_ref, k_ref, v_ref, o_ref, *, group, tq, rows, dk, dv):
    k = k_ref[...]
    v = v_ref[...]
    for h in range(group):
        for r in range(0, tq, rows):
            q = q_ref[r:r + rows, h * dk:(h + 1) * dk]
            s = lax.dot_general(q, k, (((1,), (1,)), ((), ())), preferred_element_type=F32)
            m = jnp.max(s, axis=-1, keepdims=True)
            p = jnp.exp(s - m)
            l = jnp.sum(p, axis=-1, keepdims=True)
            o = jnp.dot(p.astype(v.dtype), v, preferred_element_type=F32)
            o_ref[r:r + rows, h * dv:(h + 1) * dv] = (o / l).astype(o_ref.dtype)


def _attention(q, k, v, *, batch, sq, lk, n_kv, group, dk, dv, tq, rows, q_row_off):
    nq = sq // tq
    off = q_row_off // tq
    assert tq % rows == 0 and q_row_off % tq == 0 and sq % tq == 0
    return pl.pallas_call(
        functools.partial(_attn_kernel, group=group, tq=tq, rows=rows, dk=dk, dv=dv),
        grid=(batch, n_kv, nq),
        in_specs=[
            pl.BlockSpec((tq, group * dk), lambda b, g, i: (off + b * nq + i, g)),
            pl.BlockSpec((None, lk, dk), lambda b, g, i: (b, 0, g)),
            pl.BlockSpec((None, lk, dv), lambda b, g, i: (b, 0, g)),
        ],
        out_specs=pl.BlockSpec((tq, group * dv), lambda b, g, i: (b * nq + i, g)),
        out_shape=jax.ShapeDtypeStruct((batch * sq, n_kv * group * dv), BF16),
        compiler_params=_params(("parallel", "parallel", "parallel")),
        name="attention",
    )(q, k, v)


def _conv_kernel(gb_ref, gc_ref, hv_ref, gcp_ref, hvp_ref, gcn_ref, hvn_ref, w_ref, o_ref, *,
                 tm, lat_tiles, lat_seq_tiles, ctx_seq_tiles):
    i = pl.program_id(0)
    is_lat = i < lat_tiles
    pos = jnp.where(is_lat, i % lat_seq_tiles, (i - lat_tiles) % ctx_seq_tiles)
    seq_tiles = jnp.where(is_lat, lat_seq_tiles, ctx_seq_tiles)
    not_first = (pos != 0).astype(F32)
    not_last = (pos != seq_tiles - 1).astype(F32)
    p = gc_ref[...].astype(F32) * hv_ref[...].astype(F32)
    halo_prev = gcp_ref[SUBLANES - 1:SUBLANES, :].astype(F32) * hvp_ref[SUBLANES - 1:SUBLANES, :].astype(F32) * not_first
    halo_next = gcn_ref[0:1, :].astype(F32) * hvn_ref[0:1, :].astype(F32) * not_last
    row = lax.broadcasted_iota(jnp.int32, p.shape, 0)
    prev = jnp.where(row == 0, halo_prev, pltpu.roll(p, 1, 0))
    nxt = jnp.where(row == tm - 1, halo_next, pltpu.roll(p, tm - 1, 0))
    w = w_ref[...]
    conv = w[0:1, :] * prev + w[1:2, :] * p + w[2:3, :] * nxt
    o_ref[...] = (gb_ref[...].astype(F32) * conv).astype(o_ref.dtype)


def _conv_gate(p, conv_w, *, t, tm, tc, lat_tiles, lat_seq_tiles, ctx_seq_tiles):
    nct = CONV_DIM // tc
    hb = tm // SUBLANES
    n_halo = t // SUBLANES

    def cur(part):
        return pl.BlockSpec((tm, tc), lambda i, j: (i, part * nct + j))

    def prev(part):
        return pl.BlockSpec((SUBLANES, tc), lambda i, j: (jnp.maximum(i * hb - 1, 0), part * nct + j))

    def nxt(part):
        return pl.BlockSpec((SUBLANES, tc), lambda i, j: (jnp.minimum((i + 1) * hb, n_halo - 1), part * nct + j))

    return pl.pallas_call(
        functools.partial(_conv_kernel, tm=tm, lat_tiles=lat_tiles, lat_seq_tiles=lat_seq_tiles,
                          ctx_seq_tiles=ctx_seq_tiles),
        grid=(t // tm, nct),
        in_specs=[cur(0), cur(1), cur(2), prev(1), prev(2), nxt(1), nxt(2),
                  pl.BlockSpec((3, tc), lambda i, j: (0, j))],
        out_specs=pl.BlockSpec((tm, tc), lambda i, j: (i, j)),
        out_shape=jax.ShapeDtypeStruct((t, CONV_DIM), BF16),
        compiler_params=_params(("parallel", "parallel")),
        name="conv_gate",
    )(p, p, p, p, p, p, p, conv_w)


def _layer_norm(z, g, b):
    mu = jnp.mean(z, axis=-1, keepdims=True)
    zc = z - mu
    var = jnp.mean(zc * zc, axis=-1, keepdims=True)
    return zc * lax.rsqrt(var + LN_EPS) * g + b


def _pack_bf16_pairs(x):
    half = x.shape[1] // 2
    lo = lax.bitcast_convert_type(x[:, :half].astype(BF16).astype(F32), jnp.uint32) >> 16
    hi = lax.bitcast_convert_type(x[:, half:].astype(BF16).astype(F32), jnp.uint32) & jnp.uint32(0xFFFF0000)
    return lax.bitcast_convert_type(lo | hi, jnp.int32)


def _unpack_bf16_pairs(w):
    u = lax.bitcast_convert_type(w, jnp.uint32)
    lo = lax.bitcast_convert_type(u << 16, F32).astype(BF16)
    hi = lax.bitcast_convert_type(u & jnp.uint32(0xFFFF0000), F32).astype(BF16)
    return lo, hi


def _dot_halves(lo, hi, w_ref):
    half = lo.shape[1]
    return (jnp.dot(lo, w_ref[:half, :], preferred_element_type=F32)
            + jnp.dot(hi, w_ref[half:, :], preferred_element_type=F32))


def _outproj_ln_kernel(*refs, n_a, alpha):
    a_refs = refs[:n_a]
    w_refs = refs[n_a:2 * n_a]
    x_ref, gate_ref, lng_ref, lnb_ref, sc_ref, sh_ref, xo_ref, tok_ref = refs[2 * n_a:]
    y = jnp.dot(a_refs[0][...], w_refs[0][...], preferred_element_type=F32)
    for a_ref, w_ref in zip(a_refs[1:], w_refs[1:]):
        y = y + jnp.dot(a_ref[...], w_ref[...], preferred_element_type=F32)
    xn = _layer_norm(alpha * x_ref[...] + gate_ref[...] * y, lng_ref[...], lnb_ref[...])
    xo_ref[...] = xn
    tok_ref[...] = _pack_bf16_pairs(xn * (1.0 + sc_ref[...]) + sh_ref[...])


def _outproj_ln(a_list, w_list, x, mod, ln_g, ln_b, *, t, tm, alpha, mod_row):
    d = x.shape[1]
    n_a = len(a_list)
    in_specs = [pl.BlockSpec((tm, a.shape[1]), lambda i: (i, 0)) for a in a_list]
    in_specs += [_const_spec(w.shape) for w in w_list]
    in_specs += [
        pl.BlockSpec((tm, d), lambda i: (i, 0)),
        _mod_spec(d, 2, mod_row, tm),
        _const_spec((1, d)), _const_spec((1, d)),
        _mod_spec(d, 4, mod_row, tm),
        _mod_spec(d, 3, mod_row, tm),
    ]
    return pl.pallas_call(
        functools.partial(_outproj_ln_kernel, n_a=n_a, alpha=alpha),
        grid=(t // tm,),
        in_specs=in_specs,
        out_specs=[pl.BlockSpec((tm, d), lambda i: (i, 0)), pl.BlockSpec((tm, d // 2), lambda i: (i, 0))],
        out_shape=[jax.ShapeDtypeStruct((t, d), F32), jax.ShapeDtypeStruct((t, d // 2), jnp.int32)],
        compiler_params=_params(("parallel",)),
        name="outproj_ln",
    )(*a_list, *w_list, x, mod, ln_g, ln_b, mod, mod)


def _router_kernel(t_ref, rw_ref, rb_ref, tri_ref, idx_ref, gw_ref, rank_ref, cnt_ref):
    @pl.when(pl.program_id(0) == 0)
    def _():
        cnt_ref[...] = jnp.zeros_like(cnt_ref)

    lo, hi = _unpack_bf16_pairs(t_ref[...])
    half = lo.shape[1]
    nt = (((1,), (1,)), ((), ()))
    logits = (lax.dot_general(rw_ref[:, :half], lo, nt, preferred_element_type=F32)
              + lax.dot_general(rw_ref[:, half:], hi, nt, preferred_element_type=F32))
    scores = 1.0 / (1.0 + jnp.exp(-logits))
    sel = scores + rb_ref[...]
    gsz = N_EXPERTS // N_GROUPS
    neg = -jnp.inf
    sub = lax.broadcasted_iota(jnp.int32, (gsz, sel.shape[1]), 0)
    slabs = [sel[g * gsz:(g + 1) * gsz, :] for g in range(N_GROUPS)]
    gscore = []
    for s in slabs:
        m1 = jnp.max(s, axis=0, keepdims=True)
        a1 = jnp.min(jnp.where(s == m1, sub, gsz), axis=0, keepdims=True)
        m2 = jnp.max(jnp.where(sub == a1, neg, s), axis=0, keepdims=True)
        gscore.append(m1 + m2)
    masked = []
    for g in range(N_GROUPS):
        ahead = jnp.zeros(gscore[g].shape, jnp.int32)
        for h in range(N_GROUPS):
            if h == g:
                continue
            beats = gscore[h] >= gscore[g] if h < g else gscore[h] > gscore[g]
            ahead = ahead + beats.astype(jnp.int32)
        masked.append(jnp.where(ahead < TOPK_GROUPS, slabs[g], neg))
    cur = jnp.concatenate(masked, axis=0)
    eio = lax.broadcasted_iota(jnp.int32, cur.shape, 0)
    picks, weights = [], []
    for _ in range(TOP_K):
        m = jnp.max(cur, axis=0, keepdims=True)
        a = jnp.min(jnp.where(cur == m, eio, N_EXPERTS), axis=0, keepdims=True)
        hit = eio == a
        picks.append(a)
        weights.append(jnp.sum(jnp.where(hit, scores, 0.0), axis=0, keepdims=True))
        cur = jnp.where(hit, neg, cur)
    total = weights[0]
    for w in weights[1:]:
        total = total + w
    for k in range(TOP_K):
        idx_ref[k:k + 1, :] = picks[k]
        gw_ref[k:k + 1, :] = weights[k] / total * ROUTED_SCALE
    for k in range(TOP_K, SUBLANES):
        idx_ref[k:k + 1, :] = jnp.zeros_like(picks[0])
        gw_ref[k:k + 1, :] = jnp.zeros_like(weights[0])
        rank_ref[k:k + 1, :] = jnp.zeros_like(picks[0])
    base = cnt_ref[:, 0:1]
    for k in range(TOP_K):
        onehot = jnp.where(eio == picks[k], 1.0, 0.0)
        before = jnp.dot(onehot.astype(BF16), tri_ref[...], preferred_element_type=F32)
        rank_ref[k:k + 1, :] = jnp.sum(onehot * (before + base), axis=0, keepdims=True).astype(jnp.int32)
        base = base + jnp.sum(onehot, axis=1, keepdims=True)
    cnt_ref[...] = jnp.broadcast_to(base, cnt_ref.shape)


def _router(tok, rw_t, rb, *, t, tt):
    half = tok.shape[1]
    tri = (jnp.arange(tt)[:, None] < jnp.arange(tt)[None, :]).astype(BF16)
    blk = pl.BlockSpec((SUBLANES, tt), lambda i: (0, i))
    return pl.pallas_call(
        _router_kernel,
        grid=(t // tt,),
        in_specs=[
            pl.BlockSpec((tt, half), lambda i: (i, 0)),
            _const_spec((N_EXPERTS, 2 * half)),
            _const_spec((N_EXPERTS, 1)),
            _const_spec((tt, tt)),
        ],
        out_specs=[blk, blk, blk, _const_spec((N_EXPERTS, LANES))],
        out_shape=[jax.ShapeDtypeStruct((SUBLANES, t), jnp.int32), jax.ShapeDtypeStruct((SUBLANES, t), F32),
                   jax.ShapeDtypeStruct((SUBLANES, t), jnp.int32), jax.ShapeDtypeStruct((N_EXPERTS, LANES), F32)],
        compiler_params=_params(("arbitrary",)),
        name="router",
    )(tok, rw_t, rb, tri)


def _slots_kernel(idx_ref, rank_ref, start_ref, pos_ref):
    start = start_ref[...]
    eio = lax.broadcasted_iota(jnp.int32, (N_EXPERTS, idx_ref.shape[1]), 0)
    for k in range(TOP_K):
        seg = jnp.sum(jnp.where(eio == idx_ref[k:k + 1, :], start, 0.0), axis=0, keepdims=True)
        pos_ref[k:k + 1, :] = rank_ref[k:k + 1, :] + seg.astype(jnp.int32)
    for k in range(TOP_K, SUBLANES):
        pos_ref[k:k + 1, :] = jnp.zeros((1, idx_ref.shape[1]), jnp.int32)


def _assign_slots(idx, rank, seg_start, *, t, tt):
    blk = pl.BlockSpec((SUBLANES, tt), lambda i: (0, i))
    return pl.pallas_call(
        _slots_kernel,
        grid=(t // tt,),
        in_specs=[blk, blk, _const_spec((N_EXPERTS, 1))],
        out_specs=blk,
        out_shape=jax.ShapeDtypeStruct((SUBLANES, t), jnp.int32),
        compiler_params=_params(("parallel",)),
        name="assign_slots",
    )(idx, rank, seg_start)


def _sc_gather_rows(table, idx):
    n = idx.shape[0]
    d = table.shape[1]
    n_workers = V7X_SC_CORES * V7X_SC_SUBCORES
    per_w = n // n_workers
    n_chunks = per_w // SC_GATHER_ROWS
    assert per_w * n_workers == n and n_chunks * SC_GATHER_ROWS == per_w
    n_buf = next(b for b in range(SC_GATHER_BUFFERS, 1, -1) if n_chunks % b == 0)
    mesh = plsc.VectorSubcoreMesh(core_axis_name="c", subcore_axis_name="s", num_cores=V7X_SC_CORES,
                                  num_subcores=V7X_SC_SUBCORES)

    @functools.partial(
        pl.kernel,
        out_type=jax.ShapeDtypeStruct((n, d), table.dtype),
        mesh=mesh,
        scratch_types=[
            pltpu.VMEM((per_w,), jnp.int32),
            pltpu.VMEM((n_buf, SC_GATHER_ROWS, d), table.dtype),
            pltpu.SemaphoreType.DMA((n_buf,)),
            pltpu.SemaphoreType.DMA((n_buf,)),
        ],
        name="sc_gather_rows",
    )
    def gather(table_hbm, idx_hbm, out_hbm, idx_v, rows_v, gsem, wsem):
        wid = lax.axis_index("s") * V7X_SC_CORES + lax.axis_index("c")
        base = wid * per_w
        pltpu.sync_copy(idx_hbm.at[pl.ds(base, per_w)], idx_v)

        def gather_copy(c, b):
            return pltpu.make_async_copy(table_hbm.at[idx_v.at[pl.ds(c * SC_GATHER_ROWS, SC_GATHER_ROWS)]],
                                         rows_v.at[b], gsem.at[b])

        def write_copy(c, b):
            return pltpu.make_async_copy(rows_v.at[b], out_hbm.at[pl.ds(base + c * SC_GATHER_ROWS, SC_GATHER_ROWS)],
                                         wsem.at[b])

        for b in range(n_buf - 1):
            gather_copy(b, b).start()

        @pl.loop(0, n_chunks, step=n_buf)
        def _(g):
            for b in range(n_buf):
                c = g + b
                prev = (b + n_buf - 1) % n_buf
                gather_copy(c, b).wait()
                write_copy(c, b).start()

                @pl.when(c >= 1)
                def _():
                    write_copy(c - 1, prev).wait()

                @pl.when(c + n_buf - 1 < n_chunks)
                def _():
                    gather_copy(c + n_buf - 1, prev).start()

        write_copy(n_chunks - 1, (n_chunks - 1) % n_buf).wait()

    return gather(table, idx)


def _experts_kernel(be_ref, nbu_ref, x_ref, wg_ref, wu_ref, wd_ref, *rest, block_off):
    y_ref, wgb, wub, wdb = rest[-4:]
    step = pl.program_id(0)
    b = block_off + step
    nbu = nbu_ref[0]

    @pl.when(b < nbu)
    def _():
        changed = jnp.logical_or(step == 0, be_ref[b] != be_ref[jnp.maximum(b - 1, 0)])

        @pl.when(changed)
        def _():
            wgb[...] = wg_ref[...].astype(BF16)
            wub[...] = wu_ref[...].astype(BF16)
            wdb[...] = wd_ref[...].astype(BF16)

        lo, hi = _unpack_bf16_pairs(x_ref[...])
        hg = _dot_halves(lo, hi, wgb)
        hu = _dot_halves(lo, hi, wub)
        h = hg * (1.0 / (1.0 + jnp.exp(-hg))) * hu
        y_ref[...] = _pack_bf16_pairs(jnp.dot(h.astype(BF16), wdb[...], preferred_element_type=F32))

    @pl.when(b >= nbu)
    def _():
        y_ref[...] = jnp.zeros_like(y_ref)


def _experts(xs, block_e, nb_used, wg, wu, wd, layer, y_prev, *, block_off, n_blocks):
    half = xs.shape[1]
    d = 2 * half
    ff = wg.shape[3]
    n_call = xs.shape[0] // MOE_ROWS

    def used(b, nbu):
        return jnp.clip(jnp.minimum(block_off + b, nbu[0] - 1) - block_off, 0, n_call - 1)

    def expert(b, be):
        return be[block_off + b]

    in_specs = [
        pl.BlockSpec((MOE_ROWS, half), lambda b, be, nbu: (used(b, nbu), 0)),
        pl.BlockSpec((None, None, d, ff), lambda b, be, nbu: (layer, expert(b, be), 0, 0)),
        pl.BlockSpec((None, None, d, ff), lambda b, be, nbu: (layer, expert(b, be), 0, 0)),
        pl.BlockSpec((None, None, ff, d), lambda b, be, nbu: (layer, expert(b, be), 0, 0)),
    ]
    args = [block_e, nb_used, xs, wg, wu, wd]
    aliases = {}
    if y_prev is not None:
        in_specs.append(pl.BlockSpec(memory_space=pl.ANY))
        aliases = {len(args): 0}
        args.append(y_prev)
    grid_spec = pltpu.PrefetchScalarGridSpec(
        num_scalar_prefetch=2,
        grid=(n_call,),
        in_specs=in_specs,
        out_specs=pl.BlockSpec((MOE_ROWS, half), lambda b, be, nbu: (block_off + b, 0)),
        scratch_shapes=[
            pltpu.VMEM((d, ff), BF16),
            pltpu.VMEM((d, ff), BF16),
            pltpu.VMEM((ff, d), BF16),
        ],
    )
    return pl.pallas_call(
        functools.partial(_experts_kernel, block_off=block_off),
        grid_spec=grid_spec,
        out_shape=jax.ShapeDtypeStruct((n_blocks * MOE_ROWS, half), jnp.int32),
        input_output_aliases=aliases,
        compiler_params=_params(("arbitrary",)),
        name="experts",
    )(*args)


def _combine_ln_kernel(*refs, alpha, emit_next):
    y_ref, gw_ref, tok_ref, x_ref, sg_ref, su_ref, sd_ref, gate_ref, lng_ref, lnb_ref = refs[:10]
    if emit_next:
        sc_ref, sh_ref, xo_ref, u_ref = refs[10:]
    else:
        (xo_ref,) = refs[10:]
    lo, hi = _unpack_bf16_pairs(tok_ref[...])
    hg = _dot_halves(lo, hi, sg_ref)
    hu = _dot_halves(lo, hi, su_ref)
    h = hg * (1.0 / (1.0 + jnp.exp(-hg))) * hu
    gw = gw_ref[...]
    f_lo = f_hi = None
    for k in range(TOP_K):
        y_lo, y_hi = _unpack_bf16_pairs(y_ref[k])
        w = gw[:, k:k + 1]
        f_lo = y_lo.astype(F32) * w if f_lo is None else f_lo + y_lo.astype(F32) * w
        f_hi = y_hi.astype(F32) * w if f_hi is None else f_hi + y_hi.astype(F32) * w
    f = jnp.concatenate([f_lo, f_hi], axis=-1) + jnp.dot(h.astype(BF16), sd_ref[...], preferred_element_type=F32)
    xn = _layer_norm(alpha * x_ref[...] + gate_ref[...] * f, lng_ref[...], lnb_ref[...])
    xo_ref[...] = xn
    if emit_next:
        u_ref[...] = (xn * (1.0 + sc_ref[...]) + sh_ref[...]).astype(u_ref.dtype)


def _combine_ln(y3, gw_t, tok, x, sg, su, sd, mod, ln_g, ln_b, mod_next, *, t, tm, alpha, mod_row):
    d = x.shape[1]
    emit_next = mod_next is not None
    in_specs = [
        pl.BlockSpec((TOP_K, tm, d // 2), lambda i: (0, i, 0)),
        pl.BlockSpec((tm, SUBLANES), lambda i: (i, 0)),
        pl.BlockSpec((tm, d // 2), lambda i: (i, 0)),
        pl.BlockSpec((tm, d), lambda i: (i, 0)),
        _const_spec(sg.shape), _const_spec(su.shape), _const_spec(sd.shape),
        _mod_spec(d, 5, mod_row, tm),
        _const_spec((1, d)), _const_spec((1, d)),
    ]
    args = [y3, gw_t, tok, x, sg, su, sd, mod, ln_g, ln_b]
    out_specs = [pl.BlockSpec((tm, d), lambda i: (i, 0))]
    out_shape = [jax.ShapeDtypeStruct((t, d), F32)]
    if emit_next:
        in_specs += [_mod_spec(d, 1, mod_row, tm), _mod_spec(d, 0, mod_row, tm)]
        args += [mod_next, mod_next]
        out_specs.append(pl.BlockSpec((tm, d), lambda i: (i, 0)))
        out_shape.append(jax.ShapeDtypeStruct((t, d), BF16))
    return pl.pallas_call(
        functools.partial(_combine_ln_kernel, alpha=alpha, emit_next=emit_next),
        grid=(t // tm,),
        in_specs=in_specs,
        out_specs=out_specs,
        out_shape=out_shape,
        compiler_params=_params(("parallel",)),
        name="combine_ln",
    )(*args)


def _moe(tok, x, t, layer, router_w, router_b, wg, wu, wd, sg, su, sd, mod, ln_g, ln_b, mod_next, *, alpha, mod_row,
         tm):
    half = tok.shape[1]
    tt = 512
    idx, gw, rank, cnt = _router(tok, router_w.T.astype(BF16), router_b.reshape(N_EXPERTS, 1), t=t, tt=tt)
    n_asg = t * TOP_K
    counts = cnt[:, 0].astype(jnp.int32)
    padded = (counts + MOE_ROWS - 1) // MOE_ROWS * MOE_ROWS
    pend = jnp.cumsum(padded)
    pstart = pend - padded
    ustart = jnp.cumsum(counts) - counts
    sc_rows = V7X_SC_CORES * V7X_SC_SUBCORES * SC_GATHER_ROWS * 2
    blocks_granule = 2 * max(sc_rows // MOE_ROWS, 1)
    assert (blocks_granule // 2 * MOE_ROWS) % sc_rows == 0
    n_blocks = -(-((n_asg + N_EXPERTS * (MOE_ROWS - 1)) // MOE_ROWS + 1) // blocks_granule) * blocks_granule
    assert n_asg % sc_rows == 0
    block_start = jnp.arange(n_blocks, dtype=jnp.int32) * MOE_ROWS
    block_e = jnp.minimum(jnp.sum((pend[None, :] <= block_start[:, None]).astype(jnp.int32), axis=1), N_EXPERTS - 1)
    nb_used = (pend[-1] // MOE_ROWS).astype(jnp.int32).reshape(1)
    pos = _assign_slots(idx, rank, pstart.astype(F32).reshape(N_EXPERTS, 1), t=t, tt=tt)[:TOP_K].reshape(-1)
    tok_of_asg = jnp.tile(jnp.arange(t, dtype=jnp.int32), TOP_K)
    _, tok_sorted = lax.sort((pos, tok_of_asg), num_keys=1)
    block_valid = jnp.clip(counts[block_e] - (block_start - pstart[block_e]), 0, MOE_ROWS)
    block_shift = (pstart - ustart)[block_e]
    row = jnp.arange(MOE_ROWS, dtype=jnp.int32)[None, :]
    compact = jnp.clip(block_start[:, None] + row - block_shift[:, None], 0, n_asg - 1)
    spread = (block_start[:, None] + row) % t
    slot_tok = jnp.where(row < block_valid[:, None], tok_sorted[compact], spread).reshape(-1)
    n_first = n_blocks // 2
    xs0 = _sc_gather_rows(tok, slot_tok[:n_first * MOE_ROWS])
    xs1 = _sc_gather_rows(tok, slot_tok[n_first * MOE_ROWS:])
    y = _experts(xs0, block_e, nb_used, wg, wu, wd, layer, None, block_off=0, n_blocks=n_blocks)
    y = _experts(xs1, block_e, nb_used, wg, wu, wd, layer, y, block_off=n_first, n_blocks=n_blocks)
    y3 = _sc_gather_rows(y, pos).reshape(TOP_K, t, half)
    return _combine_ln(y3, gw.T, tok, x, sg, su, sd, mod, ln_g, ln_b, mod_next, t=t, tm=tm, alpha=alpha,
                       mod_row=mod_row)


def _rope64(r, c_ref, sa_ref, sb_ref):
    return r * c_ref[...] + pltpu.roll(r, LANES - QK_ROPE // 2, 1) * sa_ref[...] + pltpu.roll(r, QK_ROPE // 2, 1) * sb_ref[...]


def _mla_q_kernel(d_ref, gain_ref, w_ref, c_ref, sa_ref, sb_ref, q_ref, *, scale):
    n = _rms(d_ref[...], gain_ref[...]).astype(BF16)
    q = jnp.dot(n, w_ref[...], preferred_element_type=F32)
    for h in range(MLA_HEADS):
        lo = h * MLA_DK_PAD
        q_ref[:, lo:lo + QK_NOPE] = (q[:, lo:lo + QK_NOPE] * scale).astype(q_ref.dtype)
        r = _rope64(q[:, lo + QK_NOPE:lo + MLA_DK_PAD], c_ref, sa_ref, sb_ref)
        q_ref[:, lo + QK_NOPE:lo + MLA_DK_PAD] = (r * scale).astype(q_ref.dtype)


def _mla_kv_kernel(ckv_ref, kr_ref, gain_ref, wk_ref, wv_ref, c_ref, sa_ref, sb_ref, k_ref, v_ref):
    n = _rms(ckv_ref[...], gain_ref[...]).astype(BF16)
    kn = jnp.dot(n, wk_ref[...], preferred_element_type=F32)
    v_ref[...] = jnp.dot(n, wv_ref[...], preferred_element_type=F32).astype(v_ref.dtype)
    kr = _rope64(kr_ref[...], c_ref, sa_ref, sb_ref).astype(k_ref.dtype)
    for h in range(MLA_HEADS):
        lo = h * MLA_DK_PAD
        k_ref[:, lo:lo + QK_NOPE] = kn[:, h * QK_NOPE:(h + 1) * QK_NOPE].astype(k_ref.dtype)
        k_ref[:, lo + QK_NOPE:lo + MLA_DK_PAD] = kr


def _axial_angles(n_tok, rot_dim):
    rows = n_tok // GRID_W
    n_freq = rot_dim // 4
    inv = ROPE_THETA ** (-jnp.arange(n_freq, dtype=F32) / n_freq)
    row = jnp.repeat(jnp.arange(rows, dtype=F32), GRID_W)
    col = jnp.tile(jnp.arange(GRID_W, dtype=F32), rows)
    return jnp.concatenate([row[:, None] * inv, col[:, None] * inv], axis=-1)


def _rope_tables_128(n_tok, ident_rows):
    ang = _axial_angles(n_tok, HEAD_DIM)
    cos, sin = jnp.cos(ang), jnp.sin(ang)
    c = jnp.concatenate([cos, cos], axis=-1)
    s = jnp.concatenate([-sin, sin], axis=-1)
    c = jnp.concatenate([c, jnp.ones((ident_rows, HEAD_DIM), F32)], axis=0)
    s = jnp.concatenate([s, jnp.zeros((ident_rows, HEAD_DIM), F32)], axis=0)
    return c, s


def _rope_tables_64(n_tok, ident_rows):
    ang = _axial_angles(n_tok, QK_ROPE)
    cos, sin = jnp.cos(ang), jnp.sin(ang)
    half = QK_ROPE // 2
    z = jnp.zeros((n_tok, LANES - QK_ROPE), F32)
    zh = jnp.zeros((n_tok, half), F32)
    c = jnp.concatenate([cos, cos, z], axis=-1)
    sa = jnp.concatenate([-sin, zh, z], axis=-1)
    sb = jnp.concatenate([zh, sin, z], axis=-1)
    ci = jnp.concatenate([jnp.ones((ident_rows, QK_ROPE), F32), jnp.zeros((ident_rows, LANES - QK_ROPE), F32)], axis=-1)
    zi = jnp.zeros((ident_rows, LANES), F32)
    return jnp.concatenate([c, ci], 0), jnp.concatenate([sa, zi], 0), jnp.concatenate([sb, zi], 0)


def kernel(x, c, ctx, c_ctx, w_ada, b_ada, ln_g, ln_b, a_w_in, a_conv_w, a_q_gain, a_k_gain, a_w_out, m_w_down, m_q_gain, m_kv_gain, m_w_uq, m_w_ukv, m_w_out, router_w, router_b, e_w_gate, e_w_up, e_w_down, s_w_gate, s_w_up, s_w_down):
    batch, seq, d = x.shape
    ctx_len = ctx.shape[1]
    depth = w_ada.shape[0]
    assert depth == 2, "one conv+GQA layer followed by one MLA layer"
    alpha = (2 * depth) ** 0.25
    t_lat = batch * seq
    t_ctx = batch * ctx_len
    t_all = t_lat + t_ctx
    tr = 256
    assert seq % tr == 0 and ctx_len % tr == 0 and seq % GRID_W == 0
    lat_tiles = t_lat // tr
    lat_seq_tiles = seq // tr
    ctx_seq_tiles = ctx_len // tr
    lk = ctx_len + seq

    def mod_row(r):
        return jnp.minimum(r // seq, batch)

    def kv_block(i):
        is_lat = i < lat_tiles
        cidx = i - lat_tiles
        b = jnp.where(is_lat, i // lat_seq_tiles, cidx // ctx_seq_tiles)
        rb = jnp.where(is_lat, ctx_seq_tiles + i % lat_seq_tiles, cidx % ctx_seq_tiles)
        return b, rb

    def pos_block(i):
        return jnp.where(i < lat_tiles, i % lat_seq_tiles, lat_seq_tiles)

    rows = -(-(batch + 1) // SUBLANES) * SUBLANES
    cond = jnp.concatenate([c, c_ctx[None, :], jnp.zeros((rows - batch - 1, d), F32)], axis=0)
    mod = _ada_table(cond, w_ada, b_ada).reshape(depth, rows, 1, 6 * d)

    x_all = jnp.concatenate([x.reshape(t_lat, d), ctx.reshape(t_ctx, d)], axis=0)

    u0 = _modulate(x_all, mod[0], mod_row, tr)
    proj = _matmul(u0, a_w_in[0].astype(BF16), BF16, 1024 if t_all % 1024 == 0 else 512,
                   768)

    cos128, sin128 = _rope_tables_128(seq, tr)
    d_q = ATT_HEADS * HEAD_DIM
    d_kv = ATT_KV_HEADS * HEAD_DIM
    qkv_w = d_q + 2 * d_kv
    qkv_blk = 3 * CONV_DIM // qkv_w
    assert qkv_blk * qkv_w == 3 * CONV_DIM
    q0, k0, v0 = pl.pallas_call(
        functools.partial(_qkprep_kernel, scale=1.0 / math.sqrt(HEAD_DIM)),
        grid=(t_all // tr,),
        in_specs=[
            pl.BlockSpec((tr, qkv_w), lambda i: (i, qkv_blk)),
            pl.BlockSpec((tr, HEAD_DIM), lambda i: (pos_block(i), 0)),
            pl.BlockSpec((tr, HEAD_DIM), lambda i: (pos_block(i), 0)),
            _const_spec((1, HEAD_DIM)), _const_spec((1, HEAD_DIM)),
        ],
        out_specs=[
            pl.BlockSpec((tr, d_q), lambda i: (i, 0)),
            pl.BlockSpec((None, tr, d_kv), lambda i: (*kv_block(i), 0)),
            pl.BlockSpec((None, tr, d_kv), lambda i: (*kv_block(i), 0)),
        ],
        out_shape=[
            jax.ShapeDtypeStruct((t_all, d_q), BF16),
            jax.ShapeDtypeStruct((batch, lk, d_kv), BF16),
            jax.ShapeDtypeStruct((batch, lk, d_kv), BF16),
        ],
        compiler_params=_params(("parallel",)),
        name="qk_prep",
    )(proj, cos128, sin128, a_q_gain[0].reshape(1, HEAD_DIM), a_k_gain[0].reshape(1, HEAD_DIM))

    grp = ATT_HEADS // ATT_KV_HEADS
    att_lat = _attention(q0, k0, v0, batch=batch, sq=seq, lk=lk, n_kv=ATT_KV_HEADS, group=grp, dk=HEAD_DIM,
                         dv=HEAD_DIM, tq=256, rows=256, q_row_off=0)
    att_ctx = _attention(q0, k0, v0, batch=batch, sq=ctx_len, lk=ctx_len, n_kv=ATT_KV_HEADS, group=grp,
                         dk=HEAD_DIM, dv=HEAD_DIM, tq=256, rows=256, q_row_off=t_lat)
    att0 = jnp.concatenate([att_lat, att_ctx], axis=0)

    conv0 = _conv_gate(proj, a_conv_w[0], t=t_all, tm=tr, tc=512, lat_tiles=lat_tiles,
                       lat_seq_tiles=lat_seq_tiles, ctx_seq_tiles=ctx_seq_tiles)

    w_out0 = a_w_out[0].astype(BF16)
    x1, tok0 = _outproj_ln([conv0, att0], [w_out0[:CONV_DIM], w_out0[CONV_DIM:]], x_all, mod[0],
                           ln_g[0, 0].reshape(1, d), ln_b[0, 0].reshape(1, d), t=t_all, tm=tr, alpha=alpha,
                           mod_row=mod_row)

    x2, u1 = _moe(tok0, x1, t_all, 0, router_w[0], router_b[0], e_w_gate, e_w_up, e_w_down,
                  s_w_gate[0].astype(BF16), s_w_up[0].astype(BF16), s_w_down[0].astype(BF16), mod[0],
                  ln_g[0, 1].reshape(1, d), ln_b[0, 1].reshape(1, d), mod[1], alpha=alpha,
                  mod_row=mod_row, tm=256)

    n_down = Q_LORA + KV_LORA + QK_ROPE
    n_down_pad = -(-n_down // LANES) * LANES
    w_down = jnp.pad(m_w_down[0], ((0, 0), (0, n_down_pad - n_down))).astype(BF16)
    down = _matmul(u1, w_down, F32, 512, n_down_pad)

    dqk = QK_NOPE + QK_ROPE
    w_uq = m_w_uq[0].reshape(Q_LORA, MLA_HEADS, dqk)
    w_uq = jnp.pad(w_uq, ((0, 0), (0, 0), (0, MLA_DK_PAD - dqk))).reshape(Q_LORA, MLA_HEADS * MLA_DK_PAD).astype(BF16)
    w_ukv = m_w_ukv[0].reshape(KV_LORA, MLA_HEADS, QK_NOPE + V_DIM)
    w_uk = w_ukv[:, :, :QK_NOPE].reshape(KV_LORA, MLA_HEADS * QK_NOPE).astype(BF16)
    w_uv = w_ukv[:, :, QK_NOPE:].reshape(KV_LORA, MLA_HEADS * V_DIM).astype(BF16)

    c64, sa64, sb64 = _rope_tables_64(seq, tr)
    rope_specs = [pl.BlockSpec((tr, LANES), lambda i: (pos_block(i), 0))] * 3
    q1 = pl.pallas_call(
        functools.partial(_mla_q_kernel, scale=1.0 / math.sqrt(dqk)),
        grid=(lat_tiles,),
        in_specs=[
            pl.BlockSpec((tr, Q_LORA), lambda i: (i, 0)),
            _const_spec((1, Q_LORA)),
            _const_spec(w_uq.shape),
        ] + rope_specs,
        out_specs=pl.BlockSpec((tr, MLA_HEADS * MLA_DK_PAD), lambda i: (i, 0)),
        out_shape=jax.ShapeDtypeStruct((t_lat, MLA_HEADS * MLA_DK_PAD), BF16),
        compiler_params=_params(("parallel",)),
        name="mla_q",
    )(down, m_q_gain[0].reshape(1, Q_LORA), w_uq, c64, sa64, sb64)

    assert KV_LORA == Q_LORA and (Q_LORA + KV_LORA) % LANES == 0
    k1, v1 = pl.pallas_call(
        _mla_kv_kernel,
        grid=(t_all // tr,),
        in_specs=[
            pl.BlockSpec((tr, KV_LORA), lambda i: (i, 1)),
            pl.BlockSpec((tr, LANES), lambda i: (i, (Q_LORA + KV_LORA) // LANES)),
            _const_spec((1, KV_LORA)),
            _const_spec(w_uk.shape), _const_spec(w_uv.shape),
        ] + rope_specs,
        out_specs=[
            pl.BlockSpec((None, tr, MLA_HEADS * MLA_DK_PAD), lambda i: (*kv_block(i), 0)),
            pl.BlockSpec((None, tr, MLA_HEADS * V_DIM), lambda i: (*kv_block(i), 0)),
        ],
        out_shape=[
            jax.ShapeDtypeStruct((batch, lk, MLA_HEADS * MLA_DK_PAD), BF16),
            jax.ShapeDtypeStruct((batch, lk, MLA_HEADS * V_DIM), BF16),
        ],
        compiler_params=_params(("parallel",)),
        name="mla_kv",
    )(down, down, m_kv_gain[0].reshape(1, KV_LORA), w_uk, w_uv, c64, sa64, sb64)

    att1 = _attention(q1, k1, v1, batch=batch, sq=seq, lk=lk, n_kv=MLA_HEADS, group=1, dk=MLA_DK_PAD, dv=V_DIM,
                      tq=min(1024, seq), rows=256, q_row_off=0)

    x3, tok1 = _outproj_ln([att1], [m_w_out[0].astype(BF16)], x2, mod[1], ln_g[1, 0].reshape(1, d),
                           ln_b[1, 0].reshape(1, d), t=t_lat, tm=tr, alpha=alpha, mod_row=mod_row)

    (x4,) = _moe(tok1, x3, t_lat, 1, router_w[1], router_b[1], e_w_gate, e_w_up, e_w_down,
                 s_w_gate[1].astype(BF16), s_w_up[1].astype(BF16), s_w_down[1].astype(BF16), mod[1],
                 ln_g[1, 1].reshape(1, d), ln_b[1, 1].reshape(1, d), None, alpha=alpha,
                 mod_row=mod_row, tm=256)
    return x4.reshape(batch, seq, d)
```

```python
import functools
import math

import jax
import jax.numpy as jnp
from jax import lax
from jax.experimental import pallas as pl
from jax.experimental.pallas import tpu as pltpu
from jax.experimental.pallas import tpu_sc as plsc

F32 = jnp.float32
BF16 = jnp.bfloat16

GRID_W = 64
CONV_DIM = 1024
ATT_HEADS = 8
ATT_KV_HEADS = 2
HEAD_DIM = 128
MLA_HEADS = 16
Q_LORA = 512
KV_LORA = 512
QK_NOPE = 128
QK_ROPE = 64
V_DIM = 128
N_EXPERTS = 64
TOP_K = 6
N_GROUPS = 8
TOPK_GROUPS = 4
ROUTED_SCALE = 2.5
ROPE_THETA = 10000.0
LN_EPS = 1e-5
RMS_EPS = 1e-6

V7X_VMEM_LIMIT_BYTES = 56 * 1024 * 1024
LANES = 128
SUBLANES = 8
MOE_ROWS = 512
MOE_DISPATCH_PARTS = 4
MOE_COMBINE_PARTS = 2
V7X_SC_CORES = 2
V7X_SC_SUBCORES = 16
SC_GATHER_ROWS = 16
SC_GATHER_BUFFERS = 4
MLA_DK_PAD = 256


def _params(sem):
    return pltpu.CompilerParams(dimension_semantics=sem, vmem_limit_bytes=V7X_VMEM_LIMIT_BYTES)


def _const_spec(shape):
    nd = len(shape)
    return pl.BlockSpec(shape, lambda *_: (0,) * nd)


def _ada_kernel(s_ref, w_ref, b_ref, o_ref):
    s = s_ref[...]
    s = s * (1.0 / (1.0 + jnp.exp(-s)))
    o_ref[...] = jnp.dot(s.astype(BF16), w_ref[...].astype(BF16), preferred_element_type=F32) + b_ref[...]


def _ada_table(cond, w_ada, b_ada):
    depth, d, n = w_ada.shape
    r = cond.shape[0]
    tn = 1024
    return pl.pallas_call(
        _ada_kernel,
        grid=(depth, n // tn),
        in_specs=[
            pl.BlockSpec((r, d), lambda l, j: (0, 0)),
            pl.BlockSpec((None, d, tn), lambda l, j: (l, 0, j)),
            pl.BlockSpec((None, 1, tn), lambda l, j: (l, 0, j)),
        ],
        out_specs=pl.BlockSpec((None, r, tn), lambda l, j: (l, 0, j)),
        out_shape=jax.ShapeDtypeStruct((depth, r, n), F32),
        compiler_params=_params(("parallel", "parallel")),
        name="ada_table",
    )(cond, w_ada, b_ada.reshape(depth, 1, n))


def _mod_spec(d, chunk, mod_row, tm):
    return pl.BlockSpec((None, 1, d), lambda i: (mod_row(i * tm), 0, chunk))


def _modulate_kernel(x_ref, sc_ref, sh_ref, o_ref):
    o_ref[...] = (x_ref[...] * (1.0 + sc_ref[...]) + sh_ref[...]).astype(o_ref.dtype)


def _modulate(x, mod, mod_row, tm):
    t, d = x.shape
    return pl.pallas_call(
        _modulate_kernel,
        grid=(t // tm,),
        in_specs=[
            pl.BlockSpec((tm, d), lambda i: (i, 0)),
            _mod_spec(d, 1, mod_row, tm),
            _mod_spec(d, 0, mod_row, tm),
        ],
        out_specs=pl.BlockSpec((tm, d), lambda i: (i, 0)),
        out_shape=jax.ShapeDtypeStruct((t, d), BF16),
        compiler_params=_params(("parallel",)),
        name="modulate",
    )(x, mod, mod)


def _mm_kernel(a_ref, w_ref, o_ref):
    o_ref[...] = jnp.dot(a_ref[...], w_ref[...], preferred_element_type=F32).astype(o_ref.dtype)


def _matmul(a, w, out_dtype, tm, tn):
    m, k = a.shape
    n = w.shape[1]
    return pl.pallas_call(
        _mm_kernel,
        grid=(m // tm, n // tn),
        in_specs=[
            pl.BlockSpec((tm, k), lambda i, j: (i, 0)),
            pl.BlockSpec((k, tn), lambda i, j: (0, j)),
        ],
        out_specs=pl.BlockSpec((tm, tn), lambda i, j: (i, j)),
        out_shape=jax.ShapeDtypeStruct((m, n), out_dtype),
        compiler_params=_params(("parallel", "parallel")),
        name="matmul",
    )(a, w)


def _rms(t, gain):
    return t * lax.rsqrt(jnp.mean(t * t, axis=-1, keepdims=True) + RMS_EPS) * gain


def _qkprep_kernel(p_ref, cos_ref, sin_ref, qg_ref, kg_ref, q_ref, k_ref, v_ref, *, scale):
    cos = cos_ref[...]
    sin = sin_ref[...]

    def norm_rope(t, gain):
        y = _rms(t.astype(F32), gain)
        return y * cos + pltpu.roll(y, HEAD_DIM // 2, 1) * sin

    for h in range(ATT_HEADS):
        sl = slice(h * HEAD_DIM, (h + 1) * HEAD_DIM)
        q_ref[:, sl] = (norm_rope(p_ref[:, sl], qg_ref[...]) * scale).astype(q_ref.dtype)
    k0 = ATT_HEADS * HEAD_DIM
    for h in range(ATT_KV_HEADS):
        sl = slice(h * HEAD_DIM, (h + 1) * HEAD_DIM)
        k_ref[:, sl] = norm_rope(p_ref[:, k0 + h * HEAD_DIM:k0 + (h + 1) * HEAD_DIM], kg_ref[...]).astype(k_ref.dtype)
    v0 = k0 + ATT_KV_HEADS * HEAD_DIM
    v_ref[...] = p_ref[:, v0:v0 + ATT_KV_HEADS * HEAD_DIM].astype(v_ref.dtype)


def _attn_kernel(q_ref, k_ref, v_ref, o_ref, *, group, tq, rows, dk, dv):
    k = k_ref[...]
    v = v_ref[...]
    for h in range(group):
        for r in range(0, tq, rows):
            q = q_ref[r:r + rows, h * dk:(h + 1) * dk]
            s = lax.dot_general(q, k, (((1,), (1,)), ((), ())), preferred_element_type=F32)
            m = jnp.max(s, axis=-1, keepdims=True)
            p = jnp.exp(s - m)
            l = jnp.sum(p, axis=-1, keepdims=True)
            o = jnp.dot(p.astype(v.dtype), v, preferred_element_type=F32)
            o_ref[r:r + rows, h * dv:(h + 1) * dv] = (o / l).astype(o_ref.dtype)


def _attention(q, k, v, *, batch, sq, lk, n_kv, group, dk, dv, tq, rows, q_row_off):
    nq = sq // tq
    off = q_row_off // tq
    assert tq % rows == 0 and q_row_off % tq == 0 and sq % tq == 0
    return pl.pallas_call(
        functools.partial(_attn_kernel, group=group, tq=tq, rows=rows, dk=dk, dv=dv),
        grid=(batch, n_kv, nq),
        in_specs=[
            pl.BlockSpec((tq, group * dk), lambda b, g, i: (off + b * nq + i, g)),
            pl.BlockSpec((None, lk, dk), lambda b, g, i: (b, 0, g)),
            pl.BlockSpec((None, lk, dv), lambda b, g, i: (b, 0, g)),
        ],
        out_specs=pl.BlockSpec((tq, group * dv), lambda b, g, i: (b * nq + i, g)),
        out_shape=jax.ShapeDtypeStruct((batch * sq, n_kv * group * dv), BF16),
        compiler_params=_params(("parallel", "parallel", "parallel")),
        name="attention",
    )(q, k, v)


def _conv_kernel(gb_ref, gc_ref, hv_ref, gcp_ref, hvp_ref, gcn_ref, hvn_ref, w_ref, o_ref, *,
                 tm, lat_tiles, lat_seq_tiles, ctx_seq_tiles):
    i = pl.program_id(0)
    is_lat = i < lat_tiles
    pos = jnp.where(is_lat, i % lat_seq_tiles, (i - lat_tiles) % ctx_seq_tiles)
    seq_tiles = jnp.where(is_lat, lat_seq_tiles, ctx_seq_tiles)
    not_first = (pos != 0).astype(F32)
    not_last = (pos != seq_tiles - 1).astype(F32)
    p = gc_ref[...].astype(F32) * hv_ref[...].astype(F32)
    halo_prev = gcp_ref[SUBLANES - 1:SUBLANES, :].astype(F32) * hvp_ref[SUBLANES - 1:SUBLANES, :].astype(F32) * not_first
    halo_next = gcn_ref[0:1, :].astype(F32) * hvn_ref[0:1, :].astype(F32) * not_last
    row = lax.broadcasted_iota(jnp.int32, p.shape, 0)
    prev = jnp.where(row == 0, halo_prev, pltpu.roll(p, 1, 0))
    nxt = jnp.where(row == tm - 1, halo_next, pltpu.roll(p, tm - 1, 0))
    w = w_ref[...]
    conv = w[0:1, :] * prev + w[1:2, :] * p + w[2:3, :] * nxt
    o_ref[...] = (gb_ref[...].astype(F32) * conv).astype(o_ref.dtype)


def _conv_gate(p, conv_w, *, t, tm, tc, lat_tiles, lat_seq_tiles, ctx_seq_tiles):
    nct = CONV_DIM // tc
    hb = tm // SUBLANES
    n_halo = t // SUBLANES

    def cur(part):
        return pl.BlockSpec((tm, tc), lambda i, j: (i, part * nct + j))

    def prev(part):
        return pl.BlockSpec((SUBLANES, tc), lambda i, j: (jnp.maximum(i * hb - 1, 0), part * nct + j))

    def nxt(part):
        return pl.BlockSpec((SUBLANES, tc), lambda i, j: (jnp.minimum((i + 1) * hb, n_halo - 1), part * nct + j))

    return pl.pallas_call(
        functools.partial(_conv_kernel, tm=tm, lat_tiles=lat_tiles, lat_seq_tiles=lat_seq_tiles,
                          ctx_seq_tiles=ctx_seq_tiles),
        grid=(t // tm, nct),
        in_specs=[cur(0), cur(1), cur(2), prev(1), prev(2), nxt(1), nxt(2),
                  pl.BlockSpec((3, tc), lambda i, j: (0, j))],
        out_specs=pl.BlockSpec((tm, tc), lambda i, j: (i, j)),
        out_shape=jax.ShapeDtypeStruct((t, CONV_DIM), BF16),
        compiler_params=_params(("parallel", "parallel")),
        name="conv_gate",
    )(p, p, p, p, p, p, p, conv_w)


def _layer_norm(z, g, b):
    mu = jnp.mean(z, axis=-1, keepdims=True)
    zc = z - mu
    var = jnp.mean(zc * zc, axis=-1, keepdims=True)
    return zc * lax.rsqrt(var + LN_EPS) * g + b


def _pack_bf16_pairs(x):
    half = x.shape[1] // 2
    lo = lax.bitcast_convert_type(x[:, :half].astype(BF16).astype(F32), jnp.uint32) >> 16
    hi = lax.bitcast_convert_type(x[:, half:].astype(BF16).astype(F32), jnp.uint32) & jnp.uint32(0xFFFF0000)
    return lax.bitcast_convert_type(lo | hi, jnp.int32)


def _unpack_bf16_pairs(w):
    u = lax.bitcast_convert_type(w, jnp.uint32)
    lo = lax.bitcast_convert_type(u << 16, F32).astype(BF16)
    hi = lax.bitcast_convert_type(u & jnp.uint32(0xFFFF0000), F32).astype(BF16)
    return lo, hi


def _dot_halves(lo, hi, w_ref):
    half = lo.shape[1]
    return (jnp.dot(lo, w_ref[:half, :], preferred_element_type=F32)
            + jnp.dot(hi, w_ref[half:, :], preferred_element_type=F32))


def _outproj_ln_kernel(*refs, n_a, alpha):
    a_refs = refs[:n_a]
    w_refs = refs[n_a:2 * n_a]
    x_ref, gate_ref, lng_ref, lnb_ref, sc_ref, sh_ref, xo_ref, tok_ref = refs[2 * n_a:]
    y = jnp.dot(a_refs[0][...], w_refs[0][...], preferred_element_type=F32)
    for a_ref, w_ref in zip(a_refs[1:], w_refs[1:]):
        y = y + jnp.dot(a_ref[...], w_ref[...], preferred_element_type=F32)
    xn = _layer_norm(alpha * x_ref[...] + gate_ref[...] * y, lng_ref[...], lnb_ref[...])
    xo_ref[...] = xn
    tok_ref[...] = _pack_bf16_pairs(xn * (1.0 + sc_ref[...]) + sh_ref[...])


def _outproj_ln(a_list, w_list, x, mod, ln_g, ln_b, *, t, tm, alpha, mod_row):
    d = x.shape[1]
    n_a = len(a_list)
    in_specs = [pl.BlockSpec((tm, a.shape[1]), lambda i: (i, 0)) for a in a_list]
    in_specs += [_const_spec(w.shape) for w in w_list]
    in_specs += [
        pl.BlockSpec((tm, d), lambda i: (i, 0)),
        _mod_spec(d, 2, mod_row, tm),
        _const_spec((1, d)), _const_spec((1, d)),
        _mod_spec(d, 4, mod_row, tm),
        _mod_spec(d, 3, mod_row, tm),
    ]
    return pl.pallas_call(
        functools.partial(_outproj_ln_kernel, n_a=n_a, alpha=alpha),
        grid=(t // tm,),
        in_specs=in_specs,
        out_specs=[pl.BlockSpec((tm, d), lambda i: (i, 0)), pl.BlockSpec((tm, d // 2), lambda i: (i, 0))],
        out_shape=[jax.ShapeDtypeStruct((t, d), F32), jax.ShapeDtypeStruct((t, d // 2), jnp.int32)],
        compiler_params=_params(("parallel",)),
        name="outproj_ln",
    )(*a_list, *w_list, x, mod, ln_g, ln_b, mod, mod)


def _router_kernel(t_ref, rw_ref, rb_ref, tri_ref, idx_ref, gw_ref, rank_ref, cnt_ref):
    @pl.when(pl.program_id(0) == 0)
    def _():
        cnt_ref[...] = jnp.zeros_like(cnt_ref)

    lo, hi = _unpack_bf16_pairs(t_ref[...])
    half = lo.shape[1]
    nt = (((1,), (1,)), ((), ()))
    logits = (lax.dot_general(rw_ref[:, :half], lo, nt, preferred_element_type=F32)
              + lax.dot_general(rw_ref[:, half:], hi, nt, preferred_element_type=F32))
    scores = 1.0 / (1.0 + jnp.exp(-logits))
    sel = scores + rb_ref[...]
    gsz = N_EXPERTS // N_GROUPS
    neg = -jnp.inf
    sub = lax.broadcasted_iota(jnp.int32, (gsz, sel.shape[1]), 0)
    slabs = [sel[g * gsz:(g + 1) * gsz, :] for g in range(N_GROUPS)]
    gscore = []
    for s in slabs:
        m1 = jnp.max(s, axis=0, keepdims=True)
        a1 = jnp.min(jnp.where(s == m1, sub, gsz), axis=0, keepdims=True)
        m2 = jnp.max(jnp.where(sub == a1, neg, s), axis=0, keepdims=True)
        gscore.append(m1 + m2)
    masked = []
    for g in range(N_GROUPS):
        ahead = jnp.zeros(gscore[g].shape, jnp.int32)
        for h in range(N_GROUPS):
            if h == g:
                continue
            beats = gscore[h] >= gscore[g] if h < g else gscore[h] > gscore[g]
            ahead = ahead + beats.astype(jnp.int32)
        masked.append(jnp.where(ahead < TOPK_GROUPS, slabs[g], neg))
    cur = jnp.concatenate(masked, axis=0)
    eio = lax.broadcasted_iota(jnp.int32, cur.shape, 0)
    picks, weights = [], []
    for _ in range(TOP_K):
        m = jnp.max(cur, axis=0, keepdims=True)
        a = jnp.min(jnp.where(cur == m, eio, N_EXPERTS), axis=0, keepdims=True)
        hit = eio == a
        picks.append(a)
        weights.append(jnp.sum(jnp.where(hit, scores, 0.0), axis=0, keepdims=True))
        cur = jnp.where(hit, neg, cur)
    total = weights[0]
    for w in weights[1:]:
        total = total + w
    for k in range(TOP_K):
        idx_ref[k:k + 1, :] = picks[k]
        gw_ref[k:k + 1, :] = weights[k] / total * ROUTED_SCALE
    for k in range(TOP_K, SUBLANES):
        idx_ref[k:k + 1, :] = jnp.zeros_like(picks[0])
        gw_ref[k:k + 1, :] = jnp.zeros_like(weights[0])
        rank_ref[k:k + 1, :] = jnp.zeros_like(picks[0])
    base = cnt_ref[:, 0:1]
    for k in range(TOP_K):
        onehot = jnp.where(eio == picks[k], 1.0, 0.0)
        before = jnp.dot(onehot.astype(BF16), tri_ref[...], preferred_element_type=F32)
        rank_ref[k:k + 1, :] = jnp.sum(onehot * (before + base), axis=0, keepdims=True).astype(jnp.int32)
        base = base + jnp.sum(onehot, axis=1, keepdims=True)
    cnt_ref[...] = jnp.broadcast_to(base, cnt_ref.shape)


def _router(tok, rw_t, rb, *, t, tt):
    half = tok.shape[1]
    tri = (jnp.arange(tt)[:, None] < jnp.arange(tt)[None, :]).astype(BF16)
    blk = pl.BlockSpec((SUBLANES, tt), lambda i: (0, i))
    return pl.pallas_call(
        _router_kernel,
        grid=(t // tt,),
        in_specs=[
            pl.BlockSpec((tt, half), lambda i: (i, 0)),
            _const_spec((N_EXPERTS, 2 * half)),
            _const_spec((N_EXPERTS, 1)),
            _const_spec((tt, tt)),
        ],
        out_specs=[blk, blk, blk, _const_spec((N_EXPERTS, LANES))],
        out_shape=[jax.ShapeDtypeStruct((SUBLANES, t), jnp.int32), jax.ShapeDtypeStruct((SUBLANES, t), F32),
                   jax.ShapeDtypeStruct((SUBLANES, t), jnp.int32), jax.ShapeDtypeStruct((N_EXPERTS, LANES), F32)],
        compiler_params=_params(("arbitrary",)),
        name="router",
    )(tok, rw_t, rb, tri)


def _slots_kernel(idx_ref, rank_ref, start_ref, pos_ref):
    start = start_ref[...]
    eio = lax.broadcasted_iota(jnp.int32, (N_EXPERTS, idx_ref.shape[1]), 0)
    for k in range(TOP_K):
        seg = jnp.sum(jnp.where(eio == idx_ref[k:k + 1, :], start, 0.0), axis=0, keepdims=True)
        pos_ref[k:k + 1, :] = rank_ref[k:k + 1, :] + seg.astype(jnp.int32)
    for k in range(TOP_K, SUBLANES):
        pos_ref[k:k + 1, :] = jnp.zeros((1, idx_ref.shape[1]), jnp.int32)


def _assign_slots(idx, rank, seg_start, *, t, tt):
    blk = pl.BlockSpec((SUBLANES, tt), lambda i: (0, i))
    return pl.pallas_call(
        _slots_kernel,
        grid=(t // tt,),
        in_specs=[blk, blk, _const_spec((N_EXPERTS, 1))],
        out_specs=blk,
        out_shape=jax.ShapeDtypeStruct((SUBLANES, t), jnp.int32),
        compiler_params=_params(("parallel",)),
        name="assign_slots",
    )(idx, rank, seg_start)


def _sc_gather_rows(table, idx):
    n = idx.shape[0]
    d = table.shape[1]
    n_workers = V7X_SC_CORES * V7X_SC_SUBCORES
    per_w = n // n_workers
    n_chunks = per_w // SC_GATHER_ROWS
    assert per_w * n_workers == n and n_chunks * SC_GATHER_ROWS == per_w
    n_buf = next(b for b in range(SC_GATHER_BUFFERS, 1, -1) if n_chunks % b == 0)
    mesh = plsc.VectorSubcoreMesh(core_axis_name="c", subcore_axis_name="s", num_cores=V7X_SC_CORES,
                                  num_subcores=V7X_SC_SUBCORES)

    @functools.partial(
        pl.kernel,
        out_type=jax.ShapeDtypeStruct((n, d), table.dtype),
        mesh=mesh,
        scratch_types=[
            pltpu.VMEM((per_w,), jnp.int32),
            pltpu.VMEM((n_buf, SC_GATHER_ROWS, d), table.dtype),
            pltpu.SemaphoreType.DMA((n_buf,)),
            pltpu.SemaphoreType.DMA((n_buf,)),
        ],
        name="sc_gather_rows",
    )
    def gather(table_hbm, idx_hbm, out_hbm, idx_v, rows_v, gsem, wsem):
        wid = lax.axis_index("s") * V7X_SC_CORES + lax.axis_index("c")
        base = wid * per_w
        pltpu.sync_copy(idx_hbm.at[pl.ds(base, per_w)], idx_v)

        def gather_copy(c, b):
            return pltpu.make_async_copy(table_hbm.at[idx_v.at[pl.ds(c * SC_GATHER_ROWS, SC_GATHER_ROWS)]],
                                         rows_v.at[b], gsem.at[b])

        def write_copy(c, b):
            return pltpu.make_async_copy(rows_v.at[b], out_hbm.at[pl.ds(base + c * SC_GATHER_ROWS, SC_GATHER_ROWS)],
                                         wsem.at[b])

        for b in range(n_buf - 1):
            gather_copy(b, b).start()

        @pl.loop(0, n_chunks, step=n_buf)
        def _(g):
            for b in range(n_buf):
                c = g + b
                prev = (b + n_buf - 1) % n_buf
                gather_copy(c, b).wait()
                write_copy(c, b).start()

                @pl.when(c >= 1)
                def _():
                    write_copy(c - 1, prev).wait()

                @pl.when(c + n_buf - 1 < n_chunks)
                def _():
                    gather_copy(c + n_buf - 1, prev).start()

        write_copy(n_chunks - 1, (n_chunks - 1) % n_buf).wait()

    return gather(table, idx)


def _experts_kernel(be_ref, nbu_ref, x_ref, wg_ref, wu_ref, wd_ref, *rest, block_off):
    y_ref, wgb, wub, wdb = rest[-4:]
    step = pl.program_id(0)
    b = block_off + step
    nbu = nbu_ref[0]

    @pl.when(b < nbu)
    def _():
        changed = jnp.logical_or(step == 0, be_ref[b] != be_ref[jnp.maximum(b - 1, 0)])

        @pl.when(changed)
        def _():
            wgb[...] = wg_ref[...].astype(BF16)
            wub[...] = wu_ref[...].astype(BF16)
            wdb[...] = wd_ref[...].astype(BF16)

        lo, hi = _unpack_bf16_pairs(x_ref[...])
        hg = _dot_halves(lo, hi, wgb)
        hu = _dot_halves(lo, hi, wub)
        h = hg * (1.0 / (1.0 + jnp.exp(-hg))) * hu
        y_ref[...] = _pack_bf16_pairs(jnp.dot(h.astype(BF16), wdb[...], preferred_element_type=F32))

    @pl.when(b >= nbu)
    def _():
        y_ref[...] = jnp.zeros_like(y_ref)


def _experts(xs, block_e, nb_used, wg, wu, wd, layer, y_prev, *, block_off, n_blocks):
    half = xs.shape[1]
    d = 2 * half
    ff = wg.shape[3]
    n_call = xs.shape[0] // MOE_ROWS

    def used(b, nbu):
        return jnp.clip(jnp.minimum(block_off + b, nbu[0] - 1) - block_off, 0, n_call - 1)

    def expert(b, be):
        return be[block_off + b]

    in_specs = [
        pl.BlockSpec((MOE_ROWS, half), lambda b, be, nbu: (used(b, nbu), 0)),
        pl.BlockSpec((None, None, d, ff), lambda b, be, nbu: (layer, expert(b, be), 0, 0)),
        pl.BlockSpec((None, None, d, ff), lambda b, be, nbu: (layer, expert(b, be), 0, 0)),
        pl.BlockSpec((None, None, ff, d), lambda b, be, nbu: (layer, expert(b, be), 0, 0)),
    ]
    args = [block_e, nb_used, xs, wg, wu, wd]
    aliases = {}
    if y_prev is not None:
        in_specs.append(pl.BlockSpec(memory_space=pl.ANY))
        aliases = {len(args): 0}
        args.append(y_prev)
    grid_spec = pltpu.PrefetchScalarGridSpec(
        num_scalar_prefetch=2,
        grid=(n_call,),
        in_specs=in_specs,
        out_specs=pl.BlockSpec((MOE_ROWS, half), lambda b, be, nbu: (block_off + b, 0)),
        scratch_shapes=[
            pltpu.VMEM((d, ff), BF16),
            pltpu.VMEM((d, ff), BF16),
            pltpu.VMEM((ff, d), BF16),
        ],
    )
    return pl.pallas_call(
        functools.partial(_experts_kernel, block_off=block_off),
        grid_spec=grid_spec,
        out_shape=jax.ShapeDtypeStruct((n_blocks * MOE_ROWS, half), jnp.int32),
        input_output_aliases=aliases,
        compiler_params=_params(("arbitrary",)),
        name="experts",
    )(*args)


def _combine_ln_kernel(*refs, alpha, emit_next, n_prev):
    y_ref, gw_ref, tok_ref, x_ref, sg_ref, su_ref, sd_ref, gate_ref, lng_ref, lnb_ref = refs[:10]
    outs = refs[len(refs) - (2 if emit_next else 1):]
    if emit_next:
        sc_ref, sh_ref = refs[10:12]
        xo_ref, u_ref = outs
    else:
        (xo_ref,) = outs
    lo, hi = _unpack_bf16_pairs(tok_ref[...])
    hg = _dot_halves(lo, hi, sg_ref)
    hu = _dot_halves(lo, hi, su_ref)
    h = hg * (1.0 / (1.0 + jnp.exp(-hg))) * hu
    gw = gw_ref[...]
    f_lo = f_hi = None
    for k in range(TOP_K):
        y_lo, y_hi = _unpack_bf16_pairs(y_ref[k])
        w = gw[:, k:k + 1]
        f_lo = y_lo.astype(F32) * w if f_lo is None else f_lo + y_lo.astype(F32) * w
        f_hi = y_hi.astype(F32) * w if f_hi is None else f_hi + y_hi.astype(F32) * w
    f = jnp.concatenate([f_lo, f_hi], axis=-1) + jnp.dot(h.astype(BF16), sd_ref[...], preferred_element_type=F32)
    xn = _layer_norm(alpha * x_ref[...] + gate_ref[...] * f, lng_ref[...], lnb_ref[...])
    xo_ref[...] = xn
    if emit_next:
        u_ref[...] = (xn * (1.0 + sc_ref[...]) + sh_ref[...]).astype(u_ref.dtype)


def _combine_ln(y3, gw_t, tok, x, sg, su, sd, mod, ln_g, ln_b, mod_next, prev, *, t, row_off, tm, alpha, mod_row):
    d = x.shape[1]
    emit_next = mod_next is not None
    off = row_off // tm
    assert off * tm == row_off

    def rows(i):
        return (off + i, 0)

    def part_mod_row(r):
        return mod_row(r + row_off)

    in_specs = [
        pl.BlockSpec((TOP_K, tm, d // 2), lambda i: (0, i, 0)),
        pl.BlockSpec((tm, SUBLANES), rows),
        pl.BlockSpec((tm, d // 2), rows),
        pl.BlockSpec((tm, d), rows),
        _const_spec(sg.shape), _const_spec(su.shape), _const_spec(sd.shape),
        _mod_spec(d, 5, part_mod_row, tm),
        _const_spec((1, d)), _const_spec((1, d)),
    ]
    args = [y3, gw_t, tok, x, sg, su, sd, mod, ln_g, ln_b]
    out_specs = [pl.BlockSpec((tm, d), rows)]
    out_shape = [jax.ShapeDtypeStruct((t, d), F32)]
    if emit_next:
        in_specs += [_mod_spec(d, 1, part_mod_row, tm), _mod_spec(d, 0, part_mod_row, tm)]
        args += [mod_next, mod_next]
        out_specs.append(pl.BlockSpec((tm, d), rows))
        out_shape.append(jax.ShapeDtypeStruct((t, d), BF16))
    aliases = {}
    if prev is not None:
        for k, p in enumerate(prev):
            in_specs.append(pl.BlockSpec(memory_space=pl.ANY))
            aliases[len(args)] = k
            args.append(p)
    return pl.pallas_call(
        functools.partial(_combine_ln_kernel, alpha=alpha, emit_next=emit_next, n_prev=len(aliases)),
        grid=(y3.shape[1] // tm,),
        in_specs=in_specs,
        out_specs=out_specs,
        out_shape=out_shape,
        input_output_aliases=aliases,
        compiler_params=_params(("parallel",)),
        name="combine_ln",
    )(*args)


def _moe(tok, x, t, layer, router_w, router_b, wg, wu, wd, sg, su, sd, mod, ln_g, ln_b, mod_next, *, alpha, mod_row,
         tm):
    half = tok.shape[1]
    tt = 512
    idx, gw, rank, cnt = _router(tok, router_w.T.astype(BF16), router_b.reshape(N_EXPERTS, 1), t=t, tt=tt)
    n_asg = t * TOP_K
    counts = cnt[:, 0].astype(jnp.int32)
    padded = (counts + MOE_ROWS - 1) // MOE_ROWS * MOE_ROWS
    pend = jnp.cumsum(padded)
    pstart = pend - padded
    ustart = jnp.cumsum(counts) - counts
    sc_rows = V7X_SC_CORES * V7X_SC_SUBCORES * SC_GATHER_ROWS * 2
    blocks_granule = MOE_DISPATCH_PARTS * max(sc_rows // MOE_ROWS, 1)
    assert (blocks_granule // MOE_DISPATCH_PARTS * MOE_ROWS) % sc_rows == 0
    n_blocks = -(-((n_asg + N_EXPERTS * (MOE_ROWS - 1)) // MOE_ROWS + 1) // blocks_granule) * blocks_granule
    assert n_asg % sc_rows == 0
    block_start = jnp.arange(n_blocks, dtype=jnp.int32) * MOE_ROWS
    block_e = jnp.minimum(jnp.sum((pend[None, :] <= block_start[:, None]).astype(jnp.int32), axis=1), N_EXPERTS - 1)
    nb_used = (pend[-1] // MOE_ROWS).astype(jnp.int32).reshape(1)
    pos2d = _assign_slots(idx, rank, pstart.astype(F32).reshape(N_EXPERTS, 1), t=t, tt=tt)[:TOP_K]
    pos = pos2d.reshape(-1)
    tok_of_asg = jnp.tile(jnp.arange(t, dtype=jnp.int32), TOP_K)
    _, tok_sorted = lax.sort((pos, tok_of_asg), num_keys=1)
    block_valid = jnp.clip(counts[block_e] - (block_start - pstart[block_e]), 0, MOE_ROWS)
    block_shift = (pstart - ustart)[block_e]
    row = jnp.arange(MOE_ROWS, dtype=jnp.int32)[None, :]
    compact = jnp.clip(block_start[:, None] + row - block_shift[:, None], 0, n_asg - 1)
    spread = (block_start[:, None] + row) % t
    slot_tok = jnp.where(row < block_valid[:, None], tok_sorted[compact], spread).reshape(-1)
    per = n_blocks // MOE_DISPATCH_PARTS * MOE_ROWS
    xs = [_sc_gather_rows(tok, slot_tok[i * per:(i + 1) * per]) for i in range(MOE_DISPATCH_PARTS)]
    y = None
    for i in range(MOE_DISPATCH_PARTS):
        y = _experts(xs[i], block_e, nb_used, wg, wu, wd, layer, y, block_off=i * per // MOE_ROWS, n_blocks=n_blocks)
    n_cparts = MOE_COMBINE_PARTS if (t // MOE_COMBINE_PARTS * TOP_K) % sc_rows == 0 else 1
    t_part = t // n_cparts
    y3 = [_sc_gather_rows(y, pos2d[:, i * t_part:(i + 1) * t_part].reshape(-1)).reshape(TOP_K, t_part, half)
          for i in range(n_cparts)]
    outs = None
    gw_t = gw.T
    for i in range(n_cparts):
        outs = _combine_ln(y3[i], gw_t, tok, x, sg, su, sd, mod, ln_g, ln_b, mod_next, outs, t=t,
                           row_off=i * t_part, tm=tm, alpha=alpha, mod_row=mod_row)
    return outs


def _rope64(r, c_ref, sa_ref, sb_ref):
    return r * c_ref[...] + pltpu.roll(r, LANES - QK_ROPE // 2, 1) * sa_ref[...] + pltpu.roll(r, QK_ROPE // 2, 1) * sb_ref[...]


def _mla_q_kernel(d_ref, gain_ref, w_ref, c_ref, sa_ref, sb_ref, q_ref, *, scale):
    n = _rms(d_ref[...], gain_ref[...]).astype(BF16)
    q = jnp.dot(n, w_ref[...], preferred_element_type=F32)
    for h in range(MLA_HEADS):
        lo = h * MLA_DK_PAD
        q_ref[:, lo:lo + QK_NOPE] = (q[:, lo:lo + QK_NOPE] * scale).astype(q_ref.dtype)
        r = _rope64(q[:, lo + QK_NOPE:lo + MLA_DK_PAD], c_ref, sa_ref, sb_ref)
        q_ref[:, lo + QK_NOPE:lo + MLA_DK_PAD] = (r * scale).astype(q_ref.dtype)


def _mla_kv_kernel(ckv_ref, kr_ref, gain_ref, wk_ref, wv_ref, c_ref, sa_ref, sb_ref, k_ref, v_ref):
    n = _rms(ckv_ref[...], gain_ref[...]).astype(BF16)
    kn = jnp.dot(n, wk_ref[...], preferred_element_type=F32)
    v_ref[...] = jnp.dot(n, wv_ref[...], preferred_element_type=F32).astype(v_ref.dtype)
    kr = _rope64(kr_ref[...], c_ref, sa_ref, sb_ref).astype(k_ref.dtype)
    for h in range(MLA_HEADS):
        lo = h * MLA_DK_PAD
        k_ref[:, lo:lo + QK_NOPE] = kn[:, h * QK_NOPE:(h + 1) * QK_NOPE].astype(k_ref.dtype)
        k_ref[:, lo + QK_NOPE:lo + MLA_DK_PAD] = kr


def _axial_angles(n_tok, rot_dim):
    rows = n_tok // GRID_W
    n_freq = rot_dim // 4
    inv = ROPE_THETA ** (-jnp.arange(n_freq, dtype=F32) / n_freq)
    row = jnp.repeat(jnp.arange(rows, dtype=F32), GRID_W)
    col = jnp.tile(jnp.arange(GRID_W, dtype=F32), rows)
    return jnp.concatenate([row[:, None] * inv, col[:, None] * inv], axis=-1)


def _rope_tables_128(n_tok, ident_rows):
    ang = _axial_angles(n_tok, HEAD_DIM)
    cos, sin = jnp.cos(ang), jnp.sin(ang)
    c = jnp.concatenate([cos, cos], axis=-1)
    s = jnp.concatenate([-sin, sin], axis=-1)
    c = jnp.concatenate([c, jnp.ones((ident_rows, HEAD_DIM), F32)], axis=0)
    s = jnp.concatenate([s, jnp.zeros((ident_rows, HEAD_DIM), F32)], axis=0)
    return c, s


def _rope_tables_64(n_tok, ident_rows):
    ang = _axial_angles(n_tok, QK_ROPE)
    cos, sin = jnp.cos(ang), jnp.sin(ang)
    half = QK_ROPE // 2
    z = jnp.zeros((n_tok, LANES - QK_ROPE), F32)
    zh = jnp.zeros((n_tok, half), F32)
    c = jnp.concatenate([cos, cos, z], axis=-1)
    sa = jnp.concatenate([-sin, zh, z], axis=-1)
    sb = jnp.concatenate([zh, sin, z], axis=-1)
    ci = jnp.concatenate([jnp.ones((ident_rows, QK_ROPE), F32), jnp.zeros((ident_rows, LANES - QK_ROPE), F32)], axis=-1)
    zi = jnp.zeros((ident_rows, LANES), F32)
    return jnp.concatenate([c, ci], 0), jnp.concatenate([sa, zi], 0), jnp.concatenate([sb, zi], 0)


def kernel(x, c, ctx, c_ctx, w_ada, b_ada, ln_g, ln_b, a_w_in, a_conv_w, a_q_gain, a_k_gain, a_w_out, m_w_down, m_q_gain, m_kv_gain, m_w_uq, m_w_ukv, m_w_out, router_w, router_b, e_w_gate, e_w_up, e_w_down, s_w_gate, s_w_up, s_w_down):
    batch, seq, d = x.shape
    ctx_len = ctx.shape[1]
    depth = w_ada.shape[0]
    assert depth == 2, "one conv+GQA layer followed by one MLA layer"
    alpha = (2 * depth) ** 0.25
    t_lat = batch * seq
    t_ctx = batch * ctx_len
    t_all = t_lat + t_ctx
    tr = 256
    assert seq % tr == 0 and ctx_len % tr == 0 and seq % GRID_W == 0
    lat_tiles = t_lat // tr
    lat_seq_tiles = seq // tr
    ctx_seq_tiles = ctx_len // tr
    lk = ctx_len + seq

    def mod_row(r):
        return jnp.minimum(r // seq, batch)

    def kv_block(i):
        is_lat = i < lat_tiles
        cidx = i - lat_tiles
        b = jnp.where(is_lat, i // lat_seq_tiles, cidx // ctx_seq_tiles)
        rb = jnp.where(is_lat, ctx_seq_tiles + i % lat_seq_tiles, cidx % ctx_seq_tiles)
        return b, rb

    def pos_block(i):
        return jnp.where(i < lat_tiles, i % lat_seq_tiles, lat_seq_tiles)

    rows = -(-(batch + 1) // SUBLANES) * SUBLANES
    cond = jnp.concatenate([c, c_ctx[None, :], jnp.zeros((rows - batch - 1, d), F32)], axis=0)
    mod = _ada_table(cond, w_ada, b_ada).reshape(depth, rows, 1, 6 * d)

    x_all = jnp.concatenate([x.reshape(t_lat, d), ctx.reshape(t_ctx, d)], axis=0)

    u0 = _modulate(x_all, mod[0], mod_row, tr)
    proj = _matmul(u0, a_w_in[0].astype(BF16), BF16, 1024 if t_all % 1024 == 0 else 512,
                   768)

    cos128, sin128 = _rope_tables_128(seq, tr)
    d_q = ATT_HEADS * HEAD_DIM
    d_kv = ATT_KV_HEADS * HEAD_DIM
    qkv_w = d_q + 2 * d_kv
    qkv_blk = 3 * CONV_DIM // qkv_w
    assert qkv_blk * qkv_w == 3 * CONV_DIM
    q0, k0, v0 = pl.pallas_call(
        functools.partial(_qkprep_kernel, scale=1.0 / math.sqrt(HEAD_DIM)),
        grid=(t_all // tr,),
        in_specs=[
            pl.BlockSpec((tr, qkv_w), lambda i: (i, qkv_blk)),
            pl.BlockSpec((tr, HEAD_DIM), lambda i: (pos_block(i), 0)),
            pl.BlockSpec((tr, HEAD_DIM), lambda i: (pos_block(i), 0)),
            _const_spec((1, HEAD_DIM)), _const_spec((1, HEAD_DIM)),
        ],
        out_specs=[
            pl.BlockSpec((tr, d_q), lambda i: (i, 0)),
            pl.BlockSpec((None, tr, d_kv), lambda i: (*kv_block(i), 0)),
            pl.BlockSpec((None, tr, d_kv), lambda i: (*kv_block(i), 0)),
        ],
        out_shape=[
            jax.ShapeDtypeStruct((t_all, d_q), BF16),
            jax.ShapeDtypeStruct((batch, lk, d_kv), BF16),
            jax.ShapeDtypeStruct((batch, lk, d_kv), BF16),
        ],
        compiler_params=_params(("parallel",)),
        name="qk_prep",
    )(proj, cos128, sin128, a_q_gain[0].reshape(1, HEAD_DIM), a_k_gain[0].reshape(1, HEAD_DIM))

    grp = ATT_HEADS // ATT_KV_HEADS
    att_lat = _attention(q0, k0, v0, batch=batch, sq=seq, lk=lk, n_kv=ATT_KV_HEADS, group=grp, dk=HEAD_DIM,
                         dv=HEAD_DIM, tq=256, rows=256, q_row_off=0)
    att_ctx = _attention(q0, k0, v0, batch=batch, sq=ctx_len, lk=ctx_len, n_kv=ATT_KV_HEADS, group=grp,
                         dk=HEAD_DIM, dv=HEAD_DIM, tq=256, rows=256, q_row_off=t_lat)
    att0 = jnp.concatenate([att_lat, att_ctx], axis=0)

    conv0 = _conv_gate(proj, a_conv_w[0], t=t_all, tm=tr, tc=512, lat_tiles=lat_tiles,
                       lat_seq_tiles=lat_seq_tiles, ctx_seq_tiles=ctx_seq_tiles)

    w_out0 = a_w_out[0].astype(BF16)
    x1, tok0 = _outproj_ln([conv0, att0], [w_out0[:CONV_DIM], w_out0[CONV_DIM:]], x_all, mod[0],
                           ln_g[0, 0].reshape(1, d), ln_b[0, 0].reshape(1, d), t=t_all, tm=tr, alpha=alpha,
                           mod_row=mod_row)

    x2, u1 = _moe(tok0, x1, t_all, 0, router_w[0], router_b[0], e_w_gate, e_w_up, e_w_down,
                  s_w_gate[0].astype(BF16), s_w_up[0].astype(BF16), s_w_down[0].astype(BF16), mod[0],
                  ln_g[0, 1].reshape(1, d), ln_b[0, 1].reshape(1, d), mod[1], alpha=alpha,
                  mod_row=mod_row, tm=256)

    n_down = Q_LORA + KV_LORA + QK_ROPE
    n_down_pad = -(-n_down // LANES) * LANES
    w_down = jnp.pad(m_w_down[0], ((0, 0), (0, n_down_pad - n_down))).astype(BF16)
    down = _matmul(u1, w_down, F32, 512, n_down_pad)

    dqk = QK_NOPE + QK_ROPE
    w_uq = m_w_uq[0].reshape(Q_LORA, MLA_HEADS, dqk)
    w_uq = jnp.pad(w_uq, ((0, 0), (0, 0), (0, MLA_DK_PAD - dqk))).reshape(Q_LORA, MLA_HEADS * MLA_DK_PAD).astype(BF16)
    w_ukv = m_w_ukv[0].reshape(KV_LORA, MLA_HEADS, QK_NOPE + V_DIM)
    w_uk = w_ukv[:, :, :QK_NOPE].reshape(KV_LORA, MLA_HEADS * QK_NOPE).astype(BF16)
    w_uv = w_ukv[:, :, QK_NOPE:].reshape(KV_LORA, MLA_HEADS * V_DIM).astype(BF16)

    c64, sa64, sb64 = _rope_tables_64(seq, tr)
    rope_specs = [pl.BlockSpec((tr, LANES), lambda i: (pos_block(i), 0))] * 3
    q1 = pl.pallas_call(
        functools.partial(_mla_q_kernel, scale=1.0 / math.sqrt(dqk)),
        grid=(lat_tiles,),
        in_specs=[
            pl.BlockSpec((tr, Q_LORA), lambda i: (i, 0)),
            _const_spec((1, Q_LORA)),
            _const_spec(w_uq.shape),
        ] + rope_specs,
        out_specs=pl.BlockSpec((tr, MLA_HEADS * MLA_DK_PAD), lambda i: (i, 0)),
        out_shape=jax.ShapeDtypeStruct((t_lat, MLA_HEADS * MLA_DK_PAD), BF16),
        compiler_params=_params(("parallel",)),
        name="mla_q",
    )(down, m_q_gain[0].reshape(1, Q_LORA), w_uq, c64, sa64, sb64)

    assert KV_LORA == Q_LORA and (Q_LORA + KV_LORA) % LANES == 0
    k1, v1 = pl.pallas_call(
        _mla_kv_kernel,
        grid=(t_all // tr,),
        in_specs=[
            pl.BlockSpec((tr, KV_LORA), lambda i: (i, 1)),
            pl.BlockSpec((tr, LANES), lambda i: (i, (Q_LORA + KV_LORA) // LANES)),
            _const_spec((1, KV_LORA)),
            _const_spec(w_uk.shape), _const_spec(w_uv.shape),
        ] + rope_specs,
        out_specs=[
            pl.BlockSpec((None, tr, MLA_HEADS * MLA_DK_PAD), lambda i: (*kv_block(i), 0)),
            pl.BlockSpec((None, tr, MLA_HEADS * V_DIM), lambda i: (*kv_block(i), 0)),
        ],
        out_shape=[
            jax.ShapeDtypeStruct((batch, lk, MLA_HEADS * MLA_DK_PAD), BF16),
            jax.ShapeDtypeStruct((batch, lk, MLA_HEADS * V_DIM), BF16),
        ],
        compiler_params=_params(("parallel",)),
        name="mla_kv",
    )(down, down, m_kv_gain[0].reshape(1, KV_LORA), w_uk, w_uv, c64, sa64, sb64)

    att1 = _attention(q1, k1, v1, batch=batch, sq=seq, lk=lk, n_kv=MLA_HEADS, group=1, dk=MLA_DK_PAD, dv=V_DIM,
                      tq=min(1024, seq), rows=256, q_row_off=0)

    x3, tok1 = _outproj_ln([att1], [m_w_out[0].astype(BF16)], x2, mod[1], ln_g[1, 0].reshape(1, d),
                           ln_b[1, 0].reshape(1, d), t=t_lat, tm=tr, alpha=alpha, mod_row=mod_row)

    (x4,) = _moe(tok1, x3, t_lat, 1, router_w[1], router_b[1], e_w_gate, e_w_up, e_w_down,
                 s_w_gate[1].astype(BF16), s_w_up[1].astype(BF16), s_w_down[1].astype(BF16), mod[1],
                 ln_g[1, 1].reshape(1, d), ln_b[1, 1].reshape(1, d), None, alpha=alpha,
                 mod_row=mod_row, tm=256)
    return x4.reshape(batch, seq, d)
```

```python
import functools
import math

import jax
import jax.numpy as jnp
from jax import lax
from jax.experimental import pallas as pl
from jax.experimental.pallas import tpu as pltpu
from jax.experimental.pallas import tpu_sc as plsc

F32 = jnp.float32
BF16 = jnp.bfloat16

GRID_W = 64
CONV_DIM = 1024
ATT_HEADS = 8
ATT_KV_HEADS = 2
HEAD_DIM = 128
MLA_HEADS = 16
Q_LORA = 512
KV_LORA = 512
QK_NOPE = 128
QK_ROPE = 64
V_DIM = 128
N_EXPERTS = 64
TOP_K = 6
N_GROUPS = 8
TOPK_GROUPS = 4
ROUTED_SCALE = 2.5
ROPE_THETA = 10000.0
LN_EPS = 1e-5
RMS_EPS = 1e-6

V7X_VMEM_LIMIT_BYTES = 56 * 1024 * 1024
LANES = 128
SUBLANES = 8
MOE_ROWS = 512
MOE_DISPATCH_PARTS = 4
MOE_COMBINE_PARTS = 2
V7X_SC_CORES = 2
V7X_SC_SUBCORES = 16
SC_GATHER_ROWS = 16
SC_GATHER_BUFFERS = 4
MLA_DK_PAD = 256


def _params(sem):
    return pltpu.CompilerParams(dimension_semantics=sem, vmem_limit_bytes=V7X_VMEM_LIMIT_BYTES)


def _const_spec(shape):
    nd = len(shape)
    return pl.BlockSpec(shape, lambda *_: (0,) * nd)


def _ada_kernel(s_ref, w_ref, b_ref, o_ref):
    s = s_ref[...]
    s = s * (1.0 / (1.0 + jnp.exp(-s)))
    o_ref[...] = jnp.dot(s.astype(BF16), w_ref[...].astype(BF16), preferred_element_type=F32) + b_ref[...]


def _ada_table(cond, w_ada, b_ada):
    depth, d, n = w_ada.shape
    r = cond.shape[0]
    tn = 1024
    return pl.pallas_call(
        _ada_kernel,
        grid=(depth, n // tn),
        in_specs=[
            pl.BlockSpec((r, d), lambda l, j: (0, 0)),
            pl.BlockSpec((None, d, tn), lambda l, j: (l, 0, j)),
            pl.BlockSpec((None, 1, tn), lambda l, j: (l, 0, j)),
        ],
        out_specs=pl.BlockSpec((None, r, tn), lambda l, j: (l, 0, j)),
        out_shape=jax.ShapeDtypeStruct((depth, r, n), F32),
        compiler_params=_params(("parallel", "parallel")),
        name="ada_table",
    )(cond, w_ada, b_ada.reshape(depth, 1, n))


def _mod_spec(d, chunk, mod_row, tm):
    return pl.BlockSpec((None, 1, d), lambda i: (mod_row(i * tm), 0, chunk))


def _mm_kernel(a_ref, w_ref, o_ref):
    o_ref[...] = jnp.dot(a_ref[...], w_ref[...], preferred_element_type=F32).astype(o_ref.dtype)


def _matmul(a, w, out_dtype, tm, tn):
    m, k = a.shape
    n = w.shape[1]
    return pl.pallas_call(
        _mm_kernel,
        grid=(m // tm, n // tn),
        in_specs=[
            pl.BlockSpec((tm, k), lambda i, j: (i, 0)),
            pl.BlockSpec((k, tn), lambda i, j: (0, j)),
        ],
        out_specs=pl.BlockSpec((tm, tn), lambda i, j: (i, j)),
        out_shape=jax.ShapeDtypeStruct((m, n), out_dtype),
        compiler_params=_params(("parallel", "parallel")),
        name="matmul",
    )(a, w)


def _mod_mm_kernel(x_ref, sc_ref, sh_ref, w_ref, o_ref, u_ref):
    @pl.when(pl.program_id(1) == 0)
    def _():
        u_ref[...] = (x_ref[...] * (1.0 + sc_ref[...]) + sh_ref[...]).astype(u_ref.dtype)

    o_ref[...] = jnp.dot(u_ref[...], w_ref[...], preferred_element_type=F32).astype(o_ref.dtype)


def _mod_matmul(x, mod, w, out_dtype, tm, tn, mod_row):
    m, k = x.shape
    n = w.shape[1]

    def mod_spec(chunk):
        return pl.BlockSpec((None, 1, k), lambda i, j: (mod_row(i * tm), 0, chunk))

    return pl.pallas_call(
        _mod_mm_kernel,
        grid=(m // tm, n // tn),
        in_specs=[
            pl.BlockSpec((tm, k), lambda i, j: (i, 0)),
            mod_spec(1), mod_spec(0),
            pl.BlockSpec((k, tn), lambda i, j: (0, j)),
        ],
        out_specs=pl.BlockSpec((tm, tn), lambda i, j: (i, j)),
        out_shape=jax.ShapeDtypeStruct((m, n), out_dtype),
        scratch_shapes=[pltpu.VMEM((tm, k), w.dtype)],
        compiler_params=_params(("parallel", "arbitrary")),
        name="mod_matmul",
    )(x, mod, mod, w)


def _rms(t, gain):
    return t * lax.rsqrt(jnp.mean(t * t, axis=-1, keepdims=True) + RMS_EPS) * gain


def _qkprep_kernel(p_ref, cos_ref, sin_ref, qg_ref, kg_ref, q_ref, k_ref, v_ref, *, scale):
    cos = cos_ref[...]
    sin = sin_ref[...]

    def norm_rope(t, gain):
        y = _rms(t.astype(F32), gain)
        return y * cos + pltpu.roll(y, HEAD_DIM // 2, 1) * sin

    for h in range(ATT_HEADS):
        sl = slice(h * HEAD_DIM, (h + 1) * HEAD_DIM)
        q_ref[:, sl] = (norm_rope(p_ref[:, sl], qg_ref[...]) * scale).astype(q_ref.dtype)
    k0 = ATT_HEADS * HEAD_DIM
    for h in range(ATT_KV_HEADS):
        sl = slice(h * HEAD_DIM, (h + 1) * HEAD_DIM)
        k_ref[:, sl] = norm_rope(p_ref[:, k0 + h * HEAD_DIM:k0 + (h + 1) * HEAD_DIM], kg_ref[...]).astype(k_ref.dtype)
    v0 = k0 + ATT_KV_HEADS * HEAD_DIM
    v_ref[...] = p_ref[:, v0:v0 + ATT_KV_HEADS * HEAD_DIM].astype(v_ref.dtype)


def _attn_kernel(q_ref, k_ref, v_ref, o_ref, *, group, tq, rows, dk, dv):
    k = k_ref[...]
    v = v_ref[...]
    for h in range(group):
        for r in range(0, tq, rows):
            q = q_ref[r:r + rows, h * dk:(h + 1) * dk]
            s = lax.dot_general(q, k, (((1,), (1,)), ((), ())), preferred_element_type=F32)
            m = jnp.max(s, axis=-1, keepdims=True)
            p = jnp.exp(s - m)
            l = jnp.sum(p, axis=-1, keepdims=True)
            o = jnp.dot(p.astype(v.dtype), v, preferred_element_type=F32)
            o_ref[r:r + rows, h * dv:(h + 1) * dv] = (o / l).astype(o_ref.dtype)


def _attention(q, k, v, *, batch, sq, lk, n_kv, group, dk, dv, tq, rows, q_row_off):
    nq = sq // tq
    off = q_row_off // tq
    assert tq % rows == 0 and q_row_off % tq == 0 and sq % tq == 0
    return pl.pallas_call(
        functools.partial(_attn_kernel, group=group, tq=tq, rows=rows, dk=dk, dv=dv),
        grid=(batch, n_kv, nq),
        in_specs=[
            pl.BlockSpec((tq, group * dk), lambda b, g, i: (off + b * nq + i, g)),
            pl.BlockSpec((None, lk, dk), lambda b, g, i: (b, 0, g)),
            pl.BlockSpec((None, lk, dv), lambda b, g, i: (b, 0, g)),
        ],
        out_specs=pl.BlockSpec((tq, group * dv), lambda b, g, i: (b * nq + i, g)),
        out_shape=jax.ShapeDtypeStruct((batch * sq, n_kv * group * dv), BF16),
        compiler_params=_params(("parallel", "parallel", "parallel")),
        name="attention",
    )(q, k, v)


def _conv_kernel(gb_ref, gc_ref, hv_ref, gcp_ref, hvp_ref, gcn_ref, hvn_ref, w_ref, o_ref, *,
                 tm, lat_tiles, lat_seq_tiles, ctx_seq_tiles):
    i = pl.program_id(0)
    is_lat = i < lat_tiles
    pos = jnp.where(is_lat, i % lat_seq_tiles, (i - lat_tiles) % ctx_seq_tiles)
    seq_tiles = jnp.where(is_lat, lat_seq_tiles, ctx_seq_tiles)
    not_first = (pos != 0).astype(F32)
    not_last = (pos != seq_tiles - 1).astype(F32)
    p = gc_ref[...].astype(F32) * hv_ref[...].astype(F32)
    halo_prev = gcp_ref[SUBLANES - 1:SUBLANES, :].astype(F32) * hvp_ref[SUBLANES - 1:SUBLANES, :].astype(F32) * not_first
    halo_next = gcn_ref[0:1, :].astype(F32) * hvn_ref[0:1, :].astype(F32) * not_last
    row = lax.broadcasted_iota(jnp.int32, p.shape, 0)
    prev = jnp.where(row == 0, halo_prev, pltpu.roll(p, 1, 0))
    nxt = jnp.where(row == tm - 1, halo_next, pltpu.roll(p, tm - 1, 0))
    w = w_ref[...]
    conv = w[0:1, :] * prev + w[1:2, :] * p + w[2:3, :] * nxt
    o_ref[...] = (gb_ref[...].astype(F32) * conv).astype(o_ref.dtype)


def _conv_gate(p, conv_w, *, t, tm, tc, lat_tiles, lat_seq_tiles, ctx_seq_tiles):
    nct = CONV_DIM // tc
    hb = tm // SUBLANES
    n_halo = t // SUBLANES

    def cur(part):
        return pl.BlockSpec((tm, tc), lambda i, j: (i, part * nct + j))

    def prev(part):
        return pl.BlockSpec((SUBLANES, tc), lambda i, j: (jnp.maximum(i * hb - 1, 0), part * nct + j))

    def nxt(part):
        return pl.BlockSpec((SUBLANES, tc), lambda i, j: (jnp.minimum((i + 1) * hb, n_halo - 1), part * nct + j))

    return pl.pallas_call(
        functools.partial(_conv_kernel, tm=tm, lat_tiles=lat_tiles, lat_seq_tiles=lat_seq_tiles,
                          ctx_seq_tiles=ctx_seq_tiles),
        grid=(t // tm, nct),
        in_specs=[cur(0), cur(1), cur(2), prev(1), prev(2), nxt(1), nxt(2),
                  pl.BlockSpec((3, tc), lambda i, j: (0, j))],
        out_specs=pl.BlockSpec((tm, tc), lambda i, j: (i, j)),
        out_shape=jax.ShapeDtypeStruct((t, CONV_DIM), BF16),
        compiler_params=_params(("parallel", "parallel")),
        name="conv_gate",
    )(p, p, p, p, p, p, p, conv_w)


def _layer_norm(z, g, b):
    mu = jnp.mean(z, axis=-1, keepdims=True)
    zc = z - mu
    var = jnp.mean(zc * zc, axis=-1, keepdims=True)
    return zc * lax.rsqrt(var + LN_EPS) * g + b


def _pack_bf16_pairs(x):
    half = x.shape[1] // 2
    lo = lax.bitcast_convert_type(x[:, :half].astype(BF16).astype(F32), jnp.uint32) >> 16
    hi = lax.bitcast_convert_type(x[:, half:].astype(BF16).astype(F32), jnp.uint32) & jnp.uint32(0xFFFF0000)
    return lax.bitcast_convert_type(lo | hi, jnp.int32)


def _unpack_bf16_pairs(w):
    u = lax.bitcast_convert_type(w, jnp.uint32)
    lo = lax.bitcast_convert_type(u << 16, F32).astype(BF16)
    hi = lax.bitcast_convert_type(u & jnp.uint32(0xFFFF0000), F32).astype(BF16)
    return lo, hi


def _dot_halves(lo, hi, w_ref):
    half = lo.shape[1]
    return (jnp.dot(lo, w_ref[:half, :], preferred_element_type=F32)
            + jnp.dot(hi, w_ref[half:, :], preferred_element_type=F32))


def _outproj_ln_kernel(*refs, n_a, alpha):
    a_refs = refs[:n_a]
    w_refs = refs[n_a:2 * n_a]
    x_ref, gate_ref, lng_ref, lnb_ref, sc_ref, sh_ref, xo_ref, tok_ref = refs[2 * n_a:]
    y = jnp.dot(a_refs[0][...], w_refs[0][...], preferred_element_type=F32)
    for a_ref, w_ref in zip(a_refs[1:], w_refs[1:]):
        y = y + jnp.dot(a_ref[...], w_ref[...], preferred_element_type=F32)
    xn = _layer_norm(alpha * x_ref[...] + gate_ref[...] * y, lng_ref[...], lnb_ref[...])
    xo_ref[...] = xn
    tok_ref[...] = _pack_bf16_pairs(xn * (1.0 + sc_ref[...]) + sh_ref[...])


def _outproj_ln(a_list, w_list, x, mod, ln_g, ln_b, *, t, tm, alpha, mod_row):
    d = x.shape[1]
    n_a = len(a_list)
    in_specs = [pl.BlockSpec((tm, a.shape[1]), lambda i: (i, 0)) for a in a_list]
    in_specs += [_const_spec(w.shape) for w in w_list]
    in_specs += [
        pl.BlockSpec((tm, d), lambda i: (i, 0)),
        _mod_spec(d, 2, mod_row, tm),
        _const_spec((1, d)), _const_spec((1, d)),
        _mod_spec(d, 4, mod_row, tm),
        _mod_spec(d, 3, mod_row, tm),
    ]
    return pl.pallas_call(
        functools.partial(_outproj_ln_kernel, n_a=n_a, alpha=alpha),
        grid=(t // tm,),
        in_specs=in_specs,
        out_specs=[pl.BlockSpec((tm, d), lambda i: (i, 0)), pl.BlockSpec((tm, d // 2), lambda i: (i, 0))],
        out_shape=[jax.ShapeDtypeStruct((t, d), F32), jax.ShapeDtypeStruct((t, d // 2), jnp.int32)],
        compiler_params=_params(("parallel",)),
        name="outproj_ln",
    )(*a_list, *w_list, x, mod, ln_g, ln_b, mod, mod)


def _router_kernel(t_ref, rw_ref, rb_ref, tri_ref, idx_ref, gw_ref, rank_ref, cnt_ref):
    @pl.when(pl.program_id(0) == 0)
    def _():
        cnt_ref[...] = jnp.zeros_like(cnt_ref)

    lo, hi = _unpack_bf16_pairs(t_ref[...])
    half = lo.shape[1]
    nt = (((1,), (1,)), ((), ()))
    logits = (lax.dot_general(rw_ref[:, :half], lo, nt, preferred_element_type=F32)
              + lax.dot_general(rw_ref[:, half:], hi, nt, preferred_element_type=F32))
    scores = 1.0 / (1.0 + jnp.exp(-logits))
    sel = scores + rb_ref[...]
    gsz = N_EXPERTS // N_GROUPS
    neg = -jnp.inf
    sub = lax.broadcasted_iota(jnp.int32, (gsz, sel.shape[1]), 0)
    slabs = [sel[g * gsz:(g + 1) * gsz, :] for g in range(N_GROUPS)]
    gscore = []
    for s in slabs:
        m1 = jnp.max(s, axis=0, keepdims=True)
        a1 = jnp.min(jnp.where(s == m1, sub, gsz), axis=0, keepdims=True)
        m2 = jnp.max(jnp.where(sub == a1, neg, s), axis=0, keepdims=True)
        gscore.append(m1 + m2)
    masked = []
    for g in range(N_GROUPS):
        ahead = jnp.zeros(gscore[g].shape, jnp.int32)
        for h in range(N_GROUPS):
            if h == g:
                continue
            beats = gscore[h] >= gscore[g] if h < g else gscore[h] > gscore[g]
            ahead = ahead + beats.astype(jnp.int32)
        masked.append(jnp.where(ahead < TOPK_GROUPS, slabs[g], neg))
    cur = jnp.concatenate(masked, axis=0)
    eio = lax.broadcasted_iota(jnp.int32, cur.shape, 0)
    picks, weights = [], []
    for _ in range(TOP_K):
        m = jnp.max(cur, axis=0, keepdims=True)
        a = jnp.min(jnp.where(cur == m, eio, N_EXPERTS), axis=0, keepdims=True)
        hit = eio == a
        picks.append(a)
        weights.append(jnp.sum(jnp.where(hit, scores, 0.0), axis=0, keepdims=True))
        cur = jnp.where(hit, neg, cur)
    total = weights[0]
    for w in weights[1:]:
        total = total + w
    for k in range(TOP_K):
        idx_ref[k:k + 1, :] = picks[k]
        gw_ref[k:k + 1, :] = weights[k] / total * ROUTED_SCALE
    for k in range(TOP_K, SUBLANES):
        idx_ref[k:k + 1, :] = jnp.zeros_like(picks[0])
        gw_ref[k:k + 1, :] = jnp.zeros_like(weights[0])
        rank_ref[k:k + 1, :] = jnp.zeros_like(picks[0])
    base = cnt_ref[:, 0:1]
    for k in range(TOP_K):
        onehot = jnp.where(eio == picks[k], 1.0, 0.0)
        before = jnp.dot(onehot.astype(BF16), tri_ref[...], preferred_element_type=F32)
        rank_ref[k:k + 1, :] = jnp.sum(onehot * (before + base), axis=0, keepdims=True).astype(jnp.int32)
        base = base + jnp.sum(onehot, axis=1, keepdims=True)
    cnt_ref[...] = jnp.broadcast_to(base, cnt_ref.shape)


def _router(tok, rw_t, rb, *, t, tt):
    half = tok.shape[1]
    tri = (jnp.arange(tt)[:, None] < jnp.arange(tt)[None, :]).astype(BF16)
    blk = pl.BlockSpec((SUBLANES, tt), lambda i: (0, i))
    return pl.pallas_call(
        _router_kernel,
        grid=(t // tt,),
        in_specs=[
            pl.BlockSpec((tt, half), lambda i: (i, 0)),
            _const_spec((N_EXPERTS, 2 * half)),
            _const_spec((N_EXPERTS, 1)),
            _const_spec((tt, tt)),
        ],
        out_specs=[blk, blk, blk, _const_spec((N_EXPERTS, LANES))],
        out_shape=[jax.ShapeDtypeStruct((SUBLANES, t), jnp.int32), jax.ShapeDtypeStruct((SUBLANES, t), F32),
                   jax.ShapeDtypeStruct((SUBLANES, t), jnp.int32), jax.ShapeDtypeStruct((N_EXPERTS, LANES), F32)],
        compiler_params=_params(("arbitrary",)),
        name="router",
    )(tok, rw_t, rb, tri)


def _slots_kernel(idx_ref, rank_ref, start_ref, pos_ref):
    start = start_ref[...]
    eio = lax.broadcasted_iota(jnp.int32, (N_EXPERTS, idx_ref.shape[1]), 0)
    for k in range(TOP_K):
        seg = jnp.sum(jnp.where(eio == idx_ref[k:k + 1, :], start, 0.0), axis=0, keepdims=True)
        pos_ref[k:k + 1, :] = rank_ref[k:k + 1, :] + seg.astype(jnp.int32)
    for k in range(TOP_K, SUBLANES):
        pos_ref[k:k + 1, :] = jnp.zeros((1, idx_ref.shape[1]), jnp.int32)


def _assign_slots(idx, rank, seg_start, *, t, tt):
    blk = pl.BlockSpec((SUBLANES, tt), lambda i: (0, i))
    return pl.pallas_call(
        _slots_kernel,
        grid=(t // tt,),
        in_specs=[blk, blk, _const_spec((N_EXPERTS, 1))],
        out_specs=blk,
        out_shape=jax.ShapeDtypeStruct((SUBLANES, t), jnp.int32),
        compiler_params=_params(("parallel",)),
        name="assign_slots",
    )(idx, rank, seg_start)


def _sc_gather_rows(table, idx):
    n = idx.shape[0]
    d = table.shape[1]
    n_workers = V7X_SC_CORES * V7X_SC_SUBCORES
    per_w = n // n_workers
    n_chunks = per_w // SC_GATHER_ROWS
    assert per_w * n_workers == n and n_chunks * SC_GATHER_ROWS == per_w
    n_buf = next(b for b in range(SC_GATHER_BUFFERS, 1, -1) if n_chunks % b == 0)
    mesh = plsc.VectorSubcoreMesh(core_axis_name="c", subcore_axis_name="s", num_cores=V7X_SC_CORES,
                                  num_subcores=V7X_SC_SUBCORES)

    @functools.partial(
        pl.kernel,
        out_type=jax.ShapeDtypeStruct((n, d), table.dtype),
        mesh=mesh,
        scratch_types=[
            pltpu.VMEM((per_w,), jnp.int32),
            pltpu.VMEM((n_buf, SC_GATHER_ROWS, d), table.dtype),
            pltpu.SemaphoreType.DMA((n_buf,)),
            pltpu.SemaphoreType.DMA((n_buf,)),
        ],
        name="sc_gather_rows",
    )
    def gather(table_hbm, idx_hbm, out_hbm, idx_v, rows_v, gsem, wsem):
        wid = lax.axis_index("s") * V7X_SC_CORES + lax.axis_index("c")
        base = wid * per_w
        pltpu.sync_copy(idx_hbm.at[pl.ds(base, per_w)], idx_v)

        def gather_copy(c, b):
            return pltpu.make_async_copy(table_hbm.at[idx_v.at[pl.ds(c * SC_GATHER_ROWS, SC_GATHER_ROWS)]],
                                         rows_v.at[b], gsem.at[b])

        def write_copy(c, b):
            return pltpu.make_async_copy(rows_v.at[b], out_hbm.at[pl.ds(base + c * SC_GATHER_ROWS, SC_GATHER_ROWS)],
                                         wsem.at[b])

        for b in range(n_buf - 1):
            gather_copy(b, b).start()

        @pl.loop(0, n_chunks, step=n_buf)
        def _(g):
            for b in range(n_buf):
                c = g + b
                prev = (b + n_buf - 1) % n_buf
                gather_copy(c, b).wait()
                write_copy(c, b).start()

                @pl.when(c >= 1)
                def _():
                    write_copy(c - 1, prev).wait()

                @pl.when(c + n_buf - 1 < n_chunks)
                def _():
                    gather_copy(c + n_buf - 1, prev).start()

        write_copy(n_chunks - 1, (n_chunks - 1) % n_buf).wait()

    return gather(table, idx)


def _experts_kernel(be_ref, nbu_ref, x_ref, wg_ref, wu_ref, wd_ref, *rest, block_off):
    y_ref, wgb, wub, wdb = rest[-4:]
    step = pl.program_id(0)
    b = block_off + step
    nbu = nbu_ref[0]

    @pl.when(b < nbu)
    def _():
        changed = jnp.logical_or(step == 0, be_ref[b] != be_ref[jnp.maximum(b - 1, 0)])

        @pl.when(changed)
        def _():
            wgb[...] = wg_ref[...].astype(BF16)
            wub[...] = wu_ref[...].astype(BF16)
            wdb[...] = wd_ref[...].astype(BF16)

        lo, hi = _unpack_bf16_pairs(x_ref[...])
        hg = _dot_halves(lo, hi, wgb)
        hu = _dot_halves(lo, hi, wub)
        h = hg * (1.0 / (1.0 + jnp.exp(-hg))) * hu
        y_ref[...] = _pack_bf16_pairs(jnp.dot(h.astype(BF16), wdb[...], preferred_element_type=F32))

    @pl.when(b >= nbu)
    def _():
        y_ref[...] = jnp.zeros_like(y_ref)


def _experts(xs, block_e, nb_used, wg, wu, wd, layer, y_prev, *, block_off, n_blocks):
    half = xs.shape[1]
    d = 2 * half
    ff = wg.shape[3]
    n_call = xs.shape[0] // MOE_ROWS

    def used(b, nbu):
        return jnp.clip(jnp.minimum(block_off + b, nbu[0] - 1) - block_off, 0, n_call - 1)

    def expert(b, be):
        return be[block_off + b]

    in_specs = [
        pl.BlockSpec((MOE_ROWS, half), lambda b, be, nbu: (used(b, nbu), 0)),
        pl.BlockSpec((None, None, d, ff), lambda b, be, nbu: (layer, expert(b, be), 0, 0)),
        pl.BlockSpec((None, None, d, ff), lambda b, be, nbu: (layer, expert(b, be), 0, 0)),
        pl.BlockSpec((None, None, ff, d), lambda b, be, nbu: (layer, expert(b, be), 0, 0)),
    ]
    args = [block_e, nb_used, xs, wg, wu, wd]
    aliases = {}
    if y_prev is not None:
        in_specs.append(pl.BlockSpec(memory_space=pl.ANY))
        aliases = {len(args): 0}
        args.append(y_prev)
    grid_spec = pltpu.PrefetchScalarGridSpec(
        num_scalar_prefetch=2,
        grid=(n_call,),
        in_specs=in_specs,
        out_specs=pl.BlockSpec((MOE_ROWS, half), lambda b, be, nbu: (block_off + b, 0)),
        scratch_shapes=[
            pltpu.VMEM((d, ff), BF16),
            pltpu.VMEM((d, ff), BF16),
            pltpu.VMEM((ff, d), BF16),
        ],
    )
    return pl.pallas_call(
        functools.partial(_experts_kernel, block_off=block_off),
        grid_spec=grid_spec,
        out_shape=jax.ShapeDtypeStruct((n_blocks * MOE_ROWS, half), jnp.int32),
        input_output_aliases=aliases,
        compiler_params=_params(("arbitrary",)),
        name="experts",
    )(*args)


def _combine_ln_kernel(*refs, alpha, emit_next, n_prev):
    y_ref, gw_ref, tok_ref, x_ref, sg_ref, su_ref, sd_ref, gate_ref, lng_ref, lnb_ref = refs[:10]
    outs = refs[len(refs) - (2 if emit_next else 1):]
    if emit_next:
        sc_ref, sh_ref = refs[10:12]
        xo_ref, u_ref = outs
    else:
        (xo_ref,) = outs
    lo, hi = _unpack_bf16_pairs(tok_ref[...])
    hg = _dot_halves(lo, hi, sg_ref)
    hu = _dot_halves(lo, hi, su_ref)
    h = hg * (1.0 / (1.0 + jnp.exp(-hg))) * hu
    gw = gw_ref[...]
    f_lo = f_hi = None
    for k in range(TOP_K):
        y_lo, y_hi = _unpack_bf16_pairs(y_ref[k])
        w = gw[:, k:k + 1]
        f_lo = y_lo.astype(F32) * w if f_lo is None else f_lo + y_lo.astype(F32) * w
        f_hi = y_hi.astype(F32) * w if f_hi is None else f_hi + y_hi.astype(F32) * w
    f = jnp.concatenate([f_lo, f_hi], axis=-1) + jnp.dot(h.astype(BF16), sd_ref[...], preferred_element_type=F32)
    xn = _layer_norm(alpha * x_ref[...] + gate_ref[...] * f, lng_ref[...], lnb_ref[...])
    xo_ref[...] = xn
    if emit_next:
        u_ref[...] = (xn * (1.0 + sc_ref[...]) + sh_ref[...]).astype(u_ref.dtype)


def _combine_ln(y3, gw_t, tok, x, sg, su, sd, mod, ln_g, ln_b, mod_next, prev, *, t, row_off, tm, alpha, mod_row):
    d = x.shape[1]
    emit_next = mod_next is not None
    off = row_off // tm
    assert off * tm == row_off

    def rows(i):
        return (off + i, 0)

    def part_mod_row(r):
        return mod_row(r + row_off)

    in_specs = [
        pl.BlockSpec((TOP_K, tm, d // 2), lambda i: (0, i, 0)),
        pl.BlockSpec((tm, SUBLANES), rows),
        pl.BlockSpec((tm, d // 2), rows),
        pl.BlockSpec((tm, d), rows),
        _const_spec(sg.shape), _const_spec(su.shape), _const_spec(sd.shape),
        _mod_spec(d, 5, part_mod_row, tm),
        _const_spec((1, d)), _const_spec((1, d)),
    ]
    args = [y3, gw_t, tok, x, sg, su, sd, mod, ln_g, ln_b]
    out_specs = [pl.BlockSpec((tm, d), rows)]
    out_shape = [jax.ShapeDtypeStruct((t, d), F32)]
    if emit_next:
        in_specs += [_mod_spec(d, 1, part_mod_row, tm), _mod_spec(d, 0, part_mod_row, tm)]
        args += [mod_next, mod_next]
        out_specs.append(pl.BlockSpec((tm, d), rows))
        out_shape.append(jax.ShapeDtypeStruct((t, d), BF16))
    aliases = {}
    if prev is not None:
        for k, p in enumerate(prev):
            in_specs.append(pl.BlockSpec(memory_space=pl.ANY))
            aliases[len(args)] = k
            args.append(p)
    return pl.pallas_call(
        functools.partial(_combine_ln_kernel, alpha=alpha, emit_next=emit_next, n_prev=len(aliases)),
        grid=(y3.shape[1] // tm,),
        in_specs=in_specs,
        out_specs=out_specs,
        out_shape=out_shape,
        input_output_aliases=aliases,
        compiler_params=_params(("parallel",)),
        name="combine_ln",
    )(*args)


def _moe(tok, x, t, layer, router_w, router_b, wg, wu, wd, sg, su, sd, mod, ln_g, ln_b, mod_next, *, alpha, mod_row,
         tm):
    half = tok.shape[1]
    tt = 512
    idx, gw, rank, cnt = _router(tok, router_w.T.astype(BF16), router_b.reshape(N_EXPERTS, 1), t=t, tt=tt)
    n_asg = t * TOP_K
    counts = cnt[:, 0].astype(jnp.int32)
    padded = (counts + MOE_ROWS - 1) // MOE_ROWS * MOE_ROWS
    pend = jnp.cumsum(padded)
    pstart = pend - padded
    ustart = jnp.cumsum(counts) - counts
    sc_rows = V7X_SC_CORES * V7X_SC_SUBCORES * SC_GATHER_ROWS * 2
    blocks_granule = MOE_DISPATCH_PARTS * max(sc_rows // MOE_ROWS, 1)
    assert (blocks_granule // MOE_DISPATCH_PARTS * MOE_ROWS) % sc_rows == 0
    n_blocks = -(-((n_asg + N_EXPERTS * (MOE_ROWS - 1)) // MOE_ROWS + 1) // blocks_granule) * blocks_granule
    assert n_asg % sc_rows == 0
    block_start = jnp.arange(n_blocks, dtype=jnp.int32) * MOE_ROWS
    block_e = jnp.minimum(jnp.sum((pend[None, :] <= block_start[:, None]).astype(jnp.int32), axis=1), N_EXPERTS - 1)
    nb_used = (pend[-1] // MOE_ROWS).astype(jnp.int32).reshape(1)
    pos2d = _assign_slots(idx, rank, pstart.astype(F32).reshape(N_EXPERTS, 1), t=t, tt=tt)[:TOP_K]
    pos = pos2d.reshape(-1)
    tok_of_asg = jnp.tile(jnp.arange(t, dtype=jnp.int32), TOP_K)
    _, tok_sorted = lax.sort((pos, tok_of_asg), num_keys=1)
    block_valid = jnp.clip(counts[block_e] - (block_start - pstart[block_e]), 0, MOE_ROWS)
    block_shift = (pstart - ustart)[block_e]
    row = jnp.arange(MOE_ROWS, dtype=jnp.int32)[None, :]
    compact = jnp.clip(block_start[:, None] + row - block_shift[:, None], 0, n_asg - 1)
    spread = (block_start[:, None] + row) % t
    slot_tok = jnp.where(row < block_valid[:, None], tok_sorted[compact], spread).reshape(-1)
    per = n_blocks // MOE_DISPATCH_PARTS * MOE_ROWS
    xs = [_sc_gather_rows(tok, slot_tok[i * per:(i + 1) * per]) for i in range(MOE_DISPATCH_PARTS)]
    y = None
    for i in range(MOE_DISPATCH_PARTS):
        y = _experts(xs[i], block_e, nb_used, wg, wu, wd, layer, y, block_off=i * per // MOE_ROWS, n_blocks=n_blocks)
    n_cparts = MOE_COMBINE_PARTS if (t // MOE_COMBINE_PARTS * TOP_K) % sc_rows == 0 else 1
    t_part = t // n_cparts
    y3 = [_sc_gather_rows(y, pos2d[:, i * t_part:(i + 1) * t_part].reshape(-1)).reshape(TOP_K, t_part, half)
          for i in range(n_cparts)]
    outs = None
    gw_t = gw.T
    for i in range(n_cparts):
        outs = _combine_ln(y3[i], gw_t, tok, x, sg, su, sd, mod, ln_g, ln_b, mod_next, outs, t=t,
                           row_off=i * t_part, tm=tm, alpha=alpha, mod_row=mod_row)
    return outs


def _rope64(r, c_ref, sa_ref, sb_ref):
    return r * c_ref[...] + pltpu.roll(r, LANES - QK_ROPE // 2, 1) * sa_ref[...] + pltpu.roll(r, QK_ROPE // 2, 1) * sb_ref[...]


def _mla_q_kernel(d_ref, gain_ref, w_ref, c_ref, sa_ref, sb_ref, q_ref, *, scale):
    n = _rms(d_ref[...], gain_ref[...]).astype(BF16)
    q = jnp.dot(n, w_ref[...], preferred_element_type=F32)
    for h in range(MLA_HEADS):
        lo = h * MLA_DK_PAD
        q_ref[:, lo:lo + QK_NOPE] = (q[:, lo:lo + QK_NOPE] * scale).astype(q_ref.dtype)
        r = _rope64(q[:, lo + QK_NOPE:lo + MLA_DK_PAD], c_ref, sa_ref, sb_ref)
        q_ref[:, lo + QK_NOPE:lo + MLA_DK_PAD] = (r * scale).astype(q_ref.dtype)


def _mla_kv_kernel(ckv_ref, kr_ref, gain_ref, wk_ref, wv_ref, c_ref, sa_ref, sb_ref, k_ref, v_ref):
    n = _rms(ckv_ref[...], gain_ref[...]).astype(BF16)
    kn = jnp.dot(n, wk_ref[...], preferred_element_type=F32)
    v_ref[...] = jnp.dot(n, wv_ref[...], preferred_element_type=F32).astype(v_ref.dtype)
    kr = _rope64(kr_ref[...], c_ref, sa_ref, sb_ref).astype(k_ref.dtype)
    for h in range(MLA_HEADS):
        lo = h * MLA_DK_PAD
        k_ref[:, lo:lo + QK_NOPE] = kn[:, h * QK_NOPE:(h + 1) * QK_NOPE].astype(k_ref.dtype)
        k_ref[:, lo + QK_NOPE:lo + MLA_DK_PAD] = kr


def _axial_angles(n_tok, rot_dim):
    rows = n_tok // GRID_W
    n_freq = rot_dim // 4
    inv = ROPE_THETA ** (-jnp.arange(n_freq, dtype=F32) / n_freq)
    row = jnp.repeat(jnp.arange(rows, dtype=F32), GRID_W)
    col = jnp.tile(jnp.arange(GRID_W, dtype=F32), rows)
    return jnp.concatenate([row[:, None] * inv, col[:, None] * inv], axis=-1)


def _rope_tables_128(n_tok, ident_rows):
    ang = _axial_angles(n_tok, HEAD_DIM)
    cos, sin = jnp.cos(ang), jnp.sin(ang)
    c = jnp.concatenate([cos, cos], axis=-1)
    s = jnp.concatenate([-sin, sin], axis=-1)
    c = jnp.concatenate([c, jnp.ones((ident_rows, HEAD_DIM), F32)], axis=0)
    s = jnp.concatenate([s, jnp.zeros((ident_rows, HEAD_DIM), F32)], axis=0)
    return c, s


def _rope_tables_64(n_tok, ident_rows):
    ang = _axial_angles(n_tok, QK_ROPE)
    cos, sin = jnp.cos(ang), jnp.sin(ang)
    half = QK_ROPE // 2
    z = jnp.zeros((n_tok, LANES - QK_ROPE), F32)
    zh = jnp.zeros((n_tok, half), F32)
    c = jnp.concatenate([cos, cos, z], axis=-1)
    sa = jnp.concatenate([-sin, zh, z], axis=-1)
    sb = jnp.concatenate([zh, sin, z], axis=-1)
    ci = jnp.concatenate([jnp.ones((ident_rows, QK_ROPE), F32), jnp.zeros((ident_rows, LANES - QK_ROPE), F32)], axis=-1)
    zi = jnp.zeros((ident_rows, LANES), F32)
    return jnp.concatenate([c, ci], 0), jnp.concatenate([sa, zi], 0), jnp.concatenate([sb, zi], 0)


def kernel(x, c, ctx, c_ctx, w_ada, b_ada, ln_g, ln_b, a_w_in, a_conv_w, a_q_gain, a_k_gain, a_w_out, m_w_down, m_q_gain, m_kv_gain, m_w_uq, m_w_ukv, m_w_out, router_w, router_b, e_w_gate, e_w_up, e_w_down, s_w_gate, s_w_up, s_w_down):
    batch, seq, d = x.shape
    ctx_len = ctx.shape[1]
    depth = w_ada.shape[0]
    assert depth == 2, "one conv+GQA layer followed by one MLA layer"
    alpha = (2 * depth) ** 0.25
    t_lat = batch * seq
    t_ctx = batch * ctx_len
    t_all = t_lat + t_ctx
    tr = 256
    assert seq % tr == 0 and ctx_len % tr == 0 and seq % GRID_W == 0
    lat_tiles = t_lat // tr
    lat_seq_tiles = seq // tr
    ctx_seq_tiles = ctx_len // tr
    lk = ctx_len + seq

    def mod_row(r):
        return jnp.minimum(r // seq, batch)

    def kv_block(i):
        is_lat = i < lat_tiles
        cidx = i - lat_tiles
        b = jnp.where(is_lat, i // lat_seq_tiles, cidx // ctx_seq_tiles)
        rb = jnp.where(is_lat, ctx_seq_tiles + i % lat_seq_tiles, cidx % ctx_seq_tiles)
        return b, rb

    def pos_block(i):
        return jnp.where(i < lat_tiles, i % lat_seq_tiles, lat_seq_tiles)

    rows = -(-(batch + 1) // SUBLANES) * SUBLANES
    cond = jnp.concatenate([c, c_ctx[None, :], jnp.zeros((rows - batch - 1, d), F32)], axis=0)
    mod = _ada_table(cond, w_ada, b_ada).reshape(depth, rows, 1, 6 * d)

    x_all = jnp.concatenate([x.reshape(t_lat, d), ctx.reshape(t_ctx, d)], axis=0)

    tm_in = 1024 if (t_all % 1024 == 0 and seq % 1024 == 0) else tr
    proj = _mod_matmul(x_all, mod[0], a_w_in[0].astype(BF16), BF16, tm_in, 768, mod_row)

    cos128, sin128 = _rope_tables_128(seq, tr)
    d_q = ATT_HEADS * HEAD_DIM
    d_kv = ATT_KV_HEADS * HEAD_DIM
    qkv_w = d_q + 2 * d_kv
    qkv_blk = 3 * CONV_DIM // qkv_w
    assert qkv_blk * qkv_w == 3 * CONV_DIM
    q0, k0, v0 = pl.pallas_call(
        functools.partial(_qkprep_kernel, scale=1.0 / math.sqrt(HEAD_DIM)),
        grid=(t_all // tr,),
        in_specs=[
            pl.BlockSpec((tr, qkv_w), lambda i: (i, qkv_blk)),
            pl.BlockSpec((tr, HEAD_DIM), lambda i: (pos_block(i), 0)),
            pl.BlockSpec((tr, HEAD_DIM), lambda i: (pos_block(i), 0)),
            _const_spec((1, HEAD_DIM)), _const_spec((1, HEAD_DIM)),
        ],
        out_specs=[
            pl.BlockSpec((tr, d_q), lambda i: (i, 0)),
            pl.BlockSpec((None, tr, d_kv), lambda i: (*kv_block(i), 0)),
            pl.BlockSpec((None, tr, d_kv), lambda i: (*kv_block(i), 0)),
        ],
        out_shape=[
            jax.ShapeDtypeStruct((t_all, d_q), BF16),
            jax.ShapeDtypeStruct((batch, lk, d_kv), BF16),
            jax.ShapeDtypeStruct((batch, lk, d_kv), BF16),
        ],
        compiler_params=_params(("parallel",)),
        name="qk_prep",
    )(proj, cos128, sin128, a_q_gain[0].reshape(1, HEAD_DIM), a_k_gain[0].reshape(1, HEAD_DIM))

    grp = ATT_HEADS // ATT_KV_HEADS
    att_lat = _attention(q0, k0, v0, batch=batch, sq=seq, lk=lk, n_kv=ATT_KV_HEADS, group=grp, dk=HEAD_DIM,
                         dv=HEAD_DIM, tq=512 if seq % 512 == 0 else tr, rows=256, q_row_off=0)
    att_ctx = _attention(q0, k0, v0, batch=batch, sq=ctx_len, lk=ctx_len, n_kv=ATT_KV_HEADS, group=grp,
                         dk=HEAD_DIM, dv=HEAD_DIM, tq=256, rows=256, q_row_off=t_lat)
    att0 = jnp.concatenate([att_lat, att_ctx], axis=0)

    conv0 = _conv_gate(proj, a_conv_w[0], t=t_all, tm=tr, tc=512, lat_tiles=lat_tiles,
                       lat_seq_tiles=lat_seq_tiles, ctx_seq_tiles=ctx_seq_tiles)

    w_out0 = a_w_out[0].astype(BF16)
    x1, tok0 = _outproj_ln([conv0, att0], [w_out0[:CONV_DIM], w_out0[CONV_DIM:]], x_all, mod[0],
                           ln_g[0, 0].reshape(1, d), ln_b[0, 0].reshape(1, d), t=t_all, tm=tr, alpha=alpha,
                           mod_row=mod_row)

    x2, u1 = _moe(tok0, x1, t_all, 0, router_w[0], router_b[0], e_w_gate, e_w_up, e_w_down,
                  s_w_gate[0].astype(BF16), s_w_up[0].astype(BF16), s_w_down[0].astype(BF16), mod[0],
                  ln_g[0, 1].reshape(1, d), ln_b[0, 1].reshape(1, d), mod[1], alpha=alpha,
                  mod_row=mod_row, tm=256)

    n_down = Q_LORA + KV_LORA + QK_ROPE
    n_down_pad = -(-n_down // LANES) * LANES
    w_down = jnp.pad(m_w_down[0], ((0, 0), (0, n_down_pad - n_down))).astype(BF16)
    down = _matmul(u1, w_down, F32, 512, n_down_pad)

    dqk = QK_NOPE + QK_ROPE
    w_uq = m_w_uq[0].reshape(Q_LORA, MLA_HEADS, dqk)
    w_uq = jnp.pad(w_uq, ((0, 0), (0, 0), (0, MLA_DK_PAD - dqk))).reshape(Q_LORA, MLA_HEADS * MLA_DK_PAD).astype(BF16)
    w_ukv = m_w_ukv[0].reshape(KV_LORA, MLA_HEADS, QK_NOPE + V_DIM)
    w_uk = w_ukv[:, :, :QK_NOPE].reshape(KV_LORA, MLA_HEADS * QK_NOPE).astype(BF16)
    w_uv = w_ukv[:, :, QK_NOPE:].reshape(KV_LORA, MLA_HEADS * V_DIM).astype(BF16)

    c64, sa64, sb64 = _rope_tables_64(seq, tr)
    rope_specs = [pl.BlockSpec((tr, LANES), lambda i: (pos_block(i), 0))] * 3
    q1 = pl.pallas_call(
        functools.partial(_mla_q_kernel, scale=1.0 / math.sqrt(dqk)),
        grid=(lat_tiles,),
        in_specs=[
            pl.BlockSpec((tr, Q_LORA), lambda i: (i, 0)),
            _const_spec((1, Q_LORA)),
            _const_spec(w_uq.shape),
        ] + rope_specs,
        out_specs=pl.BlockSpec((tr, MLA_HEADS * MLA_DK_PAD), lambda i: (i, 0)),
        out_shape=jax.ShapeDtypeStruct((t_lat, MLA_HEADS * MLA_DK_PAD), BF16),
        compiler_params=_params(("parallel",)),
        name="mla_q",
    )(down, m_q_gain[0].reshape(1, Q_LORA), w_uq, c64, sa64, sb64)

    assert KV_LORA == Q_LORA and (Q_LORA + KV_LORA) % LANES == 0
    k1, v1 = pl.pallas_call(
        _mla_kv_kernel,
        grid=(t_all // tr,),
        in_specs=[
            pl.BlockSpec((tr, KV_LORA), lambda i: (i, 1)),
            pl.BlockSpec((tr, LANES), lambda i: (i, (Q_LORA + KV_LORA) // LANES)),
            _const_spec((1, KV_LORA)),
            _const_spec(w_uk.shape), _const_spec(w_uv.shape),
        ] + rope_specs,
        out_specs=[
            pl.BlockSpec((None, tr, MLA_HEADS * MLA_DK_PAD), lambda i: (*kv_block(i), 0)),
            pl.BlockSpec((None, tr, MLA_HEADS * V_DIM), lambda i: (*kv_block(i), 0)),
        ],
        out_shape=[
            jax.ShapeDtypeStruct((batch, lk, MLA_HEADS * MLA_DK_PAD), BF16),
            jax.ShapeDtypeStruct((batch, lk, MLA_HEADS * V_DIM), BF16),
        ],
        compiler_params=_params(("parallel",)),
        name="mla_kv",
    )(down, down, m_kv_gain[0].reshape(1, KV_LORA), w_uk, w_uv, c64, sa64, sb64)

    att1 = _attention(q1, k1, v1, batch=batch, sq=seq, lk=lk, n_kv=MLA_HEADS, group=1, dk=MLA_DK_PAD, dv=V_DIM,
                      tq=next(c for c in (2048, 1024, 512, tr) if seq % c == 0), rows=256, q_row_off=0)

    x3, tok1 = _outproj_ln([att1], [m_w_out[0].astype(BF16)], x2, mod[1], ln_g[1, 0].reshape(1, d),
                           ln_b[1, 0].reshape(1, d), t=t_lat, tm=tr, alpha=alpha, mod_row=mod_row)

    (x4,) = _moe(tok1, x3, t_lat, 1, router_w[1], router_b[1], e_w_gate, e_w_up, e_w_down,
                 s_w_gate[1].astype(BF16), s_w_up[1].astype(BF16), s_w_down[1].astype(BF16), mod[1],
                 ln_g[1, 1].reshape(1, d), ln_b[1, 1].reshape(1, d), None, alpha=alpha,
                 mod_row=mod_row, tm=256)
    return x4.reshape(batch, seq, d)
```

```python
import functools
import math

import jax
import jax.numpy as jnp
from jax import lax
from jax.experimental import pallas as pl
from jax.experimental.pallas import tpu as pltpu
from jax.experimental.pallas import tpu_sc as plsc

F32 = jnp.float32
BF16 = jnp.bfloat16

GRID_W = 64
CONV_DIM = 1024
ATT_HEADS = 8
ATT_KV_HEADS = 2
HEAD_DIM = 128
MLA_HEADS = 16
Q_LORA = 512
KV_LORA = 512
QK_NOPE = 128
QK_ROPE = 64
V_DIM = 128
N_EXPERTS = 64
TOP_K = 6
N_GROUPS = 8
TOPK_GROUPS = 4
ROUTED_SCALE = 2.5
ROPE_THETA = 10000.0
LN_EPS = 1e-5
RMS_EPS = 1e-6

V7X_VMEM_LIMIT_BYTES = 56 * 1024 * 1024
LANES = 128
SUBLANES = 8
MOE_ROWS = 512
MOE_DISPATCH_PARTS = 4
MOE_COMBINE_PARTS = 2
V7X_SC_CORES = 2
V7X_SC_SUBCORES = 16
SC_GATHER_ROWS = 16
SC_GATHER_BUFFERS = 4
MLA_DK_PAD = 256


def _params(sem):
    return pltpu.CompilerParams(dimension_semantics=sem, vmem_limit_bytes=V7X_VMEM_LIMIT_BYTES)


def _const_spec(shape):
    nd = len(shape)
    return pl.BlockSpec(shape, lambda *_: (0,) * nd)


def _ada_kernel(s_ref, w_ref, b_ref, o_ref):
    s = s_ref[...]
    s = s * (1.0 / (1.0 + jnp.exp(-s)))
    o_ref[...] = jnp.dot(s.astype(BF16), w_ref[...].astype(BF16), preferred_element_type=F32) + b_ref[...]


def _ada_table(cond, w_ada, b_ada):
    depth, d, n = w_ada.shape
    r = cond.shape[0]
    tn = 1024
    return pl.pallas_call(
        _ada_kernel,
        grid=(depth, n // tn),
        in_specs=[
            pl.BlockSpec((r, d), lambda l, j: (0, 0)),
            pl.BlockSpec((None, d, tn), lambda l, j: (l, 0, j)),
            pl.BlockSpec((None, 1, tn), lambda l, j: (l, 0, j)),
        ],
        out_specs=pl.BlockSpec((None, r, tn), lambda l, j: (l, 0, j)),
        out_shape=jax.ShapeDtypeStruct((depth, r, n), F32),
        compiler_params=_params(("parallel", "parallel")),
        name="ada_table",
    )(cond, w_ada, b_ada.reshape(depth, 1, n))


def _mod_spec(d, chunk, mod_row, tm):
    return pl.BlockSpec((None, 1, d), lambda i: (mod_row(i * tm), 0, chunk))


def _mm_kernel(a_ref, w_ref, o_ref):
    o_ref[...] = jnp.dot(a_ref[...], w_ref[...], preferred_element_type=F32).astype(o_ref.dtype)


def _matmul(a, w, out_dtype, tm, tn):
    m, k = a.shape
    n = w.shape[1]
    return pl.pallas_call(
        _mm_kernel,
        grid=(m // tm, n // tn),
        in_specs=[
            pl.BlockSpec((tm, k), lambda i, j: (i, 0)),
            pl.BlockSpec((k, tn), lambda i, j: (0, j)),
        ],
        out_specs=pl.BlockSpec((tm, tn), lambda i, j: (i, j)),
        out_shape=jax.ShapeDtypeStruct((m, n), out_dtype),
        compiler_params=_params(("parallel", "parallel")),
        name="matmul",
    )(a, w)


def _mod_mm_kernel(x_ref, sc_ref, sh_ref, w_ref, o_ref, u_ref):
    @pl.when(pl.program_id(1) == 0)
    def _():
        u_ref[...] = (x_ref[...] * (1.0 + sc_ref[...]) + sh_ref[...]).astype(u_ref.dtype)

    o_ref[...] = jnp.dot(u_ref[...], w_ref[...], preferred_element_type=F32).astype(o_ref.dtype)


def _mod_matmul(x, mod, w, out_dtype, tm, tn, mod_row):
    m, k = x.shape
    n = w.shape[1]

    def mod_spec(chunk):
        return pl.BlockSpec((None, 1, k), lambda i, j: (mod_row(i * tm), 0, chunk))

    return pl.pallas_call(
        _mod_mm_kernel,
        grid=(m // tm, n // tn),
        in_specs=[
            pl.BlockSpec((tm, k), lambda i, j: (i, 0)),
            mod_spec(1), mod_spec(0),
            pl.BlockSpec((k, tn), lambda i, j: (0, j)),
        ],
        out_specs=pl.BlockSpec((tm, tn), lambda i, j: (i, j)),
        out_shape=jax.ShapeDtypeStruct((m, n), out_dtype),
        scratch_shapes=[pltpu.VMEM((tm, k), w.dtype)],
        compiler_params=_params(("parallel", "arbitrary")),
        name="mod_matmul",
    )(x, mod, mod, w)


def _rms(t, gain):
    return t * lax.rsqrt(jnp.mean(t * t, axis=-1, keepdims=True) + RMS_EPS) * gain


def _qkprep_kernel(p_ref, cos_ref, sin_ref, qg_ref, kg_ref, q_ref, k_ref, v_ref, *, scale):
    cos = cos_ref[...]
    sin = sin_ref[...]

    def norm_rope(t, gain):
        y = _rms(t.astype(F32), gain)
        return y * cos + pltpu.roll(y, HEAD_DIM // 2, 1) * sin

    for h in range(ATT_HEADS):
        sl = slice(h * HEAD_DIM, (h + 1) * HEAD_DIM)
        q_ref[:, sl] = (norm_rope(p_ref[:, sl], qg_ref[...]) * scale).astype(q_ref.dtype)
    k0 = ATT_HEADS * HEAD_DIM
    for h in range(ATT_KV_HEADS):
        sl = slice(h * HEAD_DIM, (h + 1) * HEAD_DIM)
        k_ref[:, sl] = norm_rope(p_ref[:, k0 + h * HEAD_DIM:k0 + (h + 1) * HEAD_DIM], kg_ref[...]).astype(k_ref.dtype)
    v0 = k0 + ATT_KV_HEADS * HEAD_DIM
    v_ref[...] = p_ref[:, v0:v0 + ATT_KV_HEADS * HEAD_DIM].astype(v_ref.dtype)


def _attn_kernel(q_ref, k_ref, v_ref, o_ref, *, group, tq, rows, dk, dv):
    k = k_ref[...]
    v = v_ref[...]
    for h in range(group):
        for r in range(0, tq, rows):
            q = q_ref[r:r + rows, h * dk:(h + 1) * dk]
            s = lax.dot_general(q, k, (((1,), (1,)), ((), ())), preferred_element_type=F32)
            m = jnp.max(s, axis=-1, keepdims=True)
            p = jnp.exp(s - m)
            l = jnp.sum(p, axis=-1, keepdims=True)
            o = jnp.dot(p.astype(v.dtype), v, preferred_element_type=F32)
            o_ref[r:r + rows, h * dv:(h + 1) * dv] = (o / l).astype(o_ref.dtype)


def _attention(q, k, v, *, batch, sq, lk, n_kv, group, dk, dv, tq, rows, q_row_off):
    nq = sq // tq
    off = q_row_off // tq
    assert tq % rows == 0 and q_row_off % tq == 0 and sq % tq == 0
    return pl.pallas_call(
        functools.partial(_attn_kernel, group=group, tq=tq, rows=rows, dk=dk, dv=dv),
        grid=(batch, n_kv, nq),
        in_specs=[
            pl.BlockSpec((tq, group * dk), lambda b, g, i: (off + b * nq + i, g)),
            pl.BlockSpec((None, lk, dk), lambda b, g, i: (b, 0, g)),
            pl.BlockSpec((None, lk, dv), lambda b, g, i: (b, 0, g)),
        ],
        out_specs=pl.BlockSpec((tq, group * dv), lambda b, g, i: (b * nq + i, g)),
        out_shape=jax.ShapeDtypeStruct((batch * sq, n_kv * group * dv), BF16),
        compiler_params=_params(("parallel", "parallel", "parallel")),
        name="attention",
    )(q, k, v)


def _conv_kernel(gb_ref, gc_ref, hv_ref, gcp_ref, hvp_ref, gcn_ref, hvn_ref, w_ref, o_ref, *,
                 tm, lat_tiles, lat_seq_tiles, ctx_seq_tiles):
    i = pl.program_id(0)
    is_lat = i < lat_tiles
    pos = jnp.where(is_lat, i % lat_seq_tiles, (i - lat_tiles) % ctx_seq_tiles)
    seq_tiles = jnp.where(is_lat, lat_seq_tiles, ctx_seq_tiles)
    not_first = (pos != 0).astype(F32)
    not_last = (pos != seq_tiles - 1).astype(F32)
    p = gc_ref[...].astype(F32) * hv_ref[...].astype(F32)
    halo_prev = gcp_ref[SUBLANES - 1:SUBLANES, :].astype(F32) * hvp_ref[SUBLANES - 1:SUBLANES, :].astype(F32) * not_first
    halo_next = gcn_ref[0:1, :].astype(F32) * hvn_ref[0:1, :].astype(F32) * not_last
    row = lax.broadcasted_iota(jnp.int32, p.shape, 0)
    prev = jnp.where(row == 0, halo_prev, pltpu.roll(p, 1, 0))
    nxt = jnp.where(row == tm - 1, halo_next, pltpu.roll(p, tm - 1, 0))
    w = w_ref[...]
    conv = w[0:1, :] * prev + w[1:2, :] * p + w[2:3, :] * nxt
    o_ref[...] = (gb_ref[...].astype(F32) * conv).astype(o_ref.dtype)


def _conv_gate(p, conv_w, *, t, tm, tc, lat_tiles, lat_seq_tiles, ctx_seq_tiles):
    nct = CONV_DIM // tc
    hb = tm // SUBLANES
    n_halo = t // SUBLANES

    def cur(part):
        return pl.BlockSpec((tm, tc), lambda i, j: (i, part * nct + j))

    def prev(part):
        return pl.BlockSpec((SUBLANES, tc), lambda i, j: (jnp.maximum(i * hb - 1, 0), part * nct + j))

    def nxt(part):
        return pl.BlockSpec((SUBLANES, tc), lambda i, j: (jnp.minimum((i + 1) * hb, n_halo - 1), part * nct + j))

    return pl.pallas_call(
        functools.partial(_conv_kernel, tm=tm, lat_tiles=lat_tiles, lat_seq_tiles=lat_seq_tiles,
                          ctx_seq_tiles=ctx_seq_tiles),
        grid=(t // tm, nct),
        in_specs=[cur(0), cur(1), cur(2), prev(1), prev(2), nxt(1), nxt(2),
                  pl.BlockSpec((3, tc), lambda i, j: (0, j))],
        out_specs=pl.BlockSpec((tm, tc), lambda i, j: (i, j)),
        out_shape=jax.ShapeDtypeStruct((t, CONV_DIM), BF16),
        compiler_params=_params(("parallel", "parallel")),
        name="conv_gate",
    )(p, p, p, p, p, p, p, conv_w)


def _layer_norm(z, g, b):
    mu = jnp.mean(z, axis=-1, keepdims=True)
    zc = z - mu
    var = jnp.mean(zc * zc, axis=-1, keepdims=True)
    return zc * lax.rsqrt(var + LN_EPS) * g + b


def _pack_bf16_pairs(x):
    half = x.shape[1] // 2
    lo = lax.bitcast_convert_type(x[:, :half].astype(BF16).astype(F32), jnp.uint32) >> 16
    hi = lax.bitcast_convert_type(x[:, half:].astype(BF16).astype(F32), jnp.uint32) & jnp.uint32(0xFFFF0000)
    return lax.bitcast_convert_type(lo | hi, jnp.int32)


def _unpack_bf16_pairs(w):
    u = lax.bitcast_convert_type(w, jnp.uint32)
    lo = lax.bitcast_convert_type(u << 16, F32).astype(BF16)
    hi = lax.bitcast_convert_type(u & jnp.uint32(0xFFFF0000), F32).astype(BF16)
    return lo, hi


def _dot_halves(lo, hi, w_ref):
    half = lo.shape[1]
    return (jnp.dot(lo, w_ref[:half, :], preferred_element_type=F32)
            + jnp.dot(hi, w_ref[half:, :], preferred_element_type=F32))


def _outproj_ln_kernel(*refs, n_a, alpha):
    a_refs = refs[:n_a]
    w_refs = refs[n_a:2 * n_a]
    x_ref, gate_ref, lng_ref, lnb_ref, sc_ref, sh_ref, xo_ref, tok_ref = refs[2 * n_a:]
    y = jnp.dot(a_refs[0][...], w_refs[0][...], preferred_element_type=F32)
    for a_ref, w_ref in zip(a_refs[1:], w_refs[1:]):
        y = y + jnp.dot(a_ref[...], w_ref[...], preferred_element_type=F32)
    xn = _layer_norm(alpha * x_ref[...] + gate_ref[...] * y, lng_ref[...], lnb_ref[...])
    xo_ref[...] = xn
    tok_ref[...] = _pack_bf16_pairs(xn * (1.0 + sc_ref[...]) + sh_ref[...])


def _outproj_ln(a_list, w_list, x, mod, ln_g, ln_b, *, t, tm, alpha, mod_row):
    d = x.shape[1]
    n_a = len(a_list)
    in_specs = [pl.BlockSpec((tm, a.shape[1]), lambda i: (i, 0)) for a in a_list]
    in_specs += [_const_spec(w.shape) for w in w_list]
    in_specs += [
        pl.BlockSpec((tm, d), lambda i: (i, 0)),
        _mod_spec(d, 2, mod_row, tm),
        _const_spec((1, d)), _const_spec((1, d)),
        _mod_spec(d, 4, mod_row, tm),
        _mod_spec(d, 3, mod_row, tm),
    ]
    return pl.pallas_call(
        functools.partial(_outproj_ln_kernel, n_a=n_a, alpha=alpha),
        grid=(t // tm,),
        in_specs=in_specs,
        out_specs=[pl.BlockSpec((tm, d), lambda i: (i, 0)), pl.BlockSpec((tm, d // 2), lambda i: (i, 0))],
        out_shape=[jax.ShapeDtypeStruct((t, d), F32), jax.ShapeDtypeStruct((t, d // 2), jnp.int32)],
        compiler_params=_params(("parallel",)),
        name="outproj_ln",
    )(*a_list, *w_list, x, mod, ln_g, ln_b, mod, mod)


def _router_kernel(t_ref, rw_ref, rb_ref, tri_ref, idx_ref, gw_ref, rank_ref, cnt_ref):
    @pl.when(pl.program_id(0) == 0)
    def _():
        cnt_ref[...] = jnp.zeros_like(cnt_ref)

    lo, hi = _unpack_bf16_pairs(t_ref[...])
    half = lo.shape[1]
    nt = (((1,), (1,)), ((), ()))
    logits = (lax.dot_general(rw_ref[:, :half], lo, nt, preferred_element_type=F32)
              + lax.dot_general(rw_ref[:, half:], hi, nt, preferred_element_type=F32))
    scores = 1.0 / (1.0 + jnp.exp(-logits))
    sel = scores + rb_ref[...]
    gsz = N_EXPERTS // N_GROUPS
    neg = -jnp.inf
    sub = lax.broadcasted_iota(jnp.int32, (gsz, sel.shape[1]), 0)
    slabs = [sel[g * gsz:(g + 1) * gsz, :] for g in range(N_GROUPS)]
    gscore = []
    for s in slabs:
        m1 = jnp.max(s, axis=0, keepdims=True)
        a1 = jnp.min(jnp.where(s == m1, sub, gsz), axis=0, keepdims=True)
        m2 = jnp.max(jnp.where(sub == a1, neg, s), axis=0, keepdims=True)
        gscore.append(m1 + m2)
    masked = []
    for g in range(N_GROUPS):
        ahead = jnp.zeros(gscore[g].shape, jnp.int32)
        for h in range(N_GROUPS):
            if h == g:
                continue
            beats = gscore[h] >= gscore[g] if h < g else gscore[h] > gscore[g]
            ahead = ahead + beats.astype(jnp.int32)
        masked.append(jnp.where(ahead < TOPK_GROUPS, slabs[g], neg))
    cur = jnp.concatenate(masked, axis=0)
    eio = lax.broadcasted_iota(jnp.int32, cur.shape, 0)
    picks, weights = [], []
    for _ in range(TOP_K):
        m = jnp.max(cur, axis=0, keepdims=True)
        a = jnp.min(jnp.where(cur == m, eio, N_EXPERTS), axis=0, keepdims=True)
        hit = eio == a
        picks.append(a)
        weights.append(jnp.sum(jnp.where(hit, scores, 0.0), axis=0, keepdims=True))
        cur = jnp.where(hit, neg, cur)
    total = weights[0]
    for w in weights[1:]:
        total = total + w
    for k in range(TOP_K):
        idx_ref[k:k + 1, :] = picks[k]
        gw_ref[k:k + 1, :] = weights[k] / total * ROUTED_SCALE
    for k in range(TOP_K, SUBLANES):
        idx_ref[k:k + 1, :] = jnp.zeros_like(picks[0])
        gw_ref[k:k + 1, :] = jnp.zeros_like(weights[0])
        rank_ref[k:k + 1, :] = jnp.zeros_like(picks[0])
    base = cnt_ref[:, 0:1]
    for k in range(TOP_K):
        onehot = jnp.where(eio == picks[k], 1.0, 0.0)
        before = jnp.dot(onehot.astype(BF16), tri_ref[...], preferred_element_type=F32)
        rank_ref[k:k + 1, :] = jnp.sum(onehot * (before + base), axis=0, keepdims=True).astype(jnp.int32)
        base = base + jnp.sum(onehot, axis=1, keepdims=True)
    cnt_ref[...] = jnp.broadcast_to(base, cnt_ref.shape)


def _router(tok, rw_t, rb, *, t, tt):
    half = tok.shape[1]
    tri = (jnp.arange(tt)[:, None] < jnp.arange(tt)[None, :]).astype(BF16)
    blk = pl.BlockSpec((SUBLANES, tt), lambda i: (0, i))
    return pl.pallas_call(
        _router_kernel,
        grid=(t // tt,),
        in_specs=[
            pl.BlockSpec((tt, half), lambda i: (i, 0)),
            _const_spec((N_EXPERTS, 2 * half)),
            _const_spec((N_EXPERTS, 1)),
            _const_spec((tt, tt)),
        ],
        out_specs=[blk, blk, blk, _const_spec((N_EXPERTS, LANES))],
        out_shape=[jax.ShapeDtypeStruct((SUBLANES, t), jnp.int32), jax.ShapeDtypeStruct((SUBLANES, t), F32),
                   jax.ShapeDtypeStruct((SUBLANES, t), jnp.int32), jax.ShapeDtypeStruct((N_EXPERTS, LANES), F32)],
        compiler_params=_params(("arbitrary",)),
        name="router",
    )(tok, rw_t, rb, tri)


def _slots_kernel(idx_ref, rank_ref, start_ref, pos_ref):
    start = start_ref[...]
    eio = lax.broadcasted_iota(jnp.int32, (N_EXPERTS, idx_ref.shape[1]), 0)
    for k in range(TOP_K):
        seg = jnp.sum(jnp.where(eio == idx_ref[k:k + 1, :], start, 0.0), axis=0, keepdims=True)
        pos_ref[k:k + 1, :] = rank_ref[k:k + 1, :] + seg.astype(jnp.int32)
    for k in range(TOP_K, SUBLANES):
        pos_ref[k:k + 1, :] = jnp.zeros((1, idx_ref.shape[1]), jnp.int32)


def _assign_slots(idx, rank, seg_start, *, t, tt):
    blk = pl.BlockSpec((SUBLANES, tt), lambda i: (0, i))
    return pl.pallas_call(
        _slots_kernel,
        grid=(t // tt,),
        in_specs=[blk, blk, _const_spec((N_EXPERTS, 1))],
        out_specs=blk,
        out_shape=jax.ShapeDtypeStruct((SUBLANES, t), jnp.int32),
        compiler_params=_params(("parallel",)),
        name="assign_slots",
    )(idx, rank, seg_start)


def _sc_gather_rows(table, idx):
    n = idx.shape[0]
    d = table.shape[1]
    n_workers = V7X_SC_CORES * V7X_SC_SUBCORES
    per_w = n // n_workers
    n_chunks = per_w // SC_GATHER_ROWS
    assert per_w * n_workers == n and n_chunks * SC_GATHER_ROWS == per_w
    n_buf = next(b for b in range(SC_GATHER_BUFFERS, 1, -1) if n_chunks % b == 0)
    mesh = plsc.VectorSubcoreMesh(core_axis_name="c", subcore_axis_name="s", num_cores=V7X_SC_CORES,
                                  num_subcores=V7X_SC_SUBCORES)

    @functools.partial(
        pl.kernel,
        out_type=jax.ShapeDtypeStruct((n, d), table.dtype),
        mesh=mesh,
        scratch_types=[
            pltpu.VMEM((per_w,), jnp.int32),
            pltpu.VMEM((n_buf, SC_GATHER_ROWS, d), table.dtype),
            pltpu.SemaphoreType.DMA((n_buf,)),
            pltpu.SemaphoreType.DMA((n_buf,)),
        ],
        name="sc_gather_rows",
    )
    def gather(table_hbm, idx_hbm, out_hbm, idx_v, rows_v, gsem, wsem):
        wid = lax.axis_index("s") * V7X_SC_CORES + lax.axis_index("c")
        base = wid * per_w
        pltpu.sync_copy(idx_hbm.at[pl.ds(base, per_w)], idx_v)

        def gather_copy(c, b):
            return pltpu.make_async_copy(table_hbm.at[idx_v.at[pl.ds(c * SC_GATHER_ROWS, SC_GATHER_ROWS)]],
                                         rows_v.at[b], gsem.at[b])

        def write_copy(c, b):
            return pltpu.make_async_copy(rows_v.at[b], out_hbm.at[pl.ds(base + c * SC_GATHER_ROWS, SC_GATHER_ROWS)],
                                         wsem.at[b])

        for b in range(n_buf - 1):
            gather_copy(b, b).start()

        @pl.loop(0, n_chunks, step=n_buf)
        def _(g):
            for b in range(n_buf):
                c = g + b
                prev = (b + n_buf - 1) % n_buf
                gather_copy(c, b).wait()
                write_copy(c, b).start()

                @pl.when(c >= 1)
                def _():
                    write_copy(c - 1, prev).wait()

                @pl.when(c + n_buf - 1 < n_chunks)
                def _():
                    gather_copy(c + n_buf - 1, prev).start()

        write_copy(n_chunks - 1, (n_chunks - 1) % n_buf).wait()

    return gather(table, idx)


def _experts_kernel(be_ref, nbu_ref, x_ref, wg_ref, wu_ref, wd_ref, *rest, block_off):
    y_ref, wgb, wub, wdb = rest[-4:]
    step = pl.program_id(0)
    b = block_off + step
    nbu = nbu_ref[0]

    @pl.when(b < nbu)
    def _():
        changed = jnp.logical_or(step == 0, be_ref[b] != be_ref[jnp.maximum(b - 1, 0)])

        @pl.when(changed)
        def _():
            wgb[...] = wg_ref[...].astype(BF16)
            wub[...] = wu_ref[...].astype(BF16)
            wdb[...] = wd_ref[...].astype(BF16)

        lo, hi = _unpack_bf16_pairs(x_ref[...])
        hg = _dot_halves(lo, hi, wgb)
        hu = _dot_halves(lo, hi, wub)
        h = hg * (1.0 / (1.0 + jnp.exp(-hg))) * hu
        y_ref[...] = _pack_bf16_pairs(jnp.dot(h.astype(BF16), wdb[...], preferred_element_type=F32))

    @pl.when(b >= nbu)
    def _():
        y_ref[...] = jnp.zeros_like(y_ref)


def _experts(xs, block_e, nb_used, wg, wu, wd, layer, y_prev, *, block_off, n_blocks):
    half = xs.shape[1]
    d = 2 * half
    ff = wg.shape[3]
    n_call = xs.shape[0] // MOE_ROWS

    def used(b, nbu):
        return jnp.clip(jnp.minimum(block_off + b, nbu[0] - 1) - block_off, 0, n_call - 1)

    def expert(b, be):
        return be[block_off + b]

    in_specs = [
        pl.BlockSpec((MOE_ROWS, half), lambda b, be, nbu: (used(b, nbu), 0)),
        pl.BlockSpec((None, None, d, ff), lambda b, be, nbu: (layer, expert(b, be), 0, 0)),
        pl.BlockSpec((None, None, d, ff), lambda b, be, nbu: (layer, expert(b, be), 0, 0)),
        pl.BlockSpec((None, None, ff, d), lambda b, be, nbu: (layer, expert(b, be), 0, 0)),
    ]
    args = [block_e, nb_used, xs, wg, wu, wd]
    aliases = {}
    if y_prev is not None:
        in_specs.append(pl.BlockSpec(memory_space=pl.ANY))
        aliases = {len(args): 0}
        args.append(y_prev)
    grid_spec = pltpu.PrefetchScalarGridSpec(
        num_scalar_prefetch=2,
        grid=(n_call,),
        in_specs=in_specs,
        out_specs=pl.BlockSpec((MOE_ROWS, half), lambda b, be, nbu: (block_off + b, 0)),
        scratch_shapes=[
            pltpu.VMEM((d, ff), BF16),
            pltpu.VMEM((d, ff), BF16),
            pltpu.VMEM((ff, d), BF16),
        ],
    )
    return pl.pallas_call(
        functools.partial(_experts_kernel, block_off=block_off),
        grid_spec=grid_spec,
        out_shape=jax.ShapeDtypeStruct((n_blocks * MOE_ROWS, half), jnp.int32),
        input_output_aliases=aliases,
        compiler_params=_params(("arbitrary",)),
        name="experts",
    )(*args)


def _combine_ln_kernel(*refs, alpha, emit_next, n_prev):
    y_ref, gw_ref, tok_ref, x_ref, sg_ref, su_ref, sd_ref, gate_ref, lng_ref, lnb_ref = refs[:10]
    outs = refs[len(refs) - (2 if emit_next else 1):]
    if emit_next:
        sc_ref, sh_ref = refs[10:12]
        xo_ref, u_ref = outs
    else:
        (xo_ref,) = outs
    lo, hi = _unpack_bf16_pairs(tok_ref[...])
    hg = _dot_halves(lo, hi, sg_ref)
    hu = _dot_halves(lo, hi, su_ref)
    h = hg * (1.0 / (1.0 + jnp.exp(-hg))) * hu
    gw = gw_ref[...]
    f_lo = f_hi = None
    for k in range(TOP_K):
        y_lo, y_hi = _unpack_bf16_pairs(y_ref[k])
        w = gw[:, k:k + 1]
        f_lo = y_lo.astype(F32) * w if f_lo is None else f_lo + y_lo.astype(F32) * w
        f_hi = y_hi.astype(F32) * w if f_hi is None else f_hi + y_hi.astype(F32) * w
    f = jnp.concatenate([f_lo, f_hi], axis=-1) + jnp.dot(h.astype(BF16), sd_ref[...], preferred_element_type=F32)
    xn = _layer_norm(alpha * x_ref[...] + gate_ref[...] * f, lng_ref[...], lnb_ref[...])
    xo_ref[...] = xn
    if emit_next:
        u_ref[...] = (xn * (1.0 + sc_ref[...]) + sh_ref[...]).astype(u_ref.dtype)


def _combine_ln(y3, gw_t, tok, x, sg, su, sd, mod, ln_g, ln_b, mod_next, prev, *, t, row_off, tm, alpha, mod_row):
    d = x.shape[1]
    emit_next = mod_next is not None
    off = row_off // tm
    assert off * tm == row_off

    def rows(i):
        return (off + i, 0)

    def part_mod_row(r):
        return mod_row(r + row_off)

    in_specs = [
        pl.BlockSpec((TOP_K, tm, d // 2), lambda i: (0, i, 0)),
        pl.BlockSpec((tm, SUBLANES), rows),
        pl.BlockSpec((tm, d // 2), rows),
        pl.BlockSpec((tm, d), rows),
        _const_spec(sg.shape), _const_spec(su.shape), _const_spec(sd.shape),
        _mod_spec(d, 5, part_mod_row, tm),
        _const_spec((1, d)), _const_spec((1, d)),
    ]
    args = [y3, gw_t, tok, x, sg, su, sd, mod, ln_g, ln_b]
    out_specs = [pl.BlockSpec((tm, d), rows)]
    out_shape = [jax.ShapeDtypeStruct((t, d), F32)]
    if emit_next:
        in_specs += [_mod_spec(d, 1, part_mod_row, tm), _mod_spec(d, 0, part_mod_row, tm)]
        args += [mod_next, mod_next]
        out_specs.append(pl.BlockSpec((tm, d), rows))
        out_shape.append(jax.ShapeDtypeStruct((t, d), BF16))
    aliases = {}
    if prev is not None:
        for k, p in enumerate(prev):
            in_specs.append(pl.BlockSpec(memory_space=pl.ANY))
            aliases[len(args)] = k
            args.append(p)
    return pl.pallas_call(
        functools.partial(_combine_ln_kernel, alpha=alpha, emit_next=emit_next, n_prev=len(aliases)),
        grid=(y3.shape[1] // tm,),
        in_specs=in_specs,
        out_specs=out_specs,
        out_shape=out_shape,
        input_output_aliases=aliases,
        compiler_params=_params(("parallel",)),
        name="combine_ln",
    )(*args)


def _moe(tok, x, t, layer, router_w, router_b, wg, wu, wd, sg, su, sd, mod, ln_g, ln_b, mod_next, *, alpha, mod_row,
         tm):
    half = tok.shape[1]
    tt = 512
    idx, gw, rank, cnt = _router(tok, router_w.T.astype(BF16), router_b.reshape(N_EXPERTS, 1), t=t, tt=tt)
    n_asg = t * TOP_K
    counts = cnt[:, 0].astype(jnp.int32)
    padded = (counts + MOE_ROWS - 1) // MOE_ROWS * MOE_ROWS
    pend = jnp.cumsum(padded)
    pstart = pend - padded
    sc_rows = V7X_SC_CORES * V7X_SC_SUBCORES * SC_GATHER_ROWS * 2
    blocks_granule = MOE_DISPATCH_PARTS * max(sc_rows // MOE_ROWS, 1)
    assert (blocks_granule // MOE_DISPATCH_PARTS * MOE_ROWS) % sc_rows == 0
    n_blocks = -(-((n_asg + N_EXPERTS * (MOE_ROWS - 1)) // MOE_ROWS + 1) // blocks_granule) * blocks_granule
    assert n_asg % sc_rows == 0
    block_start = jnp.arange(n_blocks, dtype=jnp.int32) * MOE_ROWS
    block_e = jnp.minimum(jnp.sum((pend[None, :] <= block_start[:, None]).astype(jnp.int32), axis=1), N_EXPERTS - 1)
    nb_used = (pend[-1] // MOE_ROWS).astype(jnp.int32).reshape(1)
    pos2d = _assign_slots(idx, rank, pstart.astype(F32).reshape(N_EXPERTS, 1), t=t, tt=tt)[:TOP_K]
    pos = pos2d.reshape(-1)
    tok_of_asg = jnp.tile(jnp.arange(t, dtype=jnp.int32), TOP_K)
    n_pad = n_blocks * MOE_ROWS - n_asg
    seg_pad_end = jnp.cumsum(padded - counts)
    j = jnp.arange(n_pad, dtype=jnp.int32)
    pad_e = jnp.sum((seg_pad_end[None, :] <= j[:, None]).astype(jnp.int32), axis=1)
    seg_base = pstart + counts - (seg_pad_end - (padded - counts))
    in_seg = j + jnp.sum(jnp.where(pad_e[:, None] == jnp.arange(N_EXPERTS)[None, :], seg_base[None, :], 0), axis=1)
    pad_slot = jnp.where(pad_e < N_EXPERTS, in_seg, pend[-1] + j - seg_pad_end[-1])
    _, slot_tok = lax.sort((jnp.concatenate([pos, pad_slot]), jnp.concatenate([tok_of_asg, pad_slot % t])),
                           num_keys=1)
    per = n_blocks // MOE_DISPATCH_PARTS * MOE_ROWS
    xs = [_sc_gather_rows(tok, slot_tok[i * per:(i + 1) * per]) for i in range(MOE_DISPATCH_PARTS)]
    y = None
    for i in range(MOE_DISPATCH_PARTS):
        y = _experts(xs[i], block_e, nb_used, wg, wu, wd, layer, y, block_off=i * per // MOE_ROWS, n_blocks=n_blocks)
    n_cparts = MOE_COMBINE_PARTS if (t // MOE_COMBINE_PARTS * TOP_K) % sc_rows == 0 else 1
    t_part = t // n_cparts
    y3 = [_sc_gather_rows(y, pos2d[:, i * t_part:(i + 1) * t_part].reshape(-1)).reshape(TOP_K, t_part, half)
          for i in range(n_cparts)]
    outs = None
    gw_t = gw.T
    for i in range(n_cparts):
        outs = _combine_ln(y3[i], gw_t, tok, x, sg, su, sd, mod, ln_g, ln_b, mod_next, outs, t=t,
                           row_off=i * t_part, tm=tm, alpha=alpha, mod_row=mod_row)
    return outs


def _rope64(r, c_ref, sa_ref, sb_ref):
    return r * c_ref[...] + pltpu.roll(r, LANES - QK_ROPE // 2, 1) * sa_ref[...] + pltpu.roll(r, QK_ROPE // 2, 1) * sb_ref[...]


def _mla_q_kernel(d_ref, gain_ref, w_ref, c_ref, sa_ref, sb_ref, q_ref, *, scale):
    n = _rms(d_ref[...], gain_ref[...]).astype(BF16)
    q = jnp.dot(n, w_ref[...], preferred_element_type=F32)
    for h in range(MLA_HEADS):
        lo = h * MLA_DK_PAD
        q_ref[:, lo:lo + QK_NOPE] = (q[:, lo:lo + QK_NOPE] * scale).astype(q_ref.dtype)
        r = _rope64(q[:, lo + QK_NOPE:lo + MLA_DK_PAD], c_ref, sa_ref, sb_ref)
        q_ref[:, lo + QK_NOPE:lo + MLA_DK_PAD] = (r * scale).astype(q_ref.dtype)


def _mla_kv_kernel(ckv_ref, kr_ref, gain_ref, wk_ref, wv_ref, c_ref, sa_ref, sb_ref, k_ref, v_ref):
    n = _rms(ckv_ref[...], gain_ref[...]).astype(BF16)
    kn = jnp.dot(n, wk_ref[...], preferred_element_type=F32)
    v_ref[...] = jnp.dot(n, wv_ref[...], preferred_element_type=F32).astype(v_ref.dtype)
    kr = _rope64(kr_ref[...], c_ref, sa_ref, sb_ref).astype(k_ref.dtype)
    for h in range(MLA_HEADS):
        lo = h * MLA_DK_PAD
        k_ref[:, lo:lo + QK_NOPE] = kn[:, h * QK_NOPE:(h + 1) * QK_NOPE].astype(k_ref.dtype)
        k_ref[:, lo + QK_NOPE:lo + MLA_DK_PAD] = kr


def _axial_angles(n_tok, rot_dim):
    rows = n_tok // GRID_W
    n_freq = rot_dim // 4
    inv = ROPE_THETA ** (-jnp.arange(n_freq, dtype=F32) / n_freq)
    row = jnp.repeat(jnp.arange(rows, dtype=F32), GRID_W)
    col = jnp.tile(jnp.arange(GRID_W, dtype=F32), rows)
    return jnp.concatenate([row[:, None] * inv, col[:, None] * inv], axis=-1)


def _rope_tables_128(n_tok, ident_rows):
    ang = _axial_angles(n_tok, HEAD_DIM)
    cos, sin = jnp.cos(ang), jnp.sin(ang)
    c = jnp.concatenate([cos, cos], axis=-1)
    s = jnp.concatenate([-sin, sin], axis=-1)
    c = jnp.concatenate([c, jnp.ones((ident_rows, HEAD_DIM), F32)], axis=0)
    s = jnp.concatenate([s, jnp.zeros((ident_rows, HEAD_DIM), F32)], axis=0)
    return c, s


def _rope_tables_64(n_tok, ident_rows):
    ang = _axial_angles(n_tok, QK_ROPE)
    cos, sin = jnp.cos(ang), jnp.sin(ang)
    half = QK_ROPE // 2
    z = jnp.zeros((n_tok, LANES - QK_ROPE), F32)
    zh = jnp.zeros((n_tok, half), F32)
    c = jnp.concatenate([cos, cos, z], axis=-1)
    sa = jnp.concatenate([-sin, zh, z], axis=-1)
    sb = jnp.concatenate([zh, sin, z], axis=-1)
    ci = jnp.concatenate([jnp.ones((ident_rows, QK_ROPE), F32), jnp.zeros((ident_rows, LANES - QK_ROPE), F32)], axis=-1)
    zi = jnp.zeros((ident_rows, LANES), F32)
    return jnp.concatenate([c, ci], 0), jnp.concatenate([sa, zi], 0), jnp.concatenate([sb, zi], 0)


def kernel(x, c, ctx, c_ctx, w_ada, b_ada, ln_g, ln_b, a_w_in, a_conv_w, a_q_gain, a_k_gain, a_w_out, m_w_down, m_q_gain, m_kv_gain, m_w_uq, m_w_ukv, m_w_out, router_w, router_b, e_w_gate, e_w_up, e_w_down, s_w_gate, s_w_up, s_w_down):
    batch, seq, d = x.shape
    ctx_len = ctx.shape[1]
    depth = w_ada.shape[0]
    assert depth == 2, "one conv+GQA layer followed by one MLA layer"
    alpha = (2 * depth) ** 0.25
    t_lat = batch * seq
    t_ctx = batch * ctx_len
    t_all = t_lat + t_ctx
    tr = 256
    assert seq % tr == 0 and ctx_len % tr == 0 and seq % GRID_W == 0
    lat_tiles = t_lat // tr
    lat_seq_tiles = seq // tr
    ctx_seq_tiles = ctx_len // tr
    lk = ctx_len + seq

    def mod_row(r):
        return jnp.minimum(r // seq, batch)

    def kv_block(i):
        is_lat = i < lat_tiles
        cidx = i - lat_tiles
        b = jnp.where(is_lat, i // lat_seq_tiles, cidx // ctx_seq_tiles)
        rb = jnp.where(is_lat, ctx_seq_tiles + i % lat_seq_tiles, cidx % ctx_seq_tiles)
        return b, rb

    def pos_block(i):
        return jnp.where(i < lat_tiles, i % lat_seq_tiles, lat_seq_tiles)

    rows = -(-(batch + 1) // SUBLANES) * SUBLANES
    cond = jnp.concatenate([c, c_ctx[None, :], jnp.zeros((rows - batch - 1, d), F32)], axis=0)
    mod = _ada_table(cond, w_ada, b_ada).reshape(depth, rows, 1, 6 * d)

    x_all = jnp.concatenate([x.reshape(t_lat, d), ctx.reshape(t_ctx, d)], axis=0)

    tm_in = 1024 if (t_all % 1024 == 0 and seq % 1024 == 0) else tr
    proj = _mod_matmul(x_all, mod[0], a_w_in[0].astype(BF16), BF16, tm_in, 768, mod_row)

    cos128, sin128 = _rope_tables_128(seq, tr)
    d_q = ATT_HEADS * HEAD_DIM
    d_kv = ATT_KV_HEADS * HEAD_DIM
    qkv_w = d_q + 2 * d_kv
    qkv_blk = 3 * CONV_DIM // qkv_w
    assert qkv_blk * qkv_w == 3 * CONV_DIM
    q0, k0, v0 = pl.pallas_call(
        functools.partial(_qkprep_kernel, scale=1.0 / math.sqrt(HEAD_DIM)),
        grid=(t_all // tr,),
        in_specs=[
            pl.BlockSpec((tr, qkv_w), lambda i: (i, qkv_blk)),
            pl.BlockSpec((tr, HEAD_DIM), lambda i: (pos_block(i), 0)),
            pl.BlockSpec((tr, HEAD_DIM), lambda i: (pos_block(i), 0)),
            _const_spec((1, HEAD_DIM)), _const_spec((1, HEAD_DIM)),
        ],
        out_specs=[
            pl.BlockSpec((tr, d_q), lambda i: (i, 0)),
            pl.BlockSpec((None, tr, d_kv), lambda i: (*kv_block(i), 0)),
            pl.BlockSpec((None, tr, d_kv), lambda i: (*kv_block(i), 0)),
        ],
        out_shape=[
            jax.ShapeDtypeStruct((t_all, d_q), BF16),
            jax.ShapeDtypeStruct((batch, lk, d_kv), BF16),
            jax.ShapeDtypeStruct((batch, lk, d_kv), BF16),
        ],
        compiler_params=_params(("parallel",)),
        name="qk_prep",
    )(proj, cos128, sin128, a_q_gain[0].reshape(1, HEAD_DIM), a_k_gain[0].reshape(1, HEAD_DIM))

    grp = ATT_HEADS // ATT_KV_HEADS
    att_lat = _attention(q0, k0, v0, batch=batch, sq=seq, lk=lk, n_kv=ATT_KV_HEADS, group=grp, dk=HEAD_DIM,
                         dv=HEAD_DIM, tq=512 if seq % 512 == 0 else tr, rows=256, q_row_off=0)
    att_ctx = _attention(q0, k0, v0, batch=batch, sq=ctx_len, lk=ctx_len, n_kv=ATT_KV_HEADS, group=grp,
                         dk=HEAD_DIM, dv=HEAD_DIM, tq=256, rows=256, q_row_off=t_lat)
    att0 = jnp.concatenate([att_lat, att_ctx], axis=0)

    conv0 = _conv_gate(proj, a_conv_w[0], t=t_all, tm=tr, tc=512, lat_tiles=lat_tiles,
                       lat_seq_tiles=lat_seq_tiles, ctx_seq_tiles=ctx_seq_tiles)

    w_out0 = a_w_out[0].astype(BF16)
    x1, tok0 = _outproj_ln([conv0, att0], [w_out0[:CONV_DIM], w_out0[CONV_DIM:]], x_all, mod[0],
                           ln_g[0, 0].reshape(1, d), ln_b[0, 0].reshape(1, d), t=t_all, tm=tr, alpha=alpha,
                           mod_row=mod_row)

    x2, u1 = _moe(tok0, x1, t_all, 0, router_w[0], router_b[0], e_w_gate, e_w_up, e_w_down,
                  s_w_gate[0].astype(BF16), s_w_up[0].astype(BF16), s_w_down[0].astype(BF16), mod[0],
                  ln_g[0, 1].reshape(1, d), ln_b[0, 1].reshape(1, d), mod[1], alpha=alpha,
                  mod_row=mod_row, tm=256)

    n_down = Q_LORA + KV_LORA + QK_ROPE
    n_down_pad = -(-n_down // LANES) * LANES
    w_down = jnp.pad(m_w_down[0], ((0, 0), (0, n_down_pad - n_down))).astype(BF16)
    down = _matmul(u1, w_down, F32, 512, n_down_pad)

    dqk = QK_NOPE + QK_ROPE
    w_uq = m_w_uq[0].reshape(Q_LORA, MLA_HEADS, dqk)
    w_uq = jnp.pad(w_uq, ((0, 0), (0, 0), (0, MLA_DK_PAD - dqk))).reshape(Q_LORA, MLA_HEADS * MLA_DK_PAD).astype(BF16)
    w_ukv = m_w_ukv[0].reshape(KV_LORA, MLA_HEADS, QK_NOPE + V_DIM)
    w_uk = w_ukv[:, :, :QK_NOPE].reshape(KV_LORA, MLA_HEADS * QK_NOPE).astype(BF16)
    w_uv = w_ukv[:, :, QK_NOPE:].reshape(KV_LORA, MLA_HEADS * V_DIM).astype(BF16)

    c64, sa64, sb64 = _rope_tables_64(seq, tr)
    rope_specs = [pl.BlockSpec((tr, LANES), lambda i: (pos_block(i), 0))] * 3
    q1 = pl.pallas_call(
        functools.partial(_mla_q_kernel, scale=1.0 / math.sqrt(dqk)),
        grid=(lat_tiles,),
        in_specs=[
            pl.BlockSpec((tr, Q_LORA), lambda i: (i, 0)),
            _const_spec((1, Q_LORA)),
            _const_spec(w_uq.shape),
        ] + rope_specs,
        out_specs=pl.BlockSpec((tr, MLA_HEADS * MLA_DK_PAD), lambda i: (i, 0)),
        out_shape=jax.ShapeDtypeStruct((t_lat, MLA_HEADS * MLA_DK_PAD), BF16),
        compiler_params=_params(("parallel",)),
        name="mla_q",
    )(down, m_q_gain[0].reshape(1, Q_LORA), w_uq, c64, sa64, sb64)

    assert KV_LORA == Q_LORA and (Q_LORA + KV_LORA) % LANES == 0
    k1, v1 = pl.pallas_call(
        _mla_kv_kernel,
        grid=(t_all // tr,),
        in_specs=[
            pl.BlockSpec((tr, KV_LORA), lambda i: (i, 1)),
            pl.BlockSpec((tr, LANES), lambda i: (i, (Q_LORA + KV_LORA) // LANES)),
            _const_spec((1, KV_LORA)),
            _const_spec(w_uk.shape), _const_spec(w_uv.shape),
        ] + rope_specs,
        out_specs=[
            pl.BlockSpec((None, tr, MLA_HEADS * MLA_DK_PAD), lambda i: (*kv_block(i), 0)),
            pl.BlockSpec((None, tr, MLA_HEADS * V_DIM), lambda i: (*kv_block(i), 0)),
        ],
        out_shape=[
            jax.ShapeDtypeStruct((batch, lk, MLA_HEADS * MLA_DK_PAD), BF16),
            jax.ShapeDtypeStruct((batch, lk, MLA_HEADS * V_DIM), BF16),
        ],
        compiler_params=_params(("parallel",)),
        name="mla_kv",
    )(down, down, m_kv_gain[0].reshape(1, KV_LORA), w_uk, w_uv, c64, sa64, sb64)

    att1 = _attention(q1, k1, v1, batch=batch, sq=seq, lk=lk, n_kv=MLA_HEADS, group=1, dk=MLA_DK_PAD, dv=V_DIM,
                      tq=next(c for c in (2048, 1024, 512, tr) if seq % c == 0), rows=256, q_row_off=0)

    x3, tok1 = _outproj_ln([att1], [m_w_out[0].astype(BF16)], x2, mod[1], ln_g[1, 0].reshape(1, d),
                           ln_b[1, 0].reshape(1, d), t=t_lat, tm=tr, alpha=alpha, mod_row=mod_row)

    (x4,) = _moe(tok1, x3, t_lat, 1, router_w[1], router_b[1], e_w_gate, e_w_up, e_w_down,
                 s_w_gate[1].astype(BF16), s_w_up[1].astype(BF16), s_w_down[1].astype(BF16), mod[1],
                 ln_g[1, 1].reshape(1, d), ln_b[1, 1].reshape(1, d), None, alpha=alpha,
                 mod_row=mod_row, tm=256)
    return x4.reshape(batch, seq, d)
```

```python
import functools
import math

import jax
import jax.numpy as jnp
import numpy as np
from jax import lax
from jax.experimental import pallas as pl
from jax.experimental.pallas import tpu as pltpu
from jax.experimental.pallas import tpu_sc as plsc

F32 = jnp.float32
BF16 = jnp.bfloat16

GRID_W = 64
CONV_DIM = 1024
ATT_HEADS = 8
ATT_KV_HEADS = 2
HEAD_DIM = 128
MLA_HEADS = 16
Q_LORA = 512
KV_LORA = 512
QK_NOPE = 128
QK_ROPE = 64
V_DIM = 128
N_EXPERTS = 64
TOP_K = 6
N_GROUPS = 8
TOPK_GROUPS = 4
ROUTED_SCALE = 2.5
ROPE_THETA = 10000.0
LN_EPS = 1e-5
RMS_EPS = 1e-6

V7X_VMEM_LIMIT_BYTES = 56 * 1024 * 1024
LANES = 128
SUBLANES = 8
MOE_ROWS = 512
MOE_DISPATCH_PARTS = 4
MOE_COMBINE_PARTS = 2
V7X_SC_CORES = 2
V7X_SC_SUBCORES = 16
SC_GATHER_ROWS = 16
SC_GATHER_BUFFERS = 4
MLA_DK_PAD = 256


def _params(sem):
    return pltpu.CompilerParams(dimension_semantics=sem, vmem_limit_bytes=V7X_VMEM_LIMIT_BYTES)


def _const_spec(shape):
    nd = len(shape)
    return pl.BlockSpec(shape, lambda *_: (0,) * nd)


def _ada_kernel(s_ref, w_ref, b_ref, o_ref):
    s = s_ref[...]
    s = s * (1.0 / (1.0 + jnp.exp(-s)))
    o_ref[...] = jnp.dot(s.astype(BF16), w_ref[...].astype(BF16), preferred_element_type=F32) + b_ref[...]


def _ada_table(cond, w_ada, b_ada):
    depth, d, n = w_ada.shape
    r = cond.shape[0]
    tn = 1024
    return pl.pallas_call(
        _ada_kernel,
        grid=(depth, n // tn),
        in_specs=[
            pl.BlockSpec((r, d), lambda l, j: (0, 0)),
            pl.BlockSpec((None, d, tn), lambda l, j: (l, 0, j)),
            pl.BlockSpec((None, 1, tn), lambda l, j: (l, 0, j)),
        ],
        out_specs=pl.BlockSpec((None, r, tn), lambda l, j: (l, 0, j)),
        out_shape=jax.ShapeDtypeStruct((depth, r, n), F32),
        compiler_params=_params(("parallel", "parallel")),
        name="ada_table",
    )(cond, w_ada, b_ada.reshape(depth, 1, n))


def _mod_spec(d, chunk, mod_row, tm):
    return pl.BlockSpec((None, 1, d), lambda i: (mod_row(i * tm), 0, chunk))


def _mm_kernel(a_ref, w_ref, o_ref):
    o_ref[...] = jnp.dot(a_ref[...], w_ref[...], preferred_element_type=F32).astype(o_ref.dtype)


def _matmul(a, w, out_dtype, tm, tn):
    m, k = a.shape
    n = w.shape[1]
    return pl.pallas_call(
        _mm_kernel,
        grid=(m // tm, n // tn),
        in_specs=[
            pl.BlockSpec((tm, k), lambda i, j: (i, 0)),
            pl.BlockSpec((k, tn), lambda i, j: (0, j)),
        ],
        out_specs=pl.BlockSpec((tm, tn), lambda i, j: (i, j)),
        out_shape=jax.ShapeDtypeStruct((m, n), out_dtype),
        compiler_params=_params(("parallel", "parallel")),
        name="matmul",
    )(a, w)


def _mod_mm_kernel(x_ref, sc_ref, sh_ref, w_ref, o_ref, u_ref):
    @pl.when(pl.program_id(1) == 0)
    def _():
        u_ref[...] = (x_ref[...] * (1.0 + sc_ref[...]) + sh_ref[...]).astype(u_ref.dtype)

    o_ref[...] = jnp.dot(u_ref[...], w_ref[...], preferred_element_type=F32).astype(o_ref.dtype)


def _mod_matmul(x, mod, w, out_dtype, tm, tn, mod_row):
    m, k = x.shape
    n = w.shape[1]

    def mod_spec(chunk):
        return pl.BlockSpec((None, 1, k), lambda i, j: (mod_row(i * tm), 0, chunk))

    return pl.pallas_call(
        _mod_mm_kernel,
        grid=(m // tm, n // tn),
        in_specs=[
            pl.BlockSpec((tm, k), lambda i, j: (i, 0)),
            mod_spec(1), mod_spec(0),
            pl.BlockSpec((k, tn), lambda i, j: (0, j)),
        ],
        out_specs=pl.BlockSpec((tm, tn), lambda i, j: (i, j)),
        out_shape=jax.ShapeDtypeStruct((m, n), out_dtype),
        scratch_shapes=[pltpu.VMEM((tm, k), w.dtype)],
        compiler_params=_params(("parallel", "arbitrary")),
        name="mod_matmul",
    )(x, mod, mod, w)


def _rms(t, gain):
    return t * lax.rsqrt(jnp.mean(t * t, axis=-1, keepdims=True) + RMS_EPS) * gain


def _qkprep_kernel(p_ref, cos_ref, sin_ref, qg_ref, kg_ref, q_ref, k_ref, v_ref, *, scale):
    cos = cos_ref[...]
    sin = sin_ref[...]

    def norm_rope(t, gain):
        y = _rms(t.astype(F32), gain)
        return y * cos + pltpu.roll(y, HEAD_DIM // 2, 1) * sin

    for h in range(ATT_HEADS):
        sl = slice(h * HEAD_DIM, (h + 1) * HEAD_DIM)
        q_ref[:, sl] = (norm_rope(p_ref[:, sl], qg_ref[...]) * scale).astype(q_ref.dtype)
    k0 = ATT_HEADS * HEAD_DIM
    for h in range(ATT_KV_HEADS):
        sl = slice(h * HEAD_DIM, (h + 1) * HEAD_DIM)
        k_ref[:, sl] = norm_rope(p_ref[:, k0 + h * HEAD_DIM:k0 + (h + 1) * HEAD_DIM], kg_ref[...]).astype(k_ref.dtype)
    v0 = k0 + ATT_KV_HEADS * HEAD_DIM
    v_ref[...] = p_ref[:, v0:v0 + ATT_KV_HEADS * HEAD_DIM].astype(v_ref.dtype)


def _attn_kernel(q_ref, k_ref, v_ref, o_ref, *, group, tq, rows, dk, dv):
    k = k_ref[...]
    v = v_ref[...]
    for h in range(group):
        for r in range(0, tq, rows):
            q = q_ref[r:r + rows, h * dk:(h + 1) * dk]
            s = lax.dot_general(q, k, (((1,), (1,)), ((), ())), preferred_element_type=F32)
            m = jnp.max(s, axis=-1, keepdims=True)
            p = jnp.exp(s - m)
            l = jnp.sum(p, axis=-1, keepdims=True)
            o = jnp.dot(p.astype(v.dtype), v, preferred_element_type=F32)
            o_ref[r:r + rows, h * dv:(h + 1) * dv] = (o / l).astype(o_ref.dtype)


def _attention(q, k, v, *, batch, sq, lk, n_kv, group, dk, dv, tq, rows, q_row_off):
    nq = sq // tq
    off = q_row_off // tq
    assert tq % rows == 0 and q_row_off % tq == 0 and sq % tq == 0
    return pl.pallas_call(
        functools.partial(_attn_kernel, group=group, tq=tq, rows=rows, dk=dk, dv=dv),
        grid=(batch, n_kv, nq),
        in_specs=[
            pl.BlockSpec((tq, group * dk), lambda b, g, i: (off + b * nq + i, g)),
            pl.BlockSpec((None, lk, dk), lambda b, g, i: (b, 0, g)),
            pl.BlockSpec((None, lk, dv), lambda b, g, i: (b, 0, g)),
        ],
        out_specs=pl.BlockSpec((tq, group * dv), lambda b, g, i: (b * nq + i, g)),
        out_shape=jax.ShapeDtypeStruct((batch * sq, n_kv * group * dv), BF16),
        compiler_params=_params(("parallel", "parallel", "parallel")),
        name="attention",
    )(q, k, v)


def _conv_kernel(gb_ref, gc_ref, hv_ref, gcp_ref, hvp_ref, gcn_ref, hvn_ref, w_ref, o_ref, *,
                 tm, lat_tiles, lat_seq_tiles, ctx_seq_tiles):
    i = pl.program_id(0)
    is_lat = i < lat_tiles
    pos = jnp.where(is_lat, i % lat_seq_tiles, (i - lat_tiles) % ctx_seq_tiles)
    seq_tiles = jnp.where(is_lat, lat_seq_tiles, ctx_seq_tiles)
    not_first = (pos != 0).astype(F32)
    not_last = (pos != seq_tiles - 1).astype(F32)
    p = gc_ref[...].astype(F32) * hv_ref[...].astype(F32)
    halo_prev = gcp_ref[SUBLANES - 1:SUBLANES, :].astype(F32) * hvp_ref[SUBLANES - 1:SUBLANES, :].astype(F32) * not_first
    halo_next = gcn_ref[0:1, :].astype(F32) * hvn_ref[0:1, :].astype(F32) * not_last
    row = lax.broadcasted_iota(jnp.int32, p.shape, 0)
    prev = jnp.where(row == 0, halo_prev, pltpu.roll(p, 1, 0))
    nxt = jnp.where(row == tm - 1, halo_next, pltpu.roll(p, tm - 1, 0))
    w = w_ref[...]
    conv = w[0:1, :] * prev + w[1:2, :] * p + w[2:3, :] * nxt
    o_ref[...] = (gb_ref[...].astype(F32) * conv).astype(o_ref.dtype)


def _conv_gate(p, conv_w, *, t, tm, tc, lat_tiles, lat_seq_tiles, ctx_seq_tiles):
    nct = CONV_DIM // tc
    hb = tm // SUBLANES
    n_halo = t // SUBLANES

    def cur(part):
        return pl.BlockSpec((tm, tc), lambda i, j: (i, part * nct + j))

    def prev(part):
        return pl.BlockSpec((SUBLANES, tc), lambda i, j: (jnp.maximum(i * hb - 1, 0), part * nct + j))

    def nxt(part):
        return pl.BlockSpec((SUBLANES, tc), lambda i, j: (jnp.minimum((i + 1) * hb, n_halo - 1), part * nct + j))

    return pl.pallas_call(
        functools.partial(_conv_kernel, tm=tm, lat_tiles=lat_tiles, lat_seq_tiles=lat_seq_tiles,
                          ctx_seq_tiles=ctx_seq_tiles),
        grid=(t // tm, nct),
        in_specs=[cur(0), cur(1), cur(2), prev(1), prev(2), nxt(1), nxt(2),
                  pl.BlockSpec((3, tc), lambda i, j: (0, j))],
        out_specs=pl.BlockSpec((tm, tc), lambda i, j: (i, j)),
        out_shape=jax.ShapeDtypeStruct((t, CONV_DIM), BF16),
        compiler_params=_params(("parallel", "parallel")),
        name="conv_gate",
    )(p, p, p, p, p, p, p, conv_w)


def _layer_norm(z, g, b):
    mu = jnp.mean(z, axis=-1, keepdims=True)
    zc = z - mu
    var = jnp.mean(zc * zc, axis=-1, keepdims=True)
    return zc * lax.rsqrt(var + LN_EPS) * g + b


def _pack_bf16_pairs(x):
    half = x.shape[1] // 2
    lo = lax.bitcast_convert_type(x[:, :half].astype(BF16).astype(F32), jnp.uint32) >> 16
    hi = lax.bitcast_convert_type(x[:, half:].astype(BF16).astype(F32), jnp.uint32) & jnp.uint32(0xFFFF0000)
    return lax.bitcast_convert_type(lo | hi, jnp.int32)


def _unpack_bf16_pairs(w):
    u = lax.bitcast_convert_type(w, jnp.uint32)
    lo = lax.bitcast_convert_type(u << 16, F32).astype(BF16)
    hi = lax.bitcast_convert_type(u & jnp.uint32(0xFFFF0000), F32).astype(BF16)
    return lo, hi


def _dot_halves(lo, hi, w_ref):
    half = lo.shape[1]
    return (jnp.dot(lo, w_ref[:half, :], preferred_element_type=F32)
            + jnp.dot(hi, w_ref[half:, :], preferred_element_type=F32))


def _outproj_ln_kernel(*refs, n_a, alpha):
    a_refs = refs[:n_a]
    w_refs = refs[n_a:2 * n_a]
    x_ref, gate_ref, lng_ref, lnb_ref, sc_ref, sh_ref, xo_ref, tok_ref = refs[2 * n_a:]
    y = jnp.dot(a_refs[0][...], w_refs[0][...], preferred_element_type=F32)
    for a_ref, w_ref in zip(a_refs[1:], w_refs[1:]):
        y = y + jnp.dot(a_ref[...], w_ref[...], preferred_element_type=F32)
    xn = _layer_norm(alpha * x_ref[...] + gate_ref[...] * y, lng_ref[...], lnb_ref[...])
    xo_ref[...] = xn
    tok_ref[...] = _pack_bf16_pairs(xn * (1.0 + sc_ref[...]) + sh_ref[...])


def _outproj_ln(a_list, w_list, x, mod, ln_g, ln_b, *, t, tm, alpha, mod_row):
    d = x.shape[1]
    n_a = len(a_list)
    in_specs = [pl.BlockSpec((tm, a.shape[1]), lambda i: (i, 0)) for a in a_list]
    in_specs += [_const_spec(w.shape) for w in w_list]
    in_specs += [
        pl.BlockSpec((tm, d), lambda i: (i, 0)),
        _mod_spec(d, 2, mod_row, tm),
        _const_spec((1, d)), _const_spec((1, d)),
        _mod_spec(d, 4, mod_row, tm),
        _mod_spec(d, 3, mod_row, tm),
    ]
    return pl.pallas_call(
        functools.partial(_outproj_ln_kernel, n_a=n_a, alpha=alpha),
        grid=(t // tm,),
        in_specs=in_specs,
        out_specs=[pl.BlockSpec((tm, d), lambda i: (i, 0)), pl.BlockSpec((tm, d // 2), lambda i: (i, 0))],
        out_shape=[jax.ShapeDtypeStruct((t, d), F32), jax.ShapeDtypeStruct((t, d // 2), jnp.int32)],
        compiler_params=_params(("parallel",)),
        name="outproj_ln",
    )(*a_list, *w_list, x, mod, ln_g, ln_b, mod, mod)


def _router_kernel(t_ref, rw_ref, rb_ref, tri_ref, idx_ref, gw_ref, rank_ref, cnt_ref):
    @pl.when(pl.program_id(0) == 0)
    def _():
        cnt_ref[...] = jnp.zeros_like(cnt_ref)

    lo, hi = _unpack_bf16_pairs(t_ref[...])
    half = lo.shape[1]
    nt = (((1,), (1,)), ((), ()))
    logits = (lax.dot_general(rw_ref[:, :half], lo, nt, preferred_element_type=F32)
              + lax.dot_general(rw_ref[:, half:], hi, nt, preferred_element_type=F32))
    scores = 1.0 / (1.0 + jnp.exp(-logits))
    sel = scores + rb_ref[...]
    gsz = N_EXPERTS // N_GROUPS
    neg = -jnp.inf
    sub = lax.broadcasted_iota(jnp.int32, (gsz, sel.shape[1]), 0)
    slabs = [sel[g * gsz:(g + 1) * gsz, :] for g in range(N_GROUPS)]
    gscore = []
    for s in slabs:
        m1 = jnp.max(s, axis=0, keepdims=True)
        a1 = jnp.min(jnp.where(s == m1, sub, gsz), axis=0, keepdims=True)
        m2 = jnp.max(jnp.where(sub == a1, neg, s), axis=0, keepdims=True)
        gscore.append(m1 + m2)
    masked = []
    for g in range(N_GROUPS):
        ahead = jnp.zeros(gscore[g].shape, jnp.int32)
        for h in range(N_GROUPS):
            if h == g:
                continue
            beats = gscore[h] >= gscore[g] if h < g else gscore[h] > gscore[g]
            ahead = ahead + beats.astype(jnp.int32)
        masked.append(jnp.where(ahead < TOPK_GROUPS, slabs[g], neg))
    cur = jnp.concatenate(masked, axis=0)
    eio = lax.broadcasted_iota(jnp.int32, cur.shape, 0)
    picks, weights = [], []
    for _ in range(TOP_K):
        m = jnp.max(cur, axis=0, keepdims=True)
        a = jnp.min(jnp.where(cur == m, eio, N_EXPERTS), axis=0, keepdims=True)
        hit = eio == a
        picks.append(a)
        weights.append(jnp.sum(jnp.where(hit, scores, 0.0), axis=0, keepdims=True))
        cur = jnp.where(hit, neg, cur)
    total = weights[0]
    for w in weights[1:]:
        total = total + w
    for k in range(TOP_K):
        idx_ref[k:k + 1, :] = picks[k]
        gw_ref[k:k + 1, :] = weights[k] / total * ROUTED_SCALE
    for k in range(TOP_K, SUBLANES):
        idx_ref[k:k + 1, :] = jnp.zeros_like(picks[0])
        gw_ref[k:k + 1, :] = jnp.zeros_like(weights[0])
        rank_ref[k:k + 1, :] = jnp.zeros_like(picks[0])
    base = cnt_ref[:, 0:1]
    for k in range(TOP_K):
        onehot = jnp.where(eio == picks[k], 1.0, 0.0)
        before = jnp.dot(onehot.astype(BF16), tri_ref[...], preferred_element_type=F32)
        rank_ref[k:k + 1, :] = jnp.sum(onehot * (before + base), axis=0, keepdims=True).astype(jnp.int32)
        base = base + jnp.sum(onehot, axis=1, keepdims=True)
    cnt_ref[...] = jnp.broadcast_to(base, cnt_ref.shape)


def _router(tok, rw_t, rb, *, t, tt):
    half = tok.shape[1]
    tri = jnp.asarray(np.arange(tt)[:, None] < np.arange(tt)[None, :], BF16)
    blk = pl.BlockSpec((SUBLANES, tt), lambda i: (0, i))
    return pl.pallas_call(
        _router_kernel,
        grid=(t // tt,),
        in_specs=[
            pl.BlockSpec((tt, half), lambda i: (i, 0)),
            _const_spec((N_EXPERTS, 2 * half)),
            _const_spec((N_EXPERTS, 1)),
            _const_spec((tt, tt)),
        ],
        out_specs=[blk, blk, blk, _const_spec((N_EXPERTS, LANES))],
        out_shape=[jax.ShapeDtypeStruct((SUBLANES, t), jnp.int32), jax.ShapeDtypeStruct((SUBLANES, t), F32),
                   jax.ShapeDtypeStruct((SUBLANES, t), jnp.int32), jax.ShapeDtypeStruct((N_EXPERTS, LANES), F32)],
        compiler_params=_params(("arbitrary",)),
        name="router",
    )(tok, rw_t, rb, tri)


def _slots_kernel(idx_ref, rank_ref, start_ref, pos_ref):
    start = start_ref[...]
    eio = lax.broadcasted_iota(jnp.int32, (N_EXPERTS, idx_ref.shape[1]), 0)
    for k in range(TOP_K):
        seg = jnp.sum(jnp.where(eio == idx_ref[k:k + 1, :], start, 0.0), axis=0, keepdims=True)
        pos_ref[k:k + 1, :] = rank_ref[k:k + 1, :] + seg.astype(jnp.int32)
    for k in range(TOP_K, SUBLANES):
        pos_ref[k:k + 1, :] = jnp.zeros((1, idx_ref.shape[1]), jnp.int32)


def _assign_slots(idx, rank, seg_start, *, t, tt):
    blk = pl.BlockSpec((SUBLANES, tt), lambda i: (0, i))
    return pl.pallas_call(
        _slots_kernel,
        grid=(t // tt,),
        in_specs=[blk, blk, _const_spec((N_EXPERTS, 1))],
        out_specs=blk,
        out_shape=jax.ShapeDtypeStruct((SUBLANES, t), jnp.int32),
        compiler_params=_params(("parallel",)),
        name="assign_slots",
    )(idx, rank, seg_start)


def _sc_gather_rows(table, idx):
    n = idx.shape[0]
    d = table.shape[1]
    n_workers = V7X_SC_CORES * V7X_SC_SUBCORES
    per_w = n // n_workers
    n_chunks = per_w // SC_GATHER_ROWS
    assert per_w * n_workers == n and n_chunks * SC_GATHER_ROWS == per_w
    n_buf = next(b for b in range(SC_GATHER_BUFFERS, 1, -1) if n_chunks % b == 0)
    mesh = plsc.VectorSubcoreMesh(core_axis_name="c", subcore_axis_name="s", num_cores=V7X_SC_CORES,
                                  num_subcores=V7X_SC_SUBCORES)

    @functools.partial(
        pl.kernel,
        out_type=jax.ShapeDtypeStruct((n, d), table.dtype),
        mesh=mesh,
        scratch_types=[
            pltpu.VMEM((per_w,), jnp.int32),
            pltpu.VMEM((n_buf, SC_GATHER_ROWS, d), table.dtype),
            pltpu.SemaphoreType.DMA((n_buf,)),
            pltpu.SemaphoreType.DMA((n_buf,)),
        ],
        name="sc_gather_rows",
    )
    def gather(table_hbm, idx_hbm, out_hbm, idx_v, rows_v, gsem, wsem):
        wid = lax.axis_index("s") * V7X_SC_CORES + lax.axis_index("c")
        base = wid * per_w
        pltpu.sync_copy(idx_hbm.at[pl.ds(base, per_w)], idx_v)

        def gather_copy(c, b):
            return pltpu.make_async_copy(table_hbm.at[idx_v.at[pl.ds(c * SC_GATHER_ROWS, SC_GATHER_ROWS)]],
                                         rows_v.at[b], gsem.at[b])

        def write_copy(c, b):
            return pltpu.make_async_copy(rows_v.at[b], out_hbm.at[pl.ds(base + c * SC_GATHER_ROWS, SC_GATHER_ROWS)],
                                         wsem.at[b])

        for b in range(n_buf - 1):
            gather_copy(b, b).start()

        @pl.loop(0, n_chunks, step=n_buf)
        def _(g):
            for b in range(n_buf):
                c = g + b
                prev = (b + n_buf - 1) % n_buf
                gather_copy(c, b).wait()
                write_copy(c, b).start()

                @pl.when(c >= 1)
                def _():
                    write_copy(c - 1, prev).wait()

                @pl.when(c + n_buf - 1 < n_chunks)
                def _():
                    gather_copy(c + n_buf - 1, prev).start()

        write_copy(n_chunks - 1, (n_chunks - 1) % n_buf).wait()

    return gather(table, idx)


def _experts_kernel(be_ref, nbu_ref, x_ref, wg_ref, wu_ref, wd_ref, *rest, block_off):
    y_ref, wgb, wub, wdb = rest[-4:]
    step = pl.program_id(0)
    b = block_off + step
    nbu = nbu_ref[0]

    @pl.when(b < nbu)
    def _():
        changed = jnp.logical_or(step == 0, be_ref[b] != be_ref[jnp.maximum(b - 1, 0)])

        @pl.when(changed)
        def _():
            wgb[...] = wg_ref[...].astype(BF16)
            wub[...] = wu_ref[...].astype(BF16)
            wdb[...] = wd_ref[...].astype(BF16)

        lo, hi = _unpack_bf16_pairs(x_ref[...])
        hg = _dot_halves(lo, hi, wgb)
        hu = _dot_halves(lo, hi, wub)
        h = hg * (1.0 / (1.0 + jnp.exp(-hg))) * hu
        y_ref[...] = _pack_bf16_pairs(jnp.dot(h.astype(BF16), wdb[...], preferred_element_type=F32))

    @pl.when(b >= nbu)
    def _():
        y_ref[...] = jnp.zeros_like(y_ref)


def _experts(xs, block_e, nb_used, wg, wu, wd, layer, y_prev, *, block_off, n_blocks):
    half = xs.shape[1]
    d = 2 * half
    ff = wg.shape[3]
    n_call = xs.shape[0] // MOE_ROWS

    def used(b, nbu):
        return jnp.clip(jnp.minimum(block_off + b, nbu[0] - 1) - block_off, 0, n_call - 1)

    def expert(b, be):
        return be[block_off + b]

    in_specs = [
        pl.BlockSpec((MOE_ROWS, half), lambda b, be, nbu: (used(b, nbu), 0)),
        pl.BlockSpec((None, None, d, ff), lambda b, be, nbu: (layer, expert(b, be), 0, 0)),
        pl.BlockSpec((None, None, d, ff), lambda b, be, nbu: (layer, expert(b, be), 0, 0)),
        pl.BlockSpec((None, None, ff, d), lambda b, be, nbu: (layer, expert(b, be), 0, 0)),
    ]
    args = [block_e, nb_used, xs, wg, wu, wd]
    aliases = {}
    if y_prev is not None:
        in_specs.append(pl.BlockSpec(memory_space=pl.ANY))
        aliases = {len(args): 0}
        args.append(y_prev)
    grid_spec = pltpu.PrefetchScalarGridSpec(
        num_scalar_prefetch=2,
        grid=(n_call,),
        in_specs=in_specs,
        out_specs=pl.BlockSpec((MOE_ROWS, half), lambda b, be, nbu: (block_off + b, 0)),
        scratch_shapes=[
            pltpu.VMEM((d, ff), BF16),
            pltpu.VMEM((d, ff), BF16),
            pltpu.VMEM((ff, d), BF16),
        ],
    )
    return pl.pallas_call(
        functools.partial(_experts_kernel, block_off=block_off),
        grid_spec=grid_spec,
        out_shape=jax.ShapeDtypeStruct((n_blocks * MOE_ROWS, half), jnp.int32),
        input_output_aliases=aliases,
        compiler_params=_params(("arbitrary",)),
        name="experts",
    )(*args)


def _combine_ln_kernel(*refs, alpha, emit_next, n_prev):
    y_ref, gw_ref, tok_ref, x_ref, sg_ref, su_ref, sd_ref, gate_ref, lng_ref, lnb_ref = refs[:10]
    outs = refs[len(refs) - (2 if emit_next else 1):]
    if emit_next:
        sc_ref, sh_ref = refs[10:12]
        xo_ref, u_ref = outs
    else:
        (xo_ref,) = outs
    lo, hi = _unpack_bf16_pairs(tok_ref[...])
    hg = _dot_halves(lo, hi, sg_ref)
    hu = _dot_halves(lo, hi, su_ref)
    h = hg * (1.0 / (1.0 + jnp.exp(-hg))) * hu
    gw = gw_ref[...]
    f_lo = f_hi = None
    for k in range(TOP_K):
        y_lo, y_hi = _unpack_bf16_pairs(y_ref[k])
        w = gw[:, k:k + 1]
        f_lo = y_lo.astype(F32) * w if f_lo is None else f_lo + y_lo.astype(F32) * w
        f_hi = y_hi.astype(F32) * w if f_hi is None else f_hi + y_hi.astype(F32) * w
    f = jnp.concatenate([f_lo, f_hi], axis=-1) + jnp.dot(h.astype(BF16), sd_ref[...], preferred_element_type=F32)
    xn = _layer_norm(alpha * x_ref[...] + gate_ref[...] * f, lng_ref[...], lnb_ref[...])
    xo_ref[...] = xn
    if emit_next:
        u_ref[...] = (xn * (1.0 + sc_ref[...]) + sh_ref[...]).astype(u_ref.dtype)


def _combine_ln(y3, gw_t, tok, x, sg, su, sd, mod, ln_g, ln_b, mod_next, prev, *, t, row_off, tm, alpha, mod_row):
    d = x.shape[1]
    emit_next = mod_next is not None
    off = row_off // tm
    assert off * tm == row_off

    def rows(i):
        return (off + i, 0)

    def part_mod_row(r):
        return mod_row(r + row_off)

    in_specs = [
        pl.BlockSpec((TOP_K, tm, d // 2), lambda i: (0, i, 0)),
        pl.BlockSpec((tm, SUBLANES), rows),
        pl.BlockSpec((tm, d // 2), rows),
        pl.BlockSpec((tm, d), rows),
        _const_spec(sg.shape), _const_spec(su.shape), _const_spec(sd.shape),
        _mod_spec(d, 5, part_mod_row, tm),
        _const_spec((1, d)), _const_spec((1, d)),
    ]
    args = [y3, gw_t, tok, x, sg, su, sd, mod, ln_g, ln_b]
    out_specs = [pl.BlockSpec((tm, d), rows)]
    out_shape = [jax.ShapeDtypeStruct((t, d), F32)]
    if emit_next:
        in_specs += [_mod_spec(d, 1, part_mod_row, tm), _mod_spec(d, 0, part_mod_row, tm)]
        args += [mod_next, mod_next]
        out_specs.append(pl.BlockSpec((tm, d), rows))
        out_shape.append(jax.ShapeDtypeStruct((t, d), BF16))
    aliases = {}
    if prev is not None:
        for k, p in enumerate(prev):
            in_specs.append(pl.BlockSpec(memory_space=pl.ANY))
            aliases[len(args)] = k
            args.append(p)
    return pl.pallas_call(
        functools.partial(_combine_ln_kernel, alpha=alpha, emit_next=emit_next, n_prev=len(aliases)),
        grid=(y3.shape[1] // tm,),
        in_specs=in_specs,
        out_specs=out_specs,
        out_shape=out_shape,
        input_output_aliases=aliases,
        compiler_params=_params(("parallel",)),
        name="combine_ln",
    )(*args)


def _moe(tok, x, t, layer, router_w, router_b, wg, wu, wd, sg, su, sd, mod, ln_g, ln_b, mod_next, *, alpha, mod_row,
         tm):
    half = tok.shape[1]
    tt = 512
    idx, gw, rank, cnt = _router(tok, router_w.T.astype(BF16), router_b.reshape(N_EXPERTS, 1), t=t, tt=tt)
    n_asg = t * TOP_K
    counts = cnt[:, 0].astype(jnp.int32)
    padded = (counts + MOE_ROWS - 1) // MOE_ROWS * MOE_ROWS
    pend = jnp.cumsum(padded)
    pstart = pend - padded
    sc_rows = V7X_SC_CORES * V7X_SC_SUBCORES * SC_GATHER_ROWS * 2
    blocks_granule = MOE_DISPATCH_PARTS * max(sc_rows // MOE_ROWS, 1)
    assert (blocks_granule // MOE_DISPATCH_PARTS * MOE_ROWS) % sc_rows == 0
    n_blocks = -(-((n_asg + N_EXPERTS * (MOE_ROWS - 1)) // MOE_ROWS + 1) // blocks_granule) * blocks_granule
    assert n_asg % sc_rows == 0
    block_start = jnp.arange(n_blocks, dtype=jnp.int32) * MOE_ROWS
    block_e = jnp.minimum(jnp.sum((pend[None, :] <= block_start[:, None]).astype(jnp.int32), axis=1), N_EXPERTS - 1)
    nb_used = (pend[-1] // MOE_ROWS).astype(jnp.int32).reshape(1)
    pos2d = _assign_slots(idx, rank, pstart.astype(F32).reshape(N_EXPERTS, 1), t=t, tt=tt)[:TOP_K]
    pos = pos2d.reshape(-1)
    tok_of_asg = np.tile(np.arange(t, dtype=np.int32), TOP_K)
    n_pad = n_blocks * MOE_ROWS - n_asg
    seg_pad_end = jnp.cumsum(padded - counts)
    j = jnp.arange(n_pad, dtype=jnp.int32)
    pad_e = jnp.sum((seg_pad_end[None, :] <= j[:, None]).astype(jnp.int32), axis=1)
    seg_base = pstart + counts - (seg_pad_end - (padded - counts))
    in_seg = j + jnp.sum(jnp.where(pad_e[:, None] == jnp.arange(N_EXPERTS)[None, :], seg_base[None, :], 0), axis=1)
    pad_slot = jnp.where(pad_e < N_EXPERTS, in_seg, pend[-1] + j - seg_pad_end[-1])
    _, slot_tok = lax.sort((jnp.concatenate([pos, pad_slot]), jnp.concatenate([tok_of_asg, pad_slot % t])),
                           num_keys=1)
    per = n_blocks // MOE_DISPATCH_PARTS * MOE_ROWS
    xs = [_sc_gather_rows(tok, slot_tok[i * per:(i + 1) * per]) for i in range(MOE_DISPATCH_PARTS)]
    y = None
    for i in range(MOE_DISPATCH_PARTS):
        y = _experts(xs[i], block_e, nb_used, wg, wu, wd, layer, y, block_off=i * per // MOE_ROWS, n_blocks=n_blocks)
    n_cparts = MOE_COMBINE_PARTS if (t // MOE_COMBINE_PARTS * TOP_K) % sc_rows == 0 else 1
    t_part = t // n_cparts
    y3 = [_sc_gather_rows(y, pos2d[:, i * t_part:(i + 1) * t_part].reshape(-1)).reshape(TOP_K, t_part, half)
          for i in range(n_cparts)]
    outs = None
    gw_t = gw.T
    for i in range(n_cparts):
        outs = _combine_ln(y3[i], gw_t, tok, x, sg, su, sd, mod, ln_g, ln_b, mod_next, outs, t=t,
                           row_off=i * t_part, tm=tm, alpha=alpha, mod_row=mod_row)
    return outs


def _rope64(r, c_ref, sa_ref, sb_ref):
    return r * c_ref[...] + pltpu.roll(r, LANES - QK_ROPE // 2, 1) * sa_ref[...] + pltpu.roll(r, QK_ROPE // 2, 1) * sb_ref[...]


def _mla_q_kernel(d_ref, gain_ref, w_ref, c_ref, sa_ref, sb_ref, q_ref, *, scale):
    n = _rms(d_ref[...], gain_ref[...]).astype(BF16)
    q = jnp.dot(n, w_ref[...], preferred_element_type=F32)
    for h in range(MLA_HEADS):
        lo = h * MLA_DK_PAD
        q_ref[:, lo:lo + QK_NOPE] = (q[:, lo:lo + QK_NOPE] * scale).astype(q_ref.dtype)
        r = _rope64(q[:, lo + QK_NOPE:lo + MLA_DK_PAD], c_ref, sa_ref, sb_ref)
        q_ref[:, lo + QK_NOPE:lo + MLA_DK_PAD] = (r * scale).astype(q_ref.dtype)


def _mla_kv_kernel(ckv_ref, kr_ref, gain_ref, wk_ref, wv_ref, c_ref, sa_ref, sb_ref, k_ref, v_ref):
    n = _rms(ckv_ref[...], gain_ref[...]).astype(BF16)
    kn = jnp.dot(n, wk_ref[...], preferred_element_type=F32)
    v_ref[...] = jnp.dot(n, wv_ref[...], preferred_element_type=F32).astype(v_ref.dtype)
    kr = _rope64(kr_ref[...], c_ref, sa_ref, sb_ref).astype(k_ref.dtype)
    for h in range(MLA_HEADS):
        lo = h * MLA_DK_PAD
        k_ref[:, lo:lo + QK_NOPE] = kn[:, h * QK_NOPE:(h + 1) * QK_NOPE].astype(k_ref.dtype)
        k_ref[:, lo + QK_NOPE:lo + MLA_DK_PAD] = kr


def _axial_angles(n_tok, rot_dim):
    rows = n_tok // GRID_W
    n_freq = rot_dim // 4
    inv = (ROPE_THETA ** (-np.arange(n_freq, dtype=np.float32) / n_freq)).astype(np.float32)
    row = np.repeat(np.arange(rows, dtype=np.float32), GRID_W)
    col = np.tile(np.arange(GRID_W, dtype=np.float32), rows)
    return np.concatenate([row[:, None] * inv, col[:, None] * inv], axis=-1)


def _rope_tables_128(n_tok, ident_rows):
    ang = _axial_angles(n_tok, HEAD_DIM)
    cos, sin = np.cos(ang), np.sin(ang)
    c = np.concatenate([cos, cos], axis=-1)
    s = np.concatenate([-sin, sin], axis=-1)
    c = np.concatenate([c, np.ones((ident_rows, HEAD_DIM), np.float32)], axis=0)
    s = np.concatenate([s, np.zeros((ident_rows, HEAD_DIM), np.float32)], axis=0)
    return c.astype(np.float32), s.astype(np.float32)


def _rope_tables_64(n_tok, ident_rows):
    ang = _axial_angles(n_tok, QK_ROPE)
    cos, sin = np.cos(ang), np.sin(ang)
    half = QK_ROPE // 2
    z = np.zeros((n_tok, LANES - QK_ROPE), np.float32)
    zh = np.zeros((n_tok, half), np.float32)
    c = np.concatenate([cos, cos, z], axis=-1)
    sa = np.concatenate([-sin, zh, z], axis=-1)
    sb = np.concatenate([zh, sin, z], axis=-1)
    ci = np.concatenate([np.ones((ident_rows, QK_ROPE), np.float32),
                         np.zeros((ident_rows, LANES - QK_ROPE), np.float32)], axis=-1)
    zi = np.zeros((ident_rows, LANES), np.float32)
    tables = np.concatenate([c, ci], 0), np.concatenate([sa, zi], 0), np.concatenate([sb, zi], 0)
    return tuple(tab.astype(np.float32) for tab in tables)


def kernel(x, c, ctx, c_ctx, w_ada, b_ada, ln_g, ln_b, a_w_in, a_conv_w, a_q_gain, a_k_gain, a_w_out, m_w_down, m_q_gain, m_kv_gain, m_w_uq, m_w_ukv, m_w_out, router_w, router_b, e_w_gate, e_w_up, e_w_down, s_w_gate, s_w_up, s_w_down):
    batch, seq, d = x.shape
    ctx_len = ctx.shape[1]
    depth = w_ada.shape[0]
    assert depth == 2, "one conv+GQA layer followed by one MLA layer"
    alpha = (2 * depth) ** 0.25
    t_lat = batch * seq
    t_ctx = batch * ctx_len
    t_all = t_lat + t_ctx
    tr = 256
    assert seq % tr == 0 and ctx_len % tr == 0 and seq % GRID_W == 0
    lat_tiles = t_lat // tr
    lat_seq_tiles = seq // tr
    ctx_seq_tiles = ctx_len // tr
    lk = ctx_len + seq

    def mod_row(r):
        return jnp.minimum(r // seq, batch)

    def kv_block(i):
        is_lat = i < lat_tiles
        cidx = i - lat_tiles
        b = jnp.where(is_lat, i // lat_seq_tiles, cidx // ctx_seq_tiles)
        rb = jnp.where(is_lat, ctx_seq_tiles + i % lat_seq_tiles, cidx % ctx_seq_tiles)
        return b, rb

    def pos_block(i):
        return jnp.where(i < lat_tiles, i % lat_seq_tiles, lat_seq_tiles)

    rows = -(-(batch + 1) // SUBLANES) * SUBLANES
    cond = jnp.concatenate([c, c_ctx[None, :], jnp.zeros((rows - batch - 1, d), F32)], axis=0)
    mod = _ada_table(cond, w_ada, b_ada).reshape(depth, rows, 1, 6 * d)

    x_all = jnp.concatenate([x.reshape(t_lat, d), ctx.reshape(t_ctx, d)], axis=0)

    tm_in = 1024 if (t_all % 1024 == 0 and seq % 1024 == 0) else tr
    proj = _mod_matmul(x_all, mod[0], a_w_in[0].astype(BF16), BF16, tm_in, 768, mod_row)

    cos128, sin128 = _rope_tables_128(seq, tr)
    d_q = ATT_HEADS * HEAD_DIM
    d_kv = ATT_KV_HEADS * HEAD_DIM
    qkv_w = d_q + 2 * d_kv
    qkv_blk = 3 * CONV_DIM // qkv_w
    assert qkv_blk * qkv_w == 3 * CONV_DIM
    q0, k0, v0 = pl.pallas_call(
        functools.partial(_qkprep_kernel, scale=1.0 / math.sqrt(HEAD_DIM)),
        grid=(t_all // tr,),
        in_specs=[
            pl.BlockSpec((tr, qkv_w), lambda i: (i, qkv_blk)),
            pl.BlockSpec((tr, HEAD_DIM), lambda i: (pos_block(i), 0)),
            pl.BlockSpec((tr, HEAD_DIM), lambda i: (pos_block(i), 0)),
            _const_spec((1, HEAD_DIM)), _const_spec((1, HEAD_DIM)),
        ],
        out_specs=[
            pl.BlockSpec((tr, d_q), lambda i: (i, 0)),
            pl.BlockSpec((None, tr, d_kv), lambda i: (*kv_block(i), 0)),
            pl.BlockSpec((None, tr, d_kv), lambda i: (*kv_block(i), 0)),
        ],
        out_shape=[
            jax.ShapeDtypeStruct((t_all, d_q), BF16),
            jax.ShapeDtypeStruct((batch, lk, d_kv), BF16),
            jax.ShapeDtypeStruct((batch, lk, d_kv), BF16),
        ],
        compiler_params=_params(("parallel",)),
        name="qk_prep",
    )(proj, cos128, sin128, a_q_gain[0].reshape(1, HEAD_DIM), a_k_gain[0].reshape(1, HEAD_DIM))

    grp = ATT_HEADS // ATT_KV_HEADS
    att_lat = _attention(q0, k0, v0, batch=batch, sq=seq, lk=lk, n_kv=ATT_KV_HEADS, group=grp, dk=HEAD_DIM,
                         dv=HEAD_DIM, tq=512 if seq % 512 == 0 else tr, rows=256, q_row_off=0)
    att_ctx = _attention(q0, k0, v0, batch=batch, sq=ctx_len, lk=ctx_len, n_kv=ATT_KV_HEADS, group=grp,
                         dk=HEAD_DIM, dv=HEAD_DIM, tq=256, rows=256, q_row_off=t_lat)
    att0 = jnp.concatenate([att_lat, att_ctx], axis=0)

    conv0 = _conv_gate(proj, a_conv_w[0], t=t_all, tm=tr, tc=512, lat_tiles=lat_tiles,
                       lat_seq_tiles=lat_seq_tiles, ctx_seq_tiles=ctx_seq_tiles)

    w_out0 = a_w_out[0].astype(BF16)
    x1, tok0 = _outproj_ln([conv0, att0], [w_out0[:CONV_DIM], w_out0[CONV_DIM:]], x_all, mod[0],
                           ln_g[0, 0].reshape(1, d), ln_b[0, 0].reshape(1, d), t=t_all, tm=tr, alpha=alpha,
                           mod_row=mod_row)

    x2, u1 = _moe(tok0, x1, t_all, 0, router_w[0], router_b[0], e_w_gate, e_w_up, e_w_down,
                  s_w_gate[0].astype(BF16), s_w_up[0].astype(BF16), s_w_down[0].astype(BF16), mod[0],
                  ln_g[0, 1].reshape(1, d), ln_b[0, 1].reshape(1, d), mod[1], alpha=alpha,
                  mod_row=mod_row, tm=256)

    n_down = Q_LORA + KV_LORA + QK_ROPE
    n_down_pad = -(-n_down // LANES) * LANES
    w_down = jnp.pad(m_w_down[0], ((0, 0), (0, n_down_pad - n_down))).astype(BF16)
    down = _matmul(u1, w_down, F32, 512, n_down_pad)

    dqk = QK_NOPE + QK_ROPE
    w_uq = m_w_uq[0].reshape(Q_LORA, MLA_HEADS, dqk)
    w_uq = jnp.pad(w_uq, ((0, 0), (0, 0), (0, MLA_DK_PAD - dqk))).reshape(Q_LORA, MLA_HEADS * MLA_DK_PAD).astype(BF16)
    w_ukv = m_w_ukv[0].reshape(KV_LORA, MLA_HEADS, QK_NOPE + V_DIM)
    w_uk = w_ukv[:, :, :QK_NOPE].reshape(KV_LORA, MLA_HEADS * QK_NOPE).astype(BF16)
    w_uv = w_ukv[:, :, QK_NOPE:].reshape(KV_LORA, MLA_HEADS * V_DIM).astype(BF16)

    c64, sa64, sb64 = _rope_tables_64(seq, tr)
    rope_specs = [pl.BlockSpec((tr, LANES), lambda i: (pos_block(i), 0))] * 3
    q1 = pl.pallas_call(
        functools.partial(_mla_q_kernel, scale=1.0 / math.sqrt(dqk)),
        grid=(lat_tiles,),
        in_specs=[
            pl.BlockSpec((tr, Q_LORA), lambda i: (i, 0)),
            _const_spec((1, Q_LORA)),
            _const_spec(w_uq.shape),
        ] + rope_specs,
        out_specs=pl.BlockSpec((tr, MLA_HEADS * MLA_DK_PAD), lambda i: (i, 0)),
        out_shape=jax.ShapeDtypeStruct((t_lat, MLA_HEADS * MLA_DK_PAD), BF16),
        compiler_params=_params(("parallel",)),
        name="mla_q",
    )(down, m_q_gain[0].reshape(1, Q_LORA), w_uq, c64, sa64, sb64)

    assert KV_LORA == Q_LORA and (Q_LORA + KV_LORA) % LANES == 0
    k1, v1 = pl.pallas_call(
        _mla_kv_kernel,
        grid=(t_all // tr,),
        in_specs=[
            pl.BlockSpec((tr, KV_LORA), lambda i: (i, 1)),
            pl.BlockSpec((tr, LANES), lambda i: (i, (Q_LORA + KV_LORA) // LANES)),
            _const_spec((1, KV_LORA)),
            _const_spec(w_uk.shape), _const_spec(w_uv.shape),
        ] + rope_specs,
        out_specs=[
            pl.BlockSpec((None, tr, MLA_HEADS * MLA_DK_PAD), lambda i: (*kv_block(i), 0)),
            pl.BlockSpec((None, tr, MLA_HEADS * V_DIM), lambda i: (*kv_block(i), 0)),
        ],
        out_shape=[
            jax.ShapeDtypeStruct((batch, lk, MLA_HEADS * MLA_DK_PAD), BF16),
            jax.ShapeDtypeStruct((batch, lk, MLA_HEADS * V_DIM), BF16),
        ],
        compiler_params=_params(("parallel",)),
        name="mla_kv",
    )(down, down, m_kv_gain[0].reshape(1, KV_LORA), w_uk, w_uv, c64, sa64, sb64)

    att1 = _attention(q1, k1, v1, batch=batch, sq=seq, lk=lk, n_kv=MLA_HEADS, group=1, dk=MLA_DK_PAD, dv=V_DIM,
                      tq=next(c for c in (2048, 1024, 512, tr) if seq % c == 0), rows=256, q_row_off=0)

    x3, tok1 = _outproj_ln([att1], [m_w_out[0].astype(BF16)], x2, mod[1], ln_g[1, 0].reshape(1, d),
                           ln_b[1, 0].reshape(1, d), t=t_lat, tm=tr, alpha=alpha, mod_row=mod_row)

    (x4,) = _moe(tok1, x3, t_lat, 1, router_w[1], router_b[1], e_w_gate, e_w_up, e_w_down,
                 s_w_gate[1].astype(BF16), s_w_up[1].astype(BF16), s_w_down[1].astype(BF16), mod[1],
                 ln_g[1, 1].reshape(1, d), ln_b[1, 1].reshape(1, d), None, alpha=alpha,
                 mod_row=mod_row, tm=256)
    return x4.reshape(batch, seq, d)
```

```python
import functools
import math

import jax
import jax.numpy as jnp
import numpy as np
from jax import lax
from jax.experimental import pallas as pl
from jax.experimental.pallas import tpu as pltpu
from jax.experimental.pallas import tpu_sc as plsc

F32 = jnp.float32
BF16 = jnp.bfloat16

GRID_W = 64
CONV_DIM = 1024
ATT_HEADS = 8
ATT_KV_HEADS = 2
HEAD_DIM = 128
MLA_HEADS = 16
Q_LORA = 512
KV_LORA = 512
QK_NOPE = 128
QK_ROPE = 64
V_DIM = 128
N_EXPERTS = 64
TOP_K = 6
N_GROUPS = 8
TOPK_GROUPS = 4
ROUTED_SCALE = 2.5
ROPE_THETA = 10000.0
LN_EPS = 1e-5
RMS_EPS = 1e-6

V7X_VMEM_LIMIT_BYTES = 56 * 1024 * 1024
LANES = 128
SUBLANES = 8
MOE_ROWS = 512
MOE_DISPATCH_PARTS = 4
MOE_COMBINE_PARTS = 2
V7X_SC_CORES = 2
V7X_SC_SUBCORES = 16
SC_GATHER_ROWS = 16
SC_GATHER_BUFFERS = 4
MLA_DK_PAD = 256


def _params(sem):
    return pltpu.CompilerParams(dimension_semantics=sem, vmem_limit_bytes=V7X_VMEM_LIMIT_BYTES)


def _const_spec(shape):
    nd = len(shape)
    return pl.BlockSpec(shape, lambda *_: (0,) * nd)


def _ada_kernel(s_ref, w_ref, b_ref, o_ref):
    s = s_ref[...]
    s = s * (1.0 / (1.0 + jnp.exp(-s)))
    o_ref[...] = jnp.dot(s.astype(BF16), w_ref[...].astype(BF16), preferred_element_type=F32) + b_ref[...]


def _ada_table(cond, w_ada, b_ada):
    depth, d, n = w_ada.shape
    r = cond.shape[0]
    tn = 1024
    return pl.pallas_call(
        _ada_kernel,
        grid=(depth, n // tn),
        in_specs=[
            pl.BlockSpec((r, d), lambda l, j: (0, 0)),
            pl.BlockSpec((None, d, tn), lambda l, j: (l, 0, j)),
            pl.BlockSpec((None, 1, tn), lambda l, j: (l, 0, j)),
        ],
        out_specs=pl.BlockSpec((None, r, tn), lambda l, j: (l, 0, j)),
        out_shape=jax.ShapeDtypeStruct((depth, r, n), F32),
        compiler_params=_params(("parallel", "parallel")),
        name="ada_table",
    )(cond, w_ada, b_ada.reshape(depth, 1, n))


def _mod_spec(d, chunk, mod_row, tm):
    return pl.BlockSpec((None, 1, d), lambda i: (mod_row(i * tm), 0, chunk))


def _mm_kernel(a_ref, w_ref, o_ref):
    o_ref[...] = jnp.dot(a_ref[...], w_ref[...], preferred_element_type=F32).astype(o_ref.dtype)


def _matmul(a, w, out_dtype, tm, tn):
    m, k = a.shape
    n = w.shape[1]
    return pl.pallas_call(
        _mm_kernel,
        grid=(m // tm, n // tn),
        in_specs=[
            pl.BlockSpec((tm, k), lambda i, j: (i, 0)),
            pl.BlockSpec((k, tn), lambda i, j: (0, j)),
        ],
        out_specs=pl.BlockSpec((tm, tn), lambda i, j: (i, j)),
        out_shape=jax.ShapeDtypeStruct((m, n), out_dtype),
        compiler_params=_params(("parallel", "parallel")),
        name="matmul",
    )(a, w)


def _pair_specs(pair, tm):
    lat, ctx = pair
    lat_tiles = lat.shape[0] // tm
    assert lat_tiles * tm == lat.shape[0] and ctx.shape[0] % tm == 0 and lat.shape[1] == ctx.shape[1]
    width = lat.shape[1]
    return [pl.BlockSpec((tm, width), lambda i, *_: (jnp.minimum(i, lat_tiles - 1), 0)),
            pl.BlockSpec((tm, width), lambda i, *_: (jnp.maximum(i - lat_tiles, 0), 0))]


def _pair_tile(lat_ref, ctx_ref, lat_tiles):
    return jnp.where(pl.program_id(0) < lat_tiles, lat_ref[...], ctx_ref[...])


def _mod_mm_kernel(xl_ref, xc_ref, sc_ref, sh_ref, w_ref, o_ref, u_ref, *, lat_tiles):
    @pl.when(pl.program_id(1) == 0)
    def _():
        x = _pair_tile(xl_ref, xc_ref, lat_tiles)
        u_ref[...] = (x * (1.0 + sc_ref[...]) + sh_ref[...]).astype(u_ref.dtype)

    o_ref[...] = jnp.dot(u_ref[...], w_ref[...], preferred_element_type=F32).astype(o_ref.dtype)


def _mod_matmul(x_pair, mod, w, out_dtype, tm, tn, mod_row):
    m = x_pair[0].shape[0] + x_pair[1].shape[0]
    k = x_pair[0].shape[1]
    n = w.shape[1]

    def mod_spec(chunk):
        return pl.BlockSpec((None, 1, k), lambda i, j: (mod_row(i * tm), 0, chunk))

    return pl.pallas_call(
        functools.partial(_mod_mm_kernel, lat_tiles=x_pair[0].shape[0] // tm),
        grid=(m // tm, n // tn),
        in_specs=_pair_specs(x_pair, tm) + [
            mod_spec(1), mod_spec(0),
            pl.BlockSpec((k, tn), lambda i, j: (0, j)),
        ],
        out_specs=pl.BlockSpec((tm, tn), lambda i, j: (i, j)),
        out_shape=jax.ShapeDtypeStruct((m, n), out_dtype),
        scratch_shapes=[pltpu.VMEM((tm, k), w.dtype)],
        compiler_params=_params(("parallel", "arbitrary")),
        name="mod_matmul",
    )(*x_pair, mod, mod, w)


def _rms(t, gain):
    return t * lax.rsqrt(jnp.mean(t * t, axis=-1, keepdims=True) + RMS_EPS) * gain


def _qkprep_kernel(p_ref, cos_ref, sin_ref, qg_ref, kg_ref, q_ref, k_ref, v_ref, *, scale):
    cos = cos_ref[...]
    sin = sin_ref[...]

    def norm_rope(t, gain):
        y = _rms(t.astype(F32), gain)
        return y * cos + pltpu.roll(y, HEAD_DIM // 2, 1) * sin

    for h in range(ATT_HEADS):
        sl = slice(h * HEAD_DIM, (h + 1) * HEAD_DIM)
        q_ref[:, sl] = (norm_rope(p_ref[:, sl], qg_ref[...]) * scale).astype(q_ref.dtype)
    k0 = ATT_HEADS * HEAD_DIM
    for h in range(ATT_KV_HEADS):
        sl = slice(h * HEAD_DIM, (h + 1) * HEAD_DIM)
        k_ref[:, sl] = norm_rope(p_ref[:, k0 + h * HEAD_DIM:k0 + (h + 1) * HEAD_DIM], kg_ref[...]).astype(k_ref.dtype)
    v0 = k0 + ATT_KV_HEADS * HEAD_DIM
    v_ref[...] = p_ref[:, v0:v0 + ATT_KV_HEADS * HEAD_DIM].astype(v_ref.dtype)


def _attn_kernel(q_ref, k_ref, v_ref, o_ref, *, group, tq, rows, dk, dv):
    k = k_ref[...]
    v = v_ref[...]
    for h in range(group):
        for r in range(0, tq, rows):
            q = q_ref[r:r + rows, h * dk:(h + 1) * dk]
            s = lax.dot_general(q, k, (((1,), (1,)), ((), ())), preferred_element_type=F32)
            m = jnp.max(s, axis=-1, keepdims=True)
            p = jnp.exp(s - m)
            l = jnp.sum(p, axis=-1, keepdims=True)
            o = jnp.dot(p.astype(v.dtype), v, preferred_element_type=F32)
            o_ref[r:r + rows, h * dv:(h + 1) * dv] = (o / l).astype(o_ref.dtype)


def _attention(q, k, v, *, batch, sq, lk, n_kv, group, dk, dv, tq, rows, q_row_off):
    nq = sq // tq
    off = q_row_off // tq
    assert tq % rows == 0 and q_row_off % tq == 0 and sq % tq == 0
    return pl.pallas_call(
        functools.partial(_attn_kernel, group=group, tq=tq, rows=rows, dk=dk, dv=dv),
        grid=(batch, n_kv, nq),
        in_specs=[
            pl.BlockSpec((tq, group * dk), lambda b, g, i: (off + b * nq + i, g)),
            pl.BlockSpec((None, lk, dk), lambda b, g, i: (b, 0, g)),
            pl.BlockSpec((None, lk, dv), lambda b, g, i: (b, 0, g)),
        ],
        out_specs=pl.BlockSpec((tq, group * dv), lambda b, g, i: (b * nq + i, g)),
        out_shape=jax.ShapeDtypeStruct((batch * sq, n_kv * group * dv), BF16),
        compiler_params=_params(("parallel", "parallel", "parallel")),
        name="attention",
    )(q, k, v)


def _conv_kernel(gb_ref, gc_ref, hv_ref, gcp_ref, hvp_ref, gcn_ref, hvn_ref, w_ref, o_ref, *,
                 tm, lat_tiles, lat_seq_tiles, ctx_seq_tiles):
    i = pl.program_id(0)
    is_lat = i < lat_tiles
    pos = jnp.where(is_lat, i % lat_seq_tiles, (i - lat_tiles) % ctx_seq_tiles)
    seq_tiles = jnp.where(is_lat, lat_seq_tiles, ctx_seq_tiles)
    not_first = (pos != 0).astype(F32)
    not_last = (pos != seq_tiles - 1).astype(F32)
    p = gc_ref[...].astype(F32) * hv_ref[...].astype(F32)
    halo_prev = gcp_ref[SUBLANES - 1:SUBLANES, :].astype(F32) * hvp_ref[SUBLANES - 1:SUBLANES, :].astype(F32) * not_first
    halo_next = gcn_ref[0:1, :].astype(F32) * hvn_ref[0:1, :].astype(F32) * not_last
    row = lax.broadcasted_iota(jnp.int32, p.shape, 0)
    prev = jnp.where(row == 0, halo_prev, pltpu.roll(p, 1, 0))
    nxt = jnp.where(row == tm - 1, halo_next, pltpu.roll(p, tm - 1, 0))
    w = w_ref[...]
    conv = w[0:1, :] * prev + w[1:2, :] * p + w[2:3, :] * nxt
    o_ref[...] = (gb_ref[...].astype(F32) * conv).astype(o_ref.dtype)


def _conv_gate(p, conv_w, *, t, tm, tc, lat_tiles, lat_seq_tiles, ctx_seq_tiles):
    nct = CONV_DIM // tc
    hb = tm // SUBLANES
    n_halo = t // SUBLANES

    def cur(part):
        return pl.BlockSpec((tm, tc), lambda i, j: (i, part * nct + j))

    def prev(part):
        return pl.BlockSpec((SUBLANES, tc), lambda i, j: (jnp.maximum(i * hb - 1, 0), part * nct + j))

    def nxt(part):
        return pl.BlockSpec((SUBLANES, tc), lambda i, j: (jnp.minimum((i + 1) * hb, n_halo - 1), part * nct + j))

    return pl.pallas_call(
        functools.partial(_conv_kernel, tm=tm, lat_tiles=lat_tiles, lat_seq_tiles=lat_seq_tiles,
                          ctx_seq_tiles=ctx_seq_tiles),
        grid=(t // tm, nct),
        in_specs=[cur(0), cur(1), cur(2), prev(1), prev(2), nxt(1), nxt(2),
                  pl.BlockSpec((3, tc), lambda i, j: (0, j))],
        out_specs=pl.BlockSpec((tm, tc), lambda i, j: (i, j)),
        out_shape=jax.ShapeDtypeStruct((t, CONV_DIM), BF16),
        compiler_params=_params(("parallel", "parallel")),
        name="conv_gate",
    )(p, p, p, p, p, p, p, conv_w)


def _layer_norm(z, g, b):
    mu = jnp.mean(z, axis=-1, keepdims=True)
    zc = z - mu
    var = jnp.mean(zc * zc, axis=-1, keepdims=True)
    return zc * lax.rsqrt(var + LN_EPS) * g + b


def _pack_bf16_pairs(x):
    half = x.shape[1] // 2
    lo = lax.bitcast_convert_type(x[:, :half].astype(BF16).astype(F32), jnp.uint32) >> 16
    hi = lax.bitcast_convert_type(x[:, half:].astype(BF16).astype(F32), jnp.uint32) & jnp.uint32(0xFFFF0000)
    return lax.bitcast_convert_type(lo | hi, jnp.int32)


def _unpack_bf16_pairs(w):
    u = lax.bitcast_convert_type(w, jnp.uint32)
    lo = lax.bitcast_convert_type(u << 16, F32).astype(BF16)
    hi = lax.bitcast_convert_type(u & jnp.uint32(0xFFFF0000), F32).astype(BF16)
    return lo, hi


def _dot_halves(lo, hi, w_ref):
    half = lo.shape[1]
    return (jnp.dot(lo, w_ref[:half, :], preferred_element_type=F32)
            + jnp.dot(hi, w_ref[half:, :], preferred_element_type=F32))


def _outproj_ln_kernel(*refs, widths, lat_tiles, alpha):
    refs = list(refs)

    def take(width):
        got = [refs.pop(0) for _ in range(width)]
        return got[0][...] if width == 1 else _pair_tile(got[0], got[1], lat_tiles)

    acts = [take(w) for w in widths[:-1]]
    w_refs = [refs.pop(0) for _ in acts]
    x = take(widths[-1])
    gate_ref, lng_ref, lnb_ref, sc_ref, sh_ref, xo_ref, tok_ref = refs
    y = jnp.dot(acts[0], w_refs[0][...], preferred_element_type=F32)
    for a, w_ref in zip(acts[1:], w_refs[1:]):
        y = y + jnp.dot(a, w_ref[...], preferred_element_type=F32)
    xn = _layer_norm(alpha * x + gate_ref[...] * y, lng_ref[...], lnb_ref[...])
    xo_ref[...] = xn
    tok_ref[...] = _pack_bf16_pairs(xn * (1.0 + sc_ref[...]) + sh_ref[...])


def _outproj_ln(a_list, w_list, x, mod, ln_g, ln_b, *, t, tm, alpha, mod_row):
    d = w_list[0].shape[1]
    operands, in_specs, widths, lat_tiles = [], [], [], 0

    def add_rows(src):
        nonlocal lat_tiles
        if isinstance(src, tuple):
            in_specs.extend(_pair_specs(src, tm))
            operands.extend(src)
            widths.append(2)
            lat_tiles = src[0].shape[0] // tm
        else:
            in_specs.append(pl.BlockSpec((tm, src.shape[1]), lambda i: (i, 0)))
            operands.append(src)
            widths.append(1)

    for a in a_list:
        add_rows(a)
    in_specs += [_const_spec(w.shape) for w in w_list]
    operands += list(w_list)
    add_rows(x)
    in_specs += [
        _mod_spec(d, 2, mod_row, tm),
        _const_spec((1, d)), _const_spec((1, d)),
        _mod_spec(d, 4, mod_row, tm),
        _mod_spec(d, 3, mod_row, tm),
    ]
    return pl.pallas_call(
        functools.partial(_outproj_ln_kernel, widths=tuple(widths), lat_tiles=lat_tiles, alpha=alpha),
        grid=(t // tm,),
        in_specs=in_specs,
        out_specs=[pl.BlockSpec((tm, d), lambda i: (i, 0)), pl.BlockSpec((tm, d // 2), lambda i: (i, 0))],
        out_shape=[jax.ShapeDtypeStruct((t, d), F32), jax.ShapeDtypeStruct((t, d // 2), jnp.int32)],
        compiler_params=_params(("parallel",)),
        name="outproj_ln",
    )(*operands, mod, ln_g, ln_b, mod, mod)


def _router_kernel(t_ref, rw_ref, rb_ref, tri_ref, idx_ref, gw_ref, rank_ref, cnt_ref):
    @pl.when(pl.program_id(0) == 0)
    def _():
        cnt_ref[...] = jnp.zeros_like(cnt_ref)

    lo, hi = _unpack_bf16_pairs(t_ref[...])
    half = lo.shape[1]
    nt = (((1,), (1,)), ((), ()))
    logits = (lax.dot_general(rw_ref[:, :half], lo, nt, preferred_element_type=F32)
              + lax.dot_general(rw_ref[:, half:], hi, nt, preferred_element_type=F32))
    scores = 1.0 / (1.0 + jnp.exp(-logits))
    sel = scores + rb_ref[...]
    gsz = N_EXPERTS // N_GROUPS
    neg = -jnp.inf
    sub = lax.broadcasted_iota(jnp.int32, (gsz, sel.shape[1]), 0)
    slabs = [sel[g * gsz:(g + 1) * gsz, :] for g in range(N_GROUPS)]
    gscore = []
    for s in slabs:
        m1 = jnp.max(s, axis=0, keepdims=True)
        a1 = jnp.min(jnp.where(s == m1, sub, gsz), axis=0, keepdims=True)
        m2 = jnp.max(jnp.where(sub == a1, neg, s), axis=0, keepdims=True)
        gscore.append(m1 + m2)
    masked = []
    for g in range(N_GROUPS):
        ahead = jnp.zeros(gscore[g].shape, jnp.int32)
        for h in range(N_GROUPS):
            if h == g:
                continue
            beats = gscore[h] >= gscore[g] if h < g else gscore[h] > gscore[g]
            ahead = ahead + beats.astype(jnp.int32)
        masked.append(jnp.where(ahead < TOPK_GROUPS, slabs[g], neg))
    cur = jnp.concatenate(masked, axis=0)
    eio = lax.broadcasted_iota(jnp.int32, cur.shape, 0)
    picks, weights = [], []
    for _ in range(TOP_K):
        m = jnp.max(cur, axis=0, keepdims=True)
        a = jnp.min(jnp.where(cur == m, eio, N_EXPERTS), axis=0, keepdims=True)
        hit = eio == a
        picks.append(a)
        weights.append(jnp.sum(jnp.where(hit, scores, 0.0), axis=0, keepdims=True))
        cur = jnp.where(hit, neg, cur)
    total = weights[0]
    for w in weights[1:]:
        total = total + w
    for k in range(TOP_K):
        idx_ref[k:k + 1, :] = picks[k]
        gw_ref[k:k + 1, :] = weights[k] / total * ROUTED_SCALE
    for k in range(TOP_K, SUBLANES):
        idx_ref[k:k + 1, :] = jnp.zeros_like(picks[0])
        gw_ref[k:k + 1, :] = jnp.zeros_like(weights[0])
        rank_ref[k:k + 1, :] = jnp.zeros_like(picks[0])
    base = cnt_ref[:, 0:1]
    for k in range(TOP_K):
        onehot = jnp.where(eio == picks[k], 1.0, 0.0)
        before = jnp.dot(onehot.astype(BF16), tri_ref[...], preferred_element_type=F32)
        rank_ref[k:k + 1, :] = jnp.sum(onehot * (before + base), axis=0, keepdims=True).astype(jnp.int32)
        base = base + jnp.sum(onehot, axis=1, keepdims=True)
    cnt_ref[...] = jnp.broadcast_to(base, cnt_ref.shape)


def _router(tok, rw_t, rb, *, t, tt):
    half = tok.shape[1]
    tri = jnp.asarray(np.arange(tt)[:, None] < np.arange(tt)[None, :], BF16)
    blk = pl.BlockSpec((SUBLANES, tt), lambda i: (0, i))
    return pl.pallas_call(
        _router_kernel,
        grid=(t // tt,),
        in_specs=[
            pl.BlockSpec((tt, half), lambda i: (i, 0)),
            _const_spec((N_EXPERTS, 2 * half)),
            _const_spec((N_EXPERTS, 1)),
            _const_spec((tt, tt)),
        ],
        out_specs=[blk, blk, blk, _const_spec((N_EXPERTS, LANES))],
        out_shape=[jax.ShapeDtypeStruct((SUBLANES, t), jnp.int32), jax.ShapeDtypeStruct((SUBLANES, t), F32),
                   jax.ShapeDtypeStruct((SUBLANES, t), jnp.int32), jax.ShapeDtypeStruct((N_EXPERTS, LANES), F32)],
        compiler_params=_params(("arbitrary",)),
        name="router",
    )(tok, rw_t, rb, tri)


def _slots_kernel(idx_ref, rank_ref, start_ref, pos_ref):
    start = start_ref[...]
    eio = lax.broadcasted_iota(jnp.int32, (N_EXPERTS, idx_ref.shape[1]), 0)
    for k in range(TOP_K):
        seg = jnp.sum(jnp.where(eio == idx_ref[k:k + 1, :], start, 0.0), axis=0, keepdims=True)
        pos_ref[k:k + 1, :] = rank_ref[k:k + 1, :] + seg.astype(jnp.int32)
    for k in range(TOP_K, SUBLANES):
        pos_ref[k:k + 1, :] = jnp.zeros((1, idx_ref.shape[1]), jnp.int32)


def _assign_slots(idx, rank, seg_start, *, t, tt):
    blk = pl.BlockSpec((SUBLANES, tt), lambda i: (0, i))
    return pl.pallas_call(
        _slots_kernel,
        grid=(t // tt,),
        in_specs=[blk, blk, _const_spec((N_EXPERTS, 1))],
        out_specs=blk,
        out_shape=jax.ShapeDtypeStruct((SUBLANES, t), jnp.int32),
        compiler_params=_params(("parallel",)),
        name="assign_slots",
    )(idx, rank, seg_start)


def _sc_gather_rows(table, idx):
    n = idx.shape[0]
    d = table.shape[1]
    n_workers = V7X_SC_CORES * V7X_SC_SUBCORES
    per_w = n // n_workers
    n_chunks = per_w // SC_GATHER_ROWS
    assert per_w * n_workers == n and n_chunks * SC_GATHER_ROWS == per_w
    n_buf = next(b for b in range(SC_GATHER_BUFFERS, 1, -1) if n_chunks % b == 0)
    mesh = plsc.VectorSubcoreMesh(core_axis_name="c", subcore_axis_name="s", num_cores=V7X_SC_CORES,
                                  num_subcores=V7X_SC_SUBCORES)

    @functools.partial(
        pl.kernel,
        out_type=jax.ShapeDtypeStruct((n, d), table.dtype),
        mesh=mesh,
        scratch_types=[
            pltpu.VMEM((per_w,), jnp.int32),
            pltpu.VMEM((n_buf, SC_GATHER_ROWS, d), table.dtype),
            pltpu.SemaphoreType.DMA((n_buf,)),
            pltpu.SemaphoreType.DMA((n_buf,)),
        ],
        name="sc_gather_rows",
    )
    def gather(table_hbm, idx_hbm, out_hbm, idx_v, rows_v, gsem, wsem):
        wid = lax.axis_index("s") * V7X_SC_CORES + lax.axis_index("c")
        base = wid * per_w
        pltpu.sync_copy(idx_hbm.at[pl.ds(base, per_w)], idx_v)

        def gather_copy(c, b):
            return pltpu.make_async_copy(table_hbm.at[idx_v.at[pl.ds(c * SC_GATHER_ROWS, SC_GATHER_ROWS)]],
                                         rows_v.at[b], gsem.at[b])

        def write_copy(c, b):
            return pltpu.make_async_copy(rows_v.at[b], out_hbm.at[pl.ds(base + c * SC_GATHER_ROWS, SC_GATHER_ROWS)],
                                         wsem.at[b])

        for b in range(n_buf - 1):
            gather_copy(b, b).start()

        @pl.loop(0, n_chunks, step=n_buf)
        def _(g):
            for b in range(n_buf):
                c = g + b
                prev = (b + n_buf - 1) % n_buf
                gather_copy(c, b).wait()
                write_copy(c, b).start()

                @pl.when(c >= 1)
                def _():
                    write_copy(c - 1, prev).wait()

                @pl.when(c + n_buf - 1 < n_chunks)
                def _():
                    gather_copy(c + n_buf - 1, prev).start()

        write_copy(n_chunks - 1, (n_chunks - 1) % n_buf).wait()

    return gather(table, idx)


def _experts_kernel(be_ref, nbu_ref, x_ref, wg_ref, wu_ref, wd_ref, *rest, block_off):
    y_ref, wgb, wub, wdb = rest[-4:]
    step = pl.program_id(0)
    b = block_off + step
    nbu = nbu_ref[0]

    @pl.when(b < nbu)
    def _():
        changed = jnp.logical_or(step == 0, be_ref[b] != be_ref[jnp.maximum(b - 1, 0)])

        @pl.when(changed)
        def _():
            wgb[...] = wg_ref[...].astype(BF16)
            wub[...] = wu_ref[...].astype(BF16)
            wdb[...] = wd_ref[...].astype(BF16)

        lo, hi = _unpack_bf16_pairs(x_ref[...])
        hg = _dot_halves(lo, hi, wgb)
        hu = _dot_halves(lo, hi, wub)
        h = hg * (1.0 / (1.0 + jnp.exp(-hg))) * hu
        y_ref[...] = _pack_bf16_pairs(jnp.dot(h.astype(BF16), wdb[...], preferred_element_type=F32))

    @pl.when(b >= nbu)
    def _():
        y_ref[...] = jnp.zeros_like(y_ref)


def _experts(xs, block_e, nb_used, wg, wu, wd, layer, y_prev, *, block_off, n_blocks):
    half = xs.shape[1]
    d = 2 * half
    ff = wg.shape[3]
    n_call = xs.shape[0] // MOE_ROWS

    def used(b, nbu):
        return jnp.clip(jnp.minimum(block_off + b, nbu[0] - 1) - block_off, 0, n_call - 1)

    def expert(b, be):
        return be[block_off + b]

    in_specs = [
        pl.BlockSpec((MOE_ROWS, half), lambda b, be, nbu: (used(b, nbu), 0)),
        pl.BlockSpec((None, None, d, ff), lambda b, be, nbu: (layer, expert(b, be), 0, 0)),
        pl.BlockSpec((None, None, d, ff), lambda b, be, nbu: (layer, expert(b, be), 0, 0)),
        pl.BlockSpec((None, None, ff, d), lambda b, be, nbu: (layer, expert(b, be), 0, 0)),
    ]
    args = [block_e, nb_used, xs, wg, wu, wd]
    aliases = {}
    if y_prev is not None:
        in_specs.append(pl.BlockSpec(memory_space=pl.ANY))
        aliases = {len(args): 0}
        args.append(y_prev)
    grid_spec = pltpu.PrefetchScalarGridSpec(
        num_scalar_prefetch=2,
        grid=(n_call,),
        in_specs=in_specs,
        out_specs=pl.BlockSpec((MOE_ROWS, half), lambda b, be, nbu: (block_off + b, 0)),
        scratch_shapes=[
            pltpu.VMEM((d, ff), BF16),
            pltpu.VMEM((d, ff), BF16),
            pltpu.VMEM((ff, d), BF16),
        ],
    )
    return pl.pallas_call(
        functools.partial(_experts_kernel, block_off=block_off),
        grid_spec=grid_spec,
        out_shape=jax.ShapeDtypeStruct((n_blocks * MOE_ROWS, half), jnp.int32),
        input_output_aliases=aliases,
        compiler_params=_params(("arbitrary",)),
        name="experts",
    )(*args)


def _combine_ln_kernel(*refs, alpha, emit_next, n_prev):
    y_ref, gw_ref, tok_ref, x_ref, sg_ref, su_ref, sd_ref, gate_ref, lng_ref, lnb_ref = refs[:10]
    outs = refs[len(refs) - (2 if emit_next else 1):]
    if emit_next:
        sc_ref, sh_ref = refs[10:12]
        xo_ref, u_ref = outs
    else:
        (xo_ref,) = outs
    lo, hi = _unpack_bf16_pairs(tok_ref[...])
    hg = _dot_halves(lo, hi, sg_ref)
    hu = _dot_halves(lo, hi, su_ref)
    h = hg * (1.0 / (1.0 + jnp.exp(-hg))) * hu
    gw = gw_ref[...]
    f_lo = f_hi = None
    for k in range(TOP_K):
        y_lo, y_hi = _unpack_bf16_pairs(y_ref[k])
        w = gw[:, k:k + 1]
        f_lo = y_lo.astype(F32) * w if f_lo is None else f_lo + y_lo.astype(F32) * w
        f_hi = y_hi.astype(F32) * w if f_hi is None else f_hi + y_hi.astype(F32) * w
    f = jnp.concatenate([f_lo, f_hi], axis=-1) + jnp.dot(h.astype(BF16), sd_ref[...], preferred_element_type=F32)
    xn = _layer_norm(alpha * x_ref[...] + gate_ref[...] * f, lng_ref[...], lnb_ref[...])
    xo_ref[...] = xn
    if emit_next:
        u_ref[...] = (xn * (1.0 + sc_ref[...]) + sh_ref[...]).astype(u_ref.dtype)


def _combine_ln(y3, gw_t, tok, x, sg, su, sd, mod, ln_g, ln_b, mod_next, prev, *, t, row_off, tm, alpha, mod_row):
    d = x.shape[1]
    emit_next = mod_next is not None
    off = row_off // tm
    assert off * tm == row_off

    def rows(i):
        return (off + i, 0)

    def part_mod_row(r):
        return mod_row(r + row_off)

    in_specs = [
        pl.BlockSpec((TOP_K, tm, d // 2), lambda i: (0, i, 0)),
        pl.BlockSpec((tm, SUBLANES), rows),
        pl.BlockSpec((tm, d // 2), rows),
        pl.BlockSpec((tm, d), rows),
        _const_spec(sg.shape), _const_spec(su.shape), _const_spec(sd.shape),
        _mod_spec(d, 5, part_mod_row, tm),
        _const_spec((1, d)), _const_spec((1, d)),
    ]
    args = [y3, gw_t, tok, x, sg, su, sd, mod, ln_g, ln_b]
    out_specs = [pl.BlockSpec((tm, d), rows)]
    out_shape = [jax.ShapeDtypeStruct((t, d), F32)]
    if emit_next:
        in_specs += [_mod_spec(d, 1, part_mod_row, tm), _mod_spec(d, 0, part_mod_row, tm)]
        args += [mod_next, mod_next]
        out_specs.append(pl.BlockSpec((tm, d), rows))
        out_shape.append(jax.ShapeDtypeStruct((t, d), BF16))
    aliases = {}
    if prev is not None:
        for k, p in enumerate(prev):
            in_specs.append(pl.BlockSpec(memory_space=pl.ANY))
            aliases[len(args)] = k
            args.append(p)
    return pl.pallas_call(
        functools.partial(_combine_ln_kernel, alpha=alpha, emit_next=emit_next, n_prev=len(aliases)),
        grid=(y3.shape[1] // tm,),
        in_specs=in_specs,
        out_specs=out_specs,
        out_shape=out_shape,
        input_output_aliases=aliases,
        compiler_params=_params(("parallel",)),
        name="combine_ln",
    )(*args)


def _moe(tok, x, t, layer, router_w, router_b, wg, wu, wd, sg, su, sd, mod, ln_g, ln_b, mod_next, *, alpha, mod_row,
         tm):
    half = tok.shape[1]
    tt = 512
    idx, gw, rank, cnt = _router(tok, router_w.T.astype(BF16), router_b.reshape(N_EXPERTS, 1), t=t, tt=tt)
    n_asg = t * TOP_K
    counts = cnt[:, 0].astype(jnp.int32)
    padded = (counts + MOE_ROWS - 1) // MOE_ROWS * MOE_ROWS
    pend = jnp.cumsum(padded)
    pstart = pend - padded
    sc_rows = V7X_SC_CORES * V7X_SC_SUBCORES * SC_GATHER_ROWS * 2
    blocks_granule = MOE_DISPATCH_PARTS * max(sc_rows // MOE_ROWS, 1)
    assert (blocks_granule // MOE_DISPATCH_PARTS * MOE_ROWS) % sc_rows == 0
    n_blocks = -(-((n_asg + N_EXPERTS * (MOE_ROWS - 1)) // MOE_ROWS + 1) // blocks_granule) * blocks_granule
    assert n_asg % sc_rows == 0
    block_start = jnp.arange(n_blocks, dtype=jnp.int32) * MOE_ROWS
    block_e = jnp.minimum(jnp.sum((pend[None, :] <= block_start[:, None]).astype(jnp.int32), axis=1), N_EXPERTS - 1)
    nb_used = (pend[-1] // MOE_ROWS).astype(jnp.int32).reshape(1)
    pos2d = _assign_slots(idx, rank, pstart.astype(F32).reshape(N_EXPERTS, 1), t=t, tt=tt)[:TOP_K]
    pos = pos2d.reshape(-1)
    tok_of_asg = np.tile(np.arange(t, dtype=np.int32), TOP_K)
    n_pad = n_blocks * MOE_ROWS - n_asg
    seg_pad_end = jnp.cumsum(padded - counts)
    j = jnp.arange(n_pad, dtype=jnp.int32)
    pad_e = jnp.sum((seg_pad_end[None, :] <= j[:, None]).astype(jnp.int32), axis=1)
    seg_base = pstart + counts - (seg_pad_end - (padded - counts))
    in_seg = j + jnp.sum(jnp.where(pad_e[:, None] == jnp.arange(N_EXPERTS)[None, :], seg_base[None, :], 0), axis=1)
    pad_slot = jnp.where(pad_e < N_EXPERTS, in_seg, pend[-1] + j - seg_pad_end[-1])
    _, slot_tok = lax.sort((jnp.concatenate([pos, pad_slot]), jnp.concatenate([tok_of_asg, pad_slot % t])),
                           num_keys=1)
    per = n_blocks // MOE_DISPATCH_PARTS * MOE_ROWS
    xs = [_sc_gather_rows(tok, slot_tok[i * per:(i + 1) * per]) for i in range(MOE_DISPATCH_PARTS)]
    y = None
    for i in range(MOE_DISPATCH_PARTS):
        y = _experts(xs[i], block_e, nb_used, wg, wu, wd, layer, y, block_off=i * per // MOE_ROWS, n_blocks=n_blocks)
    n_cparts = MOE_COMBINE_PARTS if (t // MOE_COMBINE_PARTS * TOP_K) % sc_rows == 0 else 1
    t_part = t // n_cparts
    y3 = [_sc_gather_rows(y, pos2d[:, i * t_part:(i + 1) * t_part].reshape(-1)).reshape(TOP_K, t_part, half)
          for i in range(n_cparts)]
    outs = None
    gw_t = gw.T
    for i in range(n_cparts):
        outs = _combine_ln(y3[i], gw_t, tok, x, sg, su, sd, mod, ln_g, ln_b, mod_next, outs, t=t,
                           row_off=i * t_part, tm=tm, alpha=alpha, mod_row=mod_row)
    return outs


def _rope64(r, c_ref, sa_ref, sb_ref):
    return r * c_ref[...] + pltpu.roll(r, LANES - QK_ROPE // 2, 1) * sa_ref[...] + pltpu.roll(r, QK_ROPE // 2, 1) * sb_ref[...]


def _mla_q_kernel(d_ref, gain_ref, w_ref, c_ref, sa_ref, sb_ref, q_ref, *, scale):
    n = _rms(d_ref[...], gain_ref[...]).astype(BF16)
    q = jnp.dot(n, w_ref[...], preferred_element_type=F32)
    for h in range(MLA_HEADS):
        lo = h * MLA_DK_PAD
        q_ref[:, lo:lo + QK_NOPE] = (q[:, lo:lo + QK_NOPE] * scale).astype(q_ref.dtype)
        r = _rope64(q[:, lo + QK_NOPE:lo + MLA_DK_PAD], c_ref, sa_ref, sb_ref)
        q_ref[:, lo + QK_NOPE:lo + MLA_DK_PAD] = (r * scale).astype(q_ref.dtype)


def _mla_kv_kernel(ckv_ref, kr_ref, gain_ref, wk_ref, wv_ref, c_ref, sa_ref, sb_ref, k_ref, v_ref):
    n = _rms(ckv_ref[...], gain_ref[...]).astype(BF16)
    kn = jnp.dot(n, wk_ref[...], preferred_element_type=F32)
    v_ref[...] = jnp.dot(n, wv_ref[...], preferred_element_type=F32).astype(v_ref.dtype)
    kr = _rope64(kr_ref[...], c_ref, sa_ref, sb_ref).astype(k_ref.dtype)
    for h in range(MLA_HEADS):
        lo = h * MLA_DK_PAD
        k_ref[:, lo:lo + QK_NOPE] = kn[:, h * QK_NOPE:(h + 1) * QK_NOPE].astype(k_ref.dtype)
        k_ref[:, lo + QK_NOPE:lo + MLA_DK_PAD] = kr


def _axial_angles(n_tok, rot_dim):
    rows = n_tok // GRID_W
    n_freq = rot_dim // 4
    inv = (ROPE_THETA ** (-np.arange(n_freq, dtype=np.float32) / n_freq)).astype(np.float32)
    row = np.repeat(np.arange(rows, dtype=np.float32), GRID_W)
    col = np.tile(np.arange(GRID_W, dtype=np.float32), rows)
    return np.concatenate([row[:, None] * inv, col[:, None] * inv], axis=-1)


def _rope_tables_128(n_tok, ident_rows):
    ang = _axial_angles(n_tok, HEAD_DIM)
    cos, sin = np.cos(ang), np.sin(ang)
    c = np.concatenate([cos, cos], axis=-1)
    s = np.concatenate([-sin, sin], axis=-1)
    c = np.concatenate([c, np.ones((ident_rows, HEAD_DIM), np.float32)], axis=0)
    s = np.concatenate([s, np.zeros((ident_rows, HEAD_DIM), np.float32)], axis=0)
    return c.astype(np.float32), s.astype(np.float32)


def _rope_tables_64(n_tok, ident_rows):
    ang = _axial_angles(n_tok, QK_ROPE)
    cos, sin = np.cos(ang), np.sin(ang)
    half = QK_ROPE // 2
    z = np.zeros((n_tok, LANES - QK_ROPE), np.float32)
    zh = np.zeros((n_tok, half), np.float32)
    c = np.concatenate([cos, cos, z], axis=-1)
    sa = np.concatenate([-sin, zh, z], axis=-1)
    sb = np.concatenate([zh, sin, z], axis=-1)
    ci = np.concatenate([np.ones((ident_rows, QK_ROPE), np.float32),
                         np.zeros((ident_rows, LANES - QK_ROPE), np.float32)], axis=-1)
    zi = np.zeros((ident_rows, LANES), np.float32)
    tables = np.concatenate([c, ci], 0), np.concatenate([sa, zi], 0), np.concatenate([sb, zi], 0)
    return tuple(tab.astype(np.float32) for tab in tables)


def kernel(x, c, ctx, c_ctx, w_ada, b_ada, ln_g, ln_b, a_w_in, a_conv_w, a_q_gain, a_k_gain, a_w_out, m_w_down, m_q_gain, m_kv_gain, m_w_uq, m_w_ukv, m_w_out, router_w, router_b, e_w_gate, e_w_up, e_w_down, s_w_gate, s_w_up, s_w_down):
    batch, seq, d = x.shape
    ctx_len = ctx.shape[1]
    depth = w_ada.shape[0]
    assert depth == 2, "one conv+GQA layer followed by one MLA layer"
    alpha = (2 * depth) ** 0.25
    t_lat = batch * seq
    t_ctx = batch * ctx_len
    t_all = t_lat + t_ctx
    tr = 256
    assert seq % tr == 0 and ctx_len % tr == 0 and seq % GRID_W == 0
    lat_tiles = t_lat // tr
    lat_seq_tiles = seq // tr
    ctx_seq_tiles = ctx_len // tr
    lk = ctx_len + seq

    def mod_row(r):
        return jnp.minimum(r // seq, batch)

    def kv_block(i):
        is_lat = i < lat_tiles
        cidx = i - lat_tiles
        b = jnp.where(is_lat, i // lat_seq_tiles, cidx // ctx_seq_tiles)
        rb = jnp.where(is_lat, ctx_seq_tiles + i % lat_seq_tiles, cidx % ctx_seq_tiles)
        return b, rb

    def pos_block(i):
        return jnp.where(i < lat_tiles, i % lat_seq_tiles, lat_seq_tiles)

    rows = -(-(batch + 1) // SUBLANES) * SUBLANES
    cond = jnp.concatenate([c, c_ctx[None, :], jnp.zeros((rows - batch - 1, d), F32)], axis=0)
    mod = _ada_table(cond, w_ada, b_ada).reshape(depth, rows, 1, 6 * d)

    x_pair = (x.reshape(t_lat, d), ctx.reshape(t_ctx, d))

    tm_in = 1024 if (t_ctx % 1024 == 0 and seq % 1024 == 0) else tr
    proj = _mod_matmul(x_pair, mod[0], a_w_in[0].astype(BF16), BF16, tm_in, 768, mod_row)

    cos128, sin128 = _rope_tables_128(seq, tr)
    d_q = ATT_HEADS * HEAD_DIM
    d_kv = ATT_KV_HEADS * HEAD_DIM
    qkv_w = d_q + 2 * d_kv
    qkv_blk = 3 * CONV_DIM // qkv_w
    assert qkv_blk * qkv_w == 3 * CONV_DIM
    q0, k0, v0 = pl.pallas_call(
        functools.partial(_qkprep_kernel, scale=1.0 / math.sqrt(HEAD_DIM)),
        grid=(t_all // tr,),
        in_specs=[
            pl.BlockSpec((tr, qkv_w), lambda i: (i, qkv_blk)),
            pl.BlockSpec((tr, HEAD_DIM), lambda i: (pos_block(i), 0)),
            pl.BlockSpec((tr, HEAD_DIM), lambda i: (pos_block(i), 0)),
            _const_spec((1, HEAD_DIM)), _const_spec((1, HEAD_DIM)),
        ],
        out_specs=[
            pl.BlockSpec((tr, d_q), lambda i: (i, 0)),
            pl.BlockSpec((None, tr, d_kv), lambda i: (*kv_block(i), 0)),
            pl.BlockSpec((None, tr, d_kv), lambda i: (*kv_block(i), 0)),
        ],
        out_shape=[
            jax.ShapeDtypeStruct((t_all, d_q), BF16),
            jax.ShapeDtypeStruct((batch, lk, d_kv), BF16),
            jax.ShapeDtypeStruct((batch, lk, d_kv), BF16),
        ],
        compiler_params=_params(("parallel",)),
        name="qk_prep",
    )(proj, cos128, sin128, a_q_gain[0].reshape(1, HEAD_DIM), a_k_gain[0].reshape(1, HEAD_DIM))

    grp = ATT_HEADS // ATT_KV_HEADS
    att_lat = _attention(q0, k0, v0, batch=batch, sq=seq, lk=lk, n_kv=ATT_KV_HEADS, group=grp, dk=HEAD_DIM,
                         dv=HEAD_DIM, tq=512 if seq % 512 == 0 else tr, rows=256, q_row_off=0)
    att_ctx = _attention(q0, k0, v0, batch=batch, sq=ctx_len, lk=ctx_len, n_kv=ATT_KV_HEADS, group=grp,
                         dk=HEAD_DIM, dv=HEAD_DIM, tq=256, rows=256, q_row_off=t_lat)

    conv0 = _conv_gate(proj, a_conv_w[0], t=t_all, tm=tr, tc=512, lat_tiles=lat_tiles,
                       lat_seq_tiles=lat_seq_tiles, ctx_seq_tiles=ctx_seq_tiles)

    w_out0 = a_w_out[0].astype(BF16)
    x1, tok0 = _outproj_ln([conv0, (att_lat, att_ctx)], [w_out0[:CONV_DIM], w_out0[CONV_DIM:]], x_pair, mod[0],
                           ln_g[0, 0].reshape(1, d), ln_b[0, 0].reshape(1, d), t=t_all, tm=tr, alpha=alpha,
                           mod_row=mod_row)

    x2, u1 = _moe(tok0, x1, t_all, 0, router_w[0], router_b[0], e_w_gate, e_w_up, e_w_down,
                  s_w_gate[0].astype(BF16), s_w_up[0].astype(BF16), s_w_down[0].astype(BF16), mod[0],
                  ln_g[0, 1].reshape(1, d), ln_b[0, 1].reshape(1, d), mod[1], alpha=alpha,
                  mod_row=mod_row, tm=256)

    n_down = Q_LORA + KV_LORA + QK_ROPE
    n_down_pad = -(-n_down // LANES) * LANES
    w_down = jnp.pad(m_w_down[0], ((0, 0), (0, n_down_pad - n_down))).astype(BF16)
    down = _matmul(u1, w_down, F32, 512, n_down_pad)

    dqk = QK_NOPE + QK_ROPE
    w_uq = m_w_uq[0].reshape(Q_LORA, MLA_HEADS, dqk)
    w_uq = jnp.pad(w_uq, ((0, 0), (0, 0), (0, MLA_DK_PAD - dqk))).reshape(Q_LORA, MLA_HEADS * MLA_DK_PAD).astype(BF16)
    w_ukv = m_w_ukv[0].reshape(KV_LORA, MLA_HEADS, QK_NOPE + V_DIM)
    w_uk = w_ukv[:, :, :QK_NOPE].reshape(KV_LORA, MLA_HEADS * QK_NOPE).astype(BF16)
    w_uv = w_ukv[:, :, QK_NOPE:].reshape(KV_LORA, MLA_HEADS * V_DIM).astype(BF16)

    c64, sa64, sb64 = _rope_tables_64(seq, tr)
    rope_specs = [pl.BlockSpec((tr, LANES), lambda i: (pos_block(i), 0))] * 3
    q1 = pl.pallas_call(
        functools.partial(_mla_q_kernel, scale=1.0 / math.sqrt(dqk)),
        grid=(lat_tiles,),
        in_specs=[
            pl.BlockSpec((tr, Q_LORA), lambda i: (i, 0)),
            _const_spec((1, Q_LORA)),
            _const_spec(w_uq.shape),
        ] + rope_specs,
        out_specs=pl.BlockSpec((tr, MLA_HEADS * MLA_DK_PAD), lambda i: (i, 0)),
        out_shape=jax.ShapeDtypeStruct((t_lat, MLA_HEADS * MLA_DK_PAD), BF16),
        compiler_params=_params(("parallel",)),
        name="mla_q",
    )(down, m_q_gain[0].reshape(1, Q_LORA), w_uq, c64, sa64, sb64)

    assert KV_LORA == Q_LORA and (Q_LORA + KV_LORA) % LANES == 0
    k1, v1 = pl.pallas_call(
        _mla_kv_kernel,
        grid=(t_all // tr,),
        in_specs=[
            pl.BlockSpec((tr, KV_LORA), lambda i: (i, 1)),
            pl.BlockSpec((tr, LANES), lambda i: (i, (Q_LORA + KV_LORA) // LANES)),
            _const_spec((1, KV_LORA)),
            _const_spec(w_uk.shape), _const_spec(w_uv.shape),
        ] + rope_specs,
        out_specs=[
            pl.BlockSpec((None, tr, MLA_HEADS * MLA_DK_PAD), lambda i: (*kv_block(i), 0)),
            pl.BlockSpec((None, tr, MLA_HEADS * V_DIM), lambda i: (*kv_block(i), 0)),
        ],
        out_shape=[
            jax.ShapeDtypeStruct((batch, lk, MLA_HEADS * MLA_DK_PAD), BF16),
            jax.ShapeDtypeStruct((batch, lk, MLA_HEADS * V_DIM), BF16),
        ],
        compiler_params=_params(("parallel",)),
        name="mla_kv",
    )(down, down, m_kv_gain[0].reshape(1, KV_LORA), w_uk, w_uv, c64, sa64, sb64)

    att1 = _attention(q1, k1, v1, batch=batch, sq=seq, lk=lk, n_kv=MLA_HEADS, group=1, dk=MLA_DK_PAD, dv=V_DIM,
                      tq=next(c for c in (2048, 1024, 512, tr) if seq % c == 0), rows=256, q_row_off=0)

    x3, tok1 = _outproj_ln([att1], [m_w_out[0].astype(BF16)], x2, mod[1], ln_g[1, 0].reshape(1, d),
                           ln_b[1, 0].reshape(1, d), t=t_lat, tm=tr, alpha=alpha, mod_row=mod_row)

    (x4,) = _moe(tok1, x3, t_lat, 1, router_w[1], router_b[1], e_w_gate, e_w_up, e_w_down,
                 s_w_gate[1].astype(BF16), s_w_up[1].astype(BF16), s_w_down[1].astype(BF16), mod[1],
                 ln_g[1, 1].reshape(1, d), ln_b[1, 1].reshape(1, d), None, alpha=alpha,
                 mod_row=mod_row, tm=256)
    return x4.reshape(batch, seq, d)
```

```python
import functools
import math

import jax
import jax.numpy as jnp
import numpy as np
from jax import lax
from jax.experimental import pallas as pl
from jax.experimental.pallas import tpu as pltpu
from jax.experimental.pallas import tpu_sc as plsc

F32 = jnp.float32
BF16 = jnp.bfloat16

GRID_W = 64
CONV_DIM = 1024
ATT_HEADS = 8
ATT_KV_HEADS = 2
HEAD_DIM = 128
MLA_HEADS = 16
Q_LORA = 512
KV_LORA = 512
QK_NOPE = 128
QK_ROPE = 64
V_DIM = 128
N_EXPERTS = 64
TOP_K = 6
N_GROUPS = 8
TOPK_GROUPS = 4
ROUTED_SCALE = 2.5
ROPE_THETA = 10000.0
LN_EPS = 1e-5
RMS_EPS = 1e-6

V7X_VMEM_LIMIT_BYTES = 56 * 1024 * 1024
LANES = 128
SUBLANES = 8
MOE_ROWS = 256
MOE_DISPATCH_PARTS = 4
MOE_COMBINE_PARTS = 2
V7X_SC_CORES = 2
V7X_SC_SUBCORES = 16
SC_GATHER_ROWS = 16
SC_GATHER_BUFFERS = 4
MLA_DK_PAD = 256


def _params(sem):
    return pltpu.CompilerParams(dimension_semantics=sem, vmem_limit_bytes=V7X_VMEM_LIMIT_BYTES)


def _const_spec(shape):
    nd = len(shape)
    return pl.BlockSpec(shape, lambda *_: (0,) * nd)


def _ada_kernel(s_ref, w_ref, b_ref, o_ref):
    s = s_ref[...]
    s = s * (1.0 / (1.0 + jnp.exp(-s)))
    o_ref[...] = jnp.dot(s.astype(BF16), w_ref[...].astype(BF16), preferred_element_type=F32) + b_ref[...]


def _ada_table(cond, w_ada, b_ada):
    depth, d, n = w_ada.shape
    r = cond.shape[0]
    tn = 1024
    return pl.pallas_call(
        _ada_kernel,
        grid=(depth, n // tn),
        in_specs=[
            pl.BlockSpec((r, d), lambda l, j: (0, 0)),
            pl.BlockSpec((None, d, tn), lambda l, j: (l, 0, j)),
            pl.BlockSpec((None, 1, tn), lambda l, j: (l, 0, j)),
        ],
        out_specs=pl.BlockSpec((None, r, tn), lambda l, j: (l, 0, j)),
        out_shape=jax.ShapeDtypeStruct((depth, r, n), F32),
        compiler_params=_params(("parallel", "parallel")),
        name="ada_table",
    )(cond, w_ada, b_ada.reshape(depth, 1, n))


def _mod_spec(d, chunk, mod_row, tm):
    return pl.BlockSpec((None, 1, d), lambda i: (mod_row(i * tm), 0, chunk))


def _mm_kernel(a_ref, w_ref, o_ref):
    o_ref[...] = jnp.dot(a_ref[...], w_ref[...], preferred_element_type=F32).astype(o_ref.dtype)


def _matmul(a, w, out_dtype, tm, tn):
    m, k = a.shape
    n = w.shape[1]
    return pl.pallas_call(
        _mm_kernel,
        grid=(m // tm, n // tn),
        in_specs=[
            pl.BlockSpec((tm, k), lambda i, j: (i, 0)),
            pl.BlockSpec((k, tn), lambda i, j: (0, j)),
        ],
        out_specs=pl.BlockSpec((tm, tn), lambda i, j: (i, j)),
        out_shape=jax.ShapeDtypeStruct((m, n), out_dtype),
        compiler_params=_params(("parallel", "parallel")),
        name="matmul",
    )(a, w)


def _pair_specs(pair, tm):
    lat, ctx = pair
    lat_tiles = lat.shape[0] // tm
    assert lat_tiles * tm == lat.shape[0] and ctx.shape[0] % tm == 0 and lat.shape[1] == ctx.shape[1]
    width = lat.shape[1]
    return [pl.BlockSpec((tm, width), lambda i, *_: (jnp.minimum(i, lat_tiles - 1), 0)),
            pl.BlockSpec((tm, width), lambda i, *_: (jnp.maximum(i - lat_tiles, 0), 0))]


def _pair_tile(lat_ref, ctx_ref, lat_tiles):
    return jnp.where(pl.program_id(0) < lat_tiles, lat_ref[...], ctx_ref[...])


def _mod_mm_kernel(xl_ref, xc_ref, sc_ref, sh_ref, w_ref, o_ref, u_ref, *, lat_tiles):
    @pl.when(pl.program_id(1) == 0)
    def _():
        x = _pair_tile(xl_ref, xc_ref, lat_tiles)
        u_ref[...] = (x * (1.0 + sc_ref[...]) + sh_ref[...]).astype(u_ref.dtype)

    o_ref[...] = jnp.dot(u_ref[...], w_ref[...], preferred_element_type=F32).astype(o_ref.dtype)


def _mod_matmul(x_pair, mod, w, out_dtype, tm, tn, mod_row):
    m = x_pair[0].shape[0] + x_pair[1].shape[0]
    k = x_pair[0].shape[1]
    n = w.shape[1]

    def mod_spec(chunk):
        return pl.BlockSpec((None, 1, k), lambda i, j: (mod_row(i * tm), 0, chunk))

    return pl.pallas_call(
        functools.partial(_mod_mm_kernel, lat_tiles=x_pair[0].shape[0] // tm),
        grid=(m // tm, n // tn),
        in_specs=_pair_specs(x_pair, tm) + [
            mod_spec(1), mod_spec(0),
            pl.BlockSpec((k, tn), lambda i, j: (0, j)),
        ],
        out_specs=pl.BlockSpec((tm, tn), lambda i, j: (i, j)),
        out_shape=jax.ShapeDtypeStruct((m, n), out_dtype),
        scratch_shapes=[pltpu.VMEM((tm, k), w.dtype)],
        compiler_params=_params(("parallel", "arbitrary")),
        name="mod_matmul",
    )(*x_pair, mod, mod, w)


def _rms(t, gain):
    return t * lax.rsqrt(jnp.mean(t * t, axis=-1, keepdims=True) + RMS_EPS) * gain


def _qkprep_kernel(p_ref, cos_ref, sin_ref, qg_ref, kg_ref, q_ref, k_ref, v_ref, *, scale):
    cos = cos_ref[...]
    sin = sin_ref[...]

    def norm_rope(t, gain):
        y = _rms(t.astype(F32), gain)
        return y * cos + pltpu.roll(y, HEAD_DIM // 2, 1) * sin

    for h in range(ATT_HEADS):
        sl = slice(h * HEAD_DIM, (h + 1) * HEAD_DIM)
        q_ref[:, sl] = (norm_rope(p_ref[:, sl], qg_ref[...]) * scale).astype(q_ref.dtype)
    k0 = ATT_HEADS * HEAD_DIM
    for h in range(ATT_KV_HEADS):
        sl = slice(h * HEAD_DIM, (h + 1) * HEAD_DIM)
        k_ref[:, sl] = norm_rope(p_ref[:, k0 + h * HEAD_DIM:k0 + (h + 1) * HEAD_DIM], kg_ref[...]).astype(k_ref.dtype)
    v0 = k0 + ATT_KV_HEADS * HEAD_DIM
    v_ref[...] = p_ref[:, v0:v0 + ATT_KV_HEADS * HEAD_DIM].astype(v_ref.dtype)


def _attn_kernel(q_ref, k_ref, v_ref, o_ref, *, group, tq, rows, dk, dv):
    k = k_ref[...]
    v = v_ref[...]
    for h in range(group):
        for r in range(0, tq, rows):
            q = q_ref[r:r + rows, h * dk:(h + 1) * dk]
            s = lax.dot_general(q, k, (((1,), (1,)), ((), ())), preferred_element_type=F32)
            m = jnp.max(s, axis=-1, keepdims=True)
            p = jnp.exp(s - m)
            l = jnp.sum(p, axis=-1, keepdims=True)
            o = jnp.dot(p.astype(v.dtype), v, preferred_element_type=F32)
            o_ref[r:r + rows, h * dv:(h + 1) * dv] = (o / l).astype(o_ref.dtype)


def _attention(q, k, v, *, batch, sq, lk, n_kv, group, dk, dv, tq, rows, q_row_off):
    nq = sq // tq
    off = q_row_off // tq
    assert tq % rows == 0 and q_row_off % tq == 0 and sq % tq == 0
    return pl.pallas_call(
        functools.partial(_attn_kernel, group=group, tq=tq, rows=rows, dk=dk, dv=dv),
        grid=(batch, n_kv, nq),
        in_specs=[
            pl.BlockSpec((tq, group * dk), lambda b, g, i: (off + b * nq + i, g)),
            pl.BlockSpec((None, lk, dk), lambda b, g, i: (b, 0, g)),
            pl.BlockSpec((None, lk, dv), lambda b, g, i: (b, 0, g)),
        ],
        out_specs=pl.BlockSpec((tq, group * dv), lambda b, g, i: (b * nq + i, g)),
        out_shape=jax.ShapeDtypeStruct((batch * sq, n_kv * group * dv), BF16),
        compiler_params=_params(("parallel", "parallel", "parallel")),
        name="attention",
    )(q, k, v)


def _conv_kernel(gb_ref, gc_ref, hv_ref, gcp_ref, hvp_ref, gcn_ref, hvn_ref, w_ref, o_ref, *,
                 tm, lat_tiles, lat_seq_tiles, ctx_seq_tiles):
    i = pl.program_id(0)
    is_lat = i < lat_tiles
    pos = jnp.where(is_lat, i % lat_seq_tiles, (i - lat_tiles) % ctx_seq_tiles)
    seq_tiles = jnp.where(is_lat, lat_seq_tiles, ctx_seq_tiles)
    not_first = (pos != 0).astype(F32)
    not_last = (pos != seq_tiles - 1).astype(F32)
    p = gc_ref[...].astype(F32) * hv_ref[...].astype(F32)
    halo_prev = gcp_ref[SUBLANES - 1:SUBLANES, :].astype(F32) * hvp_ref[SUBLANES - 1:SUBLANES, :].astype(F32) * not_first
    halo_next = gcn_ref[0:1, :].astype(F32) * hvn_ref[0:1, :].astype(F32) * not_last
    row = lax.broadcasted_iota(jnp.int32, p.shape, 0)
    prev = jnp.where(row == 0, halo_prev, pltpu.roll(p, 1, 0))
    nxt = jnp.where(row == tm - 1, halo_next, pltpu.roll(p, tm - 1, 0))
    w = w_ref[...]
    conv = w[0:1, :] * prev + w[1:2, :] * p + w[2:3, :] * nxt
    o_ref[...] = (gb_ref[...].astype(F32) * conv).astype(o_ref.dtype)


def _conv_gate(p, conv_w, *, t, tm, tc, lat_tiles, lat_seq_tiles, ctx_seq_tiles):
    nct = CONV_DIM // tc
    hb = tm // SUBLANES
    n_halo = t // SUBLANES

    def cur(part):
        return pl.BlockSpec((tm, tc), lambda i, j: (i, part * nct + j))

    def prev(part):
        return pl.BlockSpec((SUBLANES, tc), lambda i, j: (jnp.maximum(i * hb - 1, 0), part * nct + j))

    def nxt(part):
        return pl.BlockSpec((SUBLANES, tc), lambda i, j: (jnp.minimum((i + 1) * hb, n_halo - 1), part * nct + j))

    return pl.pallas_call(
        functools.partial(_conv_kernel, tm=tm, lat_tiles=lat_tiles, lat_seq_tiles=lat_seq_tiles,
                          ctx_seq_tiles=ctx_seq_tiles),
        grid=(t // tm, nct),
        in_specs=[cur(0), cur(1), cur(2), prev(1), prev(2), nxt(1), nxt(2),
                  pl.BlockSpec((3, tc), lambda i, j: (0, j))],
        out_specs=pl.BlockSpec((tm, tc), lambda i, j: (i, j)),
        out_shape=jax.ShapeDtypeStruct((t, CONV_DIM), BF16),
        compiler_params=_params(("parallel", "parallel")),
        name="conv_gate",
    )(p, p, p, p, p, p, p, conv_w)


def _layer_norm(z, g, b):
    mu = jnp.mean(z, axis=-1, keepdims=True)
    zc = z - mu
    var = jnp.mean(zc * zc, axis=-1, keepdims=True)
    return zc * lax.rsqrt(var + LN_EPS) * g + b


def _pack_bf16_pairs(x):
    half = x.shape[1] // 2
    lo = lax.bitcast_convert_type(x[:, :half].astype(BF16).astype(F32), jnp.uint32) >> 16
    hi = lax.bitcast_convert_type(x[:, half:].astype(BF16).astype(F32), jnp.uint32) & jnp.uint32(0xFFFF0000)
    return lax.bitcast_convert_type(lo | hi, jnp.int32)


def _unpack_bf16_pairs(w):
    u = lax.bitcast_convert_type(w, jnp.uint32)
    lo = lax.bitcast_convert_type(u << 16, F32).astype(BF16)
    hi = lax.bitcast_convert_type(u & jnp.uint32(0xFFFF0000), F32).astype(BF16)
    return lo, hi


def _dot_halves(lo, hi, w_ref):
    half = lo.shape[1]
    return (jnp.dot(lo, w_ref[:half, :], preferred_element_type=F32)
            + jnp.dot(hi, w_ref[half:, :], preferred_element_type=F32))


def _outproj_ln_kernel(*refs, widths, lat_tiles, alpha):
    refs = list(refs)

    def take(width):
        got = [refs.pop(0) for _ in range(width)]
        return got[0][...] if width == 1 else _pair_tile(got[0], got[1], lat_tiles)

    acts = [take(w) for w in widths[:-1]]
    w_refs = [refs.pop(0) for _ in acts]
    x = take(widths[-1])
    gate_ref, lng_ref, lnb_ref, sc_ref, sh_ref, xo_ref, tok_ref = refs
    y = jnp.dot(acts[0], w_refs[0][...], preferred_element_type=F32)
    for a, w_ref in zip(acts[1:], w_refs[1:]):
        y = y + jnp.dot(a, w_ref[...], preferred_element_type=F32)
    xn = _layer_norm(alpha * x + gate_ref[...] * y, lng_ref[...], lnb_ref[...])
    xo_ref[...] = xn
    tok_ref[...] = _pack_bf16_pairs(xn * (1.0 + sc_ref[...]) + sh_ref[...])


def _outproj_ln(a_list, w_list, x, mod, ln_g, ln_b, *, t, tm, alpha, mod_row):
    d = w_list[0].shape[1]
    operands, in_specs, widths, lat_tiles = [], [], [], 0

    def add_rows(src):
        nonlocal lat_tiles
        if isinstance(src, tuple):
            in_specs.extend(_pair_specs(src, tm))
            operands.extend(src)
            widths.append(2)
            lat_tiles = src[0].shape[0] // tm
        else:
            in_specs.append(pl.BlockSpec((tm, src.shape[1]), lambda i: (i, 0)))
            operands.append(src)
            widths.append(1)

    for a in a_list:
        add_rows(a)
    in_specs += [_const_spec(w.shape) for w in w_list]
    operands += list(w_list)
    add_rows(x)
    in_specs += [
        _mod_spec(d, 2, mod_row, tm),
        _const_spec((1, d)), _const_spec((1, d)),
        _mod_spec(d, 4, mod_row, tm),
        _mod_spec(d, 3, mod_row, tm),
    ]
    return pl.pallas_call(
        functools.partial(_outproj_ln_kernel, widths=tuple(widths), lat_tiles=lat_tiles, alpha=alpha),
        grid=(t // tm,),
        in_specs=in_specs,
        out_specs=[pl.BlockSpec((tm, d), lambda i: (i, 0)), pl.BlockSpec((tm, d // 2), lambda i: (i, 0))],
        out_shape=[jax.ShapeDtypeStruct((t, d), F32), jax.ShapeDtypeStruct((t, d // 2), jnp.int32)],
        compiler_params=_params(("parallel",)),
        name="outproj_ln",
    )(*operands, mod, ln_g, ln_b, mod, mod)


def _router_kernel(t_ref, rw_ref, rb_ref, tri_ref, idx_ref, gw_ref, rank_ref, cnt_ref):
    @pl.when(pl.program_id(0) == 0)
    def _():
        cnt_ref[...] = jnp.zeros_like(cnt_ref)

    lo, hi = _unpack_bf16_pairs(t_ref[...])
    half = lo.shape[1]
    nt = (((1,), (1,)), ((), ()))
    logits = (lax.dot_general(rw_ref[:, :half], lo, nt, preferred_element_type=F32)
              + lax.dot_general(rw_ref[:, half:], hi, nt, preferred_element_type=F32))
    scores = 1.0 / (1.0 + jnp.exp(-logits))
    sel = scores + rb_ref[...]
    gsz = N_EXPERTS // N_GROUPS
    neg = -jnp.inf
    sub = lax.broadcasted_iota(jnp.int32, (gsz, sel.shape[1]), 0)
    slabs = [sel[g * gsz:(g + 1) * gsz, :] for g in range(N_GROUPS)]
    gscore = []
    for s in slabs:
        m1 = jnp.max(s, axis=0, keepdims=True)
        a1 = jnp.min(jnp.where(s == m1, sub, gsz), axis=0, keepdims=True)
        m2 = jnp.max(jnp.where(sub == a1, neg, s), axis=0, keepdims=True)
        gscore.append(m1 + m2)
    masked = []
    for g in range(N_GROUPS):
        ahead = jnp.zeros(gscore[g].shape, jnp.int32)
        for h in range(N_GROUPS):
            if h == g:
                continue
            beats = gscore[h] >= gscore[g] if h < g else gscore[h] > gscore[g]
            ahead = ahead + beats.astype(jnp.int32)
        masked.append(jnp.where(ahead < TOPK_GROUPS, slabs[g], neg))
    cur = jnp.concatenate(masked, axis=0)
    eio = lax.broadcasted_iota(jnp.int32, cur.shape, 0)
    picks, weights = [], []
    for _ in range(TOP_K):
        m = jnp.max(cur, axis=0, keepdims=True)
        a = jnp.min(jnp.where(cur == m, eio, N_EXPERTS), axis=0, keepdims=True)
        hit = eio == a
        picks.append(a)
        weights.append(jnp.sum(jnp.where(hit, scores, 0.0), axis=0, keepdims=True))
        cur = jnp.where(hit, neg, cur)
    total = weights[0]
    for w in weights[1:]:
        total = total + w
    for k in range(TOP_K):
        idx_ref[k:k + 1, :] = picks[k]
        gw_ref[k:k + 1, :] = weights[k] / total * ROUTED_SCALE
    for k in range(TOP_K, SUBLANES):
        idx_ref[k:k + 1, :] = jnp.zeros_like(picks[0])
        gw_ref[k:k + 1, :] = jnp.zeros_like(weights[0])
        rank_ref[k:k + 1, :] = jnp.zeros_like(picks[0])
    base = cnt_ref[:, 0:1]
    for k in range(TOP_K):
        onehot = jnp.where(eio == picks[k], 1.0, 0.0)
        before = jnp.dot(onehot.astype(BF16), tri_ref[...], preferred_element_type=F32)
        rank_ref[k:k + 1, :] = jnp.sum(onehot * (before + base), axis=0, keepdims=True).astype(jnp.int32)
        base = base + jnp.sum(onehot, axis=1, keepdims=True)
    cnt_ref[...] = jnp.broadcast_to(base, cnt_ref.shape)


def _router(tok, rw_t, rb, *, t, tt):
    half = tok.shape[1]
    tri = jnp.asarray(np.arange(tt)[:, None] < np.arange(tt)[None, :], BF16)
    blk = pl.BlockSpec((SUBLANES, tt), lambda i: (0, i))
    return pl.pallas_call(
        _router_kernel,
        grid=(t // tt,),
        in_specs=[
            pl.BlockSpec((tt, half), lambda i: (i, 0)),
            _const_spec((N_EXPERTS, 2 * half)),
            _const_spec((N_EXPERTS, 1)),
            _const_spec((tt, tt)),
        ],
        out_specs=[blk, blk, blk, _const_spec((N_EXPERTS, LANES))],
        out_shape=[jax.ShapeDtypeStruct((SUBLANES, t), jnp.int32), jax.ShapeDtypeStruct((SUBLANES, t), F32),
                   jax.ShapeDtypeStruct((SUBLANES, t), jnp.int32), jax.ShapeDtypeStruct((N_EXPERTS, LANES), F32)],
        compiler_params=_params(("arbitrary",)),
        name="router",
    )(tok, rw_t, rb, tri)


def _slots_kernel(idx_ref, rank_ref, start_ref, pos_ref):
    start = start_ref[...]
    eio = lax.broadcasted_iota(jnp.int32, (N_EXPERTS, idx_ref.shape[1]), 0)
    for k in range(TOP_K):
        seg = jnp.sum(jnp.where(eio == idx_ref[k:k + 1, :], start, 0.0), axis=0, keepdims=True)
        pos_ref[k:k + 1, :] = rank_ref[k:k + 1, :] + seg.astype(jnp.int32)
    for k in range(TOP_K, SUBLANES):
        pos_ref[k:k + 1, :] = jnp.zeros((1, idx_ref.shape[1]), jnp.int32)


def _assign_slots(idx, rank, seg_start, *, t, tt):
    blk = pl.BlockSpec((SUBLANES, tt), lambda i: (0, i))
    return pl.pallas_call(
        _slots_kernel,
        grid=(t // tt,),
        in_specs=[blk, blk, _const_spec((N_EXPERTS, 1))],
        out_specs=blk,
        out_shape=jax.ShapeDtypeStruct((SUBLANES, t), jnp.int32),
        compiler_params=_params(("parallel",)),
        name="assign_slots",
    )(idx, rank, seg_start)


def _sc_gather_rows(table, idx):
    n = idx.shape[0]
    d = table.shape[1]
    n_workers = V7X_SC_CORES * V7X_SC_SUBCORES
    per_w = n // n_workers
    n_chunks = per_w // SC_GATHER_ROWS
    assert per_w * n_workers == n and n_chunks * SC_GATHER_ROWS == per_w
    n_buf = next(b for b in range(SC_GATHER_BUFFERS, 1, -1) if n_chunks % b == 0)
    mesh = plsc.VectorSubcoreMesh(core_axis_name="c", subcore_axis_name="s", num_cores=V7X_SC_CORES,
                                  num_subcores=V7X_SC_SUBCORES)

    @functools.partial(
        pl.kernel,
        out_type=jax.ShapeDtypeStruct((n, d), table.dtype),
        mesh=mesh,
        scratch_types=[
            pltpu.VMEM((per_w,), jnp.int32),
            pltpu.VMEM((n_buf, SC_GATHER_ROWS, d), table.dtype),
            pltpu.SemaphoreType.DMA((n_buf,)),
            pltpu.SemaphoreType.DMA((n_buf,)),
        ],
        name="sc_gather_rows",
    )
    def gather(table_hbm, idx_hbm, out_hbm, idx_v, rows_v, gsem, wsem):
        wid = lax.axis_index("s") * V7X_SC_CORES + lax.axis_index("c")
        base = wid * per_w
        pltpu.sync_copy(idx_hbm.at[pl.ds(base, per_w)], idx_v)

        def gather_copy(c, b):
            return pltpu.make_async_copy(table_hbm.at[idx_v.at[pl.ds(c * SC_GATHER_ROWS, SC_GATHER_ROWS)]],
                                         rows_v.at[b], gsem.at[b])

        def write_copy(c, b):
            return pltpu.make_async_copy(rows_v.at[b], out_hbm.at[pl.ds(base + c * SC_GATHER_ROWS, SC_GATHER_ROWS)],
                                         wsem.at[b])

        for b in range(n_buf - 1):
            gather_copy(b, b).start()

        @pl.loop(0, n_chunks, step=n_buf)
        def _(g):
            for b in range(n_buf):
                c = g + b
                prev = (b + n_buf - 1) % n_buf
                gather_copy(c, b).wait()
                write_copy(c, b).start()

                @pl.when(c >= 1)
                def _():
                    write_copy(c - 1, prev).wait()

                @pl.when(c + n_buf - 1 < n_chunks)
                def _():
                    gather_copy(c + n_buf - 1, prev).start()

        write_copy(n_chunks - 1, (n_chunks - 1) % n_buf).wait()

    return gather(table, idx)


def _experts_kernel(be_ref, nbu_ref, x_ref, wg_ref, wu_ref, wd_ref, *rest, block_off):
    y_ref, wgb, wub, wdb = rest[-4:]
    step = pl.program_id(0)
    b = block_off + step
    nbu = nbu_ref[0]

    @pl.when(b < nbu)
    def _():
        changed = jnp.logical_or(step == 0, be_ref[b] != be_ref[jnp.maximum(b - 1, 0)])

        @pl.when(changed)
        def _():
            wgb[...] = wg_ref[...].astype(BF16)
            wub[...] = wu_ref[...].astype(BF16)
            wdb[...] = wd_ref[...].astype(BF16)

        lo, hi = _unpack_bf16_pairs(x_ref[...])
        hg = _dot_halves(lo, hi, wgb)
        hu = _dot_halves(lo, hi, wub)
        h = hg * (1.0 / (1.0 + jnp.exp(-hg))) * hu
        y_ref[...] = _pack_bf16_pairs(jnp.dot(h.astype(BF16), wdb[...], preferred_element_type=F32))

    @pl.when(b >= nbu)
    def _():
        y_ref[...] = jnp.zeros_like(y_ref)


def _experts(xs, block_e, nb_used, wg, wu, wd, layer, y_prev, *, block_off, n_blocks):
    half = xs.shape[1]
    d = 2 * half
    ff = wg.shape[3]
    n_call = xs.shape[0] // MOE_ROWS

    def used(b, nbu):
        return jnp.clip(jnp.minimum(block_off + b, nbu[0] - 1) - block_off, 0, n_call - 1)

    def expert(b, be):
        return be[block_off + b]

    in_specs = [
        pl.BlockSpec((MOE_ROWS, half), lambda b, be, nbu: (used(b, nbu), 0)),
        pl.BlockSpec((None, None, d, ff), lambda b, be, nbu: (layer, expert(b, be), 0, 0)),
        pl.BlockSpec((None, None, d, ff), lambda b, be, nbu: (layer, expert(b, be), 0, 0)),
        pl.BlockSpec((None, None, ff, d), lambda b, be, nbu: (layer, expert(b, be), 0, 0)),
    ]
    args = [block_e, nb_used, xs, wg, wu, wd]
    aliases = {}
    if y_prev is not None:
        in_specs.append(pl.BlockSpec(memory_space=pl.ANY))
        aliases = {len(args): 0}
        args.append(y_prev)
    grid_spec = pltpu.PrefetchScalarGridSpec(
        num_scalar_prefetch=2,
        grid=(n_call,),
        in_specs=in_specs,
        out_specs=pl.BlockSpec((MOE_ROWS, half), lambda b, be, nbu: (block_off + b, 0)),
        scratch_shapes=[
            pltpu.VMEM((d, ff), BF16),
            pltpu.VMEM((d, ff), BF16),
            pltpu.VMEM((ff, d), BF16),
        ],
    )
    return pl.pallas_call(
        functools.partial(_experts_kernel, block_off=block_off),
        grid_spec=grid_spec,
        out_shape=jax.ShapeDtypeStruct((n_blocks * MOE_ROWS, half), jnp.int32),
        input_output_aliases=aliases,
        compiler_params=_params(("arbitrary",)),
        name="experts",
    )(*args)


def _combine_ln_kernel(*refs, alpha, emit_next, n_prev):
    y_ref, gw_ref, tok_ref, x_ref, sg_ref, su_ref, sd_ref, gate_ref, lng_ref, lnb_ref = refs[:10]
    outs = refs[len(refs) - (2 if emit_next else 1):]
    if emit_next:
        sc_ref, sh_ref = refs[10:12]
        xo_ref, u_ref = outs
    else:
        (xo_ref,) = outs
    lo, hi = _unpack_bf16_pairs(tok_ref[...])
    hg = _dot_halves(lo, hi, sg_ref)
    hu = _dot_halves(lo, hi, su_ref)
    h = hg * (1.0 / (1.0 + jnp.exp(-hg))) * hu
    gw = gw_ref[...]
    f_lo = f_hi = None
    for k in range(TOP_K):
        y_lo, y_hi = _unpack_bf16_pairs(y_ref[k])
        w = gw[:, k:k + 1]
        f_lo = y_lo.astype(F32) * w if f_lo is None else f_lo + y_lo.astype(F32) * w
        f_hi = y_hi.astype(F32) * w if f_hi is None else f_hi + y_hi.astype(F32) * w
    f = jnp.concatenate([f_lo, f_hi], axis=-1) + jnp.dot(h.astype(BF16), sd_ref[...], preferred_element_type=F32)
    xn = _layer_norm(alpha * x_ref[...] + gate_ref[...] * f, lng_ref[...], lnb_ref[...])
    xo_ref[...] = xn
    if emit_next:
        u_ref[...] = (xn * (1.0 + sc_ref[...]) + sh_ref[...]).astype(u_ref.dtype)


def _combine_ln(y3, gw_t, tok, x, sg, su, sd, mod, ln_g, ln_b, mod_next, prev, *, t, row_off, tm, alpha, mod_row):
    d = x.shape[1]
    emit_next = mod_next is not None
    off = row_off // tm
    assert off * tm == row_off

    def rows(i):
        return (off + i, 0)

    def part_mod_row(r):
        return mod_row(r + row_off)

    in_specs = [
        pl.BlockSpec((TOP_K, tm, d // 2), lambda i: (0, i, 0)),
        pl.BlockSpec((tm, SUBLANES), rows),
        pl.BlockSpec((tm, d // 2), rows),
        pl.BlockSpec((tm, d), rows),
        _const_spec(sg.shape), _const_spec(su.shape), _const_spec(sd.shape),
        _mod_spec(d, 5, part_mod_row, tm),
        _const_spec((1, d)), _const_spec((1, d)),
    ]
    args = [y3, gw_t, tok, x, sg, su, sd, mod, ln_g, ln_b]
    out_specs = [pl.BlockSpec((tm, d), rows)]
    out_shape = [jax.ShapeDtypeStruct((t, d), F32)]
    if emit_next:
        in_specs += [_mod_spec(d, 1, part_mod_row, tm), _mod_spec(d, 0, part_mod_row, tm)]
        args += [mod_next, mod_next]
        out_specs.append(pl.BlockSpec((tm, d), rows))
        out_shape.append(jax.ShapeDtypeStruct((t, d), BF16))
    aliases = {}
    if prev is not None:
        for k, p in enumerate(prev):
            in_specs.append(pl.BlockSpec(memory_space=pl.ANY))
            aliases[len(args)] = k
            args.append(p)
    return pl.pallas_call(
        functools.partial(_combine_ln_kernel, alpha=alpha, emit_next=emit_next, n_prev=len(aliases)),
        grid=(y3.shape[1] // tm,),
        in_specs=in_specs,
        out_specs=out_specs,
        out_shape=out_shape,
        input_output_aliases=aliases,
        compiler_params=_params(("parallel",)),
        name="combine_ln",
    )(*args)


def _moe(tok, x, t, layer, router_w, router_b, wg, wu, wd, sg, su, sd, mod, ln_g, ln_b, mod_next, *, alpha, mod_row,
         tm):
    half = tok.shape[1]
    tt = 512
    idx, gw, rank, cnt = _router(tok, router_w.T.astype(BF16), router_b.reshape(N_EXPERTS, 1), t=t, tt=tt)
    n_asg = t * TOP_K
    counts = cnt[:, 0].astype(jnp.int32)
    padded = (counts + MOE_ROWS - 1) // MOE_ROWS * MOE_ROWS
    pend = jnp.cumsum(padded)
    pstart = pend - padded
    sc_rows = V7X_SC_CORES * V7X_SC_SUBCORES * SC_GATHER_ROWS * 2
    blocks_granule = MOE_DISPATCH_PARTS * max(sc_rows // MOE_ROWS, 1)
    assert (blocks_granule // MOE_DISPATCH_PARTS * MOE_ROWS) % sc_rows == 0
    n_blocks = -(-((n_asg + N_EXPERTS * (MOE_ROWS - 1)) // MOE_ROWS + 1) // blocks_granule) * blocks_granule
    assert n_asg % sc_rows == 0
    block_start = jnp.arange(n_blocks, dtype=jnp.int32) * MOE_ROWS
    block_e = jnp.minimum(jnp.sum((pend[None, :] <= block_start[:, None]).astype(jnp.int32), axis=1), N_EXPERTS - 1)
    nb_used = (pend[-1] // MOE_ROWS).astype(jnp.int32).reshape(1)
    pos2d = _assign_slots(idx, rank, pstart.astype(F32).reshape(N_EXPERTS, 1), t=t, tt=tt)[:TOP_K]
    pos = pos2d.reshape(-1)
    tok_of_asg = np.tile(np.arange(t, dtype=np.int32), TOP_K)
    n_pad = n_blocks * MOE_ROWS - n_asg
    seg_pad_end = jnp.cumsum(padded - counts)
    j = jnp.arange(n_pad, dtype=jnp.int32)
    pad_e = jnp.sum((seg_pad_end[None, :] <= j[:, None]).astype(jnp.int32), axis=1)
    seg_base = pstart + counts - (seg_pad_end - (padded - counts))
    in_seg = j + jnp.sum(jnp.where(pad_e[:, None] == jnp.arange(N_EXPERTS)[None, :], seg_base[None, :], 0), axis=1)
    pad_slot = jnp.where(pad_e < N_EXPERTS, in_seg, pend[-1] + j - seg_pad_end[-1])
    _, slot_tok = lax.sort((jnp.concatenate([pos, pad_slot]), jnp.concatenate([tok_of_asg, pad_slot % t])),
                           num_keys=1)
    per = n_blocks // MOE_DISPATCH_PARTS * MOE_ROWS
    xs = [_sc_gather_rows(tok, slot_tok[i * per:(i + 1) * per]) for i in range(MOE_DISPATCH_PARTS)]
    y = None
    for i in range(MOE_DISPATCH_PARTS):
        y = _experts(xs[i], block_e, nb_used, wg, wu, wd, layer, y, block_off=i * per // MOE_ROWS, n_blocks=n_blocks)
    n_cparts = MOE_COMBINE_PARTS if (t // MOE_COMBINE_PARTS * TOP_K) % sc_rows == 0 else 1
    t_part = t // n_cparts
    y3 = [_sc_gather_rows(y, pos2d[:, i * t_part:(i + 1) * t_part].reshape(-1)).reshape(TOP_K, t_part, half)
          for i in range(n_cparts)]
    outs = None
    gw_t = gw.T
    for i in range(n_cparts):
        outs = _combine_ln(y3[i], gw_t, tok, x, sg, su, sd, mod, ln_g, ln_b, mod_next, outs, t=t,
                           row_off=i * t_part, tm=tm, alpha=alpha, mod_row=mod_row)
    return outs


def _rope64(r, c_ref, sa_ref, sb_ref):
    return r * c_ref[...] + pltpu.roll(r, LANES - QK_ROPE // 2, 1) * sa_ref[...] + pltpu.roll(r, QK_ROPE // 2, 1) * sb_ref[...]


def _mla_q_kernel(d_ref, gain_ref, w_ref, c_ref, sa_ref, sb_ref, q_ref, *, scale):
    n = _rms(d_ref[...], gain_ref[...]).astype(BF16)
    q = jnp.dot(n, w_ref[...], preferred_element_type=F32)
    for h in range(MLA_HEADS):
        lo = h * MLA_DK_PAD
        q_ref[:, lo:lo + QK_NOPE] = (q[:, lo:lo + QK_NOPE] * scale).astype(q_ref.dtype)
        r = _rope64(q[:, lo + QK_NOPE:lo + MLA_DK_PAD], c_ref, sa_ref, sb_ref)
        q_ref[:, lo + QK_NOPE:lo + MLA_DK_PAD] = (r * scale).astype(q_ref.dtype)


def _mla_kv_kernel(ckv_ref, kr_ref, gain_ref, wk_ref, wv_ref, c_ref, sa_ref, sb_ref, k_ref, v_ref):
    n = _rms(ckv_ref[...], gain_ref[...]).astype(BF16)
    kn = jnp.dot(n, wk_ref[...], preferred_element_type=F32)
    v_ref[...] = jnp.dot(n, wv_ref[...], preferred_element_type=F32).astype(v_ref.dtype)
    kr = _rope64(kr_ref[...], c_ref, sa_ref, sb_ref).astype(k_ref.dtype)
    for h in range(MLA_HEADS):
        lo = h * MLA_DK_PAD
        k_ref[:, lo:lo + QK_NOPE] = kn[:, h * QK_NOPE:(h + 1) * QK_NOPE].astype(k_ref.dtype)
        k_ref[:, lo + QK_NOPE:lo + MLA_DK_PAD] = kr


def _axial_angles(n_tok, rot_dim):
    rows = n_tok // GRID_W
    n_freq = rot_dim // 4
    inv = (ROPE_THETA ** (-np.arange(n_freq, dtype=np.float32) / n_freq)).astype(np.float32)
    row = np.repeat(np.arange(rows, dtype=np.float32), GRID_W)
    col = np.tile(np.arange(GRID_W, dtype=np.float32), rows)
    return np.concatenate([row[:, None] * inv, col[:, None] * inv], axis=-1)


def _rope_tables_128(n_tok, ident_rows):
    ang = _axial_angles(n_tok, HEAD_DIM)
    cos, sin = np.cos(ang), np.sin(ang)
    c = np.concatenate([cos, cos], axis=-1)
    s = np.concatenate([-sin, sin], axis=-1)
    c = np.concatenate([c, np.ones((ident_rows, HEAD_DIM), np.float32)], axis=0)
    s = np.concatenate([s, np.zeros((ident_rows, HEAD_DIM), np.float32)], axis=0)
    return c.astype(np.float32), s.astype(np.float32)


def _rope_tables_64(n_tok, ident_rows):
    ang = _axial_angles(n_tok, QK_ROPE)
    cos, sin = np.cos(ang), np.sin(ang)
    half = QK_ROPE // 2
    z = np.zeros((n_tok, LANES - QK_ROPE), np.float32)
    zh = np.zeros((n_tok, half), np.float32)
    c = np.concatenate([cos, cos, z], axis=-1)
    sa = np.concatenate([-sin, zh, z], axis=-1)
    sb = np.concatenate([zh, sin, z], axis=-1)
    ci = np.concatenate([np.ones((ident_rows, QK_ROPE), np.float32),
                         np.zeros((ident_rows, LANES - QK_ROPE), np.float32)], axis=-1)
    zi = np.zeros((ident_rows, LANES), np.float32)
    tables = np.concatenate([c, ci], 0), np.concatenate([sa, zi], 0), np.concatenate([sb, zi], 0)
    return tuple(tab.astype(np.float32) for tab in tables)


def kernel(x, c, ctx, c_ctx, w_ada, b_ada, ln_g, ln_b, a_w_in, a_conv_w, a_q_gain, a_k_gain, a_w_out, m_w_down, m_q_gain, m_kv_gain, m_w_uq, m_w_ukv, m_w_out, router_w, router_b, e_w_gate, e_w_up, e_w_down, s_w_gate, s_w_up, s_w_down):
    batch, seq, d = x.shape
    ctx_len = ctx.shape[1]
    depth = w_ada.shape[0]
    assert depth == 2, "one conv+GQA layer followed by one MLA layer"
    alpha = (2 * depth) ** 0.25
    t_lat = batch * seq
    t_ctx = batch * ctx_len
    t_all = t_lat + t_ctx
    tr = 256
    assert seq % tr == 0 and ctx_len % tr == 0 and seq % GRID_W == 0
    lat_tiles = t_lat // tr
    lat_seq_tiles = seq // tr
    ctx_seq_tiles = ctx_len // tr
    lk = ctx_len + seq

    def mod_row(r):
        return jnp.minimum(r // seq, batch)

    def kv_block(i):
        is_lat = i < lat_tiles
        cidx = i - lat_tiles
        b = jnp.where(is_lat, i // lat_seq_tiles, cidx // ctx_seq_tiles)
        rb = jnp.where(is_lat, ctx_seq_tiles + i % lat_seq_tiles, cidx % ctx_seq_tiles)
        return b, rb

    def pos_block(i):
        return jnp.where(i < lat_tiles, i % lat_seq_tiles, lat_seq_tiles)

    rows = -(-(batch + 1) // SUBLANES) * SUBLANES
    cond = jnp.concatenate([c, c_ctx[None, :], jnp.zeros((rows - batch - 1, d), F32)], axis=0)
    mod = _ada_table(cond, w_ada, b_ada).reshape(depth, rows, 1, 6 * d)

    x_pair = (x.reshape(t_lat, d), ctx.reshape(t_ctx, d))

    tm_in = 1024 if (t_ctx % 1024 == 0 and seq % 1024 == 0) else tr
    proj = _mod_matmul(x_pair, mod[0], a_w_in[0].astype(BF16), BF16, tm_in, 768, mod_row)

    cos128, sin128 = _rope_tables_128(seq, tr)
    d_q = ATT_HEADS * HEAD_DIM
    d_kv = ATT_KV_HEADS * HEAD_DIM
    qkv_w = d_q + 2 * d_kv
    qkv_blk = 3 * CONV_DIM // qkv_w
    assert qkv_blk * qkv_w == 3 * CONV_DIM
    q0, k0, v0 = pl.pallas_call(
        functools.partial(_qkprep_kernel, scale=1.0 / math.sqrt(HEAD_DIM)),
        grid=(t_all // tr,),
        in_specs=[
            pl.BlockSpec((tr, qkv_w), lambda i: (i, qkv_blk)),
            pl.BlockSpec((tr, HEAD_DIM), lambda i: (pos_block(i), 0)),
            pl.BlockSpec((tr, HEAD_DIM), lambda i: (pos_block(i), 0)),
            _const_spec((1, HEAD_DIM)), _const_spec((1, HEAD_DIM)),
        ],
        out_specs=[
            pl.BlockSpec((tr, d_q), lambda i: (i, 0)),
            pl.BlockSpec((None, tr, d_kv), lambda i: (*kv_block(i), 0)),
            pl.BlockSpec((None, tr, d_kv), lambda i: (*kv_block(i), 0)),
        ],
        out_shape=[
            jax.ShapeDtypeStruct((t_all, d_q), BF16),
            jax.ShapeDtypeStruct((batch, lk, d_kv), BF16),
            jax.ShapeDtypeStruct((batch, lk, d_kv), BF16),
        ],
        compiler_params=_params(("parallel",)),
        name="qk_prep",
    )(proj, cos128, sin128, a_q_gain[0].reshape(1, HEAD_DIM), a_k_gain[0].reshape(1, HEAD_DIM))

    grp = ATT_HEADS // ATT_KV_HEADS
    att_lat = _attention(q0, k0, v0, batch=batch, sq=seq, lk=lk, n_kv=ATT_KV_HEADS, group=grp, dk=HEAD_DIM,
                         dv=HEAD_DIM, tq=512 if seq % 512 == 0 else tr, rows=256, q_row_off=0)
    att_ctx = _attention(q0, k0, v0, batch=batch, sq=ctx_len, lk=ctx_len, n_kv=ATT_KV_HEADS, group=grp,
                         dk=HEAD_DIM, dv=HEAD_DIM, tq=256, rows=256, q_row_off=t_lat)

    conv0 = _conv_gate(proj, a_conv_w[0], t=t_all, tm=tr, tc=512, lat_tiles=lat_tiles,
                       lat_seq_tiles=lat_seq_tiles, ctx_seq_tiles=ctx_seq_tiles)

    w_out0 = a_w_out[0].astype(BF16)
    x1, tok0 = _outproj_ln([conv0, (att_lat, att_ctx)], [w_out0[:CONV_DIM], w_out0[CONV_DIM:]], x_pair, mod[0],
                           ln_g[0, 0].reshape(1, d), ln_b[0, 0].reshape(1, d), t=t_all, tm=tr, alpha=alpha,
                           mod_row=mod_row)

    x2, u1 = _moe(tok0, x1, t_all, 0, router_w[0], router_b[0], e_w_gate, e_w_up, e_w_down,
                  s_w_gate[0].astype(BF16), s_w_up[0].astype(BF16), s_w_down[0].astype(BF16), mod[0],
                  ln_g[0, 1].reshape(1, d), ln_b[0, 1].reshape(1, d), mod[1], alpha=alpha,
                  mod_row=mod_row, tm=256)

    n_down = Q_LORA + KV_LORA + QK_ROPE
    n_down_pad = -(-n_down // LANES) * LANES
    w_down = jnp.pad(m_w_down[0], ((0, 0), (0, n_down_pad - n_down))).astype(BF16)
    down = _matmul(u1, w_down, F32, 512, n_down_pad)

    dqk = QK_NOPE + QK_ROPE
    w_uq = m_w_uq[0].reshape(Q_LORA, MLA_HEADS, dqk)
    w_uq = jnp.pad(w_uq, ((0, 0), (0, 0), (0, MLA_DK_PAD - dqk))).reshape(Q_LORA, MLA_HEADS * MLA_DK_PAD).astype(BF16)
    w_ukv = m_w_ukv[0].reshape(KV_LORA, MLA_HEADS, QK_NOPE + V_DIM)
    w_uk = w_ukv[:, :, :QK_NOPE].reshape(KV_LORA, MLA_HEADS * QK_NOPE).astype(BF16)
    w_uv = w_ukv[:, :, QK_NOPE:].reshape(KV_LORA, MLA_HEADS * V_DIM).astype(BF16)

    c64, sa64, sb64 = _rope_tables_64(seq, tr)
    rope_specs = [pl.BlockSpec((tr, LANES), lambda i: (pos_block(i), 0))] * 3
    q1 = pl.pallas_call(
        functools.partial(_mla_q_kernel, scale=1.0 / math.sqrt(dqk)),
        grid=(lat_tiles,),
        in_specs=[
            pl.BlockSpec((tr, Q_LORA), lambda i: (i, 0)),
            _const_spec((1, Q_LORA)),
            _const_spec(w_uq.shape),
        ] + rope_specs,
        out_specs=pl.BlockSpec((tr, MLA_HEADS * MLA_DK_PAD), lambda i: (i, 0)),
        out_shape=jax.ShapeDtypeStruct((t_lat, MLA_HEADS * MLA_DK_PAD), BF16),
        compiler_params=_params(("parallel",)),
        name="mla_q",
    )(down, m_q_gain[0].reshape(1, Q_LORA), w_uq, c64, sa64, sb64)

    assert KV_LORA == Q_LORA and (Q_LORA + KV_LORA) % LANES == 0
    k1, v1 = pl.pallas_call(
        _mla_kv_kernel,
        grid=(t_all // tr,),
        in_specs=[
            pl.BlockSpec((tr, KV_LORA), lambda i: (i, 1)),
            pl.BlockSpec((tr, LANES), lambda i: (i, (Q_LORA + KV_LORA) // LANES)),
            _const_spec((1, KV_LORA)),
            _const_spec(w_uk.shape), _const_spec(w_uv.shape),
        ] + rope_specs,
        out_specs=[
            pl.BlockSpec((None, tr, MLA_HEADS * MLA_DK_PAD), lambda i: (*kv_block(i), 0)),
            pl.BlockSpec((None, tr, MLA_HEADS * V_DIM), lambda i: (*kv_block(i), 0)),
        ],
        out_shape=[
            jax.ShapeDtypeStruct((batch, lk, MLA_HEADS * MLA_DK_PAD), BF16),
            jax.ShapeDtypeStruct((batch, lk, MLA_HEADS * V_DIM), BF16),
        ],
        compiler_params=_params(("parallel",)),
        name="mla_kv",
    )(down, down, m_kv_gain[0].reshape(1, KV_LORA), w_uk, w_uv, c64, sa64, sb64)

    att1 = _attention(q1, k1, v1, batch=batch, sq=seq, lk=lk, n_kv=MLA_HEADS, group=1, dk=MLA_DK_PAD, dv=V_DIM,
                      tq=next(c for c in (2048, 1024, 512, tr) if seq % c == 0), rows=256, q_row_off=0)

    x3, tok1 = _outproj_ln([att1], [m_w_out[0].astype(BF16)], x2, mod[1], ln_g[1, 0].reshape(1, d),
                           ln_b[1, 0].reshape(1, d), t=t_lat, tm=tr, alpha=alpha, mod_row=mod_row)

    (x4,) = _moe(tok1, x3, t_lat, 1, router_w[1], router_b[1], e_w_gate, e_w_up, e_w_down,
                 s_w_gate[1].astype(BF16), s_w_up[1].astype(BF16), s_w_down[1].astype(BF16), mod[1],
                 ln_g[1, 1].reshape(1, d), ln_b[1, 1].reshape(1, d), None, alpha=alpha,
                 mod_row=mod_row, tm=256)
    return x4.reshape(batch, seq, d)
```

```python
import functools
import math

import jax
import jax.numpy as jnp
import numpy as np
from jax import lax
from jax.experimental import pallas as pl
from jax.experimental.pallas import tpu as pltpu
from jax.experimental.pallas import tpu_sc as plsc

F32 = jnp.float32
BF16 = jnp.bfloat16

GRID_W = 64
CONV_DIM = 1024
ATT_HEADS = 8
ATT_KV_HEADS = 2
HEAD_DIM = 128
MLA_HEADS = 16
Q_LORA = 512
KV_LORA = 512
QK_NOPE = 128
QK_ROPE = 64
V_DIM = 128
N_EXPERTS = 64
TOP_K = 6
N_GROUPS = 8
TOPK_GROUPS = 4
ROUTED_SCALE = 2.5
ROPE_THETA = 10000.0
LN_EPS = 1e-5
RMS_EPS = 1e-6

V7X_VMEM_LIMIT_BYTES = 56 * 1024 * 1024
LANES = 128
SUBLANES = 8
MOE_ROWS = 512
MOE_DISPATCH_PARTS = 4
MOE_COMBINE_PARTS = 2
V7X_SC_CORES = 2
V7X_SC_SUBCORES = 16
SC_GATHER_ROWS = 16
SC_GATHER_BUFFERS = 4
MLA_DK_PAD = 256

ROW_TILE = 256
ADA_COL_TILE = 1024
W_IN_ROW_TILE = 1024
W_IN_COL_TILE = 768
DOWN_ROW_TILE = 512
CONV_COL_TILE = 512
ROUTER_TILE = 512
ATTN_CHAIN_ROWS = 256
GQA_Q_TILE = 512
MLA_Q_TILES = (2048, 1024, 512)
COMBINE_ROW_TILE = 256


def _params(sem):
    return pltpu.CompilerParams(dimension_semantics=sem, vmem_limit_bytes=V7X_VMEM_LIMIT_BYTES)


def _const_spec(shape):
    nd = len(shape)
    return pl.BlockSpec(shape, lambda *_: (0,) * nd)


def _ada_kernel(s_ref, w_ref, b_ref, o_ref):
    s = s_ref[...]
    s = s * (1.0 / (1.0 + jnp.exp(-s)))
    o_ref[...] = jnp.dot(s.astype(BF16), w_ref[...].astype(BF16), preferred_element_type=F32) + b_ref[...]


def _ada_table(cond, w_ada, b_ada):
    depth, d, n = w_ada.shape
    r = cond.shape[0]
    tn = ADA_COL_TILE
    return pl.pallas_call(
        _ada_kernel,
        grid=(depth, n // tn),
        in_specs=[
            pl.BlockSpec((r, d), lambda l, j: (0, 0)),
            pl.BlockSpec((None, d, tn), lambda l, j: (l, 0, j)),
            pl.BlockSpec((None, 1, tn), lambda l, j: (l, 0, j)),
        ],
        out_specs=pl.BlockSpec((None, r, tn), lambda l, j: (l, 0, j)),
        out_shape=jax.ShapeDtypeStruct((depth, r, n), F32),
        compiler_params=_params(("parallel", "parallel")),
        name="ada_table",
    )(cond, w_ada, b_ada.reshape(depth, 1, n))


def _mod_spec(d, chunk, mod_row, tm):
    return pl.BlockSpec((None, 1, d), lambda i: (mod_row(i * tm), 0, chunk))


def _mm_kernel(a_ref, w_ref, o_ref):
    o_ref[...] = jnp.dot(a_ref[...], w_ref[...], preferred_element_type=F32).astype(o_ref.dtype)


def _matmul(a, w, out_dtype, tm, tn):
    m, k = a.shape
    n = w.shape[1]
    return pl.pallas_call(
        _mm_kernel,
        grid=(m // tm, n // tn),
        in_specs=[
            pl.BlockSpec((tm, k), lambda i, j: (i, 0)),
            pl.BlockSpec((k, tn), lambda i, j: (0, j)),
        ],
        out_specs=pl.BlockSpec((tm, tn), lambda i, j: (i, j)),
        out_shape=jax.ShapeDtypeStruct((m, n), out_dtype),
        compiler_params=_params(("parallel", "parallel")),
        name="matmul",
    )(a, w)


def _pair_specs(pair, tm):
    lat, ctx = pair
    lat_tiles = lat.shape[0] // tm
    assert lat_tiles * tm == lat.shape[0] and ctx.shape[0] % tm == 0 and lat.shape[1] == ctx.shape[1]
    width = lat.shape[1]
    return [pl.BlockSpec((tm, width), lambda i, *_: (jnp.minimum(i, lat_tiles - 1), 0)),
            pl.BlockSpec((tm, width), lambda i, *_: (jnp.maximum(i - lat_tiles, 0), 0))]


def _pair_tile(lat_ref, ctx_ref, lat_tiles):
    return jnp.where(pl.program_id(0) < lat_tiles, lat_ref[...], ctx_ref[...])


def _mod_mm_kernel(xl_ref, xc_ref, sc_ref, sh_ref, w_ref, o_ref, u_ref, *, lat_tiles):
    @pl.when(pl.program_id(1) == 0)
    def _():
        x = _pair_tile(xl_ref, xc_ref, lat_tiles)
        u_ref[...] = (x * (1.0 + sc_ref[...]) + sh_ref[...]).astype(u_ref.dtype)

    o_ref[...] = jnp.dot(u_ref[...], w_ref[...], preferred_element_type=F32).astype(o_ref.dtype)


def _mod_matmul(x_pair, mod, w, out_dtype, tm, tn, mod_row):
    m = x_pair[0].shape[0] + x_pair[1].shape[0]
    k = x_pair[0].shape[1]
    n = w.shape[1]

    def mod_spec(chunk):
        return pl.BlockSpec((None, 1, k), lambda i, j: (mod_row(i * tm), 0, chunk))

    return pl.pallas_call(
        functools.partial(_mod_mm_kernel, lat_tiles=x_pair[0].shape[0] // tm),
        grid=(m // tm, n // tn),
        in_specs=_pair_specs(x_pair, tm) + [
            mod_spec(1), mod_spec(0),
            pl.BlockSpec((k, tn), lambda i, j: (0, j)),
        ],
        out_specs=pl.BlockSpec((tm, tn), lambda i, j: (i, j)),
        out_shape=jax.ShapeDtypeStruct((m, n), out_dtype),
        scratch_shapes=[pltpu.VMEM((tm, k), w.dtype)],
        compiler_params=_params(("parallel", "arbitrary")),
        name="mod_matmul",
    )(*x_pair, mod, mod, w)


def _rms(t, gain):
    return t * lax.rsqrt(jnp.mean(t * t, axis=-1, keepdims=True) + RMS_EPS) * gain


def _qkprep_kernel(p_ref, cos_ref, sin_ref, qg_ref, kg_ref, q_ref, k_ref, v_ref, *, scale):
    cos = cos_ref[...]
    sin = sin_ref[...]

    def norm_rope(t, gain):
        y = _rms(t.astype(F32), gain)
        return y * cos + pltpu.roll(y, HEAD_DIM // 2, 1) * sin

    for h in range(ATT_HEADS):
        sl = slice(h * HEAD_DIM, (h + 1) * HEAD_DIM)
        q_ref[:, sl] = (norm_rope(p_ref[:, sl], qg_ref[...]) * scale).astype(q_ref.dtype)
    k0 = ATT_HEADS * HEAD_DIM
    for h in range(ATT_KV_HEADS):
        sl = slice(h * HEAD_DIM, (h + 1) * HEAD_DIM)
        k_ref[:, sl] = norm_rope(p_ref[:, k0 + h * HEAD_DIM:k0 + (h + 1) * HEAD_DIM], kg_ref[...]).astype(k_ref.dtype)
    v0 = k0 + ATT_KV_HEADS * HEAD_DIM
    v_ref[...] = p_ref[:, v0:v0 + ATT_KV_HEADS * HEAD_DIM].astype(v_ref.dtype)


def _attn_kernel(q_ref, k_ref, v_ref, o_ref, *, group, tq, rows, dk, dv):
    k = k_ref[...]
    v = v_ref[...]
    for h in range(group):
        for r in range(0, tq, rows):
            q = q_ref[r:r + rows, h * dk:(h + 1) * dk]
            s = lax.dot_general(q, k, (((1,), (1,)), ((), ())), preferred_element_type=F32)
            m = jnp.max(s, axis=-1, keepdims=True)
            p = jnp.exp(s - m)
            l = jnp.sum(p, axis=-1, keepdims=True)
            o = jnp.dot(p.astype(v.dtype), v, preferred_element_type=F32)
            o_ref[r:r + rows, h * dv:(h + 1) * dv] = (o / l).astype(o_ref.dtype)


def _attention(q, k, v, *, batch, sq, lk, n_kv, group, dk, dv, tq, rows, q_row_off):
    nq = sq // tq
    off = q_row_off // tq
    assert tq % rows == 0 and q_row_off % tq == 0 and sq % tq == 0
    return pl.pallas_call(
        functools.partial(_attn_kernel, group=group, tq=tq, rows=rows, dk=dk, dv=dv),
        grid=(batch, n_kv, nq),
        in_specs=[
            pl.BlockSpec((tq, group * dk), lambda b, g, i: (off + b * nq + i, g)),
            pl.BlockSpec((None, lk, dk), lambda b, g, i: (b, 0, g)),
            pl.BlockSpec((None, lk, dv), lambda b, g, i: (b, 0, g)),
        ],
        out_specs=pl.BlockSpec((tq, group * dv), lambda b, g, i: (b * nq + i, g)),
        out_shape=jax.ShapeDtypeStruct((batch * sq, n_kv * group * dv), BF16),
        compiler_params=_params(("parallel", "parallel", "parallel")),
        name="attention",
    )(q, k, v)


def _conv_kernel(gb_ref, gc_ref, hv_ref, gcp_ref, hvp_ref, gcn_ref, hvn_ref, w_ref, o_ref, *,
                 tm, lat_tiles, lat_seq_tiles, ctx_seq_tiles):
    i = pl.program_id(0)
    is_lat = i < lat_tiles
    pos = jnp.where(is_lat, i % lat_seq_tiles, (i - lat_tiles) % ctx_seq_tiles)
    seq_tiles = jnp.where(is_lat, lat_seq_tiles, ctx_seq_tiles)
    not_first = (pos != 0).astype(F32)
    not_last = (pos != seq_tiles - 1).astype(F32)
    p = gc_ref[...].astype(F32) * hv_ref[...].astype(F32)
    halo_prev = gcp_ref[SUBLANES - 1:SUBLANES, :].astype(F32) * hvp_ref[SUBLANES - 1:SUBLANES, :].astype(F32) * not_first
    halo_next = gcn_ref[0:1, :].astype(F32) * hvn_ref[0:1, :].astype(F32) * not_last
    row = lax.broadcasted_iota(jnp.int32, p.shape, 0)
    prev = jnp.where(row == 0, halo_prev, pltpu.roll(p, 1, 0))
    nxt = jnp.where(row == tm - 1, halo_next, pltpu.roll(p, tm - 1, 0))
    w = w_ref[...]
    conv = w[0:1, :] * prev + w[1:2, :] * p + w[2:3, :] * nxt
    o_ref[...] = (gb_ref[...].astype(F32) * conv).astype(o_ref.dtype)


def _conv_gate(p, conv_w, *, t, tm, tc, lat_tiles, lat_seq_tiles, ctx_seq_tiles):
    nct = CONV_DIM // tc
    hb = tm // SUBLANES
    n_halo = t // SUBLANES

    def cur(part):
        return pl.BlockSpec((tm, tc), lambda i, j: (i, part * nct + j))

    def prev(part):
        return pl.BlockSpec((SUBLANES, tc), lambda i, j: (jnp.maximum(i * hb - 1, 0), part * nct + j))

    def nxt(part):
        return pl.BlockSpec((SUBLANES, tc), lambda i, j: (jnp.minimum((i + 1) * hb, n_halo - 1), part * nct + j))

    return pl.pallas_call(
        functools.partial(_conv_kernel, tm=tm, lat_tiles=lat_tiles, lat_seq_tiles=lat_seq_tiles,
                          ctx_seq_tiles=ctx_seq_tiles),
        grid=(t // tm, nct),
        in_specs=[cur(0), cur(1), cur(2), prev(1), prev(2), nxt(1), nxt(2),
                  pl.BlockSpec((3, tc), lambda i, j: (0, j))],
        out_specs=pl.BlockSpec((tm, tc), lambda i, j: (i, j)),
        out_shape=jax.ShapeDtypeStruct((t, CONV_DIM), BF16),
        compiler_params=_params(("parallel", "parallel")),
        name="conv_gate",
    )(p, p, p, p, p, p, p, conv_w)


def _layer_norm(z, g, b):
    mu = jnp.mean(z, axis=-1, keepdims=True)
    zc = z - mu
    var = jnp.mean(zc * zc, axis=-1, keepdims=True)
    return zc * lax.rsqrt(var + LN_EPS) * g + b


def _pack_bf16_pairs(x):
    half = x.shape[1] // 2
    lo = lax.bitcast_convert_type(x[:, :half].astype(BF16).astype(F32), jnp.uint32) >> 16
    hi = lax.bitcast_convert_type(x[:, half:].astype(BF16).astype(F32), jnp.uint32) & jnp.uint32(0xFFFF0000)
    return lax.bitcast_convert_type(lo | hi, jnp.int32)


def _unpack_bf16_pairs(w):
    u = lax.bitcast_convert_type(w, jnp.uint32)
    lo = lax.bitcast_convert_type(u << 16, F32).astype(BF16)
    hi = lax.bitcast_convert_type(u & jnp.uint32(0xFFFF0000), F32).astype(BF16)
    return lo, hi


def _dot_halves(lo, hi, w_ref):
    half = lo.shape[1]
    return (jnp.dot(lo, w_ref[:half, :], preferred_element_type=F32)
            + jnp.dot(hi, w_ref[half:, :], preferred_element_type=F32))


def _outproj_ln_kernel(*refs, widths, lat_tiles, alpha):
    refs = list(refs)

    def take(width):
        got = [refs.pop(0) for _ in range(width)]
        return got[0][...] if width == 1 else _pair_tile(got[0], got[1], lat_tiles)

    acts = [take(w) for w in widths[:-1]]
    w_refs = [refs.pop(0) for _ in acts]
    x = take(widths[-1])
    gate_ref, lng_ref, lnb_ref, sc_ref, sh_ref, xo_ref, tok_ref = refs
    y = jnp.dot(acts[0], w_refs[0][...], preferred_element_type=F32)
    for a, w_ref in zip(acts[1:], w_refs[1:]):
        y = y + jnp.dot(a, w_ref[...], preferred_element_type=F32)
    xn = _layer_norm(alpha * x + gate_ref[...] * y, lng_ref[...], lnb_ref[...])
    xo_ref[...] = xn
    tok_ref[...] = _pack_bf16_pairs(xn * (1.0 + sc_ref[...]) + sh_ref[...])


def _outproj_ln(a_list, w_list, x, mod, ln_g, ln_b, *, t, tm, alpha, mod_row):
    d = w_list[0].shape[1]
    operands, in_specs, widths, lat_tiles = [], [], [], 0

    def add_rows(src):
        nonlocal lat_tiles
        if isinstance(src, tuple):
            in_specs.extend(_pair_specs(src, tm))
            operands.extend(src)
            widths.append(2)
            lat_tiles = src[0].shape[0] // tm
        else:
            in_specs.append(pl.BlockSpec((tm, src.shape[1]), lambda i: (i, 0)))
            operands.append(src)
            widths.append(1)

    for a in a_list:
        add_rows(a)
    in_specs += [_const_spec(w.shape) for w in w_list]
    operands += list(w_list)
    add_rows(x)
    in_specs += [
        _mod_spec(d, 2, mod_row, tm),
        _const_spec((1, d)), _const_spec((1, d)),
        _mod_spec(d, 4, mod_row, tm),
        _mod_spec(d, 3, mod_row, tm),
    ]
    return pl.pallas_call(
        functools.partial(_outproj_ln_kernel, widths=tuple(widths), lat_tiles=lat_tiles, alpha=alpha),
        grid=(t // tm,),
        in_specs=in_specs,
        out_specs=[pl.BlockSpec((tm, d), lambda i: (i, 0)), pl.BlockSpec((tm, d // 2), lambda i: (i, 0))],
        out_shape=[jax.ShapeDtypeStruct((t, d), F32), jax.ShapeDtypeStruct((t, d // 2), jnp.int32)],
        compiler_params=_params(("parallel",)),
        name="outproj_ln",
    )(*operands, mod, ln_g, ln_b, mod, mod)


def _router_kernel(t_ref, rw_ref, rb_ref, tri_ref, idx_ref, gw_ref, rank_ref, cnt_ref):
    @pl.when(pl.program_id(0) == 0)
    def _():
        cnt_ref[...] = jnp.zeros_like(cnt_ref)

    lo, hi = _unpack_bf16_pairs(t_ref[...])
    half = lo.shape[1]
    nt = (((1,), (1,)), ((), ()))
    logits = (lax.dot_general(rw_ref[:, :half], lo, nt, preferred_element_type=F32)
              + lax.dot_general(rw_ref[:, half:], hi, nt, preferred_element_type=F32))
    scores = 1.0 / (1.0 + jnp.exp(-logits))
    sel = scores + rb_ref[...]
    gsz = N_EXPERTS // N_GROUPS
    neg = -jnp.inf
    sub = lax.broadcasted_iota(jnp.int32, (gsz, sel.shape[1]), 0)
    slabs = [sel[g * gsz:(g + 1) * gsz, :] for g in range(N_GROUPS)]
    gscore = []
    for s in slabs:
        m1 = jnp.max(s, axis=0, keepdims=True)
        a1 = jnp.min(jnp.where(s == m1, sub, gsz), axis=0, keepdims=True)
        m2 = jnp.max(jnp.where(sub == a1, neg, s), axis=0, keepdims=True)
        gscore.append(m1 + m2)
    masked = []
    for g in range(N_GROUPS):
        ahead = jnp.zeros(gscore[g].shape, jnp.int32)
        for h in range(N_GROUPS):
            if h == g:
                continue
            beats = gscore[h] >= gscore[g] if h < g else gscore[h] > gscore[g]
            ahead = ahead + beats.astype(jnp.int32)
        masked.append(jnp.where(ahead < TOPK_GROUPS, slabs[g], neg))
    cur = jnp.concatenate(masked, axis=0)
    eio = lax.broadcasted_iota(jnp.int32, cur.shape, 0)
    picks, weights = [], []
    for _ in range(TOP_K):
        m = jnp.max(cur, axis=0, keepdims=True)
        a = jnp.min(jnp.where(cur == m, eio, N_EXPERTS), axis=0, keepdims=True)
        hit = eio == a
        picks.append(a)
        weights.append(jnp.sum(jnp.where(hit, scores, 0.0), axis=0, keepdims=True))
        cur = jnp.where(hit, neg, cur)
    total = weights[0]
    for w in weights[1:]:
        total = total + w
    for k in range(TOP_K):
        idx_ref[k:k + 1, :] = picks[k]
        gw_ref[k:k + 1, :] = weights[k] / total * ROUTED_SCALE
    for k in range(TOP_K, SUBLANES):
        idx_ref[k:k + 1, :] = jnp.zeros_like(picks[0])
        gw_ref[k:k + 1, :] = jnp.zeros_like(weights[0])
        rank_ref[k:k + 1, :] = jnp.zeros_like(picks[0])
    base = cnt_ref[:, 0:1]
    for k in range(TOP_K):
        onehot = jnp.where(eio == picks[k], 1.0, 0.0)
        before = jnp.dot(onehot.astype(BF16), tri_ref[...], preferred_element_type=F32)
        rank_ref[k:k + 1, :] = jnp.sum(onehot * (before + base), axis=0, keepdims=True).astype(jnp.int32)
        base = base + jnp.sum(onehot, axis=1, keepdims=True)
    cnt_ref[...] = jnp.broadcast_to(base, cnt_ref.shape)


def _router(tok, rw_t, rb, *, t, tt):
    half = tok.shape[1]
    tri = jnp.asarray(np.arange(tt)[:, None] < np.arange(tt)[None, :], BF16)
    blk = pl.BlockSpec((SUBLANES, tt), lambda i: (0, i))
    return pl.pallas_call(
        _router_kernel,
        grid=(t // tt,),
        in_specs=[
            pl.BlockSpec((tt, half), lambda i: (i, 0)),
            _const_spec((N_EXPERTS, 2 * half)),
            _const_spec((N_EXPERTS, 1)),
            _const_spec((tt, tt)),
        ],
        out_specs=[blk, blk, blk, _const_spec((N_EXPERTS, LANES))],
        out_shape=[jax.ShapeDtypeStruct((SUBLANES, t), jnp.int32), jax.ShapeDtypeStruct((SUBLANES, t), F32),
                   jax.ShapeDtypeStruct((SUBLANES, t), jnp.int32), jax.ShapeDtypeStruct((N_EXPERTS, LANES), F32)],
        compiler_params=_params(("arbitrary",)),
        name="router",
    )(tok, rw_t, rb, tri)


def _slots_kernel(idx_ref, rank_ref, start_ref, pos_ref):
    start = start_ref[...]
    eio = lax.broadcasted_iota(jnp.int32, (N_EXPERTS, idx_ref.shape[1]), 0)
    for k in range(TOP_K):
        seg = jnp.sum(jnp.where(eio == idx_ref[k:k + 1, :], start, 0.0), axis=0, keepdims=True)
        pos_ref[k:k + 1, :] = rank_ref[k:k + 1, :] + seg.astype(jnp.int32)
    for k in range(TOP_K, SUBLANES):
        pos_ref[k:k + 1, :] = jnp.zeros((1, idx_ref.shape[1]), jnp.int32)


def _assign_slots(idx, rank, seg_start, *, t, tt):
    blk = pl.BlockSpec((SUBLANES, tt), lambda i: (0, i))
    return pl.pallas_call(
        _slots_kernel,
        grid=(t // tt,),
        in_specs=[blk, blk, _const_spec((N_EXPERTS, 1))],
        out_specs=blk,
        out_shape=jax.ShapeDtypeStruct((SUBLANES, t), jnp.int32),
        compiler_params=_params(("parallel",)),
        name="assign_slots",
    )(idx, rank, seg_start)


def _sc_gather_rows(table, idx):
    n = idx.shape[0]
    d = table.shape[1]
    n_workers = V7X_SC_CORES * V7X_SC_SUBCORES
    per_w = n // n_workers
    n_chunks = per_w // SC_GATHER_ROWS
    assert per_w * n_workers == n and n_chunks * SC_GATHER_ROWS == per_w
    n_buf = next(b for b in range(SC_GATHER_BUFFERS, 1, -1) if n_chunks % b == 0)
    mesh = plsc.VectorSubcoreMesh(core_axis_name="c", subcore_axis_name="s", num_cores=V7X_SC_CORES,
                                  num_subcores=V7X_SC_SUBCORES)

    @functools.partial(
        pl.kernel,
        out_type=jax.ShapeDtypeStruct((n, d), table.dtype),
        mesh=mesh,
        scratch_types=[
            pltpu.VMEM((per_w,), jnp.int32),
            pltpu.VMEM((n_buf, SC_GATHER_ROWS, d), table.dtype),
            pltpu.SemaphoreType.DMA((n_buf,)),
            pltpu.SemaphoreType.DMA((n_buf,)),
        ],
        name="sc_gather_rows",
    )
    def gather(table_hbm, idx_hbm, out_hbm, idx_v, rows_v, gsem, wsem):
        wid = lax.axis_index("s") * V7X_SC_CORES + lax.axis_index("c")
        base = wid * per_w
        pltpu.sync_copy(idx_hbm.at[pl.ds(base, per_w)], idx_v)

        def gather_copy(c, b):
            return pltpu.make_async_copy(table_hbm.at[idx_v.at[pl.ds(c * SC_GATHER_ROWS, SC_GATHER_ROWS)]],
                                         rows_v.at[b], gsem.at[b])

        def write_copy(c, b):
            return pltpu.make_async_copy(rows_v.at[b], out_hbm.at[pl.ds(base + c * SC_GATHER_ROWS, SC_GATHER_ROWS)],
                                         wsem.at[b])

        for b in range(n_buf - 1):
            gather_copy(b, b).start()

        @pl.loop(0, n_chunks, step=n_buf)
        def _(g):
            for b in range(n_buf):
                c = g + b
                prev = (b + n_buf - 1) % n_buf
                gather_copy(c, b).wait()
                write_copy(c, b).start()

                @pl.when(c >= 1)
                def _():
                    write_copy(c - 1, prev).wait()

                @pl.when(c + n_buf - 1 < n_chunks)
                def _():
                    gather_copy(c + n_buf - 1, prev).start()

        write_copy(n_chunks - 1, (n_chunks - 1) % n_buf).wait()

    return gather(table, idx)


def _experts_kernel(be_ref, nbu_ref, x_ref, wg_ref, wu_ref, wd_ref, *rest, block_off):
    y_ref, wgb, wub, wdb = rest[-4:]
    step = pl.program_id(0)
    b = block_off + step
    nbu = nbu_ref[0]

    @pl.when(b < nbu)
    def _():
        changed = jnp.logical_or(step == 0, be_ref[b] != be_ref[jnp.maximum(b - 1, 0)])

        @pl.when(changed)
        def _():
            wgb[...] = wg_ref[...].astype(BF16)
            wub[...] = wu_ref[...].astype(BF16)
            wdb[...] = wd_ref[...].astype(BF16)

        lo, hi = _unpack_bf16_pairs(x_ref[...])
        hg = _dot_halves(lo, hi, wgb)
        hu = _dot_halves(lo, hi, wub)
        h = hg * (1.0 / (1.0 + jnp.exp(-hg))) * hu
        y_ref[...] = _pack_bf16_pairs(jnp.dot(h.astype(BF16), wdb[...], preferred_element_type=F32))

    @pl.when(b >= nbu)
    def _():
        y_ref[...] = jnp.zeros_like(y_ref)


def _experts(xs, block_e, nb_used, wg, wu, wd, layer, y_prev, *, block_off, n_blocks):
    half = xs.shape[1]
    d = 2 * half
    ff = wg.shape[3]
    n_call = xs.shape[0] // MOE_ROWS

    def used(b, nbu):
        return jnp.clip(jnp.minimum(block_off + b, nbu[0] - 1) - block_off, 0, n_call - 1)

    def expert(b, be):
        return be[block_off + b]

    in_specs = [
        pl.BlockSpec((MOE_ROWS, half), lambda b, be, nbu: (used(b, nbu), 0)),
        pl.BlockSpec((None, None, d, ff), lambda b, be, nbu: (layer, expert(b, be), 0, 0)),
        pl.BlockSpec((None, None, d, ff), lambda b, be, nbu: (layer, expert(b, be), 0, 0)),
        pl.BlockSpec((None, None, ff, d), lambda b, be, nbu: (layer, expert(b, be), 0, 0)),
    ]
    args = [block_e, nb_used, xs, wg, wu, wd]
    aliases = {}
    if y_prev is not None:
        in_specs.append(pl.BlockSpec(memory_space=pl.ANY))
        aliases = {len(args): 0}
        args.append(y_prev)
    grid_spec = pltpu.PrefetchScalarGridSpec(
        num_scalar_prefetch=2,
        grid=(n_call,),
        in_specs=in_specs,
        out_specs=pl.BlockSpec((MOE_ROWS, half), lambda b, be, nbu: (block_off + b, 0)),
        scratch_shapes=[
            pltpu.VMEM((d, ff), BF16),
            pltpu.VMEM((d, ff), BF16),
            pltpu.VMEM((ff, d), BF16),
        ],
    )
    return pl.pallas_call(
        functools.partial(_experts_kernel, block_off=block_off),
        grid_spec=grid_spec,
        out_shape=jax.ShapeDtypeStruct((n_blocks * MOE_ROWS, half), jnp.int32),
        input_output_aliases=aliases,
        compiler_params=_params(("arbitrary",)),
        name="experts",
    )(*args)


def _combine_ln_kernel(*refs, alpha, emit_next, n_prev):
    y_ref, gw_ref, tok_ref, x_ref, sg_ref, su_ref, sd_ref, gate_ref, lng_ref, lnb_ref = refs[:10]
    outs = refs[len(refs) - (2 if emit_next else 1):]
    if emit_next:
        sc_ref, sh_ref = refs[10:12]
        xo_ref, u_ref = outs
    else:
        (xo_ref,) = outs
    lo, hi = _unpack_bf16_pairs(tok_ref[...])
    hg = _dot_halves(lo, hi, sg_ref)
    hu = _dot_halves(lo, hi, su_ref)
    h = hg * (1.0 / (1.0 + jnp.exp(-hg))) * hu
    gw = gw_ref[...]
    f_lo = f_hi = None
    for k in range(TOP_K):
        y_lo, y_hi = _unpack_bf16_pairs(y_ref[k])
        w = gw[:, k:k + 1]
        f_lo = y_lo.astype(F32) * w if f_lo is None else f_lo + y_lo.astype(F32) * w
        f_hi = y_hi.astype(F32) * w if f_hi is None else f_hi + y_hi.astype(F32) * w
    f = jnp.concatenate([f_lo, f_hi], axis=-1) + jnp.dot(h.astype(BF16), sd_ref[...], preferred_element_type=F32)
    xn = _layer_norm(alpha * x_ref[...] + gate_ref[...] * f, lng_ref[...], lnb_ref[...])
    xo_ref[...] = xn
    if emit_next:
        u_ref[...] = (xn * (1.0 + sc_ref[...]) + sh_ref[...]).astype(u_ref.dtype)


def _combine_ln(y3, gw_t, tok, x, sg, su, sd, mod, ln_g, ln_b, mod_next, prev, *, t, row_off, tm, alpha, mod_row):
    d = x.shape[1]
    emit_next = mod_next is not None
    off = row_off // tm
    assert off * tm == row_off

    def rows(i):
        return (off + i, 0)

    def part_mod_row(r):
        return mod_row(r + row_off)

    in_specs = [
        pl.BlockSpec((TOP_K, tm, d // 2), lambda i: (0, i, 0)),
        pl.BlockSpec((tm, SUBLANES), rows),
        pl.BlockSpec((tm, d // 2), rows),
        pl.BlockSpec((tm, d), rows),
        _const_spec(sg.shape), _const_spec(su.shape), _const_spec(sd.shape),
        _mod_spec(d, 5, part_mod_row, tm),
        _const_spec((1, d)), _const_spec((1, d)),
    ]
    args = [y3, gw_t, tok, x, sg, su, sd, mod, ln_g, ln_b]
    out_specs = [pl.BlockSpec((tm, d), rows)]
    out_shape = [jax.ShapeDtypeStruct((t, d), F32)]
    if emit_next:
        in_specs += [_mod_spec(d, 1, part_mod_row, tm), _mod_spec(d, 0, part_mod_row, tm)]
        args += [mod_next, mod_next]
        out_specs.append(pl.BlockSpec((tm, d), rows))
        out_shape.append(jax.ShapeDtypeStruct((t, d), BF16))
    aliases = {}
    if prev is not None:
        for k, p in enumerate(prev):
            in_specs.append(pl.BlockSpec(memory_space=pl.ANY))
            aliases[len(args)] = k
            args.append(p)
    return pl.pallas_call(
        functools.partial(_combine_ln_kernel, alpha=alpha, emit_next=emit_next, n_prev=len(aliases)),
        grid=(y3.shape[1] // tm,),
        in_specs=in_specs,
        out_specs=out_specs,
        out_shape=out_shape,
        input_output_aliases=aliases,
        compiler_params=_params(("parallel",)),
        name="combine_ln",
    )(*args)


def _moe(tok, x, t, layer, router_w, router_b, wg, wu, wd, sg, su, sd, mod, ln_g, ln_b, mod_next, *, alpha, mod_row,
         tm):
    half = tok.shape[1]
    tt = ROUTER_TILE
    idx, gw, rank, cnt = _router(tok, router_w.T.astype(BF16), router_b.reshape(N_EXPERTS, 1), t=t, tt=tt)
    n_asg = t * TOP_K
    counts = cnt[:, 0].astype(jnp.int32)
    padded = (counts + MOE_ROWS - 1) // MOE_ROWS * MOE_ROWS
    pend = jnp.cumsum(padded)
    pstart = pend - padded
    sc_rows = V7X_SC_CORES * V7X_SC_SUBCORES * SC_GATHER_ROWS * 2
    blocks_granule = MOE_DISPATCH_PARTS * max(sc_rows // MOE_ROWS, 1)
    assert (blocks_granule // MOE_DISPATCH_PARTS * MOE_ROWS) % sc_rows == 0
    n_blocks = -(-((n_asg + N_EXPERTS * (MOE_ROWS - 1)) // MOE_ROWS + 1) // blocks_granule) * blocks_granule
    assert n_asg % sc_rows == 0
    block_start = jnp.arange(n_blocks, dtype=jnp.int32) * MOE_ROWS
    block_e = jnp.minimum(jnp.sum((pend[None, :] <= block_start[:, None]).astype(jnp.int32), axis=1), N_EXPERTS - 1)
    nb_used = (pend[-1] // MOE_ROWS).astype(jnp.int32).reshape(1)
    pos2d = _assign_slots(idx, rank, pstart.astype(F32).reshape(N_EXPERTS, 1), t=t, tt=tt)[:TOP_K]
    pos = pos2d.reshape(-1)
    tok_of_asg = np.tile(np.arange(t, dtype=np.int32), TOP_K)
    n_pad = n_blocks * MOE_ROWS - n_asg
    seg_pad_end = jnp.cumsum(padded - counts)
    j = jnp.arange(n_pad, dtype=jnp.int32)
    pad_e = jnp.sum((seg_pad_end[None, :] <= j[:, None]).astype(jnp.int32), axis=1)
    seg_base = pstart + counts - (seg_pad_end - (padded - counts))
    in_seg = j + jnp.sum(jnp.where(pad_e[:, None] == jnp.arange(N_EXPERTS)[None, :], seg_base[None, :], 0), axis=1)
    pad_slot = jnp.where(pad_e < N_EXPERTS, in_seg, pend[-1] + j - seg_pad_end[-1])
    _, slot_tok = lax.sort((jnp.concatenate([pos, pad_slot]), jnp.concatenate([tok_of_asg, pad_slot % t])),
                           num_keys=1)
    per = n_blocks // MOE_DISPATCH_PARTS * MOE_ROWS
    xs = [_sc_gather_rows(tok, slot_tok[i * per:(i + 1) * per]) for i in range(MOE_DISPATCH_PARTS)]
    y = None
    for i in range(MOE_DISPATCH_PARTS):
        y = _experts(xs[i], block_e, nb_used, wg, wu, wd, layer, y, block_off=i * per // MOE_ROWS, n_blocks=n_blocks)
    n_cparts = MOE_COMBINE_PARTS if (t // MOE_COMBINE_PARTS * TOP_K) % sc_rows == 0 else 1
    t_part = t // n_cparts
    y3 = [_sc_gather_rows(y, pos2d[:, i * t_part:(i + 1) * t_part].reshape(-1)).reshape(TOP_K, t_part, half)
          for i in range(n_cparts)]
    outs = None
    gw_t = gw.T
    for i in range(n_cparts):
        outs = _combine_ln(y3[i], gw_t, tok, x, sg, su, sd, mod, ln_g, ln_b, mod_next, outs, t=t,
                           row_off=i * t_part, tm=tm, alpha=alpha, mod_row=mod_row)
    return outs


def _rope64(r, c_ref, sa_ref, sb_ref):
    return r * c_ref[...] + pltpu.roll(r, LANES - QK_ROPE // 2, 1) * sa_ref[...] + pltpu.roll(r, QK_ROPE // 2, 1) * sb_ref[...]


def _mla_q_kernel(d_ref, gain_ref, w_ref, c_ref, sa_ref, sb_ref, q_ref, *, scale):
    n = _rms(d_ref[...], gain_ref[...]).astype(BF16)
    q = jnp.dot(n, w_ref[...], preferred_element_type=F32)
    for h in range(MLA_HEADS):
        lo = h * MLA_DK_PAD
        q_ref[:, lo:lo + QK_NOPE] = (q[:, lo:lo + QK_NOPE] * scale).astype(q_ref.dtype)
        r = _rope64(q[:, lo + QK_NOPE:lo + MLA_DK_PAD], c_ref, sa_ref, sb_ref)
        q_ref[:, lo + QK_NOPE:lo + MLA_DK_PAD] = (r * scale).astype(q_ref.dtype)


def _mla_kv_kernel(ckv_ref, kr_ref, gain_ref, wk_ref, wv_ref, c_ref, sa_ref, sb_ref, k_ref, v_ref):
    n = _rms(ckv_ref[...], gain_ref[...]).astype(BF16)
    kn = jnp.dot(n, wk_ref[...], preferred_element_type=F32)
    v_ref[...] = jnp.dot(n, wv_ref[...], preferred_element_type=F32).astype(v_ref.dtype)
    kr = _rope64(kr_ref[...], c_ref, sa_ref, sb_ref).astype(k_ref.dtype)
    for h in range(MLA_HEADS):
        lo = h * MLA_DK_PAD
        k_ref[:, lo:lo + QK_NOPE] = kn[:, h * QK_NOPE:(h + 1) * QK_NOPE].astype(k_ref.dtype)
        k_ref[:, lo + QK_NOPE:lo + MLA_DK_PAD] = kr


def _axial_angles(n_tok, rot_dim):
    rows = n_tok // GRID_W
    n_freq = rot_dim // 4
    inv = (ROPE_THETA ** (-np.arange(n_freq, dtype=np.float32) / n_freq)).astype(np.float32)
    row = np.repeat(np.arange(rows, dtype=np.float32), GRID_W)
    col = np.tile(np.arange(GRID_W, dtype=np.float32), rows)
    return np.concatenate([row[:, None] * inv, col[:, None] * inv], axis=-1)


def _rope_tables_128(n_tok, ident_rows):
    ang = _axial_angles(n_tok, HEAD_DIM)
    cos, sin = np.cos(ang), np.sin(ang)
    c = np.concatenate([cos, cos], axis=-1)
    s = np.concatenate([-sin, sin], axis=-1)
    c = np.concatenate([c, np.ones((ident_rows, HEAD_DIM), np.float32)], axis=0)
    s = np.concatenate([s, np.zeros((ident_rows, HEAD_DIM), np.float32)], axis=0)
    return c.astype(np.float32), s.astype(np.float32)


def _rope_tables_64(n_tok, ident_rows):
    ang = _axial_angles(n_tok, QK_ROPE)
    cos, sin = np.cos(ang), np.sin(ang)
    half = QK_ROPE // 2
    z = np.zeros((n_tok, LANES - QK_ROPE), np.float32)
    zh = np.zeros((n_tok, half), np.float32)
    c = np.concatenate([cos, cos, z], axis=-1)
    sa = np.concatenate([-sin, zh, z], axis=-1)
    sb = np.concatenate([zh, sin, z], axis=-1)
    ci = np.concatenate([np.ones((ident_rows, QK_ROPE), np.float32),
                         np.zeros((ident_rows, LANES - QK_ROPE), np.float32)], axis=-1)
    zi = np.zeros((ident_rows, LANES), np.float32)
    tables = np.concatenate([c, ci], 0), np.concatenate([sa, zi], 0), np.concatenate([sb, zi], 0)
    return tuple(tab.astype(np.float32) for tab in tables)


def kernel(x, c, ctx, c_ctx, w_ada, b_ada, ln_g, ln_b, a_w_in, a_conv_w, a_q_gain, a_k_gain, a_w_out, m_w_down, m_q_gain, m_kv_gain, m_w_uq, m_w_ukv, m_w_out, router_w, router_b, e_w_gate, e_w_up, e_w_down, s_w_gate, s_w_up, s_w_down):
    batch, seq, d = x.shape
    ctx_len = ctx.shape[1]
    depth = w_ada.shape[0]
    assert depth == 2, "one conv+GQA layer followed by one MLA layer"
    alpha = (2 * depth) ** 0.25
    t_lat = batch * seq
    t_ctx = batch * ctx_len
    t_all = t_lat + t_ctx
    tr = ROW_TILE
    assert seq % tr == 0 and ctx_len % tr == 0 and seq % GRID_W == 0
    lat_tiles = t_lat // tr
    lat_seq_tiles = seq // tr
    ctx_seq_tiles = ctx_len // tr
    lk = ctx_len + seq

    def mod_row(r):
        return jnp.minimum(r // seq, batch)

    def kv_block(i):
        is_lat = i < lat_tiles
        cidx = i - lat_tiles
        b = jnp.where(is_lat, i // lat_seq_tiles, cidx // ctx_seq_tiles)
        rb = jnp.where(is_lat, ctx_seq_tiles + i % lat_seq_tiles, cidx % ctx_seq_tiles)
        return b, rb

    def pos_block(i):
        return jnp.where(i < lat_tiles, i % lat_seq_tiles, lat_seq_tiles)

    rows = -(-(batch + 1) // SUBLANES) * SUBLANES
    cond = jnp.concatenate([c, c_ctx[None, :], jnp.zeros((rows - batch - 1, d), F32)], axis=0)
    mod = _ada_table(cond, w_ada, b_ada).reshape(depth, rows, 1, 6 * d)

    x_pair = (x.reshape(t_lat, d), ctx.reshape(t_ctx, d))

    tm_in = W_IN_ROW_TILE if (t_ctx % W_IN_ROW_TILE == 0 and seq % W_IN_ROW_TILE == 0) else tr
    proj = _mod_matmul(x_pair, mod[0], a_w_in[0].astype(BF16), BF16, tm_in, W_IN_COL_TILE, mod_row)

    cos128, sin128 = _rope_tables_128(seq, tr)
    d_q = ATT_HEADS * HEAD_DIM
    d_kv = ATT_KV_HEADS * HEAD_DIM
    qkv_w = d_q + 2 * d_kv
    qkv_blk = 3 * CONV_DIM // qkv_w
    assert qkv_blk * qkv_w == 3 * CONV_DIM
    q0, k0, v0 = pl.pallas_call(
        functools.partial(_qkprep_kernel, scale=1.0 / math.sqrt(HEAD_DIM)),
        grid=(t_all // tr,),
        in_specs=[
            pl.BlockSpec((tr, qkv_w), lambda i: (i, qkv_blk)),
            pl.BlockSpec((tr, HEAD_DIM), lambda i: (pos_block(i), 0)),
            pl.BlockSpec((tr, HEAD_DIM), lambda i: (pos_block(i), 0)),
            _const_spec((1, HEAD_DIM)), _const_spec((1, HEAD_DIM)),
        ],
        out_specs=[
            pl.BlockSpec((tr, d_q), lambda i: (i, 0)),
            pl.BlockSpec((None, tr, d_kv), lambda i: (*kv_block(i), 0)),
            pl.BlockSpec((None, tr, d_kv), lambda i: (*kv_block(i), 0)),
        ],
        out_shape=[
            jax.ShapeDtypeStruct((t_all, d_q), BF16),
            jax.ShapeDtypeStruct((batch, lk, d_kv), BF16),
            jax.ShapeDtypeStruct((batch, lk, d_kv), BF16),
        ],
        compiler_params=_params(("parallel",)),
        name="qk_prep",
    )(proj, cos128, sin128, a_q_gain[0].reshape(1, HEAD_DIM), a_k_gain[0].reshape(1, HEAD_DIM))

    grp = ATT_HEADS // ATT_KV_HEADS
    att_lat = _attention(q0, k0, v0, batch=batch, sq=seq, lk=lk, n_kv=ATT_KV_HEADS, group=grp, dk=HEAD_DIM,
                         dv=HEAD_DIM, tq=GQA_Q_TILE if seq % GQA_Q_TILE == 0 else tr, rows=ATTN_CHAIN_ROWS, q_row_off=0)
    att_ctx = _attention(q0, k0, v0, batch=batch, sq=ctx_len, lk=ctx_len, n_kv=ATT_KV_HEADS, group=grp,
                         dk=HEAD_DIM, dv=HEAD_DIM, tq=tr, rows=ATTN_CHAIN_ROWS, q_row_off=t_lat)

    conv0 = _conv_gate(proj, a_conv_w[0], t=t_all, tm=tr, tc=CONV_COL_TILE, lat_tiles=lat_tiles,
                       lat_seq_tiles=lat_seq_tiles, ctx_seq_tiles=ctx_seq_tiles)

    w_out0 = a_w_out[0].astype(BF16)
    x1, tok0 = _outproj_ln([conv0, (att_lat, att_ctx)], [w_out0[:CONV_DIM], w_out0[CONV_DIM:]], x_pair, mod[0],
                           ln_g[0, 0].reshape(1, d), ln_b[0, 0].reshape(1, d), t=t_all, tm=tr, alpha=alpha,
                           mod_row=mod_row)

    x2, u1 = _moe(tok0, x1, t_all, 0, router_w[0], router_b[0], e_w_gate, e_w_up, e_w_down,
                  s_w_gate[0].astype(BF16), s_w_up[0].astype(BF16), s_w_down[0].astype(BF16), mod[0],
                  ln_g[0, 1].reshape(1, d), ln_b[0, 1].reshape(1, d), mod[1], alpha=alpha,
                  mod_row=mod_row, tm=COMBINE_ROW_TILE)

    n_down = Q_LORA + KV_LORA + QK_ROPE
    n_down_pad = -(-n_down // LANES) * LANES
    w_down = jnp.pad(m_w_down[0], ((0, 0), (0, n_down_pad - n_down))).astype(BF16)
    down = _matmul(u1, w_down, F32, DOWN_ROW_TILE, n_down_pad)

    dqk = QK_NOPE + QK_ROPE
    w_uq = m_w_uq[0].reshape(Q_LORA, MLA_HEADS, dqk)
    w_uq = jnp.pad(w_uq, ((0, 0), (0, 0), (0, MLA_DK_PAD - dqk))).reshape(Q_LORA, MLA_HEADS * MLA_DK_PAD).astype(BF16)
    w_ukv = m_w_ukv[0].reshape(KV_LORA, MLA_HEADS, QK_NOPE + V_DIM)
    w_uk = w_ukv[:, :, :QK_NOPE].reshape(KV_LORA, MLA_HEADS * QK_NOPE).astype(BF16)
    w_uv = w_ukv[:, :, QK_NOPE:].reshape(KV_LORA, MLA_HEADS * V_DIM).astype(BF16)

    c64, sa64, sb64 = _rope_tables_64(seq, tr)
    rope_specs = [pl.BlockSpec((tr, LANES), lambda i: (pos_block(i), 0))] * 3
    q1 = pl.pallas_call(
        functools.partial(_mla_q_kernel, scale=1.0 / math.sqrt(dqk)),
        grid=(lat_tiles,),
        in_specs=[
            pl.BlockSpec((tr, Q_LORA), lambda i: (i, 0)),
            _const_spec((1, Q_LORA)),
            _const_spec(w_uq.shape),
        ] + rope_specs,
        out_specs=pl.BlockSpec((tr, MLA_HEADS * MLA_DK_PAD), lambda i: (i, 0)),
        out_shape=jax.ShapeDtypeStruct((t_lat, MLA_HEADS * MLA_DK_PAD), BF16),
        compiler_params=_params(("parallel",)),
        name="mla_q",
    )(down, m_q_gain[0].reshape(1, Q_LORA), w_uq, c64, sa64, sb64)

    assert KV_LORA == Q_LORA and (Q_LORA + KV_LORA) % LANES == 0
    k1, v1 = pl.pallas_call(
        _mla_kv_kernel,
        grid=(t_all // tr,),
        in_specs=[
            pl.BlockSpec((tr, KV_LORA), lambda i: (i, 1)),
            pl.BlockSpec((tr, LANES), lambda i: (i, (Q_LORA + KV_LORA) // LANES)),
            _const_spec((1, KV_LORA)),
            _const_spec(w_uk.shape), _const_spec(w_uv.shape),
        ] + rope_specs,
        out_specs=[
            pl.BlockSpec((None, tr, MLA_HEADS * MLA_DK_PAD), lambda i: (*kv_block(i), 0)),
            pl.BlockSpec((None, tr, MLA_HEADS * V_DIM), lambda i: (*kv_block(i), 0)),
        ],
        out_shape=[
            jax.ShapeDtypeStruct((batch, lk, MLA_HEADS * MLA_DK_PAD), BF16),
            jax.ShapeDtypeStruct((batch, lk, MLA_HEADS * V_DIM), BF16),
        ],
        compiler_params=_params(("parallel",)),
        name="mla_kv",
    )(down, down, m_kv_gain[0].reshape(1, KV_LORA), w_uk, w_uv, c64, sa64, sb64)

    att1 = _attention(q1, k1, v1, batch=batch, sq=seq, lk=lk, n_kv=MLA_HEADS, group=1, dk=MLA_DK_PAD, dv=V_DIM,
                      tq=next(q for q in MLA_Q_TILES + (tr,) if seq % q == 0), rows=ATTN_CHAIN_ROWS, q_row_off=0)

    x3, tok1 = _outproj_ln([att1], [m_w_out[0].astype(BF16)], x2, mod[1], ln_g[1, 0].reshape(1, d),
                           ln_b[1, 0].reshape(1, d), t=t_lat, tm=tr, alpha=alpha, mod_row=mod_row)

    (x4,) = _moe(tok1, x3, t_lat, 1, router_w[1], router_b[1], e_w_gate, e_w_up, e_w_down,
                 s_w_gate[1].astype(BF16), s_w_up[1].astype(BF16), s_w_down[1].astype(BF16), mod[1],
                 ln_g[1, 1].reshape(1, d), ln_b[1, 1].reshape(1, d), None, alpha=alpha,
                 mod_row=mod_row, tm=COMBINE_ROW_TILE)
    return x4.reshape(batch, seq, d)
```

```python
import functools
import math

import jax
import jax.numpy as jnp
import numpy as np
from jax import lax
from jax.experimental import pallas as pl
from jax.experimental.pallas import tpu as pltpu
from jax.experimental.pallas import tpu_sc as plsc

F32 = jnp.float32
BF16 = jnp.bfloat16

GRID_W = 64
CONV_DIM = 1024
ATT_HEADS = 8
ATT_KV_HEADS = 2
HEAD_DIM = 128
MLA_HEADS = 16
Q_LORA = 512
KV_LORA = 512
QK_NOPE = 128
QK_ROPE = 64
V_DIM = 128
N_EXPERTS = 64
TOP_K = 6
N_GROUPS = 8
TOPK_GROUPS = 4
ROUTED_SCALE = 2.5
ROPE_THETA = 10000.0
LN_EPS = 1e-5
RMS_EPS = 1e-6

V7X_VMEM_LIMIT_BYTES = 56 * 1024 * 1024
LANES = 128
SUBLANES = 8
MOE_ROWS = 512
MOE_DISPATCH_PARTS = 4
MOE_COMBINE_PARTS = 2
V7X_SC_CORES = 2
V7X_SC_SUBCORES = 16
SC_GATHER_ROWS = 16
SC_GATHER_BUFFERS = 4
MLA_DK_PAD = 256

ROW_TILE = 256
ADA_COL_TILE = 1024
W_IN_ROW_TILE = 1024
W_IN_COL_TILE = 768
DOWN_ROW_TILE = 512
CONV_COL_TILE = 512
ROUTER_TILE = 512
ATTN_CHAIN_ROWS = 256
GQA_Q_TILE = 512
MLA_Q_TILES = (2048, 1024, 512)
COMBINE_ROW_TILE = 256


def _params(sem):
    return pltpu.CompilerParams(dimension_semantics=sem, vmem_limit_bytes=V7X_VMEM_LIMIT_BYTES)


def _const_spec(shape):
    nd = len(shape)
    return pl.BlockSpec(shape, lambda *_: (0,) * nd)


def _ada_kernel(s_ref, w_ref, b_ref, o_ref):
    s = s_ref[...]
    s = s * (1.0 / (1.0 + jnp.exp(-s)))
    o_ref[...] = jnp.dot(s.astype(BF16), w_ref[...].astype(BF16), preferred_element_type=F32) + b_ref[...]


def _ada_table(cond, w_ada, b_ada):
    depth, d, n = w_ada.shape
    r = cond.shape[0]
    tn = ADA_COL_TILE
    return pl.pallas_call(
        _ada_kernel,
        grid=(depth, n // tn),
        in_specs=[
            pl.BlockSpec((r, d), lambda l, j: (0, 0)),
            pl.BlockSpec((None, d, tn), lambda l, j: (l, 0, j)),
            pl.BlockSpec((None, 1, tn), lambda l, j: (l, 0, j)),
        ],
        out_specs=pl.BlockSpec((None, r, tn), lambda l, j: (l, 0, j)),
        out_shape=jax.ShapeDtypeStruct((depth, r, n), F32),
        compiler_params=_params(("parallel", "parallel")),
        name="ada_table",
    )(cond, w_ada, b_ada.reshape(depth, 1, n))


def _mod_spec(d, chunk, mod_row, tm):
    return pl.BlockSpec((None, 1, d), lambda i: (mod_row(i * tm), 0, chunk))


def _mm_kernel(a_ref, w_ref, o_ref):
    o_ref[...] = jnp.dot(a_ref[...], w_ref[...], preferred_element_type=F32).astype(o_ref.dtype)


def _matmul(a, w, out_dtype, tm, tn):
    m, k = a.shape
    n = w.shape[1]
    return pl.pallas_call(
        _mm_kernel,
        grid=(m // tm, n // tn),
        in_specs=[
            pl.BlockSpec((tm, k), lambda i, j: (i, 0)),
            pl.BlockSpec((k, tn), lambda i, j: (0, j)),
        ],
        out_specs=pl.BlockSpec((tm, tn), lambda i, j: (i, j)),
        out_shape=jax.ShapeDtypeStruct((m, n), out_dtype),
        compiler_params=_params(("parallel", "parallel")),
        name="matmul",
    )(a, w)


def _pair_specs(pair, tm):
    lat, ctx = pair
    lat_tiles = lat.shape[0] // tm
    assert lat_tiles * tm == lat.shape[0] and ctx.shape[0] % tm == 0 and lat.shape[1] == ctx.shape[1]
    width = lat.shape[1]
    return [pl.BlockSpec((tm, width), lambda i, *_: (jnp.minimum(i, lat_tiles - 1), 0)),
            pl.BlockSpec((tm, width), lambda i, *_: (jnp.maximum(i - lat_tiles, 0), 0))]


def _pair_tile(lat_ref, ctx_ref, lat_tiles):
    return jnp.where(pl.program_id(0) < lat_tiles, lat_ref[...], ctx_ref[...])


def _mod_mm_kernel(xl_ref, xc_ref, sc_ref, sh_ref, w_ref, o_ref, u_ref, *, lat_tiles):
    @pl.when(pl.program_id(1) == 0)
    def _():
        x = _pair_tile(xl_ref, xc_ref, lat_tiles)
        u_ref[...] = (x * (1.0 + sc_ref[...]) + sh_ref[...]).astype(u_ref.dtype)

    o_ref[...] = jnp.dot(u_ref[...], w_ref[...], preferred_element_type=F32).astype(o_ref.dtype)


def _mod_matmul(x_pair, mod, w, out_dtype, tm, tn, mod_row):
    m = x_pair[0].shape[0] + x_pair[1].shape[0]
    k = x_pair[0].shape[1]
    n = w.shape[1]

    def mod_spec(chunk):
        return pl.BlockSpec((None, 1, k), lambda i, j: (mod_row(i * tm), 0, chunk))

    return pl.pallas_call(
        functools.partial(_mod_mm_kernel, lat_tiles=x_pair[0].shape[0] // tm),
        grid=(m // tm, n // tn),
        in_specs=_pair_specs(x_pair, tm) + [
            mod_spec(1), mod_spec(0),
            pl.BlockSpec((k, tn), lambda i, j: (0, j)),
        ],
        out_specs=pl.BlockSpec((tm, tn), lambda i, j: (i, j)),
        out_shape=jax.ShapeDtypeStruct((m, n), out_dtype),
        scratch_shapes=[pltpu.VMEM((tm, k), w.dtype)],
        compiler_params=_params(("parallel", "arbitrary")),
        name="mod_matmul",
    )(*x_pair, mod, mod, w)


def _rms(t, gain):
    return t * lax.rsqrt(jnp.mean(t * t, axis=-1, keepdims=True) + RMS_EPS) * gain


def _qkprep_kernel(p_ref, cos_ref, sin_ref, qg_ref, kg_ref, q_ref, k_ref, v_ref, *, scale):
    cos = cos_ref[...]
    sin = sin_ref[...]

    def norm_rope(t, gain):
        y = _rms(t.astype(F32), gain)
        return y * cos + pltpu.roll(y, HEAD_DIM // 2, 1) * sin

    for h in range(ATT_HEADS):
        sl = slice(h * HEAD_DIM, (h + 1) * HEAD_DIM)
        q_ref[:, sl] = (norm_rope(p_ref[:, sl], qg_ref[...]) * scale).astype(q_ref.dtype)
    k0 = ATT_HEADS * HEAD_DIM
    for h in range(ATT_KV_HEADS):
        sl = slice(h * HEAD_DIM, (h + 1) * HEAD_DIM)
        k_ref[:, sl] = norm_rope(p_ref[:, k0 + h * HEAD_DIM:k0 + (h + 1) * HEAD_DIM], kg_ref[...]).astype(k_ref.dtype)
    v0 = k0 + ATT_KV_HEADS * HEAD_DIM
    v_ref[...] = p_ref[:, v0:v0 + ATT_KV_HEADS * HEAD_DIM].astype(v_ref.dtype)


def _attn_kernel(q_ref, k_ref, v_ref, o_ref, *, group, tq, rows, dk, dv):
    k = k_ref[...]
    v = v_ref[...]
    for h in range(group):
        for r in range(0, tq, rows):
            q = q_ref[r:r + rows, h * dk:(h + 1) * dk]
            s = lax.dot_general(q, k, (((1,), (1,)), ((), ())), preferred_element_type=F32)
            m = jnp.max(s, axis=-1, keepdims=True)
            p = jnp.exp(s - m)
            l = jnp.sum(p, axis=-1, keepdims=True)
            o = jnp.dot(p.astype(v.dtype), v, preferred_element_type=F32)
            o_ref[r:r + rows, h * dv:(h + 1) * dv] = (o / l).astype(o_ref.dtype)


def _attention(q, k, v, *, batch, sq, lk, n_kv, group, dk, dv, tq, rows, q_row_off):
    nq = sq // tq
    off = q_row_off // tq
    assert tq % rows == 0 and q_row_off % tq == 0 and sq % tq == 0
    return pl.pallas_call(
        functools.partial(_attn_kernel, group=group, tq=tq, rows=rows, dk=dk, dv=dv),
        grid=(batch, n_kv, nq),
        in_specs=[
            pl.BlockSpec((tq, group * dk), lambda b, g, i: (off + b * nq + i, g)),
            pl.BlockSpec((None, lk, dk), lambda b, g, i: (b, 0, g)),
            pl.BlockSpec((None, lk, dv), lambda b, g, i: (b, 0, g)),
        ],
        out_specs=pl.BlockSpec((tq, group * dv), lambda b, g, i: (b * nq + i, g)),
        out_shape=jax.ShapeDtypeStruct((batch * sq, n_kv * group * dv), BF16),
        compiler_params=_params(("parallel", "parallel", "parallel")),
        name="attention",
    )(q, k, v)


def _conv_kernel(gb_ref, gc_ref, hv_ref, gcp_ref, hvp_ref, gcn_ref, hvn_ref, w_ref, o_ref, *,
                 tm, lat_tiles, lat_seq_tiles, ctx_seq_tiles):
    i = pl.program_id(0)
    is_lat = i < lat_tiles
    pos = jnp.where(is_lat, i % lat_seq_tiles, (i - lat_tiles) % ctx_seq_tiles)
    seq_tiles = jnp.where(is_lat, lat_seq_tiles, ctx_seq_tiles)
    not_first = (pos != 0).astype(F32)
    not_last = (pos != seq_tiles - 1).astype(F32)
    p = gc_ref[...].astype(F32) * hv_ref[...].astype(F32)
    halo_prev = gcp_ref[SUBLANES - 1:SUBLANES, :].astype(F32) * hvp_ref[SUBLANES - 1:SUBLANES, :].astype(F32) * not_first
    halo_next = gcn_ref[0:1, :].astype(F32) * hvn_ref[0:1, :].astype(F32) * not_last
    row = lax.broadcasted_iota(jnp.int32, p.shape, 0)
    prev = jnp.where(row == 0, halo_prev, pltpu.roll(p, 1, 0))
    nxt = jnp.where(row == tm - 1, halo_next, pltpu.roll(p, tm - 1, 0))
    w = w_ref[...]
    conv = w[0:1, :] * prev + w[1:2, :] * p + w[2:3, :] * nxt
    o_ref[...] = (gb_ref[...].astype(F32) * conv).astype(o_ref.dtype)


def _conv_gate(p, conv_w, *, t, tm, tc, lat_tiles, lat_seq_tiles, ctx_seq_tiles):
    nct = CONV_DIM // tc
    hb = tm // SUBLANES
    n_halo = t // SUBLANES

    def cur(part):
        return pl.BlockSpec((tm, tc), lambda i, j: (i, part * nct + j))

    def prev(part):
        return pl.BlockSpec((SUBLANES, tc), lambda i, j: (jnp.maximum(i * hb - 1, 0), part * nct + j))

    def nxt(part):
        return pl.BlockSpec((SUBLANES, tc), lambda i, j: (jnp.minimum((i + 1) * hb, n_halo - 1), part * nct + j))

    return pl.pallas_call(
        functools.partial(_conv_kernel, tm=tm, lat_tiles=lat_tiles, lat_seq_tiles=lat_seq_tiles,
                          ctx_seq_tiles=ctx_seq_tiles),
        grid=(t // tm, nct),
        in_specs=[cur(0), cur(1), cur(2), prev(1), prev(2), nxt(1), nxt(2),
                  pl.BlockSpec((3, tc), lambda i, j: (0, j))],
        out_specs=pl.BlockSpec((tm, tc), lambda i, j: (i, j)),
        out_shape=jax.ShapeDtypeStruct((t, CONV_DIM), BF16),
        compiler_params=_params(("parallel", "parallel")),
        name="conv_gate",
    )(p, p, p, p, p, p, p, conv_w)


def _layer_norm(z, g, b):
    mu = jnp.mean(z, axis=-1, keepdims=True)
    zc = z - mu
    var = jnp.mean(zc * zc, axis=-1, keepdims=True)
    return zc * lax.rsqrt(var + LN_EPS) * g + b


def _pack_bf16_pairs(x):
    half = x.shape[1] // 2
    lo = lax.bitcast_convert_type(x[:, :half].astype(BF16).astype(F32), jnp.uint32) >> 16
    hi = lax.bitcast_convert_type(x[:, half:].astype(BF16).astype(F32), jnp.uint32) & jnp.uint32(0xFFFF0000)
    return lax.bitcast_convert_type(lo | hi, jnp.int32)


def _unpack_bf16_pairs(w):
    u = lax.bitcast_convert_type(w, jnp.uint32)
    lo = lax.bitcast_convert_type(u << 16, F32).astype(BF16)
    hi = lax.bitcast_convert_type(u & jnp.uint32(0xFFFF0000), F32).astype(BF16)
    return lo, hi


def _dot_halves(lo, hi, w_ref):
    half = lo.shape[1]
    return (jnp.dot(lo, w_ref[:half, :], preferred_element_type=F32)
            + jnp.dot(hi, w_ref[half:, :], preferred_element_type=F32))


def _outproj_ln_kernel(*refs, widths, lat_tiles, alpha):
    refs = list(refs)

    def take(width):
        got = [refs.pop(0) for _ in range(width)]
        return got[0][...] if width == 1 else _pair_tile(got[0], got[1], lat_tiles)

    acts = [take(w) for w in widths[:-1]]
    w_refs = [refs.pop(0) for _ in acts]
    x = take(widths[-1])
    gate_ref, lng_ref, lnb_ref, sc_ref, sh_ref, xo_ref, tok_ref = refs
    y = jnp.dot(acts[0], w_refs[0][...], preferred_element_type=F32)
    for a, w_ref in zip(acts[1:], w_refs[1:]):
        y = y + jnp.dot(a, w_ref[...], preferred_element_type=F32)
    xn = _layer_norm(alpha * x + gate_ref[...] * y, lng_ref[...], lnb_ref[...])
    xo_ref[...] = xn
    tok_ref[...] = _pack_bf16_pairs(xn * (1.0 + sc_ref[...]) + sh_ref[...])


def _outproj_ln(a_list, w_list, x, mod, ln_g, ln_b, *, t, tm, alpha, mod_row):
    d = w_list[0].shape[1]
    operands, in_specs, widths, lat_tiles = [], [], [], 0

    def add_rows(src):
        nonlocal lat_tiles
        if isinstance(src, tuple):
            in_specs.extend(_pair_specs(src, tm))
            operands.extend(src)
            widths.append(2)
            lat_tiles = src[0].shape[0] // tm
        else:
            in_specs.append(pl.BlockSpec((tm, src.shape[1]), lambda i: (i, 0)))
            operands.append(src)
            widths.append(1)

    for a in a_list:
        add_rows(a)
    in_specs += [_const_spec(w.shape) for w in w_list]
    operands += list(w_list)
    add_rows(x)
    in_specs += [
        _mod_spec(d, 2, mod_row, tm),
        _const_spec((1, d)), _const_spec((1, d)),
        _mod_spec(d, 4, mod_row, tm),
        _mod_spec(d, 3, mod_row, tm),
    ]
    return pl.pallas_call(
        functools.partial(_outproj_ln_kernel, widths=tuple(widths), lat_tiles=lat_tiles, alpha=alpha),
        grid=(t // tm,),
        in_specs=in_specs,
        out_specs=[pl.BlockSpec((tm, d), lambda i: (i, 0)), pl.BlockSpec((tm, d // 2), lambda i: (i, 0))],
        out_shape=[jax.ShapeDtypeStruct((t, d), F32), jax.ShapeDtypeStruct((t, d // 2), jnp.int32)],
        compiler_params=_params(("parallel",)),
        name="outproj_ln",
    )(*operands, mod, ln_g, ln_b, mod, mod)


def _router_kernel(t_ref, rw_ref, rb_ref, tri_ref, idx_ref, gw_ref, rank_ref, cnt_ref):
    @pl.when(pl.program_id(0) == 0)
    def _():
        cnt_ref[...] = jnp.zeros_like(cnt_ref)

    lo, hi = _unpack_bf16_pairs(t_ref[...])
    half = lo.shape[1]
    nt = (((1,), (1,)), ((), ()))
    logits = (lax.dot_general(rw_ref[:, :half], lo, nt, preferred_element_type=F32)
              + lax.dot_general(rw_ref[:, half:], hi, nt, preferred_element_type=F32))
    scores = 1.0 / (1.0 + jnp.exp(-logits))
    sel = scores + rb_ref[...]
    gsz = N_EXPERTS // N_GROUPS
    neg = -jnp.inf
    sub = lax.broadcasted_iota(jnp.int32, (gsz, sel.shape[1]), 0)
    slabs = [sel[g * gsz:(g + 1) * gsz, :] for g in range(N_GROUPS)]
    gscore = []
    for s in slabs:
        m1 = jnp.max(s, axis=0, keepdims=True)
        a1 = jnp.min(jnp.where(s == m1, sub, gsz), axis=0, keepdims=True)
        m2 = jnp.max(jnp.where(sub == a1, neg, s), axis=0, keepdims=True)
        gscore.append(m1 + m2)
    masked = []
    for g in range(N_GROUPS):
        ahead = jnp.zeros(gscore[g].shape, jnp.int32)
        for h in range(N_GROUPS):
            if h == g:
                continue
            beats = gscore[h] >= gscore[g] if h < g else gscore[h] > gscore[g]
            ahead = ahead + beats.astype(jnp.int32)
        masked.append(jnp.where(ahead < TOPK_GROUPS, slabs[g], neg))
    cur = jnp.concatenate(masked, axis=0)
    eio = lax.broadcasted_iota(jnp.int32, cur.shape, 0)
    picks, weights = [], []
    for _ in range(TOP_K):
        m = jnp.max(cur, axis=0, keepdims=True)
        a = jnp.min(jnp.where(cur == m, eio, N_EXPERTS), axis=0, keepdims=True)
        hit = eio == a
        picks.append(a)
        weights.append(jnp.sum(jnp.where(hit, scores, 0.0), axis=0, keepdims=True))
        cur = jnp.where(hit, neg, cur)
    total = weights[0]
    for w in weights[1:]:
        total = total + w
    for k in range(TOP_K):
        idx_ref[k:k + 1, :] = picks[k]
        gw_ref[k:k + 1, :] = weights[k] / total * ROUTED_SCALE
    for k in range(TOP_K, SUBLANES):
        idx_ref[k:k + 1, :] = jnp.zeros_like(picks[0])
        gw_ref[k:k + 1, :] = jnp.zeros_like(weights[0])
        rank_ref[k:k + 1, :] = jnp.zeros_like(picks[0])
    base = cnt_ref[:, 0:1]
    for k in range(TOP_K):
        onehot = jnp.where(eio == picks[k], 1.0, 0.0)
        before = jnp.dot(onehot.astype(BF16), tri_ref[...], preferred_element_type=F32)
        rank_ref[k:k + 1, :] = jnp.sum(onehot * (before + base), axis=0, keepdims=True).astype(jnp.int32)
        base = base + jnp.sum(onehot, axis=1, keepdims=True)
    cnt_ref[...] = jnp.broadcast_to(base, cnt_ref.shape)


def _router(tok, rw_t, rb, *, t, tt):
    half = tok.shape[1]
    tri = jnp.asarray(np.arange(tt)[:, None] < np.arange(tt)[None, :], BF16)
    blk = pl.BlockSpec((SUBLANES, tt), lambda i: (0, i))
    return pl.pallas_call(
        _router_kernel,
        grid=(t // tt,),
        in_specs=[
            pl.BlockSpec((tt, half), lambda i: (i, 0)),
            _const_spec((N_EXPERTS, 2 * half)),
            _const_spec((N_EXPERTS, 1)),
            _const_spec((tt, tt)),
        ],
        out_specs=[blk, blk, blk, _const_spec((N_EXPERTS, LANES))],
        out_shape=[jax.ShapeDtypeStruct((SUBLANES, t), jnp.int32), jax.ShapeDtypeStruct((SUBLANES, t), F32),
                   jax.ShapeDtypeStruct((SUBLANES, t), jnp.int32), jax.ShapeDtypeStruct((N_EXPERTS, LANES), F32)],
        compiler_params=_params(("arbitrary",)),
        name="router",
    )(tok, rw_t, rb, tri)


def _slots_kernel(idx_ref, rank_ref, start_ref, pos_ref):
    start = start_ref[...]
    eio = lax.broadcasted_iota(jnp.int32, (N_EXPERTS, idx_ref.shape[1]), 0)
    for k in range(TOP_K):
        seg = jnp.sum(jnp.where(eio == idx_ref[k:k + 1, :], start, 0.0), axis=0, keepdims=True)
        pos_ref[k:k + 1, :] = rank_ref[k:k + 1, :] + seg.astype(jnp.int32)
    for k in range(TOP_K, SUBLANES):
        pos_ref[k:k + 1, :] = jnp.zeros((1, idx_ref.shape[1]), jnp.int32)


def _assign_slots(idx, rank, seg_start, *, t, tt):
    blk = pl.BlockSpec((SUBLANES, tt), lambda i: (0, i))
    return pl.pallas_call(
        _slots_kernel,
        grid=(t // tt,),
        in_specs=[blk, blk, _const_spec((N_EXPERTS, 1))],
        out_specs=blk,
        out_shape=jax.ShapeDtypeStruct((SUBLANES, t), jnp.int32),
        compiler_params=_params(("parallel",)),
        name="assign_slots",
    )(idx, rank, seg_start)


def _sc_gather_rows(table, idx):
    n = idx.shape[0]
    d = table.shape[1]
    n_workers = V7X_SC_CORES * V7X_SC_SUBCORES
    per_w = n // n_workers
    n_chunks = per_w // SC_GATHER_ROWS
    assert per_w * n_workers == n and n_chunks * SC_GATHER_ROWS == per_w
    n_buf = next(b for b in range(SC_GATHER_BUFFERS, 1, -1) if n_chunks % b == 0)
    mesh = plsc.VectorSubcoreMesh(core_axis_name="c", subcore_axis_name="s", num_cores=V7X_SC_CORES,
                                  num_subcores=V7X_SC_SUBCORES)

    @functools.partial(
        pl.kernel,
        out_type=jax.ShapeDtypeStruct((n, d), table.dtype),
        mesh=mesh,
        scratch_types=[
            pltpu.VMEM((per_w,), jnp.int32),
            pltpu.VMEM((n_buf, SC_GATHER_ROWS, d), table.dtype),
            pltpu.SemaphoreType.DMA((n_buf,)),
            pltpu.SemaphoreType.DMA((n_buf,)),
        ],
        name="sc_gather_rows",
    )
    def gather(table_hbm, idx_hbm, out_hbm, idx_v, rows_v, gsem, wsem):
        wid = lax.axis_index("s") * V7X_SC_CORES + lax.axis_index("c")
        base = wid * per_w
        pltpu.sync_copy(idx_hbm.at[pl.ds(base, per_w)], idx_v)

        def gather_copy(c, b):
            return pltpu.make_async_copy(table_hbm.at[idx_v.at[pl.ds(c * SC_GATHER_ROWS, SC_GATHER_ROWS)]],
                                         rows_v.at[b], gsem.at[b])

        def write_copy(c, b):
            return pltpu.make_async_copy(rows_v.at[b], out_hbm.at[pl.ds(base + c * SC_GATHER_ROWS, SC_GATHER_ROWS)],
                                         wsem.at[b])

        for b in range(n_buf - 1):
            gather_copy(b, b).start()

        @pl.loop(0, n_chunks, step=n_buf)
        def _(g):
            for b in range(n_buf):
                c = g + b
                prev = (b + n_buf - 1) % n_buf
                gather_copy(c, b).wait()
                write_copy(c, b).start()

                @pl.when(c >= 1)
                def _():
                    write_copy(c - 1, prev).wait()

                @pl.when(c + n_buf - 1 < n_chunks)
                def _():
                    gather_copy(c + n_buf - 1, prev).start()

        write_copy(n_chunks - 1, (n_chunks - 1) % n_buf).wait()

    return gather(table, idx)


def _experts_kernel(be_ref, nbu_ref, seg_ref, nstart_ref, x_ref, wg_hbm, wu_hbm, wd_hbm, *rest, block_off, n_call,
                    layer):
    y_ref, wgf, wuf, wdf, wgb, wub, wdb, sem = rest[-8:]
    step = pl.program_id(0)
    b = block_off + step
    nbu = nbu_ref[0]
    end = jnp.minimum(nbu, block_off + n_call)

    def weight_copies(e, slot):
        return [pltpu.make_async_copy(hbm.at[layer, e], buf.at[slot], sem.at[slot, k])
                for k, (hbm, buf) in enumerate(((wg_hbm, wgf), (wu_hbm, wuf), (wd_hbm, wdf)))]

    @pl.when(b < nbu)
    def _():
        first = jnp.logical_or(step == 0, be_ref[b] != be_ref[jnp.maximum(b - 1, 0)])

        @pl.when(first)
        def _():
            slot = seg_ref[b] % 2

            @pl.when(step == 0)
            def _():
                for cp in weight_copies(be_ref[b], slot):
                    cp.start()

            for cp in weight_copies(be_ref[b], slot):
                cp.wait()
            wgb[...] = wgf[slot].astype(BF16)
            wub[...] = wuf[slot].astype(BF16)
            wdb[...] = wdf[slot].astype(BF16)
            nxt = nstart_ref[b]

            @pl.when(nxt < end)
            def _():
                for cp in weight_copies(be_ref[jnp.minimum(nxt, be_ref.shape[0] - 1)], 1 - slot):
                    cp.start()

        lo, hi = _unpack_bf16_pairs(x_ref[...])
        hg = _dot_halves(lo, hi, wgb)
        hu = _dot_halves(lo, hi, wub)
        h = hg * (1.0 / (1.0 + jnp.exp(-hg))) * hu
        y_ref[...] = _pack_bf16_pairs(jnp.dot(h.astype(BF16), wdb[...], preferred_element_type=F32))

    @pl.when(b >= nbu)
    def _():
        y_ref[...] = jnp.zeros_like(y_ref)


def _experts(xs, block_e, nb_used, seg_idx, next_start, wg, wu, wd, layer, y_prev, *, block_off, n_blocks):
    half = xs.shape[1]
    d = 2 * half
    ff = wg.shape[3]
    n_call = xs.shape[0] // MOE_ROWS

    def used(b, nbu):
        return jnp.clip(jnp.minimum(block_off + b, nbu[0] - 1) - block_off, 0, n_call - 1)

    hbm = pl.BlockSpec(memory_space=pl.ANY)
    in_specs = [pl.BlockSpec((MOE_ROWS, half), lambda b, be, nbu, seg, nst: (used(b, nbu), 0)), hbm, hbm, hbm]
    args = [block_e, nb_used, seg_idx, next_start, xs, wg, wu, wd]
    aliases = {}
    if y_prev is not None:
        in_specs.append(hbm)
        aliases = {len(args): 0}
        args.append(y_prev)
    grid_spec = pltpu.PrefetchScalarGridSpec(
        num_scalar_prefetch=4,
        grid=(n_call,),
        in_specs=in_specs,
        out_specs=pl.BlockSpec((MOE_ROWS, half), lambda b, be, nbu, seg, nst: (block_off + b, 0)),
        scratch_shapes=[
            pltpu.VMEM((2, d, ff), F32),
            pltpu.VMEM((2, d, ff), F32),
            pltpu.VMEM((2, ff, d), F32),
            pltpu.VMEM((d, ff), BF16),
            pltpu.VMEM((d, ff), BF16),
            pltpu.VMEM((ff, d), BF16),
            pltpu.SemaphoreType.DMA((2, 3)),
        ],
    )
    return pl.pallas_call(
        functools.partial(_experts_kernel, block_off=block_off, n_call=n_call, layer=layer),
        grid_spec=grid_spec,
        out_shape=jax.ShapeDtypeStruct((n_blocks * MOE_ROWS, half), jnp.int32),
        input_output_aliases=aliases,
        compiler_params=_params(("arbitrary",)),
        name="experts",
    )(*args)


def _combine_ln_kernel(*refs, alpha, emit_next, n_prev):
    y_ref, gw_ref, tok_ref, x_ref, sg_ref, su_ref, sd_ref, gate_ref, lng_ref, lnb_ref = refs[:10]
    outs = refs[len(refs) - (2 if emit_next else 1):]
    if emit_next:
        sc_ref, sh_ref = refs[10:12]
        xo_ref, u_ref = outs
    else:
        (xo_ref,) = outs
    lo, hi = _unpack_bf16_pairs(tok_ref[...])
    hg = _dot_halves(lo, hi, sg_ref)
    hu = _dot_halves(lo, hi, su_ref)
    h = hg * (1.0 / (1.0 + jnp.exp(-hg))) * hu
    gw = gw_ref[...]
    f_lo = f_hi = None
    for k in range(TOP_K):
        y_lo, y_hi = _unpack_bf16_pairs(y_ref[k])
        w = gw[:, k:k + 1]
        f_lo = y_lo.astype(F32) * w if f_lo is None else f_lo + y_lo.astype(F32) * w
        f_hi = y_hi.astype(F32) * w if f_hi is None else f_hi + y_hi.astype(F32) * w
    f = jnp.concatenate([f_lo, f_hi], axis=-1) + jnp.dot(h.astype(BF16), sd_ref[...], preferred_element_type=F32)
    xn = _layer_norm(alpha * x_ref[...] + gate_ref[...] * f, lng_ref[...], lnb_ref[...])
    xo_ref[...] = xn
    if emit_next:
        u_ref[...] = (xn * (1.0 + sc_ref[...]) + sh_ref[...]).astype(u_ref.dtype)


def _combine_ln(y3, gw_t, tok, x, sg, su, sd, mod, ln_g, ln_b, mod_next, prev, *, t, row_off, tm, alpha, mod_row):
    d = x.shape[1]
    emit_next = mod_next is not None
    off = row_off // tm
    assert off * tm == row_off

    def rows(i):
        return (off + i, 0)

    def part_mod_row(r):
        return mod_row(r + row_off)

    in_specs = [
        pl.BlockSpec((TOP_K, tm, d // 2), lambda i: (0, i, 0)),
        pl.BlockSpec((tm, SUBLANES), rows),
        pl.BlockSpec((tm, d // 2), rows),
        pl.BlockSpec((tm, d), rows),
        _const_spec(sg.shape), _const_spec(su.shape), _const_spec(sd.shape),
        _mod_spec(d, 5, part_mod_row, tm),
        _const_spec((1, d)), _const_spec((1, d)),
    ]
    args = [y3, gw_t, tok, x, sg, su, sd, mod, ln_g, ln_b]
    out_specs = [pl.BlockSpec((tm, d), rows)]
    out_shape = [jax.ShapeDtypeStruct((t, d), F32)]
    if emit_next:
        in_specs += [_mod_spec(d, 1, part_mod_row, tm), _mod_spec(d, 0, part_mod_row, tm)]
        args += [mod_next, mod_next]
        out_specs.append(pl.BlockSpec((tm, d), rows))
        out_shape.append(jax.ShapeDtypeStruct((t, d), BF16))
    aliases = {}
    if prev is not None:
        for k, p in enumerate(prev):
            in_specs.append(pl.BlockSpec(memory_space=pl.ANY))
            aliases[len(args)] = k
            args.append(p)
    return pl.pallas_call(
        functools.partial(_combine_ln_kernel, alpha=alpha, emit_next=emit_next, n_prev=len(aliases)),
        grid=(y3.shape[1] // tm,),
        in_specs=in_specs,
        out_specs=out_specs,
        out_shape=out_shape,
        input_output_aliases=aliases,
        compiler_params=_params(("parallel",)),
        name="combine_ln",
    )(*args)


def _moe(tok, x, t, layer, router_w, router_b, wg, wu, wd, sg, su, sd, mod, ln_g, ln_b, mod_next, *, alpha, mod_row,
         tm):
    half = tok.shape[1]
    tt = ROUTER_TILE
    idx, gw, rank, cnt = _router(tok, router_w.T.astype(BF16), router_b.reshape(N_EXPERTS, 1), t=t, tt=tt)
    n_asg = t * TOP_K
    counts = cnt[:, 0].astype(jnp.int32)
    padded = (counts + MOE_ROWS - 1) // MOE_ROWS * MOE_ROWS
    pend = jnp.cumsum(padded)
    pstart = pend - padded
    sc_rows = V7X_SC_CORES * V7X_SC_SUBCORES * SC_GATHER_ROWS * 2
    blocks_granule = MOE_DISPATCH_PARTS * max(sc_rows // MOE_ROWS, 1)
    assert (blocks_granule // MOE_DISPATCH_PARTS * MOE_ROWS) % sc_rows == 0
    n_blocks = -(-((n_asg + N_EXPERTS * (MOE_ROWS - 1)) // MOE_ROWS + 1) // blocks_granule) * blocks_granule
    assert n_asg % sc_rows == 0
    block_start = jnp.arange(n_blocks, dtype=jnp.int32) * MOE_ROWS
    block_e = jnp.minimum(jnp.sum((pend[None, :] <= block_start[:, None]).astype(jnp.int32), axis=1), N_EXPERTS - 1)
    nb_used = (pend[-1] // MOE_ROWS).astype(jnp.int32).reshape(1)
    blk = jnp.arange(n_blocks, dtype=jnp.int32)
    seg_first = jnp.concatenate([jnp.ones((1,), bool), block_e[1:] != block_e[:-1]])
    seg_idx = jnp.cumsum(seg_first.astype(jnp.int32)) - 1
    later_first = lax.cummin(jnp.where(seg_first, blk, n_blocks), reverse=True)
    next_start = jnp.concatenate([later_first[1:], jnp.full((1,), n_blocks, jnp.int32)])
    pos2d = _assign_slots(idx, rank, pstart.astype(F32).reshape(N_EXPERTS, 1), t=t, tt=tt)[:TOP_K]
    pos = pos2d.reshape(-1)
    tok_of_asg = np.tile(np.arange(t, dtype=np.int32), TOP_K)
    n_pad = n_blocks * MOE_ROWS - n_asg
    seg_pad_end = jnp.cumsum(padded - counts)
    j = jnp.arange(n_pad, dtype=jnp.int32)
    pad_e = jnp.sum((seg_pad_end[None, :] <= j[:, None]).astype(jnp.int32), axis=1)
    seg_base = pstart + counts - (seg_pad_end - (padded - counts))
    in_seg = j + jnp.sum(jnp.where(pad_e[:, None] == jnp.arange(N_EXPERTS)[None, :], seg_base[None, :], 0), axis=1)
    pad_slot = jnp.where(pad_e < N_EXPERTS, in_seg, pend[-1] + j - seg_pad_end[-1])
    _, slot_tok = lax.sort((jnp.concatenate([pos, pad_slot]), jnp.concatenate([tok_of_asg, pad_slot % t])),
                           num_keys=1)
    per = n_blocks // MOE_DISPATCH_PARTS * MOE_ROWS
    xs = [_sc_gather_rows(tok, slot_tok[i * per:(i + 1) * per]) for i in range(MOE_DISPATCH_PARTS)]
    y = None
    for i in range(MOE_DISPATCH_PARTS):
        y = _experts(xs[i], block_e, nb_used, seg_idx, next_start, wg, wu, wd, layer, y,
                     block_off=i * per // MOE_ROWS, n_blocks=n_blocks)
    n_cparts = MOE_COMBINE_PARTS if (t // MOE_COMBINE_PARTS * TOP_K) % sc_rows == 0 else 1
    t_part = t // n_cparts
    y3 = [_sc_gather_rows(y, pos2d[:, i * t_part:(i + 1) * t_part].reshape(-1)).reshape(TOP_K, t_part, half)
          for i in range(n_cparts)]
    outs = None
    gw_t = gw.T
    for i in range(n_cparts):
        outs = _combine_ln(y3[i], gw_t, tok, x, sg, su, sd, mod, ln_g, ln_b, mod_next, outs, t=t,
                           row_off=i * t_part, tm=tm, alpha=alpha, mod_row=mod_row)
    return outs


def _rope64(r, c_ref, sa_ref, sb_ref):
    return r * c_ref[...] + pltpu.roll(r, LANES - QK_ROPE // 2, 1) * sa_ref[...] + pltpu.roll(r, QK_ROPE // 2, 1) * sb_ref[...]


def _mla_q_kernel(d_ref, gain_ref, w_ref, c_ref, sa_ref, sb_ref, q_ref, *, scale):
    n = _rms(d_ref[...], gain_ref[...]).astype(BF16)
    q = jnp.dot(n, w_ref[...], preferred_element_type=F32)
    for h in range(MLA_HEADS):
        lo = h * MLA_DK_PAD
        q_ref[:, lo:lo + QK_NOPE] = (q[:, lo:lo + QK_NOPE] * scale).astype(q_ref.dtype)
        r = _rope64(q[:, lo + QK_NOPE:lo + MLA_DK_PAD], c_ref, sa_ref, sb_ref)
        q_ref[:, lo + QK_NOPE:lo + MLA_DK_PAD] = (r * scale).astype(q_ref.dtype)


def _mla_kv_kernel(ckv_ref, kr_ref, gain_ref, wk_ref, wv_ref, c_ref, sa_ref, sb_ref, k_ref, v_ref):
    n = _rms(ckv_ref[...], gain_ref[...]).astype(BF16)
    kn = jnp.dot(n, wk_ref[...], preferred_element_type=F32)
    v_ref[...] = jnp.dot(n, wv_ref[...], preferred_element_type=F32).astype(v_ref.dtype)
    kr = _rope64(kr_ref[...], c_ref, sa_ref, sb_ref).astype(k_ref.dtype)
    for h in range(MLA_HEADS):
        lo = h * MLA_DK_PAD
        k_ref[:, lo:lo + QK_NOPE] = kn[:, h * QK_NOPE:(h + 1) * QK_NOPE].astype(k_ref.dtype)
        k_ref[:, lo + QK_NOPE:lo + MLA_DK_PAD] = kr


def _axial_angles(n_tok, rot_dim):
    rows = n_tok // GRID_W
    n_freq = rot_dim // 4
    inv = (ROPE_THETA ** (-np.arange(n_freq, dtype=np.float32) / n_freq)).astype(np.float32)
    row = np.repeat(np.arange(rows, dtype=np.float32), GRID_W)
    col = np.tile(np.arange(GRID_W, dtype=np.float32), rows)
    return np.concatenate([row[:, None] * inv, col[:, None] * inv], axis=-1)


def _rope_tables_128(n_tok, ident_rows):
    ang = _axial_angles(n_tok, HEAD_DIM)
    cos, sin = np.cos(ang), np.sin(ang)
    c = np.concatenate([cos, cos], axis=-1)
    s = np.concatenate([-sin, sin], axis=-1)
    c = np.concatenate([c, np.ones((ident_rows, HEAD_DIM), np.float32)], axis=0)
    s = np.concatenate([s, np.zeros((ident_rows, HEAD_DIM), np.float32)], axis=0)
    return c.astype(np.float32), s.astype(np.float32)


def _rope_tables_64(n_tok, ident_rows):
    ang = _axial_angles(n_tok, QK_ROPE)
    cos, sin = np.cos(ang), np.sin(ang)
    half = QK_ROPE // 2
    z = np.zeros((n_tok, LANES - QK_ROPE), np.float32)
    zh = np.zeros((n_tok, half), np.float32)
    c = np.concatenate([cos, cos, z], axis=-1)
    sa = np.concatenate([-sin, zh, z], axis=-1)
    sb = np.concatenate([zh, sin, z], axis=-1)
    ci = np.concatenate([np.ones((ident_rows, QK_ROPE), np.float32),
                         np.zeros((ident_rows, LANES - QK_ROPE), np.float32)], axis=-1)
    zi = np.zeros((ident_rows, LANES), np.float32)
    tables = np.concatenate([c, ci], 0), np.concatenate([sa, zi], 0), np.concatenate([sb, zi], 0)
    return tuple(tab.astype(np.float32) for tab in tables)


def kernel(x, c, ctx, c_ctx, w_ada, b_ada, ln_g, ln_b, a_w_in, a_conv_w, a_q_gain, a_k_gain, a_w_out, m_w_down, m_q_gain, m_kv_gain, m_w_uq, m_w_ukv, m_w_out, router_w, router_b, e_w_gate, e_w_up, e_w_down, s_w_gate, s_w_up, s_w_down):
    batch, seq, d = x.shape
    ctx_len = ctx.shape[1]
    depth = w_ada.shape[0]
    assert depth == 2, "one conv+GQA layer followed by one MLA layer"
    alpha = (2 * depth) ** 0.25
    t_lat = batch * seq
    t_ctx = batch * ctx_len
    t_all = t_lat + t_ctx
    tr = ROW_TILE
    assert seq % tr == 0 and ctx_len % tr == 0 and seq % GRID_W == 0
    lat_tiles = t_lat // tr
    lat_seq_tiles = seq // tr
    ctx_seq_tiles = ctx_len // tr
    lk = ctx_len + seq

    def mod_row(r):
        return jnp.minimum(r // seq, batch)

    def kv_block(i):
        is_lat = i < lat_tiles
        cidx = i - lat_tiles
        b = jnp.where(is_lat, i // lat_seq_tiles, cidx // ctx_seq_tiles)
        rb = jnp.where(is_lat, ctx_seq_tiles + i % lat_seq_tiles, cidx % ctx_seq_tiles)
        return b, rb

    def pos_block(i):
        return jnp.where(i < lat_tiles, i % lat_seq_tiles, lat_seq_tiles)

    rows = -(-(batch + 1) // SUBLANES) * SUBLANES
    cond = jnp.concatenate([c, c_ctx[None, :], jnp.zeros((rows - batch - 1, d), F32)], axis=0)
    mod = _ada_table(cond, w_ada, b_ada).reshape(depth, rows, 1, 6 * d)

    x_pair = (x.reshape(t_lat, d), ctx.reshape(t_ctx, d))

    tm_in = W_IN_ROW_TILE if (t_ctx % W_IN_ROW_TILE == 0 and seq % W_IN_ROW_TILE == 0) else tr
    proj = _mod_matmul(x_pair, mod[0], a_w_in[0].astype(BF16), BF16, tm_in, W_IN_COL_TILE, mod_row)

    cos128, sin128 = _rope_tables_128(seq, tr)
    d_q = ATT_HEADS * HEAD_DIM
    d_kv = ATT_KV_HEADS * HEAD_DIM
    qkv_w = d_q + 2 * d_kv
    qkv_blk = 3 * CONV_DIM // qkv_w
    assert qkv_blk * qkv_w == 3 * CONV_DIM
    q0, k0, v0 = pl.pallas_call(
        functools.partial(_qkprep_kernel, scale=1.0 / math.sqrt(HEAD_DIM)),
        grid=(t_all // tr,),
        in_specs=[
            pl.BlockSpec((tr, qkv_w), lambda i: (i, qkv_blk)),
            pl.BlockSpec((tr, HEAD_DIM), lambda i: (pos_block(i), 0)),
            pl.BlockSpec((tr, HEAD_DIM), lambda i: (pos_block(i), 0)),
            _const_spec((1, HEAD_DIM)), _const_spec((1, HEAD_DIM)),
        ],
        out_specs=[
            pl.BlockSpec((tr, d_q), lambda i: (i, 0)),
            pl.BlockSpec((None, tr, d_kv), lambda i: (*kv_block(i), 0)),
            pl.BlockSpec((None, tr, d_kv), lambda i: (*kv_block(i), 0)),
        ],
        out_shape=[
            jax.ShapeDtypeStruct((t_all, d_q), BF16),
            jax.ShapeDtypeStruct((batch, lk, d_kv), BF16),
            jax.ShapeDtypeStruct((batch, lk, d_kv), BF16),
        ],
        compiler_params=_params(("parallel",)),
        name="qk_prep",
    )(proj, cos128, sin128, a_q_gain[0].reshape(1, HEAD_DIM), a_k_gain[0].reshape(1, HEAD_DIM))

    grp = ATT_HEADS // ATT_KV_HEADS
    att_lat = _attention(q0, k0, v0, batch=batch, sq=seq, lk=lk, n_kv=ATT_KV_HEADS, group=grp, dk=HEAD_DIM,
                         dv=HEAD_DIM, tq=GQA_Q_TILE if seq % GQA_Q_TILE == 0 else tr, rows=ATTN_CHAIN_ROWS, q_row_off=0)
    att_ctx = _attention(q0, k0, v0, batch=batch, sq=ctx_len, lk=ctx_len, n_kv=ATT_KV_HEADS, group=grp,
                         dk=HEAD_DIM, dv=HEAD_DIM, tq=tr, rows=ATTN_CHAIN_ROWS, q_row_off=t_lat)

    conv0 = _conv_gate(proj, a_conv_w[0], t=t_all, tm=tr, tc=CONV_COL_TILE, lat_tiles=lat_tiles,
                       lat_seq_tiles=lat_seq_tiles, ctx_seq_tiles=ctx_seq_tiles)

    w_out0 = a_w_out[0].astype(BF16)
    x1, tok0 = _outproj_ln([conv0, (att_lat, att_ctx)], [w_out0[:CONV_DIM], w_out0[CONV_DIM:]], x_pair, mod[0],
                           ln_g[0, 0].reshape(1, d), ln_b[0, 0].reshape(1, d), t=t_all, tm=tr, alpha=alpha,
                           mod_row=mod_row)

    x2, u1 = _moe(tok0, x1, t_all, 0, router_w[0], router_b[0], e_w_gate, e_w_up, e_w_down,
                  s_w_gate[0].astype(BF16), s_w_up[0].astype(BF16), s_w_down[0].astype(BF16), mod[0],
                  ln_g[0, 1].reshape(1, d), ln_b[0, 1].reshape(1, d), mod[1], alpha=alpha,
                  mod_row=mod_row, tm=COMBINE_ROW_TILE)

    n_down = Q_LORA + KV_LORA + QK_ROPE
    n_down_pad = -(-n_down // LANES) * LANES
    w_down = jnp.pad(m_w_down[0], ((0, 0), (0, n_down_pad - n_down))).astype(BF16)
    down = _matmul(u1, w_down, F32, DOWN_ROW_TILE, n_down_pad)

    dqk = QK_NOPE + QK_ROPE
    w_uq = m_w_uq[0].reshape(Q_LORA, MLA_HEADS, dqk)
    w_uq = jnp.pad(w_uq, ((0, 0), (0, 0), (0, MLA_DK_PAD - dqk))).reshape(Q_LORA, MLA_HEADS * MLA_DK_PAD).astype(BF16)
    w_ukv = m_w_ukv[0].reshape(KV_LORA, MLA_HEADS, QK_NOPE + V_DIM)
    w_uk = w_ukv[:, :, :QK_NOPE].reshape(KV_LORA, MLA_HEADS * QK_NOPE).astype(BF16)
    w_uv = w_ukv[:, :, QK_NOPE:].reshape(KV_LORA, MLA_HEADS * V_DIM).astype(BF16)

    c64, sa64, sb64 = _rope_tables_64(seq, tr)
    rope_specs = [pl.BlockSpec((tr, LANES), lambda i: (pos_block(i), 0))] * 3
    q1 = pl.pallas_call(
        functools.partial(_mla_q_kernel, scale=1.0 / math.sqrt(dqk)),
        grid=(lat_tiles,),
        in_specs=[
            pl.BlockSpec((tr, Q_LORA), lambda i: (i, 0)),
            _const_spec((1, Q_LORA)),
            _const_spec(w_uq.shape),
        ] + rope_specs,
        out_specs=pl.BlockSpec((tr, MLA_HEADS * MLA_DK_PAD), lambda i: (i, 0)),
        out_shape=jax.ShapeDtypeStruct((t_lat, MLA_HEADS * MLA_DK_PAD), BF16),
        compiler_params=_params(("parallel",)),
        name="mla_q",
    )(down, m_q_gain[0].reshape(1, Q_LORA), w_uq, c64, sa64, sb64)

    assert KV_LORA == Q_LORA and (Q_LORA + KV_LORA) % LANES == 0
    k1, v1 = pl.pallas_call(
        _mla_kv_kernel,
        grid=(t_all // tr,),
        in_specs=[
            pl.BlockSpec((tr, KV_LORA), lambda i: (i, 1)),
            pl.BlockSpec((tr, LANES), lambda i: (i, (Q_LORA + KV_LORA) // LANES)),
            _const_spec((1, KV_LORA)),
            _const_spec(w_uk.shape), _const_spec(w_uv.shape),
        ] + rope_specs,
        out_specs=[
            pl.BlockSpec((None, tr, MLA_HEADS * MLA_DK_PAD), lambda i: (*kv_block(i), 0)),
            pl.BlockSpec((None, tr, MLA_HEADS * V_DIM), lambda i: (*kv_block(i), 0)),
        ],
        out_shape=[
            jax.ShapeDtypeStruct((batch, lk, MLA_HEADS * MLA_DK_PAD), BF16),
            jax.ShapeDtypeStruct((batch, lk, MLA_HEADS * V_DIM), BF16),
        ],
        compiler_params=_params(("parallel",)),
        name="mla_kv",
    )(down, down, m_kv_gain[0].reshape(1, KV_LORA), w_uk, w_uv, c64, sa64, sb64)

    att1 = _attention(q1, k1, v1, batch=batch, sq=seq, lk=lk, n_kv=MLA_HEADS, group=1, dk=MLA_DK_PAD, dv=V_DIM,
                      tq=next(q for q in MLA_Q_TILES + (tr,) if seq % q == 0), rows=ATTN_CHAIN_ROWS, q_row_off=0)

    x3, tok1 = _outproj_ln([att1], [m_w_out[0].astype(BF16)], x2, mod[1], ln_g[1, 0].reshape(1, d),
                           ln_b[1, 0].reshape(1, d), t=t_lat, tm=tr, alpha=alpha, mod_row=mod_row)

    (x4,) = _moe(tok1, x3, t_lat, 1, router_w[1], router_b[1], e_w_gate, e_w_up, e_w_down,
                 s_w_gate[1].astype(BF16), s_w_up[1].astype(BF16), s_w_down[1].astype(BF16), mod[1],
                 ln_g[1, 1].reshape(1, d), ln_b[1, 1].reshape(1, d), None, alpha=alpha,
                 mod_row=mod_row, tm=COMBINE_ROW_TILE)
    return x4.reshape(batch, seq, d)
```

```python
import functools
import math

import jax
import jax.numpy as jnp
import numpy as np
from jax import lax
from jax.experimental import pallas as pl
from jax.experimental.pallas import tpu as pltpu
from jax.experimental.pallas import tpu_sc as plsc

F32 = jnp.float32
BF16 = jnp.bfloat16

GRID_W = 64
CONV_DIM = 1024
ATT_HEADS = 8
ATT_KV_HEADS = 2
HEAD_DIM = 128
MLA_HEADS = 16
Q_LORA = 512
KV_LORA = 512
QK_NOPE = 128
QK_ROPE = 64
V_DIM = 128
N_EXPERTS = 64
TOP_K = 6
N_GROUPS = 8
TOPK_GROUPS = 4
ROUTED_SCALE = 2.5
ROPE_THETA = 10000.0
LN_EPS = 1e-5
RMS_EPS = 1e-6

V7X_VMEM_LIMIT_BYTES = 56 * 1024 * 1024
LANES = 128
SUBLANES = 8
MOE_ROWS = 512
MOE_DISPATCH_SPLIT = (1, 3, 2, 2)
MOE_COMBINE_PARTS = 2
V7X_SC_CORES = 2
V7X_SC_SUBCORES = 16
SC_GATHER_ROWS = 16
SC_GATHER_BUFFERS = 4
MLA_DK_PAD = 256

ROW_TILE = 256
ADA_COL_TILE = 1024
W_IN_ROW_TILE = 1024
W_IN_COL_TILE = 768
DOWN_ROW_TILE = 512
CONV_COL_TILE = 512
ROUTER_TILE = 512
ATTN_CHAIN_ROWS = 256
GQA_Q_TILE = 512
MLA_Q_TILES = (2048, 1024, 512)
COMBINE_ROW_TILE = 256


def _params(sem):
    return pltpu.CompilerParams(dimension_semantics=sem, vmem_limit_bytes=V7X_VMEM_LIMIT_BYTES)


def _const_spec(shape):
    nd = len(shape)
    return pl.BlockSpec(shape, lambda *_: (0,) * nd)


def _ada_kernel(s_ref, w_ref, b_ref, o_ref):
    s = s_ref[...]
    s = s * (1.0 / (1.0 + jnp.exp(-s)))
    o_ref[...] = jnp.dot(s.astype(BF16), w_ref[...].astype(BF16), preferred_element_type=F32) + b_ref[...]


def _ada_table(cond, w_ada, b_ada):
    depth, d, n = w_ada.shape
    r = cond.shape[0]
    tn = ADA_COL_TILE
    return pl.pallas_call(
        _ada_kernel,
        grid=(depth, n // tn),
        in_specs=[
            pl.BlockSpec((r, d), lambda l, j: (0, 0)),
            pl.BlockSpec((None, d, tn), lambda l, j: (l, 0, j)),
            pl.BlockSpec((None, 1, tn), lambda l, j: (l, 0, j)),
        ],
        out_specs=pl.BlockSpec((None, r, tn), lambda l, j: (l, 0, j)),
        out_shape=jax.ShapeDtypeStruct((depth, r, n), F32),
        compiler_params=_params(("parallel", "parallel")),
        name="ada_table",
    )(cond, w_ada, b_ada.reshape(depth, 1, n))


def _mod_spec(d, chunk, mod_row, tm):
    return pl.BlockSpec((None, 1, d), lambda i: (mod_row(i * tm), 0, chunk))


def _mm_kernel(a_ref, w_ref, o_ref):
    o_ref[...] = jnp.dot(a_ref[...], w_ref[...], preferred_element_type=F32).astype(o_ref.dtype)


def _matmul(a, w, out_dtype, tm, tn):
    m, k = a.shape
    n = w.shape[1]
    return pl.pallas_call(
        _mm_kernel,
        grid=(m // tm, n // tn),
        in_specs=[
            pl.BlockSpec((tm, k), lambda i, j: (i, 0)),
            pl.BlockSpec((k, tn), lambda i, j: (0, j)),
        ],
        out_specs=pl.BlockSpec((tm, tn), lambda i, j: (i, j)),
        out_shape=jax.ShapeDtypeStruct((m, n), out_dtype),
        compiler_params=_params(("parallel", "parallel")),
        name="matmul",
    )(a, w)


def _pair_specs(pair, tm):
    lat, ctx = pair
    lat_tiles = lat.shape[0] // tm
    assert lat_tiles * tm == lat.shape[0] and ctx.shape[0] % tm == 0 and lat.shape[1] == ctx.shape[1]
    width = lat.shape[1]
    return [pl.BlockSpec((tm, width), lambda i, *_: (jnp.minimum(i, lat_tiles - 1), 0)),
            pl.BlockSpec((tm, width), lambda i, *_: (jnp.maximum(i - lat_tiles, 0), 0))]


def _pair_tile(lat_ref, ctx_ref, lat_tiles):
    return jnp.where(pl.program_id(0) < lat_tiles, lat_ref[...], ctx_ref[...])


def _mod_mm_kernel(xl_ref, xc_ref, sc_ref, sh_ref, w_ref, o_ref, u_ref, *, lat_tiles):
    @pl.when(pl.program_id(1) == 0)
    def _():
        x = _pair_tile(xl_ref, xc_ref, lat_tiles)
        u_ref[...] = (x * (1.0 + sc_ref[...]) + sh_ref[...]).astype(u_ref.dtype)

    o_ref[...] = jnp.dot(u_ref[...], w_ref[...], preferred_element_type=F32).astype(o_ref.dtype)


def _mod_matmul(x_pair, mod, w, out_dtype, tm, tn, mod_row):
    m = x_pair[0].shape[0] + x_pair[1].shape[0]
    k = x_pair[0].shape[1]
    n = w.shape[1]

    def mod_spec(chunk):
        return pl.BlockSpec((None, 1, k), lambda i, j: (mod_row(i * tm), 0, chunk))

    return pl.pallas_call(
        functools.partial(_mod_mm_kernel, lat_tiles=x_pair[0].shape[0] // tm),
        grid=(m // tm, n // tn),
        in_specs=_pair_specs(x_pair, tm) + [
            mod_spec(1), mod_spec(0),
            pl.BlockSpec((k, tn), lambda i, j: (0, j)),
        ],
        out_specs=pl.BlockSpec((tm, tn), lambda i, j: (i, j)),
        out_shape=jax.ShapeDtypeStruct((m, n), out_dtype),
        scratch_shapes=[pltpu.VMEM((tm, k), w.dtype)],
        compiler_params=_params(("parallel", "arbitrary")),
        name="mod_matmul",
    )(*x_pair, mod, mod, w)


def _rms(t, gain):
    return t * lax.rsqrt(jnp.mean(t * t, axis=-1, keepdims=True) + RMS_EPS) * gain


def _qkprep_kernel(p_ref, cos_ref, sin_ref, qg_ref, kg_ref, q_ref, k_ref, v_ref, *, scale):
    cos = cos_ref[...]
    sin = sin_ref[...]

    def norm_rope(t, gain):
        y = _rms(t.astype(F32), gain)
        return y * cos + pltpu.roll(y, HEAD_DIM // 2, 1) * sin

    for h in range(ATT_HEADS):
        sl = slice(h * HEAD_DIM, (h + 1) * HEAD_DIM)
        q_ref[:, sl] = (norm_rope(p_ref[:, sl], qg_ref[...]) * scale).astype(q_ref.dtype)
    k0 = ATT_HEADS * HEAD_DIM
    for h in range(ATT_KV_HEADS):
        sl = slice(h * HEAD_DIM, (h + 1) * HEAD_DIM)
        k_ref[:, sl] = norm_rope(p_ref[:, k0 + h * HEAD_DIM:k0 + (h + 1) * HEAD_DIM], kg_ref[...]).astype(k_ref.dtype)
    v0 = k0 + ATT_KV_HEADS * HEAD_DIM
    v_ref[...] = p_ref[:, v0:v0 + ATT_KV_HEADS * HEAD_DIM].astype(v_ref.dtype)


def _attn_kernel(q_ref, k_ref, v_ref, o_ref, *, group, tq, rows, dk, dv):
    k = k_ref[...]
    v = v_ref[...]
    for h in range(group):
        for r in range(0, tq, rows):
            q = q_ref[r:r + rows, h * dk:(h + 1) * dk]
            s = lax.dot_general(q, k, (((1,), (1,)), ((), ())), preferred_element_type=F32)
            m = jnp.max(s, axis=-1, keepdims=True)
            p = jnp.exp(s - m)
            l = jnp.sum(p, axis=-1, keepdims=True)
            o = jnp.dot(p.astype(v.dtype), v, preferred_element_type=F32)
            o_ref[r:r + rows, h * dv:(h + 1) * dv] = (o / l).astype(o_ref.dtype)


def _attention(q, k, v, *, batch, sq, lk, n_kv, group, dk, dv, tq, rows, q_row_off):
    nq = sq // tq
    off = q_row_off // tq
    assert tq % rows == 0 and q_row_off % tq == 0 and sq % tq == 0
    return pl.pallas_call(
        functools.partial(_attn_kernel, group=group, tq=tq, rows=rows, dk=dk, dv=dv),
        grid=(batch, n_kv, nq),
        in_specs=[
            pl.BlockSpec((tq, group * dk), lambda b, g, i: (off + b * nq + i, g)),
            pl.BlockSpec((None, lk, dk), lambda b, g, i: (b, 0, g)),
            pl.BlockSpec((None, lk, dv), lambda b, g, i: (b, 0, g)),
        ],
        out_specs=pl.BlockSpec((tq, group * dv), lambda b, g, i: (b * nq + i, g)),
        out_shape=jax.ShapeDtypeStruct((batch * sq, n_kv * group * dv), BF16),
        compiler_params=_params(("parallel", "parallel", "parallel")),
        name="attention",
    )(q, k, v)


def _conv_kernel(gb_ref, gc_ref, hv_ref, gcp_ref, hvp_ref, gcn_ref, hvn_ref, w_ref, o_ref, *,
                 tm, lat_tiles, lat_seq_tiles, ctx_seq_tiles):
    i = pl.program_id(0)
    is_lat = i < lat_tiles
    pos = jnp.where(is_lat, i % lat_seq_tiles, (i - lat_tiles) % ctx_seq_tiles)
    seq_tiles = jnp.where(is_lat, lat_seq_tiles, ctx_seq_tiles)
    not_first = (pos != 0).astype(F32)
    not_last = (pos != seq_tiles - 1).astype(F32)
    p = gc_ref[...].astype(F32) * hv_ref[...].astype(F32)
    halo_prev = gcp_ref[SUBLANES - 1:SUBLANES, :].astype(F32) * hvp_ref[SUBLANES - 1:SUBLANES, :].astype(F32) * not_first
    halo_next = gcn_ref[0:1, :].astype(F32) * hvn_ref[0:1, :].astype(F32) * not_last
    row = lax.broadcasted_iota(jnp.int32, p.shape, 0)
    prev = jnp.where(row == 0, halo_prev, pltpu.roll(p, 1, 0))
    nxt = jnp.where(row == tm - 1, halo_next, pltpu.roll(p, tm - 1, 0))
    w = w_ref[...]
    conv = w[0:1, :] * prev + w[1:2, :] * p + w[2:3, :] * nxt
    o_ref[...] = (gb_ref[...].astype(F32) * conv).astype(o_ref.dtype)


def _conv_gate(p, conv_w, *, t, tm, tc, lat_tiles, lat_seq_tiles, ctx_seq_tiles):
    nct = CONV_DIM // tc
    hb = tm // SUBLANES
    n_halo = t // SUBLANES

    def cur(part):
        return pl.BlockSpec((tm, tc), lambda i, j: (i, part * nct + j))

    def prev(part):
        return pl.BlockSpec((SUBLANES, tc), lambda i, j: (jnp.maximum(i * hb - 1, 0), part * nct + j))

    def nxt(part):
        return pl.BlockSpec((SUBLANES, tc), lambda i, j: (jnp.minimum((i + 1) * hb, n_halo - 1), part * nct + j))

    return pl.pallas_call(
        functools.partial(_conv_kernel, tm=tm, lat_tiles=lat_tiles, lat_seq_tiles=lat_seq_tiles,
                          ctx_seq_tiles=ctx_seq_tiles),
        grid=(t // tm, nct),
        in_specs=[cur(0), cur(1), cur(2), prev(1), prev(2), nxt(1), nxt(2),
                  pl.BlockSpec((3, tc), lambda i, j: (0, j))],
        out_specs=pl.BlockSpec((tm, tc), lambda i, j: (i, j)),
        out_shape=jax.ShapeDtypeStruct((t, CONV_DIM), BF16),
        compiler_params=_params(("parallel", "parallel")),
        name="conv_gate",
    )(p, p, p, p, p, p, p, conv_w)


def _layer_norm(z, g, b):
    mu = jnp.mean(z, axis=-1, keepdims=True)
    zc = z - mu
    var = jnp.mean(zc * zc, axis=-1, keepdims=True)
    return zc * lax.rsqrt(var + LN_EPS) * g + b


def _pack_bf16_pairs(x):
    half = x.shape[1] // 2
    lo = lax.bitcast_convert_type(x[:, :half].astype(BF16).astype(F32), jnp.uint32) >> 16
    hi = lax.bitcast_convert_type(x[:, half:].astype(BF16).astype(F32), jnp.uint32) & jnp.uint32(0xFFFF0000)
    return lax.bitcast_convert_type(lo | hi, jnp.int32)


def _unpack_bf16_pairs(w):
    u = lax.bitcast_convert_type(w, jnp.uint32)
    lo = lax.bitcast_convert_type(u << 16, F32).astype(BF16)
    hi = lax.bitcast_convert_type(u & jnp.uint32(0xFFFF0000), F32).astype(BF16)
    return lo, hi


def _dot_halves(lo, hi, w_ref):
    half = lo.shape[1]
    return (jnp.dot(lo, w_ref[:half, :], preferred_element_type=F32)
            + jnp.dot(hi, w_ref[half:, :], preferred_element_type=F32))


def _outproj_ln_kernel(*refs, widths, lat_tiles, alpha):
    refs = list(refs)

    def take(width):
        got = [refs.pop(0) for _ in range(width)]
        return got[0][...] if width == 1 else _pair_tile(got[0], got[1], lat_tiles)

    acts = [take(w) for w in widths[:-1]]
    w_refs = [refs.pop(0) for _ in acts]
    x = take(widths[-1])
    gate_ref, lng_ref, lnb_ref, sc_ref, sh_ref, xo_ref, tok_ref = refs
    y = jnp.dot(acts[0], w_refs[0][...], preferred_element_type=F32)
    for a, w_ref in zip(acts[1:], w_refs[1:]):
        y = y + jnp.dot(a, w_ref[...], preferred_element_type=F32)
    xn = _layer_norm(alpha * x + gate_ref[...] * y, lng_ref[...], lnb_ref[...])
    xo_ref[...] = xn
    tok_ref[...] = _pack_bf16_pairs(xn * (1.0 + sc_ref[...]) + sh_ref[...])


def _outproj_ln(a_list, w_list, x, mod, ln_g, ln_b, *, t, tm, alpha, mod_row):
    d = w_list[0].shape[1]
    operands, in_specs, widths, lat_tiles = [], [], [], 0

    def add_rows(src):
        nonlocal lat_tiles
        if isinstance(src, tuple):
            in_specs.extend(_pair_specs(src, tm))
            operands.extend(src)
            widths.append(2)
            lat_tiles = src[0].shape[0] // tm
        else:
            in_specs.append(pl.BlockSpec((tm, src.shape[1]), lambda i: (i, 0)))
            operands.append(src)
            widths.append(1)

    for a in a_list:
        add_rows(a)
    in_specs += [_const_spec(w.shape) for w in w_list]
    operands += list(w_list)
    add_rows(x)
    in_specs += [
        _mod_spec(d, 2, mod_row, tm),
        _const_spec((1, d)), _const_spec((1, d)),
        _mod_spec(d, 4, mod_row, tm),
        _mod_spec(d, 3, mod_row, tm),
    ]
    return pl.pallas_call(
        functools.partial(_outproj_ln_kernel, widths=tuple(widths), lat_tiles=lat_tiles, alpha=alpha),
        grid=(t // tm,),
        in_specs=in_specs,
        out_specs=[pl.BlockSpec((tm, d), lambda i: (i, 0)), pl.BlockSpec((tm, d // 2), lambda i: (i, 0))],
        out_shape=[jax.ShapeDtypeStruct((t, d), F32), jax.ShapeDtypeStruct((t, d // 2), jnp.int32)],
        compiler_params=_params(("parallel",)),
        name="outproj_ln",
    )(*operands, mod, ln_g, ln_b, mod, mod)


def _router_kernel(t_ref, rw_ref, rb_ref, tri_ref, idx_ref, gw_ref, rank_ref, cnt_ref):
    @pl.when(pl.program_id(0) == 0)
    def _():
        cnt_ref[...] = jnp.zeros_like(cnt_ref)

    lo, hi = _unpack_bf16_pairs(t_ref[...])
    half = lo.shape[1]
    nt = (((1,), (1,)), ((), ()))
    logits = (lax.dot_general(rw_ref[:, :half], lo, nt, preferred_element_type=F32)
              + lax.dot_general(rw_ref[:, half:], hi, nt, preferred_element_type=F32))
    scores = 1.0 / (1.0 + jnp.exp(-logits))
    sel = scores + rb_ref[...]
    gsz = N_EXPERTS // N_GROUPS
    neg = -jnp.inf
    sub = lax.broadcasted_iota(jnp.int32, (gsz, sel.shape[1]), 0)
    slabs = [sel[g * gsz:(g + 1) * gsz, :] for g in range(N_GROUPS)]
    gscore = []
    for s in slabs:
        m1 = jnp.max(s, axis=0, keepdims=True)
        a1 = jnp.min(jnp.where(s == m1, sub, gsz), axis=0, keepdims=True)
        m2 = jnp.max(jnp.where(sub == a1, neg, s), axis=0, keepdims=True)
        gscore.append(m1 + m2)
    masked = []
    for g in range(N_GROUPS):
        ahead = jnp.zeros(gscore[g].shape, jnp.int32)
        for h in range(N_GROUPS):
            if h == g:
                continue
            beats = gscore[h] >= gscore[g] if h < g else gscore[h] > gscore[g]
            ahead = ahead + beats.astype(jnp.int32)
        masked.append(jnp.where(ahead < TOPK_GROUPS, slabs[g], neg))
    cur = jnp.concatenate(masked, axis=0)
    eio = lax.broadcasted_iota(jnp.int32, cur.shape, 0)
    picks, weights = [], []
    for _ in range(TOP_K):
        m = jnp.max(cur, axis=0, keepdims=True)
        a = jnp.min(jnp.where(cur == m, eio, N_EXPERTS), axis=0, keepdims=True)
        hit = eio == a
        picks.append(a)
        weights.append(jnp.sum(jnp.where(hit, scores, 0.0), axis=0, keepdims=True))
        cur = jnp.where(hit, neg, cur)
    total = weights[0]
    for w in weights[1:]:
        total = total + w
    for k in range(TOP_K):
        idx_ref[k:k + 1, :] = picks[k]
        gw_ref[k:k + 1, :] = weights[k] / total * ROUTED_SCALE
    for k in range(TOP_K, SUBLANES):
        idx_ref[k:k + 1, :] = jnp.zeros_like(picks[0])
        gw_ref[k:k + 1, :] = jnp.zeros_like(weights[0])
        rank_ref[k:k + 1, :] = jnp.zeros_like(picks[0])
    base = cnt_ref[:, 0:1]
    for k in range(TOP_K):
        onehot = jnp.where(eio == picks[k], 1.0, 0.0)
        before = jnp.dot(onehot.astype(BF16), tri_ref[...], preferred_element_type=F32)
        rank_ref[k:k + 1, :] = jnp.sum(onehot * (before + base), axis=0, keepdims=True).astype(jnp.int32)
        base = base + jnp.sum(onehot, axis=1, keepdims=True)
    cnt_ref[...] = jnp.broadcast_to(base, cnt_ref.shape)


def _router(tok, rw_t, rb, *, t, tt):
    half = tok.shape[1]
    tri = jnp.asarray(np.arange(tt)[:, None] < np.arange(tt)[None, :], BF16)
    blk = pl.BlockSpec((SUBLANES, tt), lambda i: (0, i))
    return pl.pallas_call(
        _router_kernel,
        grid=(t // tt,),
        in_specs=[
            pl.BlockSpec((tt, half), lambda i: (i, 0)),
            _const_spec((N_EXPERTS, 2 * half)),
            _const_spec((N_EXPERTS, 1)),
            _const_spec((tt, tt)),
        ],
        out_specs=[blk, blk, blk, _const_spec((N_EXPERTS, LANES))],
        out_shape=[jax.ShapeDtypeStruct((SUBLANES, t), jnp.int32), jax.ShapeDtypeStruct((SUBLANES, t), F32),
                   jax.ShapeDtypeStruct((SUBLANES, t), jnp.int32), jax.ShapeDtypeStruct((N_EXPERTS, LANES), F32)],
        compiler_params=_params(("arbitrary",)),
        name="router",
    )(tok, rw_t, rb, tri)


def _slots_kernel(idx_ref, rank_ref, start_ref, pos_ref):
    start = start_ref[...]
    eio = lax.broadcasted_iota(jnp.int32, (N_EXPERTS, idx_ref.shape[1]), 0)
    for k in range(TOP_K):
        seg = jnp.sum(jnp.where(eio == idx_ref[k:k + 1, :], start, 0.0), axis=0, keepdims=True)
        pos_ref[k:k + 1, :] = rank_ref[k:k + 1, :] + seg.astype(jnp.int32)
    for k in range(TOP_K, SUBLANES):
        pos_ref[k:k + 1, :] = jnp.zeros((1, idx_ref.shape[1]), jnp.int32)


def _assign_slots(idx, rank, seg_start, *, t, tt):
    blk = pl.BlockSpec((SUBLANES, tt), lambda i: (0, i))
    return pl.pallas_call(
        _slots_kernel,
        grid=(t // tt,),
        in_specs=[blk, blk, _const_spec((N_EXPERTS, 1))],
        out_specs=blk,
        out_shape=jax.ShapeDtypeStruct((SUBLANES, t), jnp.int32),
        compiler_params=_params(("parallel",)),
        name="assign_slots",
    )(idx, rank, seg_start)


def _sc_gather_rows(table, idx):
    n = idx.shape[0]
    d = table.shape[1]
    n_workers = V7X_SC_CORES * V7X_SC_SUBCORES
    per_w = n // n_workers
    n_chunks = per_w // SC_GATHER_ROWS
    assert per_w * n_workers == n and n_chunks * SC_GATHER_ROWS == per_w
    n_buf = next(b for b in range(SC_GATHER_BUFFERS, 1, -1) if n_chunks % b == 0)
    mesh = plsc.VectorSubcoreMesh(core_axis_name="c", subcore_axis_name="s", num_cores=V7X_SC_CORES,
                                  num_subcores=V7X_SC_SUBCORES)

    @functools.partial(
        pl.kernel,
        out_type=jax.ShapeDtypeStruct((n, d), table.dtype),
        mesh=mesh,
        scratch_types=[
            pltpu.VMEM((per_w,), jnp.int32),
            pltpu.VMEM((n_buf, SC_GATHER_ROWS, d), table.dtype),
            pltpu.SemaphoreType.DMA((n_buf,)),
            pltpu.SemaphoreType.DMA((n_buf,)),
        ],
        name="sc_gather_rows",
    )
    def gather(table_hbm, idx_hbm, out_hbm, idx_v, rows_v, gsem, wsem):
        wid = lax.axis_index("s") * V7X_SC_CORES + lax.axis_index("c")
        base = wid * per_w
        pltpu.sync_copy(idx_hbm.at[pl.ds(base, per_w)], idx_v)

        def gather_copy(c, b):
            return pltpu.make_async_copy(table_hbm.at[idx_v.at[pl.ds(c * SC_GATHER_ROWS, SC_GATHER_ROWS)]],
                                         rows_v.at[b], gsem.at[b])

        def write_copy(c, b):
            return pltpu.make_async_copy(rows_v.at[b], out_hbm.at[pl.ds(base + c * SC_GATHER_ROWS, SC_GATHER_ROWS)],
                                         wsem.at[b])

        for b in range(n_buf - 1):
            gather_copy(b, b).start()

        @pl.loop(0, n_chunks, step=n_buf)
        def _(g):
            for b in range(n_buf):
                c = g + b
                prev = (b + n_buf - 1) % n_buf
                gather_copy(c, b).wait()
                write_copy(c, b).start()

                @pl.when(c >= 1)
                def _():
                    write_copy(c - 1, prev).wait()

                @pl.when(c + n_buf - 1 < n_chunks)
                def _():
                    gather_copy(c + n_buf - 1, prev).start()

        write_copy(n_chunks - 1, (n_chunks - 1) % n_buf).wait()

    return gather(table, idx)


def _experts_kernel(be_ref, nbu_ref, seg_ref, nstart_ref, x_ref, wg_hbm, wu_hbm, wd_hbm, *rest, block_off, n_call,
                    layer):
    y_ref, wgf, wuf, wdf, wgb, wub, wdb, sem = rest[-8:]
    step = pl.program_id(0)
    b = block_off + step
    nbu = nbu_ref[0]
    end = jnp.minimum(nbu, block_off + n_call)

    def weight_copies(e, slot):
        return [pltpu.make_async_copy(hbm.at[layer, e], buf.at[slot], sem.at[slot, k])
                for k, (hbm, buf) in enumerate(((wg_hbm, wgf), (wu_hbm, wuf), (wd_hbm, wdf)))]

    @pl.when(b < nbu)
    def _():
        first = jnp.logical_or(step == 0, be_ref[b] != be_ref[jnp.maximum(b - 1, 0)])

        @pl.when(first)
        def _():
            slot = seg_ref[b] % 2

            @pl.when(step == 0)
            def _():
                for cp in weight_copies(be_ref[b], slot):
                    cp.start()

            for cp in weight_copies(be_ref[b], slot):
                cp.wait()
            wgb[...] = wgf[slot].astype(BF16)
            wub[...] = wuf[slot].astype(BF16)
            wdb[...] = wdf[slot].astype(BF16)
            nxt = nstart_ref[b]

            @pl.when(nxt < end)
            def _():
                for cp in weight_copies(be_ref[jnp.minimum(nxt, be_ref.shape[0] - 1)], 1 - slot):
                    cp.start()

        lo, hi = _unpack_bf16_pairs(x_ref[...])
        hg = _dot_halves(lo, hi, wgb)
        hu = _dot_halves(lo, hi, wub)
        h = hg * (1.0 / (1.0 + jnp.exp(-hg))) * hu
        y_ref[...] = _pack_bf16_pairs(jnp.dot(h.astype(BF16), wdb[...], preferred_element_type=F32))

    @pl.when(b >= nbu)
    def _():
        y_ref[...] = jnp.zeros_like(y_ref)


def _experts(xs, block_e, nb_used, seg_idx, next_start, wg, wu, wd, layer, y_prev, *, block_off, n_blocks):
    half = xs.shape[1]
    d = 2 * half
    ff = wg.shape[3]
    n_call = xs.shape[0] // MOE_ROWS

    def used(b, nbu):
        return jnp.clip(jnp.minimum(block_off + b, nbu[0] - 1) - block_off, 0, n_call - 1)

    hbm = pl.BlockSpec(memory_space=pl.ANY)
    in_specs = [pl.BlockSpec((MOE_ROWS, half), lambda b, be, nbu, seg, nst: (used(b, nbu), 0)), hbm, hbm, hbm]
    args = [block_e, nb_used, seg_idx, next_start, xs, wg, wu, wd]
    aliases = {}
    if y_prev is not None:
        in_specs.append(hbm)
        aliases = {len(args): 0}
        args.append(y_prev)
    grid_spec = pltpu.PrefetchScalarGridSpec(
        num_scalar_prefetch=4,
        grid=(n_call,),
        in_specs=in_specs,
        out_specs=pl.BlockSpec((MOE_ROWS, half), lambda b, be, nbu, seg, nst: (block_off + b, 0)),
        scratch_shapes=[
            pltpu.VMEM((2, d, ff), F32),
            pltpu.VMEM((2, d, ff), F32),
            pltpu.VMEM((2, ff, d), F32),
            pltpu.VMEM((d, ff), BF16),
            pltpu.VMEM((d, ff), BF16),
            pltpu.VMEM((ff, d), BF16),
            pltpu.SemaphoreType.DMA((2, 3)),
        ],
    )
    return pl.pallas_call(
        functools.partial(_experts_kernel, block_off=block_off, n_call=n_call, layer=layer),
        grid_spec=grid_spec,
        out_shape=jax.ShapeDtypeStruct((n_blocks * MOE_ROWS, half), jnp.int32),
        input_output_aliases=aliases,
        compiler_params=_params(("arbitrary",)),
        name="experts",
    )(*args)


def _combine_ln_kernel(*refs, alpha, emit_next, n_prev):
    y_ref, gw_ref, tok_ref, x_ref, sg_ref, su_ref, sd_ref, gate_ref, lng_ref, lnb_ref = refs[:10]
    outs = refs[len(refs) - (2 if emit_next else 1):]
    if emit_next:
        sc_ref, sh_ref = refs[10:12]
        xo_ref, u_ref = outs
    else:
        (xo_ref,) = outs
    lo, hi = _unpack_bf16_pairs(tok_ref[...])
    hg = _dot_halves(lo, hi, sg_ref)
    hu = _dot_halves(lo, hi, su_ref)
    h = hg * (1.0 / (1.0 + jnp.exp(-hg))) * hu
    gw = gw_ref[...]
    f_lo = f_hi = None
    for k in range(TOP_K):
        y_lo, y_hi = _unpack_bf16_pairs(y_ref[k])
        w = gw[:, k:k + 1]
        f_lo = y_lo.astype(F32) * w if f_lo is None else f_lo + y_lo.astype(F32) * w
        f_hi = y_hi.astype(F32) * w if f_hi is None else f_hi + y_hi.astype(F32) * w
    f = jnp.concatenate([f_lo, f_hi], axis=-1) + jnp.dot(h.astype(BF16), sd_ref[...], preferred_element_type=F32)
    xn = _layer_norm(alpha * x_ref[...] + gate_ref[...] * f, lng_ref[...], lnb_ref[...])
    xo_ref[...] = xn
    if emit_next:
        u_ref[...] = (xn * (1.0 + sc_ref[...]) + sh_ref[...]).astype(u_ref.dtype)


def _combine_ln(y3, gw_t, tok, x, sg, su, sd, mod, ln_g, ln_b, mod_next, prev, *, t, row_off, tm, alpha, mod_row):
    d = x.shape[1]
    emit_next = mod_next is not None
    off = row_off // tm
    assert off * tm == row_off

    def rows(i):
        return (off + i, 0)

    def part_mod_row(r):
        return mod_row(r + row_off)

    in_specs = [
        pl.BlockSpec((TOP_K, tm, d // 2), lambda i: (0, i, 0)),
        pl.BlockSpec((tm, SUBLANES), rows),
        pl.BlockSpec((tm, d // 2), rows),
        pl.BlockSpec((tm, d), rows),
        _const_spec(sg.shape), _const_spec(su.shape), _const_spec(sd.shape),
        _mod_spec(d, 5, part_mod_row, tm),
        _const_spec((1, d)), _const_spec((1, d)),
    ]
    args = [y3, gw_t, tok, x, sg, su, sd, mod, ln_g, ln_b]
    out_specs = [pl.BlockSpec((tm, d), rows)]
    out_shape = [jax.ShapeDtypeStruct((t, d), F32)]
    if emit_next:
        in_specs += [_mod_spec(d, 1, part_mod_row, tm), _mod_spec(d, 0, part_mod_row, tm)]
        args += [mod_next, mod_next]
        out_specs.append(pl.BlockSpec((tm, d), rows))
        out_shape.append(jax.ShapeDtypeStruct((t, d), BF16))
    aliases = {}
    if prev is not None:
        for k, p in enumerate(prev):
            in_specs.append(pl.BlockSpec(memory_space=pl.ANY))
            aliases[len(args)] = k
            args.append(p)
    return pl.pallas_call(
        functools.partial(_combine_ln_kernel, alpha=alpha, emit_next=emit_next, n_prev=len(aliases)),
        grid=(y3.shape[1] // tm,),
        in_specs=in_specs,
        out_specs=out_specs,
        out_shape=out_shape,
        input_output_aliases=aliases,
        compiler_params=_params(("parallel",)),
        name="combine_ln",
    )(*args)


def _moe(tok, x, t, layer, router_w, router_b, wg, wu, wd, sg, su, sd, mod, ln_g, ln_b, mod_next, *, alpha, mod_row,
         tm):
    half = tok.shape[1]
    tt = ROUTER_TILE
    idx, gw, rank, cnt = _router(tok, router_w.T.astype(BF16), router_b.reshape(N_EXPERTS, 1), t=t, tt=tt)
    n_asg = t * TOP_K
    counts = cnt[:, 0].astype(jnp.int32)
    padded = (counts + MOE_ROWS - 1) // MOE_ROWS * MOE_ROWS
    pend = jnp.cumsum(padded)
    pstart = pend - padded
    sc_rows = V7X_SC_CORES * V7X_SC_SUBCORES * SC_GATHER_ROWS * 2
    blocks_granule = sum(MOE_DISPATCH_SPLIT) * max(sc_rows // MOE_ROWS, 1)
    assert (blocks_granule // sum(MOE_DISPATCH_SPLIT) * MOE_ROWS) % sc_rows == 0
    n_blocks = -(-((n_asg + N_EXPERTS * (MOE_ROWS - 1)) // MOE_ROWS + 1) // blocks_granule) * blocks_granule
    assert n_asg % sc_rows == 0
    block_start = jnp.arange(n_blocks, dtype=jnp.int32) * MOE_ROWS
    block_e = jnp.minimum(jnp.sum((pend[None, :] <= block_start[:, None]).astype(jnp.int32), axis=1), N_EXPERTS - 1)
    nb_used = (pend[-1] // MOE_ROWS).astype(jnp.int32).reshape(1)
    blk = jnp.arange(n_blocks, dtype=jnp.int32)
    seg_first = jnp.concatenate([jnp.ones((1,), bool), block_e[1:] != block_e[:-1]])
    seg_idx = jnp.cumsum(seg_first.astype(jnp.int32)) - 1
    later_first = lax.cummin(jnp.where(seg_first, blk, n_blocks), reverse=True)
    next_start = jnp.concatenate([later_first[1:], jnp.full((1,), n_blocks, jnp.int32)])
    pos2d = _assign_slots(idx, rank, pstart.astype(F32).reshape(N_EXPERTS, 1), t=t, tt=tt)[:TOP_K]
    pos = pos2d.reshape(-1)
    tok_of_asg = np.tile(np.arange(t, dtype=np.int32), TOP_K)
    n_pad = n_blocks * MOE_ROWS - n_asg
    seg_pad_end = jnp.cumsum(padded - counts)
    j = jnp.arange(n_pad, dtype=jnp.int32)
    pad_e = jnp.sum((seg_pad_end[None, :] <= j[:, None]).astype(jnp.int32), axis=1)
    seg_base = pstart + counts - (seg_pad_end - (padded - counts))
    in_seg = j + jnp.sum(jnp.where(pad_e[:, None] == jnp.arange(N_EXPERTS)[None, :], seg_base[None, :], 0), axis=1)
    pad_slot = jnp.where(pad_e < N_EXPERTS, in_seg, pend[-1] + j - seg_pad_end[-1])
    _, slot_tok = lax.sort((jnp.concatenate([pos, pad_slot]), jnp.concatenate([tok_of_asg, pad_slot % t])),
                           num_keys=1)
    unit = n_blocks // sum(MOE_DISPATCH_SPLIT)
    starts = [unit * sum(MOE_DISPATCH_SPLIT[:i]) for i in range(len(MOE_DISPATCH_SPLIT) + 1)]
    xs = [_sc_gather_rows(tok, slot_tok[lo * MOE_ROWS:hi * MOE_ROWS]) for lo, hi in zip(starts[:-1], starts[1:])]
    y = None
    for part, lo in zip(xs, starts[:-1]):
        y = _experts(part, block_e, nb_used, seg_idx, next_start, wg, wu, wd, layer, y, block_off=lo,
                     n_blocks=n_blocks)
    n_cparts = MOE_COMBINE_PARTS if (t // MOE_COMBINE_PARTS * TOP_K) % sc_rows == 0 else 1
    t_part = t // n_cparts
    y3 = [_sc_gather_rows(y, pos2d[:, i * t_part:(i + 1) * t_part].reshape(-1)).reshape(TOP_K, t_part, half)
          for i in range(n_cparts)]
    outs = None
    gw_t = gw.T
    for i in range(n_cparts):
        outs = _combine_ln(y3[i], gw_t, tok, x, sg, su, sd, mod, ln_g, ln_b, mod_next, outs, t=t,
                           row_off=i * t_part, tm=tm, alpha=alpha, mod_row=mod_row)
    return outs


def _rope64(r, c_ref, sa_ref, sb_ref):
    return r * c_ref[...] + pltpu.roll(r, LANES - QK_ROPE // 2, 1) * sa_ref[...] + pltpu.roll(r, QK_ROPE // 2, 1) * sb_ref[...]


def _mla_q_kernel(d_ref, gain_ref, w_ref, c_ref, sa_ref, sb_ref, q_ref, *, scale):
    n = _rms(d_ref[...], gain_ref[...]).astype(BF16)
    q = jnp.dot(n, w_ref[...], preferred_element_type=F32)
    for h in range(MLA_HEADS):
        lo = h * MLA_DK_PAD
        q_ref[:, lo:lo + QK_NOPE] = (q[:, lo:lo + QK_NOPE] * scale).astype(q_ref.dtype)
        r = _rope64(q[:, lo + QK_NOPE:lo + MLA_DK_PAD], c_ref, sa_ref, sb_ref)
        q_ref[:, lo + QK_NOPE:lo + MLA_DK_PAD] = (r * scale).astype(q_ref.dtype)


def _mla_kv_kernel(ckv_ref, kr_ref, gain_ref, wk_ref, wv_ref, c_ref, sa_ref, sb_ref, k_ref, v_ref):
    n = _rms(ckv_ref[...], gain_ref[...]).astype(BF16)
    kn = jnp.dot(n, wk_ref[...], preferred_element_type=F32)
    v_ref[...] = jnp.dot(n, wv_ref[...], preferred_element_type=F32).astype(v_ref.dtype)
    kr = _rope64(kr_ref[...], c_ref, sa_ref, sb_ref).astype(k_ref.dtype)
    for h in range(MLA_HEADS):
        lo = h * MLA_DK_PAD
        k_ref[:, lo:lo + QK_NOPE] = kn[:, h * QK_NOPE:(h + 1) * QK_NOPE].astype(k_ref.dtype)
        k_ref[:, lo + QK_NOPE:lo + MLA_DK_PAD] = kr


def _axial_angles(n_tok, rot_dim):
    rows = n_tok // GRID_W
    n_freq = rot_dim // 4
    inv = (ROPE_THETA ** (-np.arange(n_freq, dtype=np.float32) / n_freq)).astype(np.float32)
    row = np.repeat(np.arange(rows, dtype=np.float32), GRID_W)
    col = np.tile(np.arange(GRID_W, dtype=np.float32), rows)
    return np.concatenate([row[:, None] * inv, col[:, None] * inv], axis=-1)


def _rope_tables_128(n_tok, ident_rows):
    ang = _axial_angles(n_tok, HEAD_DIM)
    cos, sin = np.cos(ang), np.sin(ang)
    c = np.concatenate([cos, cos], axis=-1)
    s = np.concatenate([-sin, sin], axis=-1)
    c = np.concatenate([c, np.ones((ident_rows, HEAD_DIM), np.float32)], axis=0)
    s = np.concatenate([s, np.zeros((ident_rows, HEAD_DIM), np.float32)], axis=0)
    return c.astype(np.float32), s.astype(np.float32)


def _rope_tables_64(n_tok, ident_rows):
    ang = _axial_angles(n_tok, QK_ROPE)
    cos, sin = np.cos(ang), np.sin(ang)
    half = QK_ROPE // 2
    z = np.zeros((n_tok, LANES - QK_ROPE), np.float32)
    zh = np.zeros((n_tok, half), np.float32)
    c = np.concatenate([cos, cos, z], axis=-1)
    sa = np.concatenate([-sin, zh, z], axis=-1)
    sb = np.concatenate([zh, sin, z], axis=-1)
    ci = np.concatenate([np.ones((ident_rows, QK_ROPE), np.float32),
                         np.zeros((ident_rows, LANES - QK_ROPE), np.float32)], axis=-1)
    zi = np.zeros((ident_rows, LANES), np.float32)
    tables = np.concatenate([c, ci], 0), np.concatenate([sa, zi], 0), np.concatenate([sb, zi], 0)
    return tuple(tab.astype(np.float32) for tab in tables)


def kernel(x, c, ctx, c_ctx, w_ada, b_ada, ln_g, ln_b, a_w_in, a_conv_w, a_q_gain, a_k_gain, a_w_out, m_w_down, m_q_gain, m_kv_gain, m_w_uq, m_w_ukv, m_w_out, router_w, router_b, e_w_gate, e_w_up, e_w_down, s_w_gate, s_w_up, s_w_down):
    batch, seq, d = x.shape
    ctx_len = ctx.shape[1]
    depth = w_ada.shape[0]
    assert depth == 2, "one conv+GQA layer followed by one MLA layer"
    alpha = (2 * depth) ** 0.25
    t_lat = batch * seq
    t_ctx = batch * ctx_len
    t_all = t_lat + t_ctx
    tr = ROW_TILE
    assert seq % tr == 0 and ctx_len % tr == 0 and seq % GRID_W == 0
    lat_tiles = t_lat // tr
    lat_seq_tiles = seq // tr
    ctx_seq_tiles = ctx_len // tr
    lk = ctx_len + seq

    def mod_row(r):
        return jnp.minimum(r // seq, batch)

    def kv_block(i):
        is_lat = i < lat_tiles
        cidx = i - lat_tiles
        b = jnp.where(is_lat, i // lat_seq_tiles, cidx // ctx_seq_tiles)
        rb = jnp.where(is_lat, ctx_seq_tiles + i % lat_seq_tiles, cidx % ctx_seq_tiles)
        return b, rb

    def pos_block(i):
        return jnp.where(i < lat_tiles, i % lat_seq_tiles, lat_seq_tiles)

    rows = -(-(batch + 1) // SUBLANES) * SUBLANES
    cond = jnp.concatenate([c, c_ctx[None, :], jnp.zeros((rows - batch - 1, d), F32)], axis=0)
    mod = _ada_table(cond, w_ada, b_ada).reshape(depth, rows, 1, 6 * d)

    x_pair = (x.reshape(t_lat, d), ctx.reshape(t_ctx, d))

    tm_in = W_IN_ROW_TILE if (t_ctx % W_IN_ROW_TILE == 0 and seq % W_IN_ROW_TILE == 0) else tr
    proj = _mod_matmul(x_pair, mod[0], a_w_in[0].astype(BF16), BF16, tm_in, W_IN_COL_TILE, mod_row)

    cos128, sin128 = _rope_tables_128(seq, tr)
    d_q = ATT_HEADS * HEAD_DIM
    d_kv = ATT_KV_HEADS * HEAD_DIM
    qkv_w = d_q + 2 * d_kv
    qkv_blk = 3 * CONV_DIM // qkv_w
    assert qkv_blk * qkv_w == 3 * CONV_DIM
    q0, k0, v0 = pl.pallas_call(
        functools.partial(_qkprep_kernel, scale=1.0 / math.sqrt(HEAD_DIM)),
        grid=(t_all // tr,),
        in_specs=[
            pl.BlockSpec((tr, qkv_w), lambda i: (i, qkv_blk)),
            pl.BlockSpec((tr, HEAD_DIM), lambda i: (pos_block(i), 0)),
            pl.BlockSpec((tr, HEAD_DIM), lambda i: (pos_block(i), 0)),
            _const_spec((1, HEAD_DIM)), _const_spec((1, HEAD_DIM)),
        ],
        out_specs=[
            pl.BlockSpec((tr, d_q), lambda i: (i, 0)),
            pl.BlockSpec((None, tr, d_kv), lambda i: (*kv_block(i), 0)),
            pl.BlockSpec((None, tr, d_kv), lambda i: (*kv_block(i), 0)),
        ],
        out_shape=[
            jax.ShapeDtypeStruct((t_all, d_q), BF16),
            jax.ShapeDtypeStruct((batch, lk, d_kv), BF16),
            jax.ShapeDtypeStruct((batch, lk, d_kv), BF16),
        ],
        compiler_params=_params(("parallel",)),
        name="qk_prep",
    )(proj, cos128, sin128, a_q_gain[0].reshape(1, HEAD_DIM), a_k_gain[0].reshape(1, HEAD_DIM))

    grp = ATT_HEADS // ATT_KV_HEADS
    att_lat = _attention(q0, k0, v0, batch=batch, sq=seq, lk=lk, n_kv=ATT_KV_HEADS, group=grp, dk=HEAD_DIM,
                         dv=HEAD_DIM, tq=GQA_Q_TILE if seq % GQA_Q_TILE == 0 else tr, rows=ATTN_CHAIN_ROWS, q_row_off=0)
    att_ctx = _attention(q0, k0, v0, batch=batch, sq=ctx_len, lk=ctx_len, n_kv=ATT_KV_HEADS, group=grp,
                         dk=HEAD_DIM, dv=HEAD_DIM, tq=tr, rows=ATTN_CHAIN_ROWS, q_row_off=t_lat)

    conv0 = _conv_gate(proj, a_conv_w[0], t=t_all, tm=tr, tc=CONV_COL_TILE, lat_tiles=lat_tiles,
                       lat_seq_tiles=lat_seq_tiles, ctx_seq_tiles=ctx_seq_tiles)

    w_out0 = a_w_out[0].astype(BF16)
    x1, tok0 = _outproj_ln([conv0, (att_lat, att_ctx)], [w_out0[:CONV_DIM], w_out0[CONV_DIM:]], x_pair, mod[0],
                           ln_g[0, 0].reshape(1, d), ln_b[0, 0].reshape(1, d), t=t_all, tm=tr, alpha=alpha,
                           mod_row=mod_row)

    x2, u1 = _moe(tok0, x1, t_all, 0, router_w[0], router_b[0], e_w_gate, e_w_up, e_w_down,
                  s_w_gate[0].astype(BF16), s_w_up[0].astype(BF16), s_w_down[0].astype(BF16), mod[0],
                  ln_g[0, 1].reshape(1, d), ln_b[0, 1].reshape(1, d), mod[1], alpha=alpha,
                  mod_row=mod_row, tm=COMBINE_ROW_TILE)

    n_down = Q_LORA + KV_LORA + QK_ROPE
    n_down_pad = -(-n_down // LANES) * LANES
    w_down = jnp.pad(m_w_down[0], ((0, 0), (0, n_down_pad - n_down))).astype(BF16)
    down = _matmul(u1, w_down, F32, DOWN_ROW_TILE, n_down_pad)

    dqk = QK_NOPE + QK_ROPE
    w_uq = m_w_uq[0].reshape(Q_LORA, MLA_HEADS, dqk)
    w_uq = jnp.pad(w_uq, ((0, 0), (0, 0), (0, MLA_DK_PAD - dqk))).reshape(Q_LORA, MLA_HEADS * MLA_DK_PAD).astype(BF16)
    w_ukv = m_w_ukv[0].reshape(KV_LORA, MLA_HEADS, QK_NOPE + V_DIM)
    w_uk = w_ukv[:, :, :QK_NOPE].reshape(KV_LORA, MLA_HEADS * QK_NOPE).astype(BF16)
    w_uv = w_ukv[:, :, QK_NOPE:].reshape(KV_LORA, MLA_HEADS * V_DIM).astype(BF16)

    c64, sa64, sb64 = _rope_tables_64(seq, tr)
    rope_specs = [pl.BlockSpec((tr, LANES), lambda i: (pos_block(i), 0))] * 3
    q1 = pl.pallas_call(
        functools.partial(_mla_q_kernel, scale=1.0 / math.sqrt(dqk)),
        grid=(lat_tiles,),
        in_specs=[
            pl.BlockSpec((tr, Q_LORA), lambda i: (i, 0)),
            _const_spec((1, Q_LORA)),
            _const_spec(w_uq.shape),
        ] + rope_specs,
        out_specs=pl.BlockSpec((tr, MLA_HEADS * MLA_DK_PAD), lambda i: (i, 0)),
        out_shape=jax.ShapeDtypeStruct((t_lat, MLA_HEADS * MLA_DK_PAD), BF16),
        compiler_params=_params(("parallel",)),
        name="mla_q",
    )(down, m_q_gain[0].reshape(1, Q_LORA), w_uq, c64, sa64, sb64)

    assert KV_LORA == Q_LORA and (Q_LORA + KV_LORA) % LANES == 0
    k1, v1 = pl.pallas_call(
        _mla_kv_kernel,
        grid=(t_all // tr,),
        in_specs=[
            pl.BlockSpec((tr, KV_LORA), lambda i: (i, 1)),
            pl.BlockSpec((tr, LANES), lambda i: (i, (Q_LORA + KV_LORA) // LANES)),
            _const_spec((1, KV_LORA)),
            _const_spec(w_uk.shape), _const_spec(w_uv.shape),
        ] + rope_specs,
        out_specs=[
            pl.BlockSpec((None, tr, MLA_HEADS * MLA_DK_PAD), lambda i: (*kv_block(i), 0)),
            pl.BlockSpec((None, tr, MLA_HEADS * V_DIM), lambda i: (*kv_block(i), 0)),
        ],
        out_shape=[
            jax.ShapeDtypeStruct((batch, lk, MLA_HEADS * MLA_DK_PAD), BF16),
            jax.ShapeDtypeStruct((batch, lk, MLA_HEADS * V_DIM), BF16),
        ],
        compiler_params=_params(("parallel",)),
        name="mla_kv",
    )(down, down, m_kv_gain[0].reshape(1, KV_LORA), w_uk, w_uv, c64, sa64, sb64)

    att1 = _attention(q1, k1, v1, batch=batch, sq=seq, lk=lk, n_kv=MLA_HEADS, group=1, dk=MLA_DK_PAD, dv=V_DIM,
                      tq=next(q for q in MLA_Q_TILES + (tr,) if seq % q == 0), rows=ATTN_CHAIN_ROWS, q_row_off=0)

    x3, tok1 = _outproj_ln([att1], [m_w_out[0].astype(BF16)], x2, mod[1], ln_g[1, 0].reshape(1, d),
                           ln_b[1, 0].reshape(1, d), t=t_lat, tm=tr, alpha=alpha, mod_row=mod_row)

    (x4,) = _moe(tok1, x3, t_lat, 1, router_w[1], router_b[1], e_w_gate, e_w_up, e_w_down,
                 s_w_gate[1].astype(BF16), s_w_up[1].astype(BF16), s_w_down[1].astype(BF16), mod[1],
                 ln_g[1, 1].reshape(1, d), ln_b[1, 1].reshape(1, d), None, alpha=alpha,
                 mod_row=mod_row, tm=COMBINE_ROW_TILE)
    return x4.reshape(batch, seq, d)
```

```python
import functools
import math

import jax
import jax.numpy as jnp
import numpy as np
from jax import lax
from jax.experimental import pallas as pl
from jax.experimental.pallas import tpu as pltpu
from jax.experimental.pallas import tpu_sc as plsc

F32 = jnp.float32
BF16 = jnp.bfloat16

GRID_W = 64
CONV_DIM = 1024
ATT_HEADS = 8
ATT_KV_HEADS = 2
HEAD_DIM = 128
MLA_HEADS = 16
Q_LORA = 512
KV_LORA = 512
QK_NOPE = 128
QK_ROPE = 64
V_DIM = 128
N_EXPERTS = 64
TOP_K = 6
N_GROUPS = 8
TOPK_GROUPS = 4
ROUTED_SCALE = 2.5
ROPE_THETA = 10000.0
LN_EPS = 1e-5
RMS_EPS = 1e-6

V7X_VMEM_LIMIT_BYTES = 56 * 1024 * 1024
LANES = 128
SUBLANES = 8
MOE_ROWS = 512
MOE_DISPATCH_SPLIT = (1, 3, 2, 2)
MOE_COMBINE_FIRST = 0.25
V7X_SC_CORES = 2
V7X_SC_SUBCORES = 16
SC_GATHER_ROWS = 16
SC_GATHER_BUFFERS = 4
MLA_DK_PAD = 256

ROW_TILE = 256
ADA_COL_TILE = 1024
W_IN_ROW_TILE = 1024
W_IN_COL_TILE = 768
DOWN_ROW_TILE = 512
CONV_COL_TILE = 512
ROUTER_TILE = 512
ATTN_CHAIN_ROWS = 256
GQA_Q_TILE = 512
MLA_Q_TILES = (2048, 1024, 512)
COMBINE_ROW_TILE = 256


def _params(sem):
    return pltpu.CompilerParams(dimension_semantics=sem, vmem_limit_bytes=V7X_VMEM_LIMIT_BYTES)


def _const_spec(shape):
    nd = len(shape)
    return pl.BlockSpec(shape, lambda *_: (0,) * nd)


def _ada_kernel(s_ref, w_ref, b_ref, o_ref):
    s = s_ref[...]
    s = s * (1.0 / (1.0 + jnp.exp(-s)))
    o_ref[...] = jnp.dot(s.astype(BF16), w_ref[...].astype(BF16), preferred_element_type=F32) + b_ref[...]


def _ada_table(cond, w_ada, b_ada):
    depth, d, n = w_ada.shape
    r = cond.shape[0]
    tn = ADA_COL_TILE
    return pl.pallas_call(
        _ada_kernel,
        grid=(depth, n // tn),
        in_specs=[
            pl.BlockSpec((r, d), lambda l, j: (0, 0)),
            pl.BlockSpec((None, d, tn), lambda l, j: (l, 0, j)),
            pl.BlockSpec((None, 1, tn), lambda l, j: (l, 0, j)),
        ],
        out_specs=pl.BlockSpec((None, r, tn), lambda l, j: (l, 0, j)),
        out_shape=jax.ShapeDtypeStruct((depth, r, n), F32),
        compiler_params=_params(("parallel", "parallel")),
        name="ada_table",
    )(cond, w_ada, b_ada.reshape(depth, 1, n))


def _mod_spec(d, chunk, mod_row, tm):
    return pl.BlockSpec((None, 1, d), lambda i: (mod_row(i * tm), 0, chunk))


def _mm_kernel(a_ref, w_ref, o_ref):
    o_ref[...] = jnp.dot(a_ref[...], w_ref[...], preferred_element_type=F32).astype(o_ref.dtype)


def _matmul(a, w, out_dtype, tm, tn):
    m, k = a.shape
    n = w.shape[1]
    return pl.pallas_call(
        _mm_kernel,
        grid=(m // tm, n // tn),
        in_specs=[
            pl.BlockSpec((tm, k), lambda i, j: (i, 0)),
            pl.BlockSpec((k, tn), lambda i, j: (0, j)),
        ],
        out_specs=pl.BlockSpec((tm, tn), lambda i, j: (i, j)),
        out_shape=jax.ShapeDtypeStruct((m, n), out_dtype),
        compiler_params=_params(("parallel", "parallel")),
        name="matmul",
    )(a, w)


def _pair_specs(pair, tm):
    lat, ctx = pair
    lat_tiles = lat.shape[0] // tm
    assert lat_tiles * tm == lat.shape[0] and ctx.shape[0] % tm == 0 and lat.shape[1] == ctx.shape[1]
    width = lat.shape[1]
    return [pl.BlockSpec((tm, width), lambda i, *_: (jnp.minimum(i, lat_tiles - 1), 0)),
            pl.BlockSpec((tm, width), lambda i, *_: (jnp.maximum(i - lat_tiles, 0), 0))]


def _pair_tile(lat_ref, ctx_ref, lat_tiles):
    return jnp.where(pl.program_id(0) < lat_tiles, lat_ref[...], ctx_ref[...])


def _mod_mm_kernel(xl_ref, xc_ref, sc_ref, sh_ref, w_ref, o_ref, u_ref, *, lat_tiles):
    @pl.when(pl.program_id(1) == 0)
    def _():
        x = _pair_tile(xl_ref, xc_ref, lat_tiles)
        u_ref[...] = (x * (1.0 + sc_ref[...]) + sh_ref[...]).astype(u_ref.dtype)

    o_ref[...] = jnp.dot(u_ref[...], w_ref[...], preferred_element_type=F32).astype(o_ref.dtype)


def _mod_matmul(x_pair, mod, w, out_dtype, tm, tn, mod_row):
    m = x_pair[0].shape[0] + x_pair[1].shape[0]
    k = x_pair[0].shape[1]
    n = w.shape[1]

    def mod_spec(chunk):
        return pl.BlockSpec((None, 1, k), lambda i, j: (mod_row(i * tm), 0, chunk))

    return pl.pallas_call(
        functools.partial(_mod_mm_kernel, lat_tiles=x_pair[0].shape[0] // tm),
        grid=(m // tm, n // tn),
        in_specs=_pair_specs(x_pair, tm) + [
            mod_spec(1), mod_spec(0),
            pl.BlockSpec((k, tn), lambda i, j: (0, j)),
        ],
        out_specs=pl.BlockSpec((tm, tn), lambda i, j: (i, j)),
        out_shape=jax.ShapeDtypeStruct((m, n), out_dtype),
        scratch_shapes=[pltpu.VMEM((tm, k), w.dtype)],
        compiler_params=_params(("parallel", "arbitrary")),
        name="mod_matmul",
    )(*x_pair, mod, mod, w)


def _rms(t, gain):
    return t * lax.rsqrt(jnp.mean(t * t, axis=-1, keepdims=True) + RMS_EPS) * gain


def _qkprep_kernel(p_ref, cos_ref, sin_ref, qg_ref, kg_ref, q_ref, k_ref, v_ref, *, scale):
    cos = cos_ref[...]
    sin = sin_ref[...]

    def norm_rope(t, gain):
        y = _rms(t.astype(F32), gain)
        return y * cos + pltpu.roll(y, HEAD_DIM // 2, 1) * sin

    for h in range(ATT_HEADS):
        sl = slice(h * HEAD_DIM, (h + 1) * HEAD_DIM)
        q_ref[:, sl] = (norm_rope(p_ref[:, sl], qg_ref[...]) * scale).astype(q_ref.dtype)
    k0 = ATT_HEADS * HEAD_DIM
    for h in range(ATT_KV_HEADS):
        sl = slice(h * HEAD_DIM, (h + 1) * HEAD_DIM)
        k_ref[:, sl] = norm_rope(p_ref[:, k0 + h * HEAD_DIM:k0 + (h + 1) * HEAD_DIM], kg_ref[...]).astype(k_ref.dtype)
    v0 = k0 + ATT_KV_HEADS * HEAD_DIM
    v_ref[...] = p_ref[:, v0:v0 + ATT_KV_HEADS * HEAD_DIM].astype(v_ref.dtype)


def _attn_kernel(q_ref, k_ref, v_ref, o_ref, *, group, tq, rows, dk, dv):
    k = k_ref[...]
    v = v_ref[...]
    for h in range(group):
        for r in range(0, tq, rows):
            q = q_ref[r:r + rows, h * dk:(h + 1) * dk]
            s = lax.dot_general(q, k, (((1,), (1,)), ((), ())), preferred_element_type=F32)
            m = jnp.max(s, axis=-1, keepdims=True)
            p = jnp.exp(s - m)
            l = jnp.sum(p, axis=-1, keepdims=True)
            o = jnp.dot(p.astype(v.dtype), v, preferred_element_type=F32)
            o_ref[r:r + rows, h * dv:(h + 1) * dv] = (o / l).astype(o_ref.dtype)


def _attention(q, k, v, *, batch, sq, lk, n_kv, group, dk, dv, tq, rows, q_row_off):
    nq = sq // tq
    off = q_row_off // tq
    assert tq % rows == 0 and q_row_off % tq == 0 and sq % tq == 0
    return pl.pallas_call(
        functools.partial(_attn_kernel, group=group, tq=tq, rows=rows, dk=dk, dv=dv),
        grid=(batch, n_kv, nq),
        in_specs=[
            pl.BlockSpec((tq, group * dk), lambda b, g, i: (off + b * nq + i, g)),
            pl.BlockSpec((None, lk, dk), lambda b, g, i: (b, 0, g)),
            pl.BlockSpec((None, lk, dv), lambda b, g, i: (b, 0, g)),
        ],
        out_specs=pl.BlockSpec((tq, group * dv), lambda b, g, i: (b * nq + i, g)),
        out_shape=jax.ShapeDtypeStruct((batch * sq, n_kv * group * dv), BF16),
        compiler_params=_params(("parallel", "parallel", "parallel")),
        name="attention",
    )(q, k, v)


def _conv_kernel(gb_ref, gc_ref, hv_ref, gcp_ref, hvp_ref, gcn_ref, hvn_ref, w_ref, o_ref, *,
                 tm, lat_tiles, lat_seq_tiles, ctx_seq_tiles):
    i = pl.program_id(0)
    is_lat = i < lat_tiles
    pos = jnp.where(is_lat, i % lat_seq_tiles, (i - lat_tiles) % ctx_seq_tiles)
    seq_tiles = jnp.where(is_lat, lat_seq_tiles, ctx_seq_tiles)
    not_first = (pos != 0).astype(F32)
    not_last = (pos != seq_tiles - 1).astype(F32)
    p = gc_ref[...].astype(F32) * hv_ref[...].astype(F32)
    halo_prev = gcp_ref[SUBLANES - 1:SUBLANES, :].astype(F32) * hvp_ref[SUBLANES - 1:SUBLANES, :].astype(F32) * not_first
    halo_next = gcn_ref[0:1, :].astype(F32) * hvn_ref[0:1, :].astype(F32) * not_last
    row = lax.broadcasted_iota(jnp.int32, p.shape, 0)
    prev = jnp.where(row == 0, halo_prev, pltpu.roll(p, 1, 0))
    nxt = jnp.where(row == tm - 1, halo_next, pltpu.roll(p, tm - 1, 0))
    w = w_ref[...]
    conv = w[0:1, :] * prev + w[1:2, :] * p + w[2:3, :] * nxt
    o_ref[...] = (gb_ref[...].astype(F32) * conv).astype(o_ref.dtype)


def _conv_gate(p, conv_w, *, t, tm, tc, lat_tiles, lat_seq_tiles, ctx_seq_tiles):
    nct = CONV_DIM // tc
    hb = tm // SUBLANES
    n_halo = t // SUBLANES

    def cur(part):
        return pl.BlockSpec((tm, tc), lambda i, j: (i, part * nct + j))

    def prev(part):
        return pl.BlockSpec((SUBLANES, tc), lambda i, j: (jnp.maximum(i * hb - 1, 0), part * nct + j))

    def nxt(part):
        return pl.BlockSpec((SUBLANES, tc), lambda i, j: (jnp.minimum((i + 1) * hb, n_halo - 1), part * nct + j))

    return pl.pallas_call(
        functools.partial(_conv_kernel, tm=tm, lat_tiles=lat_tiles, lat_seq_tiles=lat_seq_tiles,
                          ctx_seq_tiles=ctx_seq_tiles),
        grid=(t // tm, nct),
        in_specs=[cur(0), cur(1), cur(2), prev(1), prev(2), nxt(1), nxt(2),
                  pl.BlockSpec((3, tc), lambda i, j: (0, j))],
        out_specs=pl.BlockSpec((tm, tc), lambda i, j: (i, j)),
        out_shape=jax.ShapeDtypeStruct((t, CONV_DIM), BF16),
        compiler_params=_params(("parallel", "parallel")),
        name="conv_gate",
    )(p, p, p, p, p, p, p, conv_w)


def _layer_norm(z, g, b):
    mu = jnp.mean(z, axis=-1, keepdims=True)
    zc = z - mu
    var = jnp.mean(zc * zc, axis=-1, keepdims=True)
    return zc * lax.rsqrt(var + LN_EPS) * g + b


def _pack_bf16_pairs(x):
    half = x.shape[1] // 2
    lo = lax.bitcast_convert_type(x[:, :half].astype(BF16).astype(F32), jnp.uint32) >> 16
    hi = lax.bitcast_convert_type(x[:, half:].astype(BF16).astype(F32), jnp.uint32) & jnp.uint32(0xFFFF0000)
    return lax.bitcast_convert_type(lo | hi, jnp.int32)


def _unpack_bf16_pairs(w):
    u = lax.bitcast_convert_type(w, jnp.uint32)
    lo = lax.bitcast_convert_type(u << 16, F32).astype(BF16)
    hi = lax.bitcast_convert_type(u & jnp.uint32(0xFFFF0000), F32).astype(BF16)
    return lo, hi


def _dot_halves(lo, hi, w_ref):
    half = lo.shape[1]
    return (jnp.dot(lo, w_ref[:half, :], preferred_element_type=F32)
            + jnp.dot(hi, w_ref[half:, :], preferred_element_type=F32))


def _outproj_ln_kernel(*refs, widths, lat_tiles, alpha):
    refs = list(refs)

    def take(width):
        got = [refs.pop(0) for _ in range(width)]
        return got[0][...] if width == 1 else _pair_tile(got[0], got[1], lat_tiles)

    acts = [take(w) for w in widths[:-1]]
    w_refs = [refs.pop(0) for _ in acts]
    x = take(widths[-1])
    gate_ref, lng_ref, lnb_ref, sc_ref, sh_ref, xo_ref, tok_ref = refs
    y = jnp.dot(acts[0], w_refs[0][...], preferred_element_type=F32)
    for a, w_ref in zip(acts[1:], w_refs[1:]):
        y = y + jnp.dot(a, w_ref[...], preferred_element_type=F32)
    xn = _layer_norm(alpha * x + gate_ref[...] * y, lng_ref[...], lnb_ref[...])
    xo_ref[...] = xn
    tok_ref[...] = _pack_bf16_pairs(xn * (1.0 + sc_ref[...]) + sh_ref[...])


def _outproj_ln(a_list, w_list, x, mod, ln_g, ln_b, *, t, tm, alpha, mod_row):
    d = w_list[0].shape[1]
    operands, in_specs, widths, lat_tiles = [], [], [], 0

    def add_rows(src):
        nonlocal lat_tiles
        if isinstance(src, tuple):
            in_specs.extend(_pair_specs(src, tm))
            operands.extend(src)
            widths.append(2)
            lat_tiles = src[0].shape[0] // tm
        else:
            in_specs.append(pl.BlockSpec((tm, src.shape[1]), lambda i: (i, 0)))
            operands.append(src)
            widths.append(1)

    for a in a_list:
        add_rows(a)
    in_specs += [_const_spec(w.shape) for w in w_list]
    operands += list(w_list)
    add_rows(x)
    in_specs += [
        _mod_spec(d, 2, mod_row, tm),
        _const_spec((1, d)), _const_spec((1, d)),
        _mod_spec(d, 4, mod_row, tm),
        _mod_spec(d, 3, mod_row, tm),
    ]
    return pl.pallas_call(
        functools.partial(_outproj_ln_kernel, widths=tuple(widths), lat_tiles=lat_tiles, alpha=alpha),
        grid=(t // tm,),
        in_specs=in_specs,
        out_specs=[pl.BlockSpec((tm, d), lambda i: (i, 0)), pl.BlockSpec((tm, d // 2), lambda i: (i, 0))],
        out_shape=[jax.ShapeDtypeStruct((t, d), F32), jax.ShapeDtypeStruct((t, d // 2), jnp.int32)],
        compiler_params=_params(("parallel",)),
        name="outproj_ln",
    )(*operands, mod, ln_g, ln_b, mod, mod)


def _router_kernel(t_ref, rw_ref, rb_ref, tri_ref, idx_ref, gw_ref, rank_ref, cnt_ref):
    @pl.when(pl.program_id(0) == 0)
    def _():
        cnt_ref[...] = jnp.zeros_like(cnt_ref)

    lo, hi = _unpack_bf16_pairs(t_ref[...])
    half = lo.shape[1]
    nt = (((1,), (1,)), ((), ()))
    logits = (lax.dot_general(rw_ref[:, :half], lo, nt, preferred_element_type=F32)
              + lax.dot_general(rw_ref[:, half:], hi, nt, preferred_element_type=F32))
    scores = 1.0 / (1.0 + jnp.exp(-logits))
    sel = scores + rb_ref[...]
    gsz = N_EXPERTS // N_GROUPS
    neg = -jnp.inf
    sub = lax.broadcasted_iota(jnp.int32, (gsz, sel.shape[1]), 0)
    slabs = [sel[g * gsz:(g + 1) * gsz, :] for g in range(N_GROUPS)]
    gscore = []
    for s in slabs:
        m1 = jnp.max(s, axis=0, keepdims=True)
        a1 = jnp.min(jnp.where(s == m1, sub, gsz), axis=0, keepdims=True)
        m2 = jnp.max(jnp.where(sub == a1, neg, s), axis=0, keepdims=True)
        gscore.append(m1 + m2)
    masked = []
    for g in range(N_GROUPS):
        ahead = jnp.zeros(gscore[g].shape, jnp.int32)
        for h in range(N_GROUPS):
            if h == g:
                continue
            beats = gscore[h] >= gscore[g] if h < g else gscore[h] > gscore[g]
            ahead = ahead + beats.astype(jnp.int32)
        masked.append(jnp.where(ahead < TOPK_GROUPS, slabs[g], neg))
    cur = jnp.concatenate(masked, axis=0)
    eio = lax.broadcasted_iota(jnp.int32, cur.shape, 0)
    picks, weights = [], []
    for _ in range(TOP_K):
        m = jnp.max(cur, axis=0, keepdims=True)
        a = jnp.min(jnp.where(cur == m, eio, N_EXPERTS), axis=0, keepdims=True)
        hit = eio == a
        picks.append(a)
        weights.append(jnp.sum(jnp.where(hit, scores, 0.0), axis=0, keepdims=True))
        cur = jnp.where(hit, neg, cur)
    total = weights[0]
    for w in weights[1:]:
        total = total + w
    for k in range(TOP_K):
        idx_ref[k:k + 1, :] = picks[k]
        gw_ref[k:k + 1, :] = weights[k] / total * ROUTED_SCALE
    for k in range(TOP_K, SUBLANES):
        idx_ref[k:k + 1, :] = jnp.zeros_like(picks[0])
        gw_ref[k:k + 1, :] = jnp.zeros_like(weights[0])
        rank_ref[k:k + 1, :] = jnp.zeros_like(picks[0])
    base = cnt_ref[:, 0:1]
    for k in range(TOP_K):
        onehot = jnp.where(eio == picks[k], 1.0, 0.0)
        before = jnp.dot(onehot.astype(BF16), tri_ref[...], preferred_element_type=F32)
        rank_ref[k:k + 1, :] = jnp.sum(onehot * (before + base), axis=0, keepdims=True).astype(jnp.int32)
        base = base + jnp.sum(onehot, axis=1, keepdims=True)
    cnt_ref[...] = jnp.broadcast_to(base, cnt_ref.shape)


def _router(tok, rw_t, rb, *, t, tt):
    half = tok.shape[1]
    tri = jnp.asarray(np.arange(tt)[:, None] < np.arange(tt)[None, :], BF16)
    blk = pl.BlockSpec((SUBLANES, tt), lambda i: (0, i))
    return pl.pallas_call(
        _router_kernel,
        grid=(t // tt,),
        in_specs=[
            pl.BlockSpec((tt, half), lambda i: (i, 0)),
            _const_spec((N_EXPERTS, 2 * half)),
            _const_spec((N_EXPERTS, 1)),
            _const_spec((tt, tt)),
        ],
        out_specs=[blk, blk, blk, _const_spec((N_EXPERTS, LANES))],
        out_shape=[jax.ShapeDtypeStruct((SUBLANES, t), jnp.int32), jax.ShapeDtypeStruct((SUBLANES, t), F32),
                   jax.ShapeDtypeStruct((SUBLANES, t), jnp.int32), jax.ShapeDtypeStruct((N_EXPERTS, LANES), F32)],
        compiler_params=_params(("arbitrary",)),
        name="router",
    )(tok, rw_t, rb, tri)


def _slots_kernel(idx_ref, rank_ref, start_ref, pos_ref):
    start = start_ref[...]
    eio = lax.broadcasted_iota(jnp.int32, (N_EXPERTS, idx_ref.shape[1]), 0)
    for k in range(TOP_K):
        seg = jnp.sum(jnp.where(eio == idx_ref[k:k + 1, :], start, 0.0), axis=0, keepdims=True)
        pos_ref[k:k + 1, :] = rank_ref[k:k + 1, :] + seg.astype(jnp.int32)
    for k in range(TOP_K, SUBLANES):
        pos_ref[k:k + 1, :] = jnp.zeros((1, idx_ref.shape[1]), jnp.int32)


def _assign_slots(idx, rank, seg_start, *, t, tt):
    blk = pl.BlockSpec((SUBLANES, tt), lambda i: (0, i))
    return pl.pallas_call(
        _slots_kernel,
        grid=(t // tt,),
        in_specs=[blk, blk, _const_spec((N_EXPERTS, 1))],
        out_specs=blk,
        out_shape=jax.ShapeDtypeStruct((SUBLANES, t), jnp.int32),
        compiler_params=_params(("parallel",)),
        name="assign_slots",
    )(idx, rank, seg_start)


def _sc_gather_rows(table, idx):
    n = idx.shape[0]
    d = table.shape[1]
    n_workers = V7X_SC_CORES * V7X_SC_SUBCORES
    per_w = n // n_workers
    n_chunks = per_w // SC_GATHER_ROWS
    assert per_w * n_workers == n and n_chunks * SC_GATHER_ROWS == per_w
    n_buf = next(b for b in range(SC_GATHER_BUFFERS, 1, -1) if n_chunks % b == 0)
    mesh = plsc.VectorSubcoreMesh(core_axis_name="c", subcore_axis_name="s", num_cores=V7X_SC_CORES,
                                  num_subcores=V7X_SC_SUBCORES)

    @functools.partial(
        pl.kernel,
        out_type=jax.ShapeDtypeStruct((n, d), table.dtype),
        mesh=mesh,
        scratch_types=[
            pltpu.VMEM((per_w,), jnp.int32),
            pltpu.VMEM((n_buf, SC_GATHER_ROWS, d), table.dtype),
            pltpu.SemaphoreType.DMA((n_buf,)),
            pltpu.SemaphoreType.DMA((n_buf,)),
        ],
        name="sc_gather_rows",
    )
    def gather(table_hbm, idx_hbm, out_hbm, idx_v, rows_v, gsem, wsem):
        wid = lax.axis_index("s") * V7X_SC_CORES + lax.axis_index("c")
        base = wid * per_w
        pltpu.sync_copy(idx_hbm.at[pl.ds(base, per_w)], idx_v)

        def gather_copy(c, b):
            return pltpu.make_async_copy(table_hbm.at[idx_v.at[pl.ds(c * SC_GATHER_ROWS, SC_GATHER_ROWS)]],
                                         rows_v.at[b], gsem.at[b])

        def write_copy(c, b):
            return pltpu.make_async_copy(rows_v.at[b], out_hbm.at[pl.ds(base + c * SC_GATHER_ROWS, SC_GATHER_ROWS)],
                                         wsem.at[b])

        for b in range(n_buf - 1):
            gather_copy(b, b).start()

        @pl.loop(0, n_chunks, step=n_buf)
        def _(g):
            for b in range(n_buf):
                c = g + b
                prev = (b + n_buf - 1) % n_buf
                gather_copy(c, b).wait()
                write_copy(c, b).start()

                @pl.when(c >= 1)
                def _():
                    write_copy(c - 1, prev).wait()

                @pl.when(c + n_buf - 1 < n_chunks)
                def _():
                    gather_copy(c + n_buf - 1, prev).start()

        write_copy(n_chunks - 1, (n_chunks - 1) % n_buf).wait()

    return gather(table, idx)


def _experts_kernel(be_ref, nbu_ref, seg_ref, nstart_ref, x_ref, wg_hbm, wu_hbm, wd_hbm, *rest, block_off, n_call,
                    layer):
    y_ref, wgf, wuf, wdf, wgb, wub, wdb, sem = rest[-8:]
    step = pl.program_id(0)
    b = block_off + step
    nbu = nbu_ref[0]
    end = jnp.minimum(nbu, block_off + n_call)

    def weight_copies(e, slot):
        return [pltpu.make_async_copy(hbm.at[layer, e], buf.at[slot], sem.at[slot, k])
                for k, (hbm, buf) in enumerate(((wg_hbm, wgf), (wu_hbm, wuf), (wd_hbm, wdf)))]

    @pl.when(b < nbu)
    def _():
        first = jnp.logical_or(step == 0, be_ref[b] != be_ref[jnp.maximum(b - 1, 0)])

        @pl.when(first)
        def _():
            slot = seg_ref[b] % 2

            @pl.when(step == 0)
            def _():
                for cp in weight_copies(be_ref[b], slot):
                    cp.start()

            for cp in weight_copies(be_ref[b], slot):
                cp.wait()
            wgb[...] = wgf[slot].astype(BF16)
            wub[...] = wuf[slot].astype(BF16)
            wdb[...] = wdf[slot].astype(BF16)
            nxt = nstart_ref[b]

            @pl.when(nxt < end)
            def _():
                for cp in weight_copies(be_ref[jnp.minimum(nxt, be_ref.shape[0] - 1)], 1 - slot):
                    cp.start()

        lo, hi = _unpack_bf16_pairs(x_ref[...])
        hg = _dot_halves(lo, hi, wgb)
        hu = _dot_halves(lo, hi, wub)
        h = hg * (1.0 / (1.0 + jnp.exp(-hg))) * hu
        y_ref[...] = _pack_bf16_pairs(jnp.dot(h.astype(BF16), wdb[...], preferred_element_type=F32))

    @pl.when(b >= nbu)
    def _():
        y_ref[...] = jnp.zeros_like(y_ref)


def _experts(xs, block_e, nb_used, seg_idx, next_start, wg, wu, wd, layer, y_prev, *, block_off, n_blocks):
    half = xs.shape[1]
    d = 2 * half
    ff = wg.shape[3]
    n_call = xs.shape[0] // MOE_ROWS

    def used(b, nbu):
        return jnp.clip(jnp.minimum(block_off + b, nbu[0] - 1) - block_off, 0, n_call - 1)

    hbm = pl.BlockSpec(memory_space=pl.ANY)
    in_specs = [pl.BlockSpec((MOE_ROWS, half), lambda b, be, nbu, seg, nst: (used(b, nbu), 0)), hbm, hbm, hbm]
    args = [block_e, nb_used, seg_idx, next_start, xs, wg, wu, wd]
    aliases = {}
    if y_prev is not None:
        in_specs.append(hbm)
        aliases = {len(args): 0}
        args.append(y_prev)
    grid_spec = pltpu.PrefetchScalarGridSpec(
        num_scalar_prefetch=4,
        grid=(n_call,),
        in_specs=in_specs,
        out_specs=pl.BlockSpec((MOE_ROWS, half), lambda b, be, nbu, seg, nst: (block_off + b, 0)),
        scratch_shapes=[
            pltpu.VMEM((2, d, ff), F32),
            pltpu.VMEM((2, d, ff), F32),
            pltpu.VMEM((2, ff, d), F32),
            pltpu.VMEM((d, ff), BF16),
            pltpu.VMEM((d, ff), BF16),
            pltpu.VMEM((ff, d), BF16),
            pltpu.SemaphoreType.DMA((2, 3)),
        ],
    )
    return pl.pallas_call(
        functools.partial(_experts_kernel, block_off=block_off, n_call=n_call, layer=layer),
        grid_spec=grid_spec,
        out_shape=jax.ShapeDtypeStruct((n_blocks * MOE_ROWS, half), jnp.int32),
        input_output_aliases=aliases,
        compiler_params=_params(("arbitrary",)),
        name="experts",
    )(*args)


def _combine_ln_kernel(*refs, alpha, emit_next, n_prev):
    y_ref, gw_ref, tok_ref, x_ref, sg_ref, su_ref, sd_ref, gate_ref, lng_ref, lnb_ref = refs[:10]
    outs = refs[len(refs) - (2 if emit_next else 1):]
    if emit_next:
        sc_ref, sh_ref = refs[10:12]
        xo_ref, u_ref = outs
    else:
        (xo_ref,) = outs
    lo, hi = _unpack_bf16_pairs(tok_ref[...])
    hg = _dot_halves(lo, hi, sg_ref)
    hu = _dot_halves(lo, hi, su_ref)
    h = hg * (1.0 / (1.0 + jnp.exp(-hg))) * hu
    gw = gw_ref[...]
    f_lo = f_hi = None
    for k in range(TOP_K):
        y_lo, y_hi = _unpack_bf16_pairs(y_ref[k])
        w = gw[:, k:k + 1]
        f_lo = y_lo.astype(F32) * w if f_lo is None else f_lo + y_lo.astype(F32) * w
        f_hi = y_hi.astype(F32) * w if f_hi is None else f_hi + y_hi.astype(F32) * w
    f = jnp.concatenate([f_lo, f_hi], axis=-1) + jnp.dot(h.astype(BF16), sd_ref[...], preferred_element_type=F32)
    xn = _layer_norm(alpha * x_ref[...] + gate_ref[...] * f, lng_ref[...], lnb_ref[...])
    xo_ref[...] = xn
    if emit_next:
        u_ref[...] = (xn * (1.0 + sc_ref[...]) + sh_ref[...]).astype(u_ref.dtype)


def _combine_ln(y3, gw_t, tok, x, sg, su, sd, mod, ln_g, ln_b, mod_next, prev, *, t, row_off, tm, alpha, mod_row):
    d = x.shape[1]
    emit_next = mod_next is not None
    off = row_off // tm
    assert off * tm == row_off

    def rows(i):
        return (off + i, 0)

    def part_mod_row(r):
        return mod_row(r + row_off)

    in_specs = [
        pl.BlockSpec((TOP_K, tm, d // 2), lambda i: (0, i, 0)),
        pl.BlockSpec((tm, SUBLANES), rows),
        pl.BlockSpec((tm, d // 2), rows),
        pl.BlockSpec((tm, d), rows),
        _const_spec(sg.shape), _const_spec(su.shape), _const_spec(sd.shape),
        _mod_spec(d, 5, part_mod_row, tm),
        _const_spec((1, d)), _const_spec((1, d)),
    ]
    args = [y3, gw_t, tok, x, sg, su, sd, mod, ln_g, ln_b]
    out_specs = [pl.BlockSpec((tm, d), rows)]
    out_shape = [jax.ShapeDtypeStruct((t, d), F32)]
    if emit_next:
        in_specs += [_mod_spec(d, 1, part_mod_row, tm), _mod_spec(d, 0, part_mod_row, tm)]
        args += [mod_next, mod_next]
        out_specs.append(pl.BlockSpec((tm, d), rows))
        out_shape.append(jax.ShapeDtypeStruct((t, d), BF16))
    aliases = {}
    if prev is not None:
        for k, p in enumerate(prev):
            in_specs.append(pl.BlockSpec(memory_space=pl.ANY))
            aliases[len(args)] = k
            args.append(p)
    return pl.pallas_call(
        functools.partial(_combine_ln_kernel, alpha=alpha, emit_next=emit_next, n_prev=len(aliases)),
        grid=(y3.shape[1] // tm,),
        in_specs=in_specs,
        out_specs=out_specs,
        out_shape=out_shape,
        input_output_aliases=aliases,
        compiler_params=_params(("parallel",)),
        name="combine_ln",
    )(*args)


def _moe(tok, x, t, layer, router_w, router_b, wg, wu, wd, sg, su, sd, mod, ln_g, ln_b, mod_next, *, alpha, mod_row,
         tm):
    half = tok.shape[1]
    tt = ROUTER_TILE
    idx, gw, rank, cnt = _router(tok, router_w.T.astype(BF16), router_b.reshape(N_EXPERTS, 1), t=t, tt=tt)
    n_asg = t * TOP_K
    counts = cnt[:, 0].astype(jnp.int32)
    padded = (counts + MOE_ROWS - 1) // MOE_ROWS * MOE_ROWS
    pend = jnp.cumsum(padded)
    pstart = pend - padded
    sc_rows = V7X_SC_CORES * V7X_SC_SUBCORES * SC_GATHER_ROWS * 2
    blocks_granule = sum(MOE_DISPATCH_SPLIT) * max(sc_rows // MOE_ROWS, 1)
    assert (blocks_granule // sum(MOE_DISPATCH_SPLIT) * MOE_ROWS) % sc_rows == 0
    n_blocks = -(-((n_asg + N_EXPERTS * (MOE_ROWS - 1)) // MOE_ROWS + 1) // blocks_granule) * blocks_granule
    assert n_asg % sc_rows == 0
    block_start = jnp.arange(n_blocks, dtype=jnp.int32) * MOE_ROWS
    block_e = jnp.minimum(jnp.sum((pend[None, :] <= block_start[:, None]).astype(jnp.int32), axis=1), N_EXPERTS - 1)
    nb_used = (pend[-1] // MOE_ROWS).astype(jnp.int32).reshape(1)
    blk = jnp.arange(n_blocks, dtype=jnp.int32)
    seg_first = jnp.concatenate([jnp.ones((1,), bool), block_e[1:] != block_e[:-1]])
    seg_idx = jnp.cumsum(seg_first.astype(jnp.int32)) - 1
    later_first = lax.cummin(jnp.where(seg_first, blk, n_blocks), reverse=True)
    next_start = jnp.concatenate([later_first[1:], jnp.full((1,), n_blocks, jnp.int32)])
    pos2d = _assign_slots(idx, rank, pstart.astype(F32).reshape(N_EXPERTS, 1), t=t, tt=tt)[:TOP_K]
    pos = pos2d.reshape(-1)
    tok_of_asg = np.tile(np.arange(t, dtype=np.int32), TOP_K)
    n_pad = n_blocks * MOE_ROWS - n_asg
    seg_pad_end = jnp.cumsum(padded - counts)
    j = jnp.arange(n_pad, dtype=jnp.int32)
    pad_e = jnp.sum((seg_pad_end[None, :] <= j[:, None]).astype(jnp.int32), axis=1)
    seg_base = pstart + counts - (seg_pad_end - (padded - counts))
    in_seg = j + jnp.sum(jnp.where(pad_e[:, None] == jnp.arange(N_EXPERTS)[None, :], seg_base[None, :], 0), axis=1)
    pad_slot = jnp.where(pad_e < N_EXPERTS, in_seg, pend[-1] + j - seg_pad_end[-1])
    _, slot_tok = lax.sort((jnp.concatenate([pos, pad_slot]), jnp.concatenate([tok_of_asg, pad_slot % t])),
                           num_keys=1)
    unit = n_blocks // sum(MOE_DISPATCH_SPLIT)
    starts = [unit * sum(MOE_DISPATCH_SPLIT[:i]) for i in range(len(MOE_DISPATCH_SPLIT) + 1)]
    xs = [_sc_gather_rows(tok, slot_tok[lo * MOE_ROWS:hi * MOE_ROWS]) for lo, hi in zip(starts[:-1], starts[1:])]
    y = None
    for part, lo in zip(xs, starts[:-1]):
        y = _experts(part, block_e, nb_used, seg_idx, next_start, wg, wu, wd, layer, y, block_off=lo,
                     n_blocks=n_blocks)
    step = tm * sc_rows // math.gcd(tm * TOP_K, sc_rows)
    t_first = int(t * MOE_COMBINE_FIRST) // step * step
    cuts = [0, t_first, t] if 0 < t_first < t and (t - t_first) % step == 0 else [0, t]
    outs = None
    gw_t = gw.T
    y3 = [_sc_gather_rows(y, pos2d[:, lo:hi].reshape(-1)).reshape(TOP_K, hi - lo, half)
          for lo, hi in zip(cuts[:-1], cuts[1:])]
    for part, lo in zip(y3, cuts[:-1]):
        outs = _combine_ln(part, gw_t, tok, x, sg, su, sd, mod, ln_g, ln_b, mod_next, outs, t=t,
                           row_off=lo, tm=tm, alpha=alpha, mod_row=mod_row)
    return outs


def _rope64(r, c_ref, sa_ref, sb_ref):
    return r * c_ref[...] + pltpu.roll(r, LANES - QK_ROPE // 2, 1) * sa_ref[...] + pltpu.roll(r, QK_ROPE // 2, 1) * sb_ref[...]


def _mla_q_kernel(d_ref, gain_ref, w_ref, c_ref, sa_ref, sb_ref, q_ref, *, scale):
    n = _rms(d_ref[...], gain_ref[...]).astype(BF16)
    q = jnp.dot(n, w_ref[...], preferred_element_type=F32)
    for h in range(MLA_HEADS):
        lo = h * MLA_DK_PAD
        q_ref[:, lo:lo + QK_NOPE] = (q[:, lo:lo + QK_NOPE] * scale).astype(q_ref.dtype)
        r = _rope64(q[:, lo + QK_NOPE:lo + MLA_DK_PAD], c_ref, sa_ref, sb_ref)
        q_ref[:, lo + QK_NOPE:lo + MLA_DK_PAD] = (r * scale).astype(q_ref.dtype)


def _mla_kv_kernel(ckv_ref, kr_ref, gain_ref, wk_ref, wv_ref, c_ref, sa_ref, sb_ref, k_ref, v_ref):
    n = _rms(ckv_ref[...], gain_ref[...]).astype(BF16)
    kn = jnp.dot(n, wk_ref[...], preferred_element_type=F32)
    v_ref[...] = jnp.dot(n, wv_ref[...], preferred_element_type=F32).astype(v_ref.dtype)
    kr = _rope64(kr_ref[...], c_ref, sa_ref, sb_ref).astype(k_ref.dtype)
    for h in range(MLA_HEADS):
        lo = h * MLA_DK_PAD
        k_ref[:, lo:lo + QK_NOPE] = kn[:, h * QK_NOPE:(h + 1) * QK_NOPE].astype(k_ref.dtype)
        k_ref[:, lo + QK_NOPE:lo + MLA_DK_PAD] = kr


def _axial_angles(n_tok, rot_dim):
    rows = n_tok // GRID_W
    n_freq = rot_dim // 4
    inv = (ROPE_THETA ** (-np.arange(n_freq, dtype=np.float32) / n_freq)).astype(np.float32)
    row = np.repeat(np.arange(rows, dtype=np.float32), GRID_W)
    col = np.tile(np.arange(GRID_W, dtype=np.float32), rows)
    return np.concatenate([row[:, None] * inv, col[:, None] * inv], axis=-1)


def _rope_tables_128(n_tok, ident_rows):
    ang = _axial_angles(n_tok, HEAD_DIM)
    cos, sin = np.cos(ang), np.sin(ang)
    c = np.concatenate([cos, cos], axis=-1)
    s = np.concatenate([-sin, sin], axis=-1)
    c = np.concatenate([c, np.ones((ident_rows, HEAD_DIM), np.float32)], axis=0)
    s = np.concatenate([s, np.zeros((ident_rows, HEAD_DIM), np.float32)], axis=0)
    return c.astype(np.float32), s.astype(np.float32)


def _rope_tables_64(n_tok, ident_rows):
    ang = _axial_angles(n_tok, QK_ROPE)
    cos, sin = np.cos(ang), np.sin(ang)
    half = QK_ROPE // 2
    z = np.zeros((n_tok, LANES - QK_ROPE), np.float32)
    zh = np.zeros((n_tok, half), np.float32)
    c = np.concatenate([cos, cos, z], axis=-1)
    sa = np.concatenate([-sin, zh, z], axis=-1)
    sb = np.concatenate([zh, sin, z], axis=-1)
    ci = np.concatenate([np.ones((ident_rows, QK_ROPE), np.float32),
                         np.zeros((ident_rows, LANES - QK_ROPE), np.float32)], axis=-1)
    zi = np.zeros((ident_rows, LANES), np.float32)
    tables = np.concatenate([c, ci], 0), np.concatenate([sa, zi], 0), np.concatenate([sb, zi], 0)
    return tuple(tab.astype(np.float32) for tab in tables)


def kernel(x, c, ctx, c_ctx, w_ada, b_ada, ln_g, ln_b, a_w_in, a_conv_w, a_q_gain, a_k_gain, a_w_out, m_w_down, m_q_gain, m_kv_gain, m_w_uq, m_w_ukv, m_w_out, router_w, router_b, e_w_gate, e_w_up, e_w_down, s_w_gate, s_w_up, s_w_down):
    batch, seq, d = x.shape
    ctx_len = ctx.shape[1]
    depth = w_ada.shape[0]
    assert depth == 2, "one conv+GQA layer followed by one MLA layer"
    alpha = (2 * depth) ** 0.25
    t_lat = batch * seq
    t_ctx = batch * ctx_len
    t_all = t_lat + t_ctx
    tr = ROW_TILE
    assert seq % tr == 0 and ctx_len % tr == 0 and seq % GRID_W == 0
    lat_tiles = t_lat // tr
    lat_seq_tiles = seq // tr
    ctx_seq_tiles = ctx_len // tr
    lk = ctx_len + seq

    def mod_row(r):
        return jnp.minimum(r // seq, batch)

    def kv_block(i):
        is_lat = i < lat_tiles
        cidx = i - lat_tiles
        b = jnp.where(is_lat, i // lat_seq_tiles, cidx // ctx_seq_tiles)
        rb = jnp.where(is_lat, ctx_seq_tiles + i % lat_seq_tiles, cidx % ctx_seq_tiles)
        return b, rb

    def pos_block(i):
        return jnp.where(i < lat_tiles, i % lat_seq_tiles, lat_seq_tiles)

    rows = -(-(batch + 1) // SUBLANES) * SUBLANES
    cond = jnp.concatenate([c, c_ctx[None, :], jnp.zeros((rows - batch - 1, d), F32)], axis=0)
    mod = _ada_table(cond, w_ada, b_ada).reshape(depth, rows, 1, 6 * d)

    x_pair = (x.reshape(t_lat, d), ctx.reshape(t_ctx, d))

    tm_in = W_IN_ROW_TILE if (t_ctx % W_IN_ROW_TILE == 0 and seq % W_IN_ROW_TILE == 0) else tr
    proj = _mod_matmul(x_pair, mod[0], a_w_in[0].astype(BF16), BF16, tm_in, W_IN_COL_TILE, mod_row)

    cos128, sin128 = _rope_tables_128(seq, tr)
    d_q = ATT_HEADS * HEAD_DIM
    d_kv = ATT_KV_HEADS * HEAD_DIM
    qkv_w = d_q + 2 * d_kv
    qkv_blk = 3 * CONV_DIM // qkv_w
    assert qkv_blk * qkv_w == 3 * CONV_DIM
    q0, k0, v0 = pl.pallas_call(
        functools.partial(_qkprep_kernel, scale=1.0 / math.sqrt(HEAD_DIM)),
        grid=(t_all // tr,),
        in_specs=[
            pl.BlockSpec((tr, qkv_w), lambda i: (i, qkv_blk)),
            pl.BlockSpec((tr, HEAD_DIM), lambda i: (pos_block(i), 0)),
            pl.BlockSpec((tr, HEAD_DIM), lambda i: (pos_block(i), 0)),
            _const_spec((1, HEAD_DIM)), _const_spec((1, HEAD_DIM)),
        ],
        out_specs=[
            pl.BlockSpec((tr, d_q), lambda i: (i, 0)),
            pl.BlockSpec((None, tr, d_kv), lambda i: (*kv_block(i), 0)),
            pl.BlockSpec((None, tr, d_kv), lambda i: (*kv_block(i), 0)),
        ],
        out_shape=[
            jax.ShapeDtypeStruct((t_all, d_q), BF16),
            jax.ShapeDtypeStruct((batch, lk, d_kv), BF16),
            jax.ShapeDtypeStruct((batch, lk, d_kv), BF16),
        ],
        compiler_params=_params(("parallel",)),
        name="qk_prep",
    )(proj, cos128, sin128, a_q_gain[0].reshape(1, HEAD_DIM), a_k_gain[0].reshape(1, HEAD_DIM))

    grp = ATT_HEADS // ATT_KV_HEADS
    att_lat = _attention(q0, k0, v0, batch=batch, sq=seq, lk=lk, n_kv=ATT_KV_HEADS, group=grp, dk=HEAD_DIM,
                         dv=HEAD_DIM, tq=GQA_Q_TILE if seq % GQA_Q_TILE == 0 else tr, rows=ATTN_CHAIN_ROWS, q_row_off=0)
    att_ctx = _attention(q0, k0, v0, batch=batch, sq=ctx_len, lk=ctx_len, n_kv=ATT_KV_HEADS, group=grp,
                         dk=HEAD_DIM, dv=HEAD_DIM, tq=tr, rows=ATTN_CHAIN_ROWS, q_row_off=t_lat)

    conv0 = _conv_gate(proj, a_conv_w[0], t=t_all, tm=tr, tc=CONV_COL_TILE, lat_tiles=lat_tiles,
                       lat_seq_tiles=lat_seq_tiles, ctx_seq_tiles=ctx_seq_tiles)

    w_out0 = a_w_out[0].astype(BF16)
    x1, tok0 = _outproj_ln([conv0, (att_lat, att_ctx)], [w_out0[:CONV_DIM], w_out0[CONV_DIM:]], x_pair, mod[0],
                           ln_g[0, 0].reshape(1, d), ln_b[0, 0].reshape(1, d), t=t_all, tm=tr, alpha=alpha,
                           mod_row=mod_row)

    x2, u1 = _moe(tok0, x1, t_all, 0, router_w[0], router_b[0], e_w_gate, e_w_up, e_w_down,
                  s_w_gate[0].astype(BF16), s_w_up[0].astype(BF16), s_w_down[0].astype(BF16), mod[0],
                  ln_g[0, 1].reshape(1, d), ln_b[0, 1].reshape(1, d), mod[1], alpha=alpha,
                  mod_row=mod_row, tm=COMBINE_ROW_TILE)

    n_down = Q_LORA + KV_LORA + QK_ROPE
    n_down_pad = -(-n_down // LANES) * LANES
    w_down = jnp.pad(m_w_down[0], ((0, 0), (0, n_down_pad - n_down))).astype(BF16)
    down = _matmul(u1, w_down, F32, DOWN_ROW_TILE, n_down_pad)

    dqk = QK_NOPE + QK_ROPE
    w_uq = m_w_uq[0].reshape(Q_LORA, MLA_HEADS, dqk)
    w_uq = jnp.pad(w_uq, ((0, 0), (0, 0), (0, MLA_DK_PAD - dqk))).reshape(Q_LORA, MLA_HEADS * MLA_DK_PAD).astype(BF16)
    w_ukv = m_w_ukv[0].reshape(KV_LORA, MLA_HEADS, QK_NOPE + V_DIM)
    w_uk = w_ukv[:, :, :QK_NOPE].reshape(KV_LORA, MLA_HEADS * QK_NOPE).astype(BF16)
    w_uv = w_ukv[:, :, QK_NOPE:].reshape(KV_LORA, MLA_HEADS * V_DIM).astype(BF16)

    c64, sa64, sb64 = _rope_tables_64(seq, tr)
    rope_specs = [pl.BlockSpec((tr, LANES), lambda i: (pos_block(i), 0))] * 3
    q1 = pl.pallas_call(
        functools.partial(_mla_q_kernel, scale=1.0 / math.sqrt(dqk)),
        grid=(lat_tiles,),
        in_specs=[
            pl.BlockSpec((tr, Q_LORA), lambda i: (i, 0)),
            _const_spec((1, Q_LORA)),
            _const_spec(w_uq.shape),
        ] + rope_specs,
        out_specs=pl.BlockSpec((tr, MLA_HEADS * MLA_DK_PAD), lambda i: (i, 0)),
        out_shape=jax.ShapeDtypeStruct((t_lat, MLA_HEADS * MLA_DK_PAD), BF16),
        compiler_params=_params(("parallel",)),
        name="mla_q",
    )(down, m_q_gain[0].reshape(1, Q_LORA), w_uq, c64, sa64, sb64)

    assert KV_LORA == Q_LORA and (Q_LORA + KV_LORA) % LANES == 0
    k1, v1 = pl.pallas_call(
        _mla_kv_kernel,
        grid=(t_all // tr,),
        in_specs=[
            pl.BlockSpec((tr, KV_LORA), lambda i: (i, 1)),
            pl.BlockSpec((tr, LANES), lambda i: (i, (Q_LORA + KV_LORA) // LANES)),
            _const_spec((1, KV_LORA)),
            _const_spec(w_uk.shape), _const_spec(w_uv.shape),
        ] + rope_specs,
        out_specs=[
            pl.BlockSpec((None, tr, MLA_HEADS * MLA_DK_PAD), lambda i: (*kv_block(i), 0)),
            pl.BlockSpec((None, tr, MLA_HEADS * V_DIM), lambda i: (*kv_block(i), 0)),
        ],
        out_shape=[
            jax.ShapeDtypeStruct((batch, lk, MLA_HEADS * MLA_DK_PAD), BF16),
            jax.ShapeDtypeStruct((batch, lk, MLA_HEADS * V_DIM), BF16),
        ],
        compiler_params=_params(("parallel",)),
        name="mla_kv",
    )(down, down, m_kv_gain[0].reshape(1, KV_LORA), w_uk, w_uv, c64, sa64, sb64)

    att1 = _attention(q1, k1, v1, batch=batch, sq=seq, lk=lk, n_kv=MLA_HEADS, group=1, dk=MLA_DK_PAD, dv=V_DIM,
                      tq=next(q for q in MLA_Q_TILES + (tr,) if seq % q == 0), rows=ATTN_CHAIN_ROWS, q_row_off=0)

    x3, tok1 = _outproj_ln([att1], [m_w_out[0].astype(BF16)], x2, mod[1], ln_g[1, 0].reshape(1, d),
                           ln_b[1, 0].reshape(1, d), t=t_lat, tm=tr, alpha=alpha, mod_row=mod_row)

    (x4,) = _moe(tok1, x3, t_lat, 1, router_w[1], router_b[1], e_w_gate, e_w_up, e_w_down,
                 s_w_gate[1].astype(BF16), s_w_up[1].astype(BF16), s_w_down[1].astype(BF16), mod[1],
                 ln_g[1, 1].reshape(1, d), ln_b[1, 1].reshape(1, d), None, alpha=alpha,
                 mod_row=mod_row, tm=COMBINE_ROW_TILE)
    return x4.reshape(batch, seq, d)
```

```python
import functools
import math

import jax
import jax.numpy as jnp
import numpy as np
from jax import lax
from jax.experimental import pallas as pl
from jax.experimental.pallas import tpu as pltpu
from jax.experimental.pallas import tpu_sc as plsc

F32 = jnp.float32
BF16 = jnp.bfloat16

GRID_W = 64
CONV_DIM = 1024
ATT_HEADS = 8
ATT_KV_HEADS = 2
HEAD_DIM = 128
MLA_HEADS = 16
Q_LORA = 512
KV_LORA = 512
QK_NOPE = 128
QK_ROPE = 64
V_DIM = 128
N_EXPERTS = 64
TOP_K = 6
N_GROUPS = 8
TOPK_GROUPS = 4
ROUTED_SCALE = 2.5
ROPE_THETA = 10000.0
LN_EPS = 1e-5
RMS_EPS = 1e-6

V7X_VMEM_LIMIT_BYTES = 56 * 1024 * 1024
LANES = 128
SUBLANES = 8
MOE_ROWS = 512
MOE_DISPATCH_SPLIT = (1, 3, 2, 2)
MOE_COMBINE_PARTS = 2
V7X_SC_CORES = 2
V7X_SC_SUBCORES = 16
SC_GATHER_ROWS = 16
SC_GATHER_BUFFERS = 4
MLA_DK_PAD = 256

ROW_TILE = 256
ADA_COL_TILE = 1024
W_IN_ROW_TILE = 1024
W_IN_COL_TILE = 768
DOWN_ROW_TILE = 512
CONV_COL_TILE = 1024
OUTPROJ_ROW_TILE = 512
ROUTER_TILE = 512
ATTN_CHAIN_ROWS = 256
GQA_Q_TILE = 512
MLA_Q_TILES = (2048, 1024, 512)
COMBINE_ROW_TILE = 256


def _params(sem):
    return pltpu.CompilerParams(dimension_semantics=sem, vmem_limit_bytes=V7X_VMEM_LIMIT_BYTES)


def _const_spec(shape):
    nd = len(shape)
    return pl.BlockSpec(shape, lambda *_: (0,) * nd)


def _ada_kernel(s_ref, w_ref, b_ref, o_ref):
    s = s_ref[...]
    s = s * (1.0 / (1.0 + jnp.exp(-s)))
    o_ref[...] = jnp.dot(s.astype(BF16), w_ref[...].astype(BF16), preferred_element_type=F32) + b_ref[...]


def _ada_table(cond, w_ada, b_ada):
    depth, d, n = w_ada.shape
    r = cond.shape[0]
    tn = ADA_COL_TILE
    return pl.pallas_call(
        _ada_kernel,
        grid=(depth, n // tn),
        in_specs=[
            pl.BlockSpec((r, d), lambda l, j: (0, 0)),
            pl.BlockSpec((None, d, tn), lambda l, j: (l, 0, j)),
            pl.BlockSpec((None, 1, tn), lambda l, j: (l, 0, j)),
        ],
        out_specs=pl.BlockSpec((None, r, tn), lambda l, j: (l, 0, j)),
        out_shape=jax.ShapeDtypeStruct((depth, r, n), F32),
        compiler_params=_params(("parallel", "parallel")),
        name="ada_table",
    )(cond, w_ada, b_ada.reshape(depth, 1, n))


def _mod_spec(d, chunk, mod_row, tm):
    return pl.BlockSpec((None, 1, d), lambda i: (mod_row(i * tm), 0, chunk))


def _mm_kernel(a_ref, w_ref, o_ref):
    o_ref[...] = jnp.dot(a_ref[...], w_ref[...], preferred_element_type=F32).astype(o_ref.dtype)


def _matmul(a, w, out_dtype, tm, tn):
    m, k = a.shape
    n = w.shape[1]
    return pl.pallas_call(
        _mm_kernel,
        grid=(m // tm, n // tn),
        in_specs=[
            pl.BlockSpec((tm, k), lambda i, j: (i, 0)),
            pl.BlockSpec((k, tn), lambda i, j: (0, j)),
        ],
        out_specs=pl.BlockSpec((tm, tn), lambda i, j: (i, j)),
        out_shape=jax.ShapeDtypeStruct((m, n), out_dtype),
        compiler_params=_params(("parallel", "parallel")),
        name="matmul",
    )(a, w)


def _pair_specs(pair, tm):
    lat, ctx = pair
    lat_tiles = lat.shape[0] // tm
    assert lat_tiles * tm == lat.shape[0] and ctx.shape[0] % tm == 0 and lat.shape[1] == ctx.shape[1]
    width = lat.shape[1]
    return [pl.BlockSpec((tm, width), lambda i, *_: (jnp.minimum(i, lat_tiles - 1), 0)),
            pl.BlockSpec((tm, width), lambda i, *_: (jnp.maximum(i - lat_tiles, 0), 0))]


def _pair_tile(lat_ref, ctx_ref, lat_tiles):
    return jnp.where(pl.program_id(0) < lat_tiles, lat_ref[...], ctx_ref[...])


def _mod_mm_kernel(xl_ref, xc_ref, sc_ref, sh_ref, w_ref, o_ref, u_ref, *, lat_tiles):
    @pl.when(pl.program_id(1) == 0)
    def _():
        x = _pair_tile(xl_ref, xc_ref, lat_tiles)
        u_ref[...] = (x * (1.0 + sc_ref[...]) + sh_ref[...]).astype(u_ref.dtype)

    o_ref[...] = jnp.dot(u_ref[...], w_ref[...], preferred_element_type=F32).astype(o_ref.dtype)


def _mod_matmul(x_pair, mod, w, out_dtype, tm, tn, mod_row):
    m = x_pair[0].shape[0] + x_pair[1].shape[0]
    k = x_pair[0].shape[1]
    n = w.shape[1]

    def mod_spec(chunk):
        return pl.BlockSpec((None, 1, k), lambda i, j: (mod_row(i * tm), 0, chunk))

    return pl.pallas_call(
        functools.partial(_mod_mm_kernel, lat_tiles=x_pair[0].shape[0] // tm),
        grid=(m // tm, n // tn),
        in_specs=_pair_specs(x_pair, tm) + [
            mod_spec(1), mod_spec(0),
            pl.BlockSpec((k, tn), lambda i, j: (0, j)),
        ],
        out_specs=pl.BlockSpec((tm, tn), lambda i, j: (i, j)),
        out_shape=jax.ShapeDtypeStruct((m, n), out_dtype),
        scratch_shapes=[pltpu.VMEM((tm, k), w.dtype)],
        compiler_params=_params(("parallel", "arbitrary")),
        name="mod_matmul",
    )(*x_pair, mod, mod, w)


def _rms(t, gain):
    return t * lax.rsqrt(jnp.mean(t * t, axis=-1, keepdims=True) + RMS_EPS) * gain


def _qkprep_kernel(p_ref, cos_ref, sin_ref, qg_ref, kg_ref, q_ref, k_ref, v_ref, *, scale):
    cos = cos_ref[...]
    sin = sin_ref[...]

    def norm_rope(t, gain):
        y = _rms(t.astype(F32), gain)
        return y * cos + pltpu.roll(y, HEAD_DIM // 2, 1) * sin

    for h in range(ATT_HEADS):
        sl = slice(h * HEAD_DIM, (h + 1) * HEAD_DIM)
        q_ref[:, sl] = (norm_rope(p_ref[:, sl], qg_ref[...]) * scale).astype(q_ref.dtype)
    k0 = ATT_HEADS * HEAD_DIM
    for h in range(ATT_KV_HEADS):
        sl = slice(h * HEAD_DIM, (h + 1) * HEAD_DIM)
        k_ref[:, sl] = norm_rope(p_ref[:, k0 + h * HEAD_DIM:k0 + (h + 1) * HEAD_DIM], kg_ref[...]).astype(k_ref.dtype)
    v0 = k0 + ATT_KV_HEADS * HEAD_DIM
    v_ref[...] = p_ref[:, v0:v0 + ATT_KV_HEADS * HEAD_DIM].astype(v_ref.dtype)


def _attn_kernel(q_ref, k_ref, v_ref, o_ref, *, group, tq, rows, dk, dv):
    k = k_ref[...]
    v = v_ref[...]
    for h in range(group):
        for r in range(0, tq, rows):
            q = q_ref[r:r + rows, h * dk:(h + 1) * dk]
            s = lax.dot_general(q, k, (((1,), (1,)), ((), ())), preferred_element_type=F32)
            m = jnp.max(s, axis=-1, keepdims=True)
            p = jnp.exp(s - m)
            l = jnp.sum(p, axis=-1, keepdims=True)
            o = jnp.dot(p.astype(v.dtype), v, preferred_element_type=F32)
            o_ref[r:r + rows, h * dv:(h + 1) * dv] = (o / l).astype(o_ref.dtype)


def _attention(q, k, v, *, batch, sq, lk, n_kv, group, dk, dv, tq, rows, q_row_off):
    nq = sq // tq
    off = q_row_off // tq
    assert tq % rows == 0 and q_row_off % tq == 0 and sq % tq == 0
    return pl.pallas_call(
        functools.partial(_attn_kernel, group=group, tq=tq, rows=rows, dk=dk, dv=dv),
        grid=(batch, n_kv, nq),
        in_specs=[
            pl.BlockSpec((tq, group * dk), lambda b, g, i: (off + b * nq + i, g)),
            pl.BlockSpec((None, lk, dk), lambda b, g, i: (b, 0, g)),
            pl.BlockSpec((None, lk, dv), lambda b, g, i: (b, 0, g)),
        ],
        out_specs=pl.BlockSpec((tq, group * dv), lambda b, g, i: (b * nq + i, g)),
        out_shape=jax.ShapeDtypeStruct((batch * sq, n_kv * group * dv), BF16),
        compiler_params=_params(("parallel", "parallel", "parallel")),
        name="attention",
    )(q, k, v)


def _conv_kernel(gb_ref, gc_ref, hv_ref, gcp_ref, hvp_ref, gcn_ref, hvn_ref, w_ref, o_ref, *,
                 tm, lat_tiles, lat_seq_tiles, ctx_seq_tiles):
    i = pl.program_id(0)
    is_lat = i < lat_tiles
    pos = jnp.where(is_lat, i % lat_seq_tiles, (i - lat_tiles) % ctx_seq_tiles)
    seq_tiles = jnp.where(is_lat, lat_seq_tiles, ctx_seq_tiles)
    not_first = (pos != 0).astype(F32)
    not_last = (pos != seq_tiles - 1).astype(F32)
    p = gc_ref[...].astype(F32) * hv_ref[...].astype(F32)
    halo_prev = gcp_ref[SUBLANES - 1:SUBLANES, :].astype(F32) * hvp_ref[SUBLANES - 1:SUBLANES, :].astype(F32) * not_first
    halo_next = gcn_ref[0:1, :].astype(F32) * hvn_ref[0:1, :].astype(F32) * not_last
    row = lax.broadcasted_iota(jnp.int32, p.shape, 0)
    prev = jnp.where(row == 0, halo_prev, pltpu.roll(p, 1, 0))
    nxt = jnp.where(row == tm - 1, halo_next, pltpu.roll(p, tm - 1, 0))
    w = w_ref[...]
    conv = w[0:1, :] * prev + w[1:2, :] * p + w[2:3, :] * nxt
    o_ref[...] = (gb_ref[...].astype(F32) * conv).astype(o_ref.dtype)


def _conv_gate(p, conv_w, *, t, tm, tc, lat_tiles, lat_seq_tiles, ctx_seq_tiles):
    nct = CONV_DIM // tc
    hb = tm // SUBLANES
    n_halo = t // SUBLANES

    def cur(part):
        return pl.BlockSpec((tm, tc), lambda i, j: (i, part * nct + j))

    def prev(part):
        return pl.BlockSpec((SUBLANES, tc), lambda i, j: (jnp.maximum(i * hb - 1, 0), part * nct + j))

    def nxt(part):
        return pl.BlockSpec((SUBLANES, tc), lambda i, j: (jnp.minimum((i + 1) * hb, n_halo - 1), part * nct + j))

    return pl.pallas_call(
        functools.partial(_conv_kernel, tm=tm, lat_tiles=lat_tiles, lat_seq_tiles=lat_seq_tiles,
                          ctx_seq_tiles=ctx_seq_tiles),
        grid=(t // tm, nct),
        in_specs=[cur(0), cur(1), cur(2), prev(1), prev(2), nxt(1), nxt(2),
                  pl.BlockSpec((3, tc), lambda i, j: (0, j))],
        out_specs=pl.BlockSpec((tm, tc), lambda i, j: (i, j)),
        out_shape=jax.ShapeDtypeStruct((t, CONV_DIM), BF16),
        compiler_params=_params(("parallel", "parallel")),
        name="conv_gate",
    )(p, p, p, p, p, p, p, conv_w)


def _layer_norm(z, g, b):
    mu = jnp.mean(z, axis=-1, keepdims=True)
    zc = z - mu
    var = jnp.mean(zc * zc, axis=-1, keepdims=True)
    return zc * lax.rsqrt(var + LN_EPS) * g + b


def _pack_bf16_pairs(x):
    half = x.shape[1] // 2
    lo = lax.bitcast_convert_type(x[:, :half].astype(BF16).astype(F32), jnp.uint32) >> 16
    hi = lax.bitcast_convert_type(x[:, half:].astype(BF16).astype(F32), jnp.uint32) & jnp.uint32(0xFFFF0000)
    return lax.bitcast_convert_type(lo | hi, jnp.int32)


def _unpack_bf16_pairs(w):
    u = lax.bitcast_convert_type(w, jnp.uint32)
    lo = lax.bitcast_convert_type(u << 16, F32).astype(BF16)
    hi = lax.bitcast_convert_type(u & jnp.uint32(0xFFFF0000), F32).astype(BF16)
    return lo, hi


def _dot_halves(lo, hi, w_ref):
    half = lo.shape[1]
    return (jnp.dot(lo, w_ref[:half, :], preferred_element_type=F32)
            + jnp.dot(hi, w_ref[half:, :], preferred_element_type=F32))


def _outproj_ln_kernel(*refs, widths, lat_tiles, alpha):
    refs = list(refs)

    def take(width):
        got = [refs.pop(0) for _ in range(width)]
        return got[0][...] if width == 1 else _pair_tile(got[0], got[1], lat_tiles)

    acts = [take(w) for w in widths[:-1]]
    w_refs = [refs.pop(0) for _ in acts]
    x = take(widths[-1])
    gate_ref, lng_ref, lnb_ref, sc_ref, sh_ref, xo_ref, tok_ref = refs
    y = jnp.dot(acts[0], w_refs[0][...], preferred_element_type=F32)
    for a, w_ref in zip(acts[1:], w_refs[1:]):
        y = y + jnp.dot(a, w_ref[...], preferred_element_type=F32)
    xn = _layer_norm(alpha * x + gate_ref[...] * y, lng_ref[...], lnb_ref[...])
    xo_ref[...] = xn
    tok_ref[...] = _pack_bf16_pairs(xn * (1.0 + sc_ref[...]) + sh_ref[...])


def _outproj_ln(a_list, w_list, x, mod, ln_g, ln_b, *, t, tm, alpha, mod_row):
    d = w_list[0].shape[1]
    operands, in_specs, widths, lat_tiles = [], [], [], 0

    def add_rows(src):
        nonlocal lat_tiles
        if isinstance(src, tuple):
            in_specs.extend(_pair_specs(src, tm))
            operands.extend(src)
            widths.append(2)
            lat_tiles = src[0].shape[0] // tm
        else:
            in_specs.append(pl.BlockSpec((tm, src.shape[1]), lambda i: (i, 0)))
            operands.append(src)
            widths.append(1)

    for a in a_list:
        add_rows(a)
    in_specs += [_const_spec(w.shape) for w in w_list]
    operands += list(w_list)
    add_rows(x)
    in_specs += [
        _mod_spec(d, 2, mod_row, tm),
        _const_spec((1, d)), _const_spec((1, d)),
        _mod_spec(d, 4, mod_row, tm),
        _mod_spec(d, 3, mod_row, tm),
    ]
    return pl.pallas_call(
        functools.partial(_outproj_ln_kernel, widths=tuple(widths), lat_tiles=lat_tiles, alpha=alpha),
        grid=(t // tm,),
        in_specs=in_specs,
        out_specs=[pl.BlockSpec((tm, d), lambda i: (i, 0)), pl.BlockSpec((tm, d // 2), lambda i: (i, 0))],
        out_shape=[jax.ShapeDtypeStruct((t, d), F32), jax.ShapeDtypeStruct((t, d // 2), jnp.int32)],
        compiler_params=_params(("parallel",)),
        name="outproj_ln",
    )(*operands, mod, ln_g, ln_b, mod, mod)


def _router_kernel(t_ref, rw_ref, rb_ref, tri_ref, idx_ref, gw_ref, rank_ref, cnt_ref):
    @pl.when(pl.program_id(0) == 0)
    def _():
        cnt_ref[...] = jnp.zeros_like(cnt_ref)

    lo, hi = _unpack_bf16_pairs(t_ref[...])
    half = lo.shape[1]
    nt = (((1,), (1,)), ((), ()))
    logits = (lax.dot_general(rw_ref[:, :half], lo, nt, preferred_element_type=F32)
              + lax.dot_general(rw_ref[:, half:], hi, nt, preferred_element_type=F32))
    scores = 1.0 / (1.0 + jnp.exp(-logits))
    sel = scores + rb_ref[...]
    gsz = N_EXPERTS // N_GROUPS
    neg = -jnp.inf
    sub = lax.broadcasted_iota(jnp.int32, (gsz, sel.shape[1]), 0)
    slabs = [sel[g * gsz:(g + 1) * gsz, :] for g in range(N_GROUPS)]
    gscore = []
    for s in slabs:
        m1 = jnp.max(s, axis=0, keepdims=True)
        a1 = jnp.min(jnp.where(s == m1, sub, gsz), axis=0, keepdims=True)
        m2 = jnp.max(jnp.where(sub == a1, neg, s), axis=0, keepdims=True)
        gscore.append(m1 + m2)
    masked = []
    for g in range(N_GROUPS):
        ahead = jnp.zeros(gscore[g].shape, jnp.int32)
        for h in range(N_GROUPS):
            if h == g:
                continue
            beats = gscore[h] >= gscore[g] if h < g else gscore[h] > gscore[g]
            ahead = ahead + beats.astype(jnp.int32)
        masked.append(jnp.where(ahead < TOPK_GROUPS, slabs[g], neg))
    cur = jnp.concatenate(masked, axis=0)
    eio = lax.broadcasted_iota(jnp.int32, cur.shape, 0)
    picks, weights = [], []
    for _ in range(TOP_K):
        m = jnp.max(cur, axis=0, keepdims=True)
        a = jnp.min(jnp.where(cur == m, eio, N_EXPERTS), axis=0, keepdims=True)
        hit = eio == a
        picks.append(a)
        weights.append(jnp.sum(jnp.where(hit, scores, 0.0), axis=0, keepdims=True))
        cur = jnp.where(hit, neg, cur)
    total = weights[0]
    for w in weights[1:]:
        total = total + w
    for k in range(TOP_K):
        idx_ref[k:k + 1, :] = picks[k]
        gw_ref[k:k + 1, :] = weights[k] / total * ROUTED_SCALE
    for k in range(TOP_K, SUBLANES):
        idx_ref[k:k + 1, :] = jnp.zeros_like(picks[0])
        gw_ref[k:k + 1, :] = jnp.zeros_like(weights[0])
        rank_ref[k:k + 1, :] = jnp.zeros_like(picks[0])
    base = cnt_ref[:, 0:1]
    for k in range(TOP_K):
        onehot = jnp.where(eio == picks[k], 1.0, 0.0)
        before = jnp.dot(onehot.astype(BF16), tri_ref[...], preferred_element_type=F32)
        rank_ref[k:k + 1, :] = jnp.sum(onehot * (before + base), axis=0, keepdims=True).astype(jnp.int32)
        base = base + jnp.sum(onehot, axis=1, keepdims=True)
    cnt_ref[...] = jnp.broadcast_to(base, cnt_ref.shape)


def _router(tok, rw_t, rb, *, t, tt):
    half = tok.shape[1]
    tri = jnp.asarray(np.arange(tt)[:, None] < np.arange(tt)[None, :], BF16)
    blk = pl.BlockSpec((SUBLANES, tt), lambda i: (0, i))
    return pl.pallas_call(
        _router_kernel,
        grid=(t // tt,),
        in_specs=[
            pl.BlockSpec((tt, half), lambda i: (i, 0)),
            _const_spec((N_EXPERTS, 2 * half)),
            _const_spec((N_EXPERTS, 1)),
            _const_spec((tt, tt)),
        ],
        out_specs=[blk, blk, blk, _const_spec((N_EXPERTS, LANES))],
        out_shape=[jax.ShapeDtypeStruct((SUBLANES, t), jnp.int32), jax.ShapeDtypeStruct((SUBLANES, t), F32),
                   jax.ShapeDtypeStruct((SUBLANES, t), jnp.int32), jax.ShapeDtypeStruct((N_EXPERTS, LANES), F32)],
        compiler_params=_params(("arbitrary",)),
        name="router",
    )(tok, rw_t, rb, tri)


def _slots_kernel(idx_ref, rank_ref, start_ref, pos_ref):
    start = start_ref[...]
    eio = lax.broadcasted_iota(jnp.int32, (N_EXPERTS, idx_ref.shape[1]), 0)
    for k in range(TOP_K):
        seg = jnp.sum(jnp.where(eio == idx_ref[k:k + 1, :], start, 0.0), axis=0, keepdims=True)
        pos_ref[k:k + 1, :] = rank_ref[k:k + 1, :] + seg.astype(jnp.int32)
    for k in range(TOP_K, SUBLANES):
        pos_ref[k:k + 1, :] = jnp.zeros((1, idx_ref.shape[1]), jnp.int32)


def _assign_slots(idx, rank, seg_start, *, t, tt):
    blk = pl.BlockSpec((SUBLANES, tt), lambda i: (0, i))
    return pl.pallas_call(
        _slots_kernel,
        grid=(t // tt,),
        in_specs=[blk, blk, _const_spec((N_EXPERTS, 1))],
        out_specs=blk,
        out_shape=jax.ShapeDtypeStruct((SUBLANES, t), jnp.int32),
        compiler_params=_params(("parallel",)),
        name="assign_slots",
    )(idx, rank, seg_start)


def _sc_gather_rows(table, idx):
    n = idx.shape[0]
    d = table.shape[1]
    n_workers = V7X_SC_CORES * V7X_SC_SUBCORES
    per_w = n // n_workers
    n_chunks = per_w // SC_GATHER_ROWS
    assert per_w * n_workers == n and n_chunks * SC_GATHER_ROWS == per_w
    n_buf = next(b for b in range(SC_GATHER_BUFFERS, 1, -1) if n_chunks % b == 0)
    mesh = plsc.VectorSubcoreMesh(core_axis_name="c", subcore_axis_name="s", num_cores=V7X_SC_CORES,
                                  num_subcores=V7X_SC_SUBCORES)

    @functools.partial(
        pl.kernel,
        out_type=jax.ShapeDtypeStruct((n, d), table.dtype),
        mesh=mesh,
        scratch_types=[
            pltpu.VMEM((per_w,), jnp.int32),
            pltpu.VMEM((n_buf, SC_GATHER_ROWS, d), table.dtype),
            pltpu.SemaphoreType.DMA((n_buf,)),
            pltpu.SemaphoreType.DMA((n_buf,)),
        ],
        name="sc_gather_rows",
    )
    def gather(table_hbm, idx_hbm, out_hbm, idx_v, rows_v, gsem, wsem):
        wid = lax.axis_index("s") * V7X_SC_CORES + lax.axis_index("c")
        base = wid * per_w
        pltpu.sync_copy(idx_hbm.at[pl.ds(base, per_w)], idx_v)

        def gather_copy(c, b):
            return pltpu.make_async_copy(table_hbm.at[idx_v.at[pl.ds(c * SC_GATHER_ROWS, SC_GATHER_ROWS)]],
                                         rows_v.at[b], gsem.at[b])

        def write_copy(c, b):
            return pltpu.make_async_copy(rows_v.at[b], out_hbm.at[pl.ds(base + c * SC_GATHER_ROWS, SC_GATHER_ROWS)],
                                         wsem.at[b])

        for b in range(n_buf - 1):
            gather_copy(b, b).start()

        @pl.loop(0, n_chunks, step=n_buf)
        def _(g):
            for b in range(n_buf):
                c = g + b
                prev = (b + n_buf - 1) % n_buf
                gather_copy(c, b).wait()
                write_copy(c, b).start()

                @pl.when(c >= 1)
                def _():
                    write_copy(c - 1, prev).wait()

                @pl.when(c + n_buf - 1 < n_chunks)
                def _():
                    gather_copy(c + n_buf - 1, prev).start()

        write_copy(n_chunks - 1, (n_chunks - 1) % n_buf).wait()

    return gather(table, idx)


def _experts_kernel(be_ref, nbu_ref, seg_ref, nstart_ref, x_ref, wg_hbm, wu_hbm, wd_hbm, *rest, block_off, n_call,
                    layer):
    y_ref, wgf, wuf, wdf, wgb, wub, wdb, sem = rest[-8:]
    step = pl.program_id(0)
    b = block_off + step
    nbu = nbu_ref[0]
    end = jnp.minimum(nbu, block_off + n_call)

    def weight_copies(e, slot):
        return [pltpu.make_async_copy(hbm.at[layer, e], buf.at[slot], sem.at[slot, k])
                for k, (hbm, buf) in enumerate(((wg_hbm, wgf), (wu_hbm, wuf), (wd_hbm, wdf)))]

    @pl.when(b < nbu)
    def _():
        first = jnp.logical_or(step == 0, be_ref[b] != be_ref[jnp.maximum(b - 1, 0)])

        @pl.when(first)
        def _():
            slot = seg_ref[b] % 2

            @pl.when(step == 0)
            def _():
                for cp in weight_copies(be_ref[b], slot):
                    cp.start()

            for cp in weight_copies(be_ref[b], slot):
                cp.wait()
            wgb[...] = wgf[slot].astype(BF16)
            wub[...] = wuf[slot].astype(BF16)
            wdb[...] = wdf[slot].astype(BF16)
            nxt = nstart_ref[b]

            @pl.when(nxt < end)
            def _():
                for cp in weight_copies(be_ref[jnp.minimum(nxt, be_ref.shape[0] - 1)], 1 - slot):
                    cp.start()

        lo, hi = _unpack_bf16_pairs(x_ref[...])
        hg = _dot_halves(lo, hi, wgb)
        hu = _dot_halves(lo, hi, wub)
        h = hg * (1.0 / (1.0 + jnp.exp(-hg))) * hu
        y_ref[...] = _pack_bf16_pairs(jnp.dot(h.astype(BF16), wdb[...], preferred_element_type=F32))

    @pl.when(b >= nbu)
    def _():
        y_ref[...] = jnp.zeros_like(y_ref)


def _experts(xs, block_e, nb_used, seg_idx, next_start, wg, wu, wd, layer, y_prev, *, block_off, n_blocks):
    half = xs.shape[1]
    d = 2 * half
    ff = wg.shape[3]
    n_call = xs.shape[0] // MOE_ROWS

    def used(b, nbu):
        return jnp.clip(jnp.minimum(block_off + b, nbu[0] - 1) - block_off, 0, n_call - 1)

    hbm = pl.BlockSpec(memory_space=pl.ANY)
    in_specs = [pl.BlockSpec((MOE_ROWS, half), lambda b, be, nbu, seg, nst: (used(b, nbu), 0)), hbm, hbm, hbm]
    args = [block_e, nb_used, seg_idx, next_start, xs, wg, wu, wd]
    aliases = {}
    if y_prev is not None:
        in_specs.append(hbm)
        aliases = {len(args): 0}
        args.append(y_prev)
    grid_spec = pltpu.PrefetchScalarGridSpec(
        num_scalar_prefetch=4,
        grid=(n_call,),
        in_specs=in_specs,
        out_specs=pl.BlockSpec((MOE_ROWS, half), lambda b, be, nbu, seg, nst: (block_off + b, 0)),
        scratch_shapes=[
            pltpu.VMEM((2, d, ff), F32),
            pltpu.VMEM((2, d, ff), F32),
            pltpu.VMEM((2, ff, d), F32),
            pltpu.VMEM((d, ff), BF16),
            pltpu.VMEM((d, ff), BF16),
            pltpu.VMEM((ff, d), BF16),
            pltpu.SemaphoreType.DMA((2, 3)),
        ],
    )
    return pl.pallas_call(
        functools.partial(_experts_kernel, block_off=block_off, n_call=n_call, layer=layer),
        grid_spec=grid_spec,
        out_shape=jax.ShapeDtypeStruct((n_blocks * MOE_ROWS, half), jnp.int32),
        input_output_aliases=aliases,
        compiler_params=_params(("arbitrary",)),
        name="experts",
    )(*args)


def _combine_ln_kernel(*refs, alpha, emit_next, n_prev):
    y_ref, gw_ref, tok_ref, x_ref, sg_ref, su_ref, sd_ref, gate_ref, lng_ref, lnb_ref = refs[:10]
    outs = refs[len(refs) - (2 if emit_next else 1):]
    if emit_next:
        sc_ref, sh_ref = refs[10:12]
        xo_ref, u_ref = outs
    else:
        (xo_ref,) = outs
    lo, hi = _unpack_bf16_pairs(tok_ref[...])
    hg = _dot_halves(lo, hi, sg_ref)
    hu = _dot_halves(lo, hi, su_ref)
    h = hg * (1.0 / (1.0 + jnp.exp(-hg))) * hu
    gw = gw_ref[...]
    f_lo = f_hi = None
    for k in range(TOP_K):
        y_lo, y_hi = _unpack_bf16_pairs(y_ref[k])
        w = gw[:, k:k + 1]
        f_lo = y_lo.astype(F32) * w if f_lo is None else f_lo + y_lo.astype(F32) * w
        f_hi = y_hi.astype(F32) * w if f_hi is None else f_hi + y_hi.astype(F32) * w
    f = jnp.concatenate([f_lo, f_hi], axis=-1) + jnp.dot(h.astype(BF16), sd_ref[...], preferred_element_type=F32)
    xn = _layer_norm(alpha * x_ref[...] + gate_ref[...] * f, lng_ref[...], lnb_ref[...])
    xo_ref[...] = xn
    if emit_next:
        u_ref[...] = (xn * (1.0 + sc_ref[...]) + sh_ref[...]).astype(u_ref.dtype)


def _combine_ln(y3, gw_t, tok, x, sg, su, sd, mod, ln_g, ln_b, mod_next, prev, *, t, row_off, tm, alpha, mod_row):
    d = x.shape[1]
    emit_next = mod_next is not None
    off = row_off // tm
    assert off * tm == row_off

    def rows(i):
        return (off + i, 0)

    def part_mod_row(r):
        return mod_row(r + row_off)

    in_specs = [
        pl.BlockSpec((TOP_K, tm, d // 2), lambda i: (0, i, 0)),
        pl.BlockSpec((tm, SUBLANES), rows),
        pl.BlockSpec((tm, d // 2), rows),
        pl.BlockSpec((tm, d), rows),
        _const_spec(sg.shape), _const_spec(su.shape), _const_spec(sd.shape),
        _mod_spec(d, 5, part_mod_row, tm),
        _const_spec((1, d)), _const_spec((1, d)),
    ]
    args = [y3, gw_t, tok, x, sg, su, sd, mod, ln_g, ln_b]
    out_specs = [pl.BlockSpec((tm, d), rows)]
    out_shape = [jax.ShapeDtypeStruct((t, d), F32)]
    if emit_next:
        in_specs += [_mod_spec(d, 1, part_mod_row, tm), _mod_spec(d, 0, part_mod_row, tm)]
        args += [mod_next, mod_next]
        out_specs.append(pl.BlockSpec((tm, d), rows))
        out_shape.append(jax.ShapeDtypeStruct((t, d), BF16))
    aliases = {}
    if prev is not None:
        for k, p in enumerate(prev):
            in_specs.append(pl.BlockSpec(memory_space=pl.ANY))
            aliases[len(args)] = k
            args.append(p)
    return pl.pallas_call(
        functools.partial(_combine_ln_kernel, alpha=alpha, emit_next=emit_next, n_prev=len(aliases)),
        grid=(y3.shape[1] // tm,),
        in_specs=in_specs,
        out_specs=out_specs,
        out_shape=out_shape,
        input_output_aliases=aliases,
        compiler_params=_params(("parallel",)),
        name="combine_ln",
    )(*args)


def _moe(tok, x, t, layer, router_w, router_b, wg, wu, wd, sg, su, sd, mod, ln_g, ln_b, mod_next, *, alpha, mod_row,
         tm):
    half = tok.shape[1]
    tt = ROUTER_TILE
    idx, gw, rank, cnt = _router(tok, router_w.T.astype(BF16), router_b.reshape(N_EXPERTS, 1), t=t, tt=tt)
    n_asg = t * TOP_K
    counts = cnt[:, 0].astype(jnp.int32)
    padded = (counts + MOE_ROWS - 1) // MOE_ROWS * MOE_ROWS
    pend = jnp.cumsum(padded)
    pstart = pend - padded
    sc_rows = V7X_SC_CORES * V7X_SC_SUBCORES * SC_GATHER_ROWS * 2
    blocks_granule = sum(MOE_DISPATCH_SPLIT) * max(sc_rows // MOE_ROWS, 1)
    assert (blocks_granule // sum(MOE_DISPATCH_SPLIT) * MOE_ROWS) % sc_rows == 0
    n_blocks = -(-((n_asg + N_EXPERTS * (MOE_ROWS - 1)) // MOE_ROWS + 1) // blocks_granule) * blocks_granule
    assert n_asg % sc_rows == 0
    block_start = jnp.arange(n_blocks, dtype=jnp.int32) * MOE_ROWS
    block_e = jnp.minimum(jnp.sum((pend[None, :] <= block_start[:, None]).astype(jnp.int32), axis=1), N_EXPERTS - 1)
    nb_used = (pend[-1] // MOE_ROWS).astype(jnp.int32).reshape(1)
    blk = jnp.arange(n_blocks, dtype=jnp.int32)
    seg_first = jnp.concatenate([jnp.ones((1,), bool), block_e[1:] != block_e[:-1]])
    seg_idx = jnp.cumsum(seg_first.astype(jnp.int32)) - 1
    later_first = lax.cummin(jnp.where(seg_first, blk, n_blocks), reverse=True)
    next_start = jnp.concatenate([later_first[1:], jnp.full((1,), n_blocks, jnp.int32)])
    pos2d = _assign_slots(idx, rank, pstart.astype(F32).reshape(N_EXPERTS, 1), t=t, tt=tt)[:TOP_K]
    pos = pos2d.reshape(-1)
    tok_of_asg = np.tile(np.arange(t, dtype=np.int32), TOP_K)
    n_pad = n_blocks * MOE_ROWS - n_asg
    seg_pad_end = jnp.cumsum(padded - counts)
    j = jnp.arange(n_pad, dtype=jnp.int32)
    pad_e = jnp.sum((seg_pad_end[None, :] <= j[:, None]).astype(jnp.int32), axis=1)
    seg_base = pstart + counts - (seg_pad_end - (padded - counts))
    in_seg = j + jnp.sum(jnp.where(pad_e[:, None] == jnp.arange(N_EXPERTS)[None, :], seg_base[None, :], 0), axis=1)
    pad_slot = jnp.where(pad_e < N_EXPERTS, in_seg, pend[-1] + j - seg_pad_end[-1])
    _, slot_tok = lax.sort((jnp.concatenate([pos, pad_slot]), jnp.concatenate([tok_of_asg, pad_slot % t])),
                           num_keys=1)
    unit = n_blocks // sum(MOE_DISPATCH_SPLIT)
    starts = [unit * sum(MOE_DISPATCH_SPLIT[:i]) for i in range(len(MOE_DISPATCH_SPLIT) + 1)]
    xs = [_sc_gather_rows(tok, slot_tok[lo * MOE_ROWS:hi * MOE_ROWS]) for lo, hi in zip(starts[:-1], starts[1:])]
    y = None
    for part, lo in zip(xs, starts[:-1]):
        y = _experts(part, block_e, nb_used, seg_idx, next_start, wg, wu, wd, layer, y, block_off=lo,
                     n_blocks=n_blocks)
    n_cparts = MOE_COMBINE_PARTS if (t // MOE_COMBINE_PARTS * TOP_K) % sc_rows == 0 else 1
    t_part = t // n_cparts
    y3 = [_sc_gather_rows(y, pos2d[:, i * t_part:(i + 1) * t_part].reshape(-1)).reshape(TOP_K, t_part, half)
          for i in range(n_cparts)]
    outs = None
    gw_t = gw.T
    for i in range(n_cparts):
        outs = _combine_ln(y3[i], gw_t, tok, x, sg, su, sd, mod, ln_g, ln_b, mod_next, outs, t=t,
                           row_off=i * t_part, tm=tm, alpha=alpha, mod_row=mod_row)
    return outs


def _rope64(r, c_ref, sa_ref, sb_ref):
    return r * c_ref[...] + pltpu.roll(r, LANES - QK_ROPE // 2, 1) * sa_ref[...] + pltpu.roll(r, QK_ROPE // 2, 1) * sb_ref[...]


def _mla_q_kernel(d_ref, gain_ref, w_ref, c_ref, sa_ref, sb_ref, q_ref, *, scale):
    n = _rms(d_ref[...], gain_ref[...]).astype(BF16)
    q = jnp.dot(n, w_ref[...], preferred_element_type=F32)
    for h in range(MLA_HEADS):
        lo = h * MLA_DK_PAD
        q_ref[:, lo:lo + QK_NOPE] = (q[:, lo:lo + QK_NOPE] * scale).astype(q_ref.dtype)
        r = _rope64(q[:, lo + QK_NOPE:lo + MLA_DK_PAD], c_ref, sa_ref, sb_ref)
        q_ref[:, lo + QK_NOPE:lo + MLA_DK_PAD] = (r * scale).astype(q_ref.dtype)


def _mla_kv_kernel(ckv_ref, kr_ref, gain_ref, wk_ref, wv_ref, c_ref, sa_ref, sb_ref, k_ref, v_ref):
    n = _rms(ckv_ref[...], gain_ref[...]).astype(BF16)
    kn = jnp.dot(n, wk_ref[...], preferred_element_type=F32)
    v_ref[...] = jnp.dot(n, wv_ref[...], preferred_element_type=F32).astype(v_ref.dtype)
    kr = _rope64(kr_ref[...], c_ref, sa_ref, sb_ref).astype(k_ref.dtype)
    for h in range(MLA_HEADS):
        lo = h * MLA_DK_PAD
        k_ref[:, lo:lo + QK_NOPE] = kn[:, h * QK_NOPE:(h + 1) * QK_NOPE].astype(k_ref.dtype)
        k_ref[:, lo + QK_NOPE:lo + MLA_DK_PAD] = kr


def _axial_angles(n_tok, rot_dim):
    rows = n_tok // GRID_W
    n_freq = rot_dim // 4
    inv = (ROPE_THETA ** (-np.arange(n_freq, dtype=np.float32) / n_freq)).astype(np.float32)
    row = np.repeat(np.arange(rows, dtype=np.float32), GRID_W)
    col = np.tile(np.arange(GRID_W, dtype=np.float32), rows)
    return np.concatenate([row[:, None] * inv, col[:, None] * inv], axis=-1)


def _rope_tables_128(n_tok, ident_rows):
    ang = _axial_angles(n_tok, HEAD_DIM)
    cos, sin = np.cos(ang), np.sin(ang)
    c = np.concatenate([cos, cos], axis=-1)
    s = np.concatenate([-sin, sin], axis=-1)
    c = np.concatenate([c, np.ones((ident_rows, HEAD_DIM), np.float32)], axis=0)
    s = np.concatenate([s, np.zeros((ident_rows, HEAD_DIM), np.float32)], axis=0)
    return c.astype(np.float32), s.astype(np.float32)


def _rope_tables_64(n_tok, ident_rows):
    ang = _axial_angles(n_tok, QK_ROPE)
    cos, sin = np.cos(ang), np.sin(ang)
    half = QK_ROPE // 2
    z = np.zeros((n_tok, LANES - QK_ROPE), np.float32)
    zh = np.zeros((n_tok, half), np.float32)
    c = np.concatenate([cos, cos, z], axis=-1)
    sa = np.concatenate([-sin, zh, z], axis=-1)
    sb = np.concatenate([zh, sin, z], axis=-1)
    ci = np.concatenate([np.ones((ident_rows, QK_ROPE), np.float32),
                         np.zeros((ident_rows, LANES - QK_ROPE), np.float32)], axis=-1)
    zi = np.zeros((ident_rows, LANES), np.float32)
    tables = np.concatenate([c, ci], 0), np.concatenate([sa, zi], 0), np.concatenate([sb, zi], 0)
    return tuple(tab.astype(np.float32) for tab in tables)


def kernel(x, c, ctx, c_ctx, w_ada, b_ada, ln_g, ln_b, a_w_in, a_conv_w, a_q_gain, a_k_gain, a_w_out, m_w_down, m_q_gain, m_kv_gain, m_w_uq, m_w_ukv, m_w_out, router_w, router_b, e_w_gate, e_w_up, e_w_down, s_w_gate, s_w_up, s_w_down):
    batch, seq, d = x.shape
    ctx_len = ctx.shape[1]
    depth = w_ada.shape[0]
    assert depth == 2, "one conv+GQA layer followed by one MLA layer"
    alpha = (2 * depth) ** 0.25
    t_lat = batch * seq
    t_ctx = batch * ctx_len
    t_all = t_lat + t_ctx
    tr = ROW_TILE
    assert seq % tr == 0 and ctx_len % tr == 0 and seq % GRID_W == 0
    lat_tiles = t_lat // tr
    lat_seq_tiles = seq // tr
    ctx_seq_tiles = ctx_len // tr
    lk = ctx_len + seq

    def mod_row(r):
        return jnp.minimum(r // seq, batch)

    def kv_block(i):
        is_lat = i < lat_tiles
        cidx = i - lat_tiles
        b = jnp.where(is_lat, i // lat_seq_tiles, cidx // ctx_seq_tiles)
        rb = jnp.where(is_lat, ctx_seq_tiles + i % lat_seq_tiles, cidx % ctx_seq_tiles)
        return b, rb

    def pos_block(i):
        return jnp.where(i < lat_tiles, i % lat_seq_tiles, lat_seq_tiles)

    rows = -(-(batch + 1) // SUBLANES) * SUBLANES
    cond = jnp.concatenate([c, c_ctx[None, :], jnp.zeros((rows - batch - 1, d), F32)], axis=0)
    mod = _ada_table(cond, w_ada, b_ada).reshape(depth, rows, 1, 6 * d)

    x_pair = (x.reshape(t_lat, d), ctx.reshape(t_ctx, d))
    tm_out = OUTPROJ_ROW_TILE if (seq % OUTPROJ_ROW_TILE == 0 and t_ctx % OUTPROJ_ROW_TILE == 0) else tr

    tm_in = W_IN_ROW_TILE if (t_ctx % W_IN_ROW_TILE == 0 and seq % W_IN_ROW_TILE == 0) else tr
    proj = _mod_matmul(x_pair, mod[0], a_w_in[0].astype(BF16), BF16, tm_in, W_IN_COL_TILE, mod_row)

    cos128, sin128 = _rope_tables_128(seq, tr)
    d_q = ATT_HEADS * HEAD_DIM
    d_kv = ATT_KV_HEADS * HEAD_DIM
    qkv_w = d_q + 2 * d_kv
    qkv_blk = 3 * CONV_DIM // qkv_w
    assert qkv_blk * qkv_w == 3 * CONV_DIM
    q0, k0, v0 = pl.pallas_call(
        functools.partial(_qkprep_kernel, scale=1.0 / math.sqrt(HEAD_DIM)),
        grid=(t_all // tr,),
        in_specs=[
            pl.BlockSpec((tr, qkv_w), lambda i: (i, qkv_blk)),
            pl.BlockSpec((tr, HEAD_DIM), lambda i: (pos_block(i), 0)),
            pl.BlockSpec((tr, HEAD_DIM), lambda i: (pos_block(i), 0)),
            _const_spec((1, HEAD_DIM)), _const_spec((1, HEAD_DIM)),
        ],
        out_specs=[
            pl.BlockSpec((tr, d_q), lambda i: (i, 0)),
            pl.BlockSpec((None, tr, d_kv), lambda i: (*kv_block(i), 0)),
            pl.BlockSpec((None, tr, d_kv), lambda i: (*kv_block(i), 0)),
        ],
        out_shape=[
            jax.ShapeDtypeStruct((t_all, d_q), BF16),
            jax.ShapeDtypeStruct((batch, lk, d_kv), BF16),
            jax.ShapeDtypeStruct((batch, lk, d_kv), BF16),
        ],
        compiler_params=_params(("parallel",)),
        name="qk_prep",
    )(proj, cos128, sin128, a_q_gain[0].reshape(1, HEAD_DIM), a_k_gain[0].reshape(1, HEAD_DIM))

    grp = ATT_HEADS // ATT_KV_HEADS
    att_lat = _attention(q0, k0, v0, batch=batch, sq=seq, lk=lk, n_kv=ATT_KV_HEADS, group=grp, dk=HEAD_DIM,
                         dv=HEAD_DIM, tq=GQA_Q_TILE if seq % GQA_Q_TILE == 0 else tr, rows=ATTN_CHAIN_ROWS, q_row_off=0)
    att_ctx = _attention(q0, k0, v0, batch=batch, sq=ctx_len, lk=ctx_len, n_kv=ATT_KV_HEADS, group=grp,
                         dk=HEAD_DIM, dv=HEAD_DIM, tq=tr, rows=ATTN_CHAIN_ROWS, q_row_off=t_lat)

    conv0 = _conv_gate(proj, a_conv_w[0], t=t_all, tm=tr, tc=CONV_COL_TILE, lat_tiles=lat_tiles,
                       lat_seq_tiles=lat_seq_tiles, ctx_seq_tiles=ctx_seq_tiles)

    w_out0 = a_w_out[0].astype(BF16)
    x1, tok0 = _outproj_ln([conv0, (att_lat, att_ctx)], [w_out0[:CONV_DIM], w_out0[CONV_DIM:]], x_pair, mod[0],
                           ln_g[0, 0].reshape(1, d), ln_b[0, 0].reshape(1, d), t=t_all, tm=tm_out, alpha=alpha,
                           mod_row=mod_row)

    x2, u1 = _moe(tok0, x1, t_all, 0, router_w[0], router_b[0], e_w_gate, e_w_up, e_w_down,
                  s_w_gate[0].astype(BF16), s_w_up[0].astype(BF16), s_w_down[0].astype(BF16), mod[0],
                  ln_g[0, 1].reshape(1, d), ln_b[0, 1].reshape(1, d), mod[1], alpha=alpha,
                  mod_row=mod_row, tm=COMBINE_ROW_TILE)

    n_down = Q_LORA + KV_LORA + QK_ROPE
    n_down_pad = -(-n_down // LANES) * LANES
    w_down = jnp.pad(m_w_down[0], ((0, 0), (0, n_down_pad - n_down))).astype(BF16)
    down = _matmul(u1, w_down, F32, DOWN_ROW_TILE, n_down_pad)

    dqk = QK_NOPE + QK_ROPE
    w_uq = m_w_uq[0].reshape(Q_LORA, MLA_HEADS, dqk)
    w_uq = jnp.pad(w_uq, ((0, 0), (0, 0), (0, MLA_DK_PAD - dqk))).reshape(Q_LORA, MLA_HEADS * MLA_DK_PAD).astype(BF16)
    w_ukv = m_w_ukv[0].reshape(KV_LORA, MLA_HEADS, QK_NOPE + V_DIM)
    w_uk = w_ukv[:, :, :QK_NOPE].reshape(KV_LORA, MLA_HEADS * QK_NOPE).astype(BF16)
    w_uv = w_ukv[:, :, QK_NOPE:].reshape(KV_LORA, MLA_HEADS * V_DIM).astype(BF16)

    c64, sa64, sb64 = _rope_tables_64(seq, tr)
    rope_specs = [pl.BlockSpec((tr, LANES), lambda i: (pos_block(i), 0))] * 3
    q1 = pl.pallas_call(
        functools.partial(_mla_q_kernel, scale=1.0 / math.sqrt(dqk)),
        grid=(lat_tiles,),
        in_specs=[
            pl.BlockSpec((tr, Q_LORA), lambda i: (i, 0)),
            _const_spec((1, Q_LORA)),
            _const_spec(w_uq.shape),
        ] + rope_specs,
        out_specs=pl.BlockSpec((tr, MLA_HEADS * MLA_DK_PAD), lambda i: (i, 0)),
        out_shape=jax.ShapeDtypeStruct((t_lat, MLA_HEADS * MLA_DK_PAD), BF16),
        compiler_params=_params(("parallel",)),
        name="mla_q",
    )(down, m_q_gain[0].reshape(1, Q_LORA), w_uq, c64, sa64, sb64)

    assert KV_LORA == Q_LORA and (Q_LORA + KV_LORA) % LANES == 0
    k1, v1 = pl.pallas_call(
        _mla_kv_kernel,
        grid=(t_all // tr,),
        in_specs=[
            pl.BlockSpec((tr, KV_LORA), lambda i: (i, 1)),
            pl.BlockSpec((tr, LANES), lambda i: (i, (Q_LORA + KV_LORA) // LANES)),
            _const_spec((1, KV_LORA)),
            _const_spec(w_uk.shape), _const_spec(w_uv.shape),
        ] + rope_specs,
        out_specs=[
            pl.BlockSpec((None, tr, MLA_HEADS * MLA_DK_PAD), lambda i: (*kv_block(i), 0)),
            pl.BlockSpec((None, tr, MLA_HEADS * V_DIM), lambda i: (*kv_block(i), 0)),
        ],
        out_shape=[
            jax.ShapeDtypeStruct((batch, lk, MLA_HEADS * MLA_DK_PAD), BF16),
            jax.ShapeDtypeStruct((batch, lk, MLA_HEADS * V_DIM), BF16),
        ],
        compiler_params=_params(("parallel",)),
        name="mla_kv",
    )(down, down, m_kv_gain[0].reshape(1, KV_LORA), w_uk, w_uv, c64, sa64, sb64)

    att1 = _attention(q1, k1, v1, batch=batch, sq=seq, lk=lk, n_kv=MLA_HEADS, group=1, dk=MLA_DK_PAD, dv=V_DIM,
                      tq=next(q for q in MLA_Q_TILES + (tr,) if seq % q == 0), rows=ATTN_CHAIN_ROWS, q_row_off=0)

    x3, tok1 = _outproj_ln([att1], [m_w_out[0].astype(BF16)], x2, mod[1], ln_g[1, 0].reshape(1, d),
                           ln_b[1, 0].reshape(1, d), t=t_lat, tm=tm_out, alpha=alpha, mod_row=mod_row)

    (x4,) = _moe(tok1, x3, t_lat, 1, router_w[1], router_b[1], e_w_gate, e_w_up, e_w_down,
                 s_w_gate[1].astype(BF16), s_w_up[1].astype(BF16), s_w_down[1].astype(BF16), mod[1],
                 ln_g[1, 1].reshape(1, d), ln_b[1, 1].reshape(1, d), None, alpha=alpha,
                 mod_row=mod_row, tm=COMBINE_ROW_TILE)
    return x4.reshape(batch, seq, d)
```

```python
import functools
import math

import jax
import jax.numpy as jnp
import numpy as np
from jax import lax
from jax.experimental import pallas as pl
from jax.experimental.pallas import tpu as pltpu
from jax.experimental.pallas import tpu_sc as plsc

F32 = jnp.float32
BF16 = jnp.bfloat16

GRID_W = 64
CONV_DIM = 1024
ATT_HEADS = 8
ATT_KV_HEADS = 2
HEAD_DIM = 128
MLA_HEADS = 16
Q_LORA = 512
KV_LORA = 512
QK_NOPE = 128
QK_ROPE = 64
V_DIM = 128
N_EXPERTS = 64
TOP_K = 6
N_GROUPS = 8
TOPK_GROUPS = 4
ROUTED_SCALE = 2.5
ROPE_THETA = 10000.0
LN_EPS = 1e-5
RMS_EPS = 1e-6

V7X_VMEM_LIMIT_BYTES = 56 * 1024 * 1024
LANES = 128
SUBLANES = 8
MOE_ROWS = 512
MOE_DISPATCH_SPLIT = (1, 3, 2, 2)
MOE_COMBINE_PARTS = 2
V7X_SC_CORES = 2
V7X_SC_SUBCORES = 16
SC_GATHER_ROWS = 16
SC_GATHER_BUFFERS = 4
MLA_DK_PAD = 256

ROW_TILE = 256
ADA_COL_TILE = 1024
W_IN_ROW_TILE = 1024
W_IN_COL_TILE = 768
DOWN_ROW_TILE = 512
CONV_COL_TILE = 1024
OUTPROJ_ROW_TILE = 512
ROUTER_TILE = 512
ATTN_CHAIN_ROWS = 256
GQA_Q_TILE = 512
MLA_Q_TILES = (2048, 1024, 512)
COMBINE_ROW_TILE = 256


def _params(sem):
    return pltpu.CompilerParams(dimension_semantics=sem, vmem_limit_bytes=V7X_VMEM_LIMIT_BYTES)


def _const_spec(shape):
    nd = len(shape)
    return pl.BlockSpec(shape, lambda *_: (0,) * nd)


def _ada_kernel(s_ref, w_ref, b_ref, o_ref):
    s = s_ref[...]
    s = s * (1.0 / (1.0 + jnp.exp(-s)))
    o_ref[...] = jnp.dot(s.astype(BF16), w_ref[...].astype(BF16), preferred_element_type=F32) + b_ref[...]


def _ada_table(cond, w_ada, b_ada):
    depth, d, n = w_ada.shape
    r = cond.shape[0]
    tn = ADA_COL_TILE
    return pl.pallas_call(
        _ada_kernel,
        grid=(depth, n // tn),
        in_specs=[
            pl.BlockSpec((r, d), lambda l, j: (0, 0)),
            pl.BlockSpec((None, d, tn), lambda l, j: (l, 0, j)),
            pl.BlockSpec((None, 1, tn), lambda l, j: (l, 0, j)),
        ],
        out_specs=pl.BlockSpec((None, r, tn), lambda l, j: (l, 0, j)),
        out_shape=jax.ShapeDtypeStruct((depth, r, n), F32),
        compiler_params=_params(("parallel", "parallel")),
        name="ada_table",
    )(cond, w_ada, b_ada.reshape(depth, 1, n))


def _mod_spec(d, chunk, mod_row, tm):
    return pl.BlockSpec((None, 1, d), lambda i: (mod_row(i * tm), 0, chunk))


def _mm_kernel(a_ref, w_ref, o_ref):
    o_ref[...] = jnp.dot(a_ref[...], w_ref[...], preferred_element_type=F32).astype(o_ref.dtype)


def _matmul(a, w, out_dtype, tm, tn):
    m, k = a.shape
    n = w.shape[1]
    return pl.pallas_call(
        _mm_kernel,
        grid=(m // tm, n // tn),
        in_specs=[
            pl.BlockSpec((tm, k), lambda i, j: (i, 0)),
            pl.BlockSpec((k, tn), lambda i, j: (0, j)),
        ],
        out_specs=pl.BlockSpec((tm, tn), lambda i, j: (i, j)),
        out_shape=jax.ShapeDtypeStruct((m, n), out_dtype),
        compiler_params=_params(("parallel", "parallel")),
        name="matmul",
    )(a, w)


def _pair_specs(pair, tm):
    lat, ctx = pair
    lat_tiles = lat.shape[0] // tm
    assert lat_tiles * tm == lat.shape[0] and ctx.shape[0] % tm == 0 and lat.shape[1] == ctx.shape[1]
    width = lat.shape[1]
    return [pl.BlockSpec((tm, width), lambda i, *_: (jnp.minimum(i, lat_tiles - 1), 0)),
            pl.BlockSpec((tm, width), lambda i, *_: (jnp.maximum(i - lat_tiles, 0), 0))]


def _pair_tile(lat_ref, ctx_ref, lat_tiles):
    return jnp.where(pl.program_id(0) < lat_tiles, lat_ref[...], ctx_ref[...])


def _mod_mm_kernel(xl_ref, xc_ref, sc_ref, sh_ref, w_ref, o_ref, u_ref, *, lat_tiles):
    @pl.when(pl.program_id(1) == 0)
    def _():
        x = _pair_tile(xl_ref, xc_ref, lat_tiles)
        u_ref[...] = (x * (1.0 + sc_ref[...]) + sh_ref[...]).astype(u_ref.dtype)

    o_ref[...] = jnp.dot(u_ref[...], w_ref[...], preferred_element_type=F32).astype(o_ref.dtype)


def _mod_matmul(x_pair, mod, w, out_dtype, tm, tn, mod_row):
    m = x_pair[0].shape[0] + x_pair[1].shape[0]
    k = x_pair[0].shape[1]
    n = w.shape[1]

    def mod_spec(chunk):
        return pl.BlockSpec((None, 1, k), lambda i, j: (mod_row(i * tm), 0, chunk))

    return pl.pallas_call(
        functools.partial(_mod_mm_kernel, lat_tiles=x_pair[0].shape[0] // tm),
        grid=(m // tm, n // tn),
        in_specs=_pair_specs(x_pair, tm) + [
            mod_spec(1), mod_spec(0),
            pl.BlockSpec((k, tn), lambda i, j: (0, j)),
        ],
        out_specs=pl.BlockSpec((tm, tn), lambda i, j: (i, j)),
        out_shape=jax.ShapeDtypeStruct((m, n), out_dtype),
        scratch_shapes=[pltpu.VMEM((tm, k), w.dtype)],
        compiler_params=_params(("parallel", "arbitrary")),
        name="mod_matmul",
    )(*x_pair, mod, mod, w)


def _rms(t, gain):
    return t * lax.rsqrt(jnp.mean(t * t, axis=-1, keepdims=True) + RMS_EPS) * gain


def _qkprep_kernel(p_ref, cos_ref, sin_ref, qg_ref, kg_ref, q_ref, k_ref, v_ref, *, scale):
    cos = cos_ref[...]
    sin = sin_ref[...]

    def norm_rope(t, gain):
        y = _rms(t.astype(F32), gain)
        return y * cos + pltpu.roll(y, HEAD_DIM // 2, 1) * sin

    for h in range(ATT_HEADS):
        sl = slice(h * HEAD_DIM, (h + 1) * HEAD_DIM)
        q_ref[:, sl] = (norm_rope(p_ref[:, sl], qg_ref[...]) * scale).astype(q_ref.dtype)
    k0 = ATT_HEADS * HEAD_DIM
    for h in range(ATT_KV_HEADS):
        sl = slice(h * HEAD_DIM, (h + 1) * HEAD_DIM)
        k_ref[:, sl] = norm_rope(p_ref[:, k0 + h * HEAD_DIM:k0 + (h + 1) * HEAD_DIM], kg_ref[...]).astype(k_ref.dtype)
    v0 = k0 + ATT_KV_HEADS * HEAD_DIM
    v_ref[...] = p_ref[:, v0:v0 + ATT_KV_HEADS * HEAD_DIM].astype(v_ref.dtype)


def _attn_kernel(q_ref, k_ref, v_ref, o_ref, *, group, tq, rows, dk, dv):
    k = k_ref[...]
    v = v_ref[...]
    for h in range(group):
        for r in range(0, tq, rows):
            q = q_ref[r:r + rows, h * dk:(h + 1) * dk]
            s = lax.dot_general(q, k, (((1,), (1,)), ((), ())), preferred_element_type=F32)
            m = jnp.max(s, axis=-1, keepdims=True)
            p = jnp.exp(s - m)
            l = jnp.sum(p, axis=-1, keepdims=True)
            o = jnp.dot(p.astype(v.dtype), v, preferred_element_type=F32)
            o_ref[r:r + rows, h * dv:(h + 1) * dv] = (o / l).astype(o_ref.dtype)


def _attention(q, k, v, *, batch, sq, lk, n_kv, group, dk, dv, tq, rows, q_row_off):
    nq = sq // tq
    off = q_row_off // tq
    assert tq % rows == 0 and q_row_off % tq == 0 and sq % tq == 0
    return pl.pallas_call(
        functools.partial(_attn_kernel, group=group, tq=tq, rows=rows, dk=dk, dv=dv),
        grid=(batch, n_kv, nq),
        in_specs=[
            pl.BlockSpec((tq, group * dk), lambda b, g, i: (off + b * nq + i, g)),
            pl.BlockSpec((None, lk, dk), lambda b, g, i: (b, 0, g)),
            pl.BlockSpec((None, lk, dv), lambda b, g, i: (b, 0, g)),
        ],
        out_specs=pl.BlockSpec((tq, group * dv), lambda b, g, i: (b * nq + i, g)),
        out_shape=jax.ShapeDtypeStruct((batch * sq, n_kv * group * dv), BF16),
        compiler_params=_params(("parallel", "parallel", "parallel")),
        name="attention",
    )(q, k, v)


def _conv_kernel(gb_ref, gc_ref, hv_ref, gcp_ref, hvp_ref, gcn_ref, hvn_ref, w_ref, o_ref, *,
                 tm, lat_tiles, lat_seq_tiles, ctx_seq_tiles):
    i = pl.program_id(0)
    is_lat = i < lat_tiles
    pos = jnp.where(is_lat, i % lat_seq_tiles, (i - lat_tiles) % ctx_seq_tiles)
    seq_tiles = jnp.where(is_lat, lat_seq_tiles, ctx_seq_tiles)
    not_first = (pos != 0).astype(F32)
    not_last = (pos != seq_tiles - 1).astype(F32)
    p = gc_ref[...].astype(F32) * hv_ref[...].astype(F32)
    halo_prev = gcp_ref[SUBLANES - 1:SUBLANES, :].astype(F32) * hvp_ref[SUBLANES - 1:SUBLANES, :].astype(F32) * not_first
    halo_next = gcn_ref[0:1, :].astype(F32) * hvn_ref[0:1, :].astype(F32) * not_last
    row = lax.broadcasted_iota(jnp.int32, p.shape, 0)
    prev = jnp.where(row == 0, halo_prev, pltpu.roll(p, 1, 0))
    nxt = jnp.where(row == tm - 1, halo_next, pltpu.roll(p, tm - 1, 0))
    w = w_ref[...]
    conv = w[0:1, :] * prev + w[1:2, :] * p + w[2:3, :] * nxt
    o_ref[...] = (gb_ref[...].astype(F32) * conv).astype(o_ref.dtype)


def _conv_gate(p, conv_w, *, t, tm, tc, lat_tiles, lat_seq_tiles, ctx_seq_tiles):
    nct = CONV_DIM // tc
    hb = tm // SUBLANES
    n_halo = t // SUBLANES

    def cur(part):
        return pl.BlockSpec((tm, tc), lambda i, j: (i, part * nct + j))

    def prev(part):
        return pl.BlockSpec((SUBLANES, tc), lambda i, j: (jnp.maximum(i * hb - 1, 0), part * nct + j))

    def nxt(part):
        return pl.BlockSpec((SUBLANES, tc), lambda i, j: (jnp.minimum((i + 1) * hb, n_halo - 1), part * nct + j))

    return pl.pallas_call(
        functools.partial(_conv_kernel, tm=tm, lat_tiles=lat_tiles, lat_seq_tiles=lat_seq_tiles,
                          ctx_seq_tiles=ctx_seq_tiles),
        grid=(t // tm, nct),
        in_specs=[cur(0), cur(1), cur(2), prev(1), prev(2), nxt(1), nxt(2),
                  pl.BlockSpec((3, tc), lambda i, j: (0, j))],
        out_specs=pl.BlockSpec((tm, tc), lambda i, j: (i, j)),
        out_shape=jax.ShapeDtypeStruct((t, CONV_DIM), BF16),
        compiler_params=_params(("parallel", "parallel")),
        name="conv_gate",
    )(p, p, p, p, p, p, p, conv_w)


def _layer_norm(z, g, b):
    mu = jnp.mean(z, axis=-1, keepdims=True)
    zc = z - mu
    var = jnp.mean(zc * zc, axis=-1, keepdims=True)
    return zc * lax.rsqrt(var + LN_EPS) * g + b


def _pack_bf16_pairs(x):
    half = x.shape[1] // 2
    lo = lax.bitcast_convert_type(x[:, :half].astype(BF16).astype(F32), jnp.uint32) >> 16
    hi = lax.bitcast_convert_type(x[:, half:].astype(BF16).astype(F32), jnp.uint32) & jnp.uint32(0xFFFF0000)
    return lax.bitcast_convert_type(lo | hi, jnp.int32)


def _unpack_bf16_pairs(w):
    u = lax.bitcast_convert_type(w, jnp.uint32)
    lo = lax.bitcast_convert_type(u << 16, F32).astype(BF16)
    hi = lax.bitcast_convert_type(u & jnp.uint32(0xFFFF0000), F32).astype(BF16)
    return lo, hi


def _dot_halves(lo, hi, w_ref):
    half = lo.shape[1]
    return (jnp.dot(lo, w_ref[:half, :], preferred_element_type=F32)
            + jnp.dot(hi, w_ref[half:, :], preferred_element_type=F32))


def _outproj_ln_kernel(*refs, widths, lat_tiles, alpha):
    refs = list(refs)

    def take(width):
        got = [refs.pop(0) for _ in range(width)]
        return got[0][...] if width == 1 else _pair_tile(got[0], got[1], lat_tiles)

    acts = [take(w) for w in widths[:-1]]
    w_refs = [refs.pop(0) for _ in acts]
    x = take(widths[-1])
    gate_ref, lng_ref, lnb_ref, sc_ref, sh_ref, xo_ref, tok_ref = refs
    y = jnp.dot(acts[0], w_refs[0][...], preferred_element_type=F32)
    for a, w_ref in zip(acts[1:], w_refs[1:]):
        y = y + jnp.dot(a, w_ref[...], preferred_element_type=F32)
    xn = _layer_norm(alpha * x + gate_ref[...] * y, lng_ref[...], lnb_ref[...])
    xo_ref[...] = xn
    tok_ref[...] = _pack_bf16_pairs(xn * (1.0 + sc_ref[...]) + sh_ref[...])


def _outproj_ln(a_list, w_list, x, mod, ln_g, ln_b, *, t, tm, alpha, mod_row):
    d = w_list[0].shape[1]
    operands, in_specs, widths, lat_tiles = [], [], [], 0

    def add_rows(src):
        nonlocal lat_tiles
        if isinstance(src, tuple):
            in_specs.extend(_pair_specs(src, tm))
            operands.extend(src)
            widths.append(2)
            lat_tiles = src[0].shape[0] // tm
        else:
            in_specs.append(pl.BlockSpec((tm, src.shape[1]), lambda i: (i, 0)))
            operands.append(src)
            widths.append(1)

    for a in a_list:
        add_rows(a)
    in_specs += [_const_spec(w.shape) for w in w_list]
    operands += list(w_list)
    add_rows(x)
    in_specs += [
        _mod_spec(d, 2, mod_row, tm),
        _const_spec((1, d)), _const_spec((1, d)),
        _mod_spec(d, 4, mod_row, tm),
        _mod_spec(d, 3, mod_row, tm),
    ]
    return pl.pallas_call(
        functools.partial(_outproj_ln_kernel, widths=tuple(widths), lat_tiles=lat_tiles, alpha=alpha),
        grid=(t // tm,),
        in_specs=in_specs,
        out_specs=[pl.BlockSpec((tm, d), lambda i: (i, 0)), pl.BlockSpec((tm, d // 2), lambda i: (i, 0))],
        out_shape=[jax.ShapeDtypeStruct((t, d), F32), jax.ShapeDtypeStruct((t, d // 2), jnp.int32)],
        compiler_params=_params(("parallel",)),
        name="outproj_ln",
    )(*operands, mod, ln_g, ln_b, mod, mod)


def _router_kernel(t_ref, rw_ref, rb_ref, tri_ref, idx_ref, gw_ref, rank_ref, cnt_ref):
    @pl.when(pl.program_id(0) == 0)
    def _():
        cnt_ref[...] = jnp.zeros_like(cnt_ref)

    lo, hi = _unpack_bf16_pairs(t_ref[...])
    half = lo.shape[1]
    nt = (((1,), (1,)), ((), ()))
    logits = (lax.dot_general(rw_ref[:, :half], lo, nt, preferred_element_type=F32)
              + lax.dot_general(rw_ref[:, half:], hi, nt, preferred_element_type=F32))
    scores = 1.0 / (1.0 + jnp.exp(-logits))
    sel = scores + rb_ref[...]
    gsz = N_EXPERTS // N_GROUPS
    neg = -jnp.inf
    sub = lax.broadcasted_iota(jnp.int32, (gsz, sel.shape[1]), 0)
    slabs = [sel[g * gsz:(g + 1) * gsz, :] for g in range(N_GROUPS)]
    gscore = []
    for s in slabs:
        m1 = jnp.max(s, axis=0, keepdims=True)
        a1 = jnp.min(jnp.where(s == m1, sub, gsz), axis=0, keepdims=True)
        m2 = jnp.max(jnp.where(sub == a1, neg, s), axis=0, keepdims=True)
        gscore.append(m1 + m2)
    masked = []
    for g in range(N_GROUPS):
        ahead = jnp.zeros(gscore[g].shape, jnp.int32)
        for h in range(N_GROUPS):
            if h == g:
                continue
            beats = gscore[h] >= gscore[g] if h < g else gscore[h] > gscore[g]
            ahead = ahead + beats.astype(jnp.int32)
        masked.append(jnp.where(ahead < TOPK_GROUPS, slabs[g], neg))
    cur = jnp.concatenate(masked, axis=0)
    eio = lax.broadcasted_iota(jnp.int32, cur.shape, 0)
    picks, weights = [], []
    for _ in range(TOP_K):
        m = jnp.max(cur, axis=0, keepdims=True)
        a = jnp.min(jnp.where(cur == m, eio, N_EXPERTS), axis=0, keepdims=True)
        hit = eio == a
        picks.append(a)
        weights.append(jnp.sum(jnp.where(hit, scores, 0.0), axis=0, keepdims=True))
        cur = jnp.where(hit, neg, cur)
    total = weights[0]
    for w in weights[1:]:
        total = total + w
    for k in range(TOP_K):
        idx_ref[k:k + 1, :] = picks[k]
        gw_ref[k:k + 1, :] = weights[k] / total * ROUTED_SCALE
    for k in range(TOP_K, SUBLANES):
        idx_ref[k:k + 1, :] = jnp.zeros_like(picks[0])
        gw_ref[k:k + 1, :] = jnp.zeros_like(weights[0])
        rank_ref[k:k + 1, :] = jnp.zeros_like(picks[0])
    base = cnt_ref[:, 0:1]
    for k in range(TOP_K):
        onehot = jnp.where(eio == picks[k], 1.0, 0.0)
        before = jnp.dot(onehot.astype(BF16), tri_ref[...], preferred_element_type=F32)
        rank_ref[k:k + 1, :] = jnp.sum(onehot * (before + base), axis=0, keepdims=True).astype(jnp.int32)
        base = base + jnp.sum(onehot, axis=1, keepdims=True)
    cnt_ref[...] = jnp.broadcast_to(base, cnt_ref.shape)


def _router(tok, rw_t, rb, *, t, tt):
    half = tok.shape[1]
    tri = jnp.asarray(np.arange(tt)[:, None] < np.arange(tt)[None, :], BF16)
    blk = pl.BlockSpec((SUBLANES, tt), lambda i: (0, i))
    return pl.pallas_call(
        _router_kernel,
        grid=(t // tt,),
        in_specs=[
            pl.BlockSpec((tt, half), lambda i: (i, 0)),
            _const_spec((N_EXPERTS, 2 * half)),
            _const_spec((N_EXPERTS, 1)),
            _const_spec((tt, tt)),
        ],
        out_specs=[blk, blk, blk, _const_spec((N_EXPERTS, LANES))],
        out_shape=[jax.ShapeDtypeStruct((SUBLANES, t), jnp.int32), jax.ShapeDtypeStruct((SUBLANES, t), F32),
                   jax.ShapeDtypeStruct((SUBLANES, t), jnp.int32), jax.ShapeDtypeStruct((N_EXPERTS, LANES), F32)],
        compiler_params=_params(("arbitrary",)),
        name="router",
    )(tok, rw_t, rb, tri)


def _slots_kernel(idx_ref, rank_ref, start_ref, pos_ref):
    start = start_ref[...]
    eio = lax.broadcasted_iota(jnp.int32, (N_EXPERTS, idx_ref.shape[1]), 0)
    for k in range(TOP_K):
        seg = jnp.sum(jnp.where(eio == idx_ref[k:k + 1, :], start, 0.0), axis=0, keepdims=True)
        pos_ref[k:k + 1, :] = rank_ref[k:k + 1, :] + seg.astype(jnp.int32)
    for k in range(TOP_K, SUBLANES):
        pos_ref[k:k + 1, :] = jnp.zeros((1, idx_ref.shape[1]), jnp.int32)


def _assign_slots(idx, rank, seg_start, *, t, tt):
    blk = pl.BlockSpec((SUBLANES, tt), lambda i: (0, i))
    return pl.pallas_call(
        _slots_kernel,
        grid=(t // tt,),
        in_specs=[blk, blk, _const_spec((N_EXPERTS, 1))],
        out_specs=blk,
        out_shape=jax.ShapeDtypeStruct((SUBLANES, t), jnp.int32),
        compiler_params=_params(("parallel",)),
        name="assign_slots",
    )(idx, rank, seg_start)


def _sc_gather_rows(table, idx):
    n = idx.shape[0]
    d = table.shape[1]
    n_workers = V7X_SC_CORES * V7X_SC_SUBCORES
    per_w = n // n_workers
    n_chunks = per_w // SC_GATHER_ROWS
    assert per_w * n_workers == n and n_chunks * SC_GATHER_ROWS == per_w
    n_buf = next(b for b in range(SC_GATHER_BUFFERS, 1, -1) if n_chunks % b == 0)
    mesh = plsc.VectorSubcoreMesh(core_axis_name="c", subcore_axis_name="s", num_cores=V7X_SC_CORES,
                                  num_subcores=V7X_SC_SUBCORES)

    @functools.partial(
        pl.kernel,
        out_type=jax.ShapeDtypeStruct((n, d), table.dtype),
        mesh=mesh,
        scratch_types=[
            pltpu.VMEM((per_w,), jnp.int32),
            pltpu.VMEM((n_buf, SC_GATHER_ROWS, d), table.dtype),
            pltpu.SemaphoreType.DMA((n_buf,)),
            pltpu.SemaphoreType.DMA((n_buf,)),
        ],
        name="sc_gather_rows",
    )
    def gather(table_hbm, idx_hbm, out_hbm, idx_v, rows_v, gsem, wsem):
        wid = lax.axis_index("s") * V7X_SC_CORES + lax.axis_index("c")
        base = wid * per_w
        pltpu.sync_copy(idx_hbm.at[pl.ds(base, per_w)], idx_v)

        def gather_copy(c, b):
            return pltpu.make_async_copy(table_hbm.at[idx_v.at[pl.ds(c * SC_GATHER_ROWS, SC_GATHER_ROWS)]],
                                         rows_v.at[b], gsem.at[b])

        def write_copy(c, b):
            return pltpu.make_async_copy(rows_v.at[b], out_hbm.at[pl.ds(base + c * SC_GATHER_ROWS, SC_GATHER_ROWS)],
                                         wsem.at[b])

        for b in range(n_buf - 1):
            gather_copy(b, b).start()

        @pl.loop(0, n_chunks, step=n_buf)
        def _(g):
            for b in range(n_buf):
                c = g + b
                prev = (b + n_buf - 1) % n_buf
                gather_copy(c, b).wait()
                write_copy(c, b).start()

                @pl.when(c >= 1)
                def _():
                    write_copy(c - 1, prev).wait()

                @pl.when(c + n_buf - 1 < n_chunks)
                def _():
                    gather_copy(c + n_buf - 1, prev).start()

        write_copy(n_chunks - 1, (n_chunks - 1) % n_buf).wait()

    return gather(table, idx)


def _experts_kernel(be_ref, nbu_ref, seg_ref, nstart_ref, x_ref, wg_hbm, wu_hbm, wd_hbm, *rest, block_off, n_call,
                    layer):
    y_ref, wgf, wuf, wdf, wgub, wdb, sem = rest[-7:]
    ff = wdb.shape[0]
    step = pl.program_id(0)
    b = block_off + step
    nbu = nbu_ref[0]
    end = jnp.minimum(nbu, block_off + n_call)

    def weight_copies(e, slot):
        return [pltpu.make_async_copy(hbm.at[layer, e], buf.at[slot], sem.at[slot, k])
                for k, (hbm, buf) in enumerate(((wg_hbm, wgf), (wu_hbm, wuf), (wd_hbm, wdf)))]

    @pl.when(b < nbu)
    def _():
        first = jnp.logical_or(step == 0, be_ref[b] != be_ref[jnp.maximum(b - 1, 0)])

        @pl.when(first)
        def _():
            slot = seg_ref[b] % 2

            @pl.when(step == 0)
            def _():
                for cp in weight_copies(be_ref[b], slot):
                    cp.start()

            for cp in weight_copies(be_ref[b], slot):
                cp.wait()
            wgub[:, :ff] = wgf[slot].astype(BF16)
            wgub[:, ff:] = wuf[slot].astype(BF16)
            wdb[...] = wdf[slot].astype(BF16)
            nxt = nstart_ref[b]

            @pl.when(nxt < end)
            def _():
                for cp in weight_copies(be_ref[jnp.minimum(nxt, be_ref.shape[0] - 1)], 1 - slot):
                    cp.start()

        lo, hi = _unpack_bf16_pairs(x_ref[...])
        hgu = _dot_halves(lo, hi, wgub)
        hg = hgu[:, :ff]
        hu = hgu[:, ff:]
        h = hg * (1.0 / (1.0 + jnp.exp(-hg))) * hu
        y_ref[...] = _pack_bf16_pairs(jnp.dot(h.astype(BF16), wdb[...], preferred_element_type=F32))

    @pl.when(b >= nbu)
    def _():
        y_ref[...] = jnp.zeros_like(y_ref)


def _experts(xs, block_e, nb_used, seg_idx, next_start, wg, wu, wd, layer, y_prev, *, block_off, n_blocks):
    half = xs.shape[1]
    d = 2 * half
    ff = wg.shape[3]
    n_call = xs.shape[0] // MOE_ROWS

    def used(b, nbu):
        return jnp.clip(jnp.minimum(block_off + b, nbu[0] - 1) - block_off, 0, n_call - 1)

    hbm = pl.BlockSpec(memory_space=pl.ANY)
    in_specs = [pl.BlockSpec((MOE_ROWS, half), lambda b, be, nbu, seg, nst: (used(b, nbu), 0)), hbm, hbm, hbm]
    args = [block_e, nb_used, seg_idx, next_start, xs, wg, wu, wd]
    aliases = {}
    if y_prev is not None:
        in_specs.append(hbm)
        aliases = {len(args): 0}
        args.append(y_prev)
    grid_spec = pltpu.PrefetchScalarGridSpec(
        num_scalar_prefetch=4,
        grid=(n_call,),
        in_specs=in_specs,
        out_specs=pl.BlockSpec((MOE_ROWS, half), lambda b, be, nbu, seg, nst: (block_off + b, 0)),
        scratch_shapes=[
            pltpu.VMEM((2, d, ff), F32),
            pltpu.VMEM((2, d, ff), F32),
            pltpu.VMEM((2, ff, d), F32),
            pltpu.VMEM((d, 2 * ff), BF16),
            pltpu.VMEM((ff, d), BF16),
            pltpu.SemaphoreType.DMA((2, 3)),
        ],
    )
    return pl.pallas_call(
        functools.partial(_experts_kernel, block_off=block_off, n_call=n_call, layer=layer),
        grid_spec=grid_spec,
        out_shape=jax.ShapeDtypeStruct((n_blocks * MOE_ROWS, half), jnp.int32),
        input_output_aliases=aliases,
        compiler_params=_params(("arbitrary",)),
        name="experts",
    )(*args)


def _combine_ln_kernel(*refs, alpha, emit_next, n_prev):
    y_ref, gw_ref, tok_ref, x_ref, sg_ref, su_ref, sd_ref, gate_ref, lng_ref, lnb_ref = refs[:10]
    outs = refs[len(refs) - (2 if emit_next else 1):]
    if emit_next:
        sc_ref, sh_ref = refs[10:12]
        xo_ref, u_ref = outs
    else:
        (xo_ref,) = outs
    lo, hi = _unpack_bf16_pairs(tok_ref[...])
    hg = _dot_halves(lo, hi, sg_ref)
    hu = _dot_halves(lo, hi, su_ref)
    h = hg * (1.0 / (1.0 + jnp.exp(-hg))) * hu
    gw = gw_ref[...]
    f_lo = f_hi = None
    for k in range(TOP_K):
        y_lo, y_hi = _unpack_bf16_pairs(y_ref[k])
        w = gw[:, k:k + 1]
        f_lo = y_lo.astype(F32) * w if f_lo is None else f_lo + y_lo.astype(F32) * w
        f_hi = y_hi.astype(F32) * w if f_hi is None else f_hi + y_hi.astype(F32) * w
    f = jnp.concatenate([f_lo, f_hi], axis=-1) + jnp.dot(h.astype(BF16), sd_ref[...], preferred_element_type=F32)
    xn = _layer_norm(alpha * x_ref[...] + gate_ref[...] * f, lng_ref[...], lnb_ref[...])
    xo_ref[...] = xn
    if emit_next:
        u_ref[...] = (xn * (1.0 + sc_ref[...]) + sh_ref[...]).astype(u_ref.dtype)


def _combine_ln(y3, gw_t, tok, x, sg, su, sd, mod, ln_g, ln_b, mod_next, prev, *, t, row_off, tm, alpha, mod_row):
    d = x.shape[1]
    emit_next = mod_next is not None
    off = row_off // tm
    assert off * tm == row_off

    def rows(i):
        return (off + i, 0)

    def part_mod_row(r):
        return mod_row(r + row_off)

    in_specs = [
        pl.BlockSpec((TOP_K, tm, d // 2), lambda i: (0, i, 0)),
        pl.BlockSpec((tm, SUBLANES), rows),
        pl.BlockSpec((tm, d // 2), rows),
        pl.BlockSpec((tm, d), rows),
        _const_spec(sg.shape), _const_spec(su.shape), _const_spec(sd.shape),
        _mod_spec(d, 5, part_mod_row, tm),
        _const_spec((1, d)), _const_spec((1, d)),
    ]
    args = [y3, gw_t, tok, x, sg, su, sd, mod, ln_g, ln_b]
    out_specs = [pl.BlockSpec((tm, d), rows)]
    out_shape = [jax.ShapeDtypeStruct((t, d), F32)]
    if emit_next:
        in_specs += [_mod_spec(d, 1, part_mod_row, tm), _mod_spec(d, 0, part_mod_row, tm)]
        args += [mod_next, mod_next]
        out_specs.append(pl.BlockSpec((tm, d), rows))
        out_shape.append(jax.ShapeDtypeStruct((t, d), BF16))
    aliases = {}
    if prev is not None:
        for k, p in enumerate(prev):
            in_specs.append(pl.BlockSpec(memory_space=pl.ANY))
            aliases[len(args)] = k
            args.append(p)
    return pl.pallas_call(
        functools.partial(_combine_ln_kernel, alpha=alpha, emit_next=emit_next, n_prev=len(aliases)),
        grid=(y3.shape[1] // tm,),
        in_specs=in_specs,
        out_specs=out_specs,
        out_shape=out_shape,
        input_output_aliases=aliases,
        compiler_params=_params(("parallel",)),
        name="combine_ln",
    )(*args)


def _moe(tok, x, t, layer, router_w, router_b, wg, wu, wd, sg, su, sd, mod, ln_g, ln_b, mod_next, *, alpha, mod_row,
         tm):
    half = tok.shape[1]
    tt = ROUTER_TILE
    idx, gw, rank, cnt = _router(tok, router_w.T.astype(BF16), router_b.reshape(N_EXPERTS, 1), t=t, tt=tt)
    n_asg = t * TOP_K
    counts = cnt[:, 0].astype(jnp.int32)
    padded = (counts + MOE_ROWS - 1) // MOE_ROWS * MOE_ROWS
    pend = jnp.cumsum(padded)
    pstart = pend - padded
    sc_rows = V7X_SC_CORES * V7X_SC_SUBCORES * SC_GATHER_ROWS * 2
    blocks_granule = sum(MOE_DISPATCH_SPLIT) * max(sc_rows // MOE_ROWS, 1)
    assert (blocks_granule // sum(MOE_DISPATCH_SPLIT) * MOE_ROWS) % sc_rows == 0
    n_blocks = -(-((n_asg + N_EXPERTS * (MOE_ROWS - 1)) // MOE_ROWS + 1) // blocks_granule) * blocks_granule
    assert n_asg % sc_rows == 0
    block_start = jnp.arange(n_blocks, dtype=jnp.int32) * MOE_ROWS
    block_e = jnp.minimum(jnp.sum((pend[None, :] <= block_start[:, None]).astype(jnp.int32), axis=1), N_EXPERTS - 1)
    nb_used = (pend[-1] // MOE_ROWS).astype(jnp.int32).reshape(1)
    blk = jnp.arange(n_blocks, dtype=jnp.int32)
    seg_first = jnp.concatenate([jnp.ones((1,), bool), block_e[1:] != block_e[:-1]])
    seg_idx = jnp.cumsum(seg_first.astype(jnp.int32)) - 1
    later_first = lax.cummin(jnp.where(seg_first, blk, n_blocks), reverse=True)
    next_start = jnp.concatenate([later_first[1:], jnp.full((1,), n_blocks, jnp.int32)])
    pos2d = _assign_slots(idx, rank, pstart.astype(F32).reshape(N_EXPERTS, 1), t=t, tt=tt)[:TOP_K]
    pos = pos2d.reshape(-1)
    tok_of_asg = np.tile(np.arange(t, dtype=np.int32), TOP_K)
    n_pad = n_blocks * MOE_ROWS - n_asg
    seg_pad_end = jnp.cumsum(padded - counts)
    j = jnp.arange(n_pad, dtype=jnp.int32)
    pad_e = jnp.sum((seg_pad_end[None, :] <= j[:, None]).astype(jnp.int32), axis=1)
    seg_base = pstart + counts - (seg_pad_end - (padded - counts))
    in_seg = j + jnp.sum(jnp.where(pad_e[:, None] == jnp.arange(N_EXPERTS)[None, :], seg_base[None, :], 0), axis=1)
    pad_slot = jnp.where(pad_e < N_EXPERTS, in_seg, pend[-1] + j - seg_pad_end[-1])
    _, slot_tok = lax.sort((jnp.concatenate([pos, pad_slot]), jnp.concatenate([tok_of_asg, pad_slot % t])),
                           num_keys=1)
    unit = n_blocks // sum(MOE_DISPATCH_SPLIT)
    starts = [unit * sum(MOE_DISPATCH_SPLIT[:i]) for i in range(len(MOE_DISPATCH_SPLIT) + 1)]
    xs = [_sc_gather_rows(tok, slot_tok[lo * MOE_ROWS:hi * MOE_ROWS]) for lo, hi in zip(starts[:-1], starts[1:])]
    y = None
    for part, lo in zip(xs, starts[:-1]):
        y = _experts(part, block_e, nb_used, seg_idx, next_start, wg, wu, wd, layer, y, block_off=lo,
                     n_blocks=n_blocks)
    n_cparts = MOE_COMBINE_PARTS if (t // MOE_COMBINE_PARTS * TOP_K) % sc_rows == 0 else 1
    t_part = t // n_cparts
    y3 = [_sc_gather_rows(y, pos2d[:, i * t_part:(i + 1) * t_part].reshape(-1)).reshape(TOP_K, t_part, half)
          for i in range(n_cparts)]
    outs = None
    gw_t = gw.T
    for i in range(n_cparts):
        outs = _combine_ln(y3[i], gw_t, tok, x, sg, su, sd, mod, ln_g, ln_b, mod_next, outs, t=t,
                           row_off=i * t_part, tm=tm, alpha=alpha, mod_row=mod_row)
    return outs


def _rope64(r, c_ref, sa_ref, sb_ref):
    return r * c_ref[...] + pltpu.roll(r, LANES - QK_ROPE // 2, 1) * sa_ref[...] + pltpu.roll(r, QK_ROPE // 2, 1) * sb_ref[...]


def _mla_q_kernel(d_ref, gain_ref, w_ref, c_ref, sa_ref, sb_ref, q_ref, *, scale):
    n = _rms(d_ref[...], gain_ref[...]).astype(BF16)
    q = jnp.dot(n, w_ref[...], preferred_element_type=F32)
    for h in range(MLA_HEADS):
        lo = h * MLA_DK_PAD
        q_ref[:, lo:lo + QK_NOPE] = (q[:, lo:lo + QK_NOPE] * scale).astype(q_ref.dtype)
        r = _rope64(q[:, lo + QK_NOPE:lo + MLA_DK_PAD], c_ref, sa_ref, sb_ref)
        q_ref[:, lo + QK_NOPE:lo + MLA_DK_PAD] = (r * scale).astype(q_ref.dtype)


def _mla_kv_kernel(ckv_ref, kr_ref, gain_ref, wk_ref, wv_ref, c_ref, sa_ref, sb_ref, k_ref, v_ref):
    n = _rms(ckv_ref[...], gain_ref[...]).astype(BF16)
    kn = jnp.dot(n, wk_ref[...], preferred_element_type=F32)
    v_ref[...] = jnp.dot(n, wv_ref[...], preferred_element_type=F32).astype(v_ref.dtype)
    kr = _rope64(kr_ref[...], c_ref, sa_ref, sb_ref).astype(k_ref.dtype)
    for h in range(MLA_HEADS):
        lo = h * MLA_DK_PAD
        k_ref[:, lo:lo + QK_NOPE] = kn[:, h * QK_NOPE:(h + 1) * QK_NOPE].astype(k_ref.dtype)
        k_ref[:, lo + QK_NOPE:lo + MLA_DK_PAD] = kr


def _axial_angles(n_tok, rot_dim):
    rows = n_tok // GRID_W
    n_freq = rot_dim // 4
    inv = (ROPE_THETA ** (-np.arange(n_freq, dtype=np.float32) / n_freq)).astype(np.float32)
    row = np.repeat(np.arange(rows, dtype=np.float32), GRID_W)
    col = np.tile(np.arange(GRID_W, dtype=np.float32), rows)
    return np.concatenate([row[:, None] * inv, col[:, None] * inv], axis=-1)


def _rope_tables_128(n_tok, ident_rows):
    ang = _axial_angles(n_tok, HEAD_DIM)
    cos, sin = np.cos(ang), np.sin(ang)
    c = np.concatenate([cos, cos], axis=-1)
    s = np.concatenate([-sin, sin], axis=-1)
    c = np.concatenate([c, np.ones((ident_rows, HEAD_DIM), np.float32)], axis=0)
    s = np.concatenate([s, np.zeros((ident_rows, HEAD_DIM), np.float32)], axis=0)
    return c.astype(np.float32), s.astype(np.float32)


def _rope_tables_64(n_tok, ident_rows):
    ang = _axial_angles(n_tok, QK_ROPE)
    cos, sin = np.cos(ang), np.sin(ang)
    half = QK_ROPE // 2
    z = np.zeros((n_tok, LANES - QK_ROPE), np.float32)
    zh = np.zeros((n_tok, half), np.float32)
    c = np.concatenate([cos, cos, z], axis=-1)
    sa = np.concatenate([-sin, zh, z], axis=-1)
    sb = np.concatenate([zh, sin, z], axis=-1)
    ci = np.concatenate([np.ones((ident_rows, QK_ROPE), np.float32),
                         np.zeros((ident_rows, LANES - QK_ROPE), np.float32)], axis=-1)
    zi = np.zeros((ident_rows, LANES), np.float32)
    tables = np.concatenate([c, ci], 0), np.concatenate([sa, zi], 0), np.concatenate([sb, zi], 0)
    return tuple(tab.astype(np.float32) for tab in tables)


def kernel(x, c, ctx, c_ctx, w_ada, b_ada, ln_g, ln_b, a_w_in, a_conv_w, a_q_gain, a_k_gain, a_w_out, m_w_down, m_q_gain, m_kv_gain, m_w_uq, m_w_ukv, m_w_out, router_w, router_b, e_w_gate, e_w_up, e_w_down, s_w_gate, s_w_up, s_w_down):
    batch, seq, d = x.shape
    ctx_len = ctx.shape[1]
    depth = w_ada.shape[0]
    assert depth == 2, "one conv+GQA layer followed by one MLA layer"
    alpha = (2 * depth) ** 0.25
    t_lat = batch * seq
    t_ctx = batch * ctx_len
    t_all = t_lat + t_ctx
    tr = ROW_TILE
    assert seq % tr == 0 and ctx_len % tr == 0 and seq % GRID_W == 0
    lat_tiles = t_lat // tr
    lat_seq_tiles = seq // tr
    ctx_seq_tiles = ctx_len // tr
    lk = ctx_len + seq

    def mod_row(r):
        return jnp.minimum(r // seq, batch)

    def kv_block(i):
        is_lat = i < lat_tiles
        cidx = i - lat_tiles
        b = jnp.where(is_lat, i // lat_seq_tiles, cidx // ctx_seq_tiles)
        rb = jnp.where(is_lat, ctx_seq_tiles + i % lat_seq_tiles, cidx % ctx_seq_tiles)
        return b, rb

    def pos_block(i):
        return jnp.where(i < lat_tiles, i % lat_seq_tiles, lat_seq_tiles)

    rows = -(-(batch + 1) // SUBLANES) * SUBLANES
    cond = jnp.concatenate([c, c_ctx[None, :], jnp.zeros((rows - batch - 1, d), F32)], axis=0)
    mod = _ada_table(cond, w_ada, b_ada).reshape(depth, rows, 1, 6 * d)

    x_pair = (x.reshape(t_lat, d), ctx.reshape(t_ctx, d))
    tm_out = OUTPROJ_ROW_TILE if (seq % OUTPROJ_ROW_TILE == 0 and t_ctx % OUTPROJ_ROW_TILE == 0) else tr

    tm_in = W_IN_ROW_TILE if (t_ctx % W_IN_ROW_TILE == 0 and seq % W_IN_ROW_TILE == 0) else tr
    proj = _mod_matmul(x_pair, mod[0], a_w_in[0].astype(BF16), BF16, tm_in, W_IN_COL_TILE, mod_row)

    cos128, sin128 = _rope_tables_128(seq, tr)
    d_q = ATT_HEADS * HEAD_DIM
    d_kv = ATT_KV_HEADS * HEAD_DIM
    qkv_w = d_q + 2 * d_kv
    qkv_blk = 3 * CONV_DIM // qkv_w
    assert qkv_blk * qkv_w == 3 * CONV_DIM
    q0, k0, v0 = pl.pallas_call(
        functools.partial(_qkprep_kernel, scale=1.0 / math.sqrt(HEAD_DIM)),
        grid=(t_all // tr,),
        in_specs=[
            pl.BlockSpec((tr, qkv_w), lambda i: (i, qkv_blk)),
            pl.BlockSpec((tr, HEAD_DIM), lambda i: (pos_block(i), 0)),
            pl.BlockSpec((tr, HEAD_DIM), lambda i: (pos_block(i), 0)),
            _const_spec((1, HEAD_DIM)), _const_spec((1, HEAD_DIM)),
        ],
        out_specs=[
            pl.BlockSpec((tr, d_q), lambda i: (i, 0)),
            pl.BlockSpec((None, tr, d_kv), lambda i: (*kv_block(i), 0)),
            pl.BlockSpec((None, tr, d_kv), lambda i: (*kv_block(i), 0)),
        ],
        out_shape=[
            jax.ShapeDtypeStruct((t_all, d_q), BF16),
            jax.ShapeDtypeStruct((batch, lk, d_kv), BF16),
            jax.ShapeDtypeStruct((batch, lk, d_kv), BF16),
        ],
        compiler_params=_params(("parallel",)),
        name="qk_prep",
    )(proj, cos128, sin128, a_q_gain[0].reshape(1, HEAD_DIM), a_k_gain[0].reshape(1, HEAD_DIM))

    grp = ATT_HEADS // ATT_KV_HEADS
    att_lat = _attention(q0, k0, v0, batch=batch, sq=seq, lk=lk, n_kv=ATT_KV_HEADS, group=grp, dk=HEAD_DIM,
                         dv=HEAD_DIM, tq=GQA_Q_TILE if seq % GQA_Q_TILE == 0 else tr, rows=ATTN_CHAIN_ROWS, q_row_off=0)
    att_ctx = _attention(q0, k0, v0, batch=batch, sq=ctx_len, lk=ctx_len, n_kv=ATT_KV_HEADS, group=grp,
                         dk=HEAD_DIM, dv=HEAD_DIM, tq=tr, rows=ATTN_CHAIN_ROWS, q_row_off=t_lat)

    conv0 = _conv_gate(proj, a_conv_w[0], t=t_all, tm=tr, tc=CONV_COL_TILE, lat_tiles=lat_tiles,
                       lat_seq_tiles=lat_seq_tiles, ctx_seq_tiles=ctx_seq_tiles)

    w_out0 = a_w_out[0].astype(BF16)
    x1, tok0 = _outproj_ln([conv0, (att_lat, att_ctx)], [w_out0[:CONV_DIM], w_out0[CONV_DIM:]], x_pair, mod[0],
                           ln_g[0, 0].reshape(1, d), ln_b[0, 0].reshape(1, d), t=t_all, tm=tm_out, alpha=alpha,
                           mod_row=mod_row)

    x2, u1 = _moe(tok0, x1, t_all, 0, router_w[0], router_b[0], e_w_gate, e_w_up, e_w_down,
                  s_w_gate[0].astype(BF16), s_w_up[0].astype(BF16), s_w_down[0].astype(BF16), mod[0],
                  ln_g[0, 1].reshape(1, d), ln_b[0, 1].reshape(1, d), mod[1], alpha=alpha,
                  mod_row=mod_row, tm=COMBINE_ROW_TILE)

    n_down = Q_LORA + KV_LORA + QK_ROPE
    n_down_pad = -(-n_down // LANES) * LANES
    w_down = jnp.pad(m_w_down[0], ((0, 0), (0, n_down_pad - n_down))).astype(BF16)
    down = _matmul(u1, w_down, F32, DOWN_ROW_TILE, n_down_pad)

    dqk = QK_NOPE + QK_ROPE
    w_uq = m_w_uq[0].reshape(Q_LORA, MLA_HEADS, dqk)
    w_uq = jnp.pad(w_uq, ((0, 0), (0, 0), (0, MLA_DK_PAD - dqk))).reshape(Q_LORA, MLA_HEADS * MLA_DK_PAD).astype(BF16)
    w_ukv = m_w_ukv[0].reshape(KV_LORA, MLA_HEADS, QK_NOPE + V_DIM)
    w_uk = w_ukv[:, :, :QK_NOPE].reshape(KV_LORA, MLA_HEADS * QK_NOPE).astype(BF16)
    w_uv = w_ukv[:, :, QK_NOPE:].reshape(KV_LORA, MLA_HEADS * V_DIM).astype(BF16)

    c64, sa64, sb64 = _rope_tables_64(seq, tr)
    rope_specs = [pl.BlockSpec((tr, LANES), lambda i: (pos_block(i), 0))] * 3
    q1 = pl.pallas_call(
        functools.partial(_mla_q_kernel, scale=1.0 / math.sqrt(dqk)),
        grid=(lat_tiles,),
        in_specs=[
            pl.BlockSpec((tr, Q_LORA), lambda i: (i, 0)),
            _const_spec((1, Q_LORA)),
            _const_spec(w_uq.shape),
        ] + rope_specs,
        out_specs=pl.BlockSpec((tr, MLA_HEADS * MLA_DK_PAD), lambda i: (i, 0)),
        out_shape=jax.ShapeDtypeStruct((t_lat, MLA_HEADS * MLA_DK_PAD), BF16),
        compiler_params=_params(("parallel",)),
        name="mla_q",
    )(down, m_q_gain[0].reshape(1, Q_LORA), w_uq, c64, sa64, sb64)

    assert KV_LORA == Q_LORA and (Q_LORA + KV_LORA) % LANES == 0
    k1, v1 = pl.pallas_call(
        _mla_kv_kernel,
        grid=(t_all // tr,),
        in_specs=[
            pl.BlockSpec((tr, KV_LORA), lambda i: (i, 1)),
            pl.BlockSpec((tr, LANES), lambda i: (i, (Q_LORA + KV_LORA) // LANES)),
            _const_spec((1, KV_LORA)),
            _const_spec(w_uk.shape), _const_spec(w_uv.shape),
        ] + rope_specs,
        out_specs=[
            pl.BlockSpec((None, tr, MLA_HEADS * MLA_DK_PAD), lambda i: (*kv_block(i), 0)),
            pl.BlockSpec((None, tr, MLA_HEADS * V_DIM), lambda i: (*kv_block(i), 0)),
        ],
        out_shape=[
            jax.ShapeDtypeStruct((batch, lk, MLA_HEADS * MLA_DK_PAD), BF16),
            jax.ShapeDtypeStruct((batch, lk, MLA_HEADS * V_DIM), BF16),
        ],
        compiler_params=_params(("parallel",)),
        name="mla_kv",
    )(down, down, m_kv_gain[0].reshape(1, KV_LORA), w_uk, w_uv, c64, sa64, sb64)

    att1 = _attention(q1, k1, v1, batch=batch, sq=seq, lk=lk, n_kv=MLA_HEADS, group=1, dk=MLA_DK_PAD, dv=V_DIM,
                      tq=next(q for q in MLA_Q_TILES + (tr,) if seq % q == 0), rows=ATTN_CHAIN_ROWS, q_row_off=0)

    x3, tok1 = _outproj_ln([att1], [m_w_out[0].astype(BF16)], x2, mod[1], ln_g[1, 0].reshape(1, d),
                           ln_b[1, 0].reshape(1, d), t=t_lat, tm=tm_out, alpha=alpha, mod_row=mod_row)

    (x4,) = _moe(tok1, x3, t_lat, 1, router_w[1], router_b[1], e_w_gate, e_w_up, e_w_down,
                 s_w_gate[1].astype(BF16), s_w_up[1].astype(BF16), s_w_down[1].astype(BF16), mod[1],
                 ln_g[1, 1].reshape(1, d), ln_b[1, 1].reshape(1, d), None, alpha=alpha,
                 mod_row=mod_row, tm=COMBINE_ROW_TILE)
    return x4.reshape(batch, seq, d)
```
